```python
import math
import jax
import jax.numpy as jnp
from jax import lax
import numpy as np

D_MODEL = 1024
BATCH = 1
SEQ = 16384
DEPTH = 2
DEC_BATCH = 16
DEC_SEQ = 16
PAST_LEN = 2048

CHUNK = 64
HEAD_DIM = 64
N_GROUP_HEADS = 4
GROUP_W = N_GROUP_HEADS * HEAD_DIM
N_GROUPS = 4
MIX_W = N_GROUPS * GROUP_W
A_PAST_CHUNKS = 8
A_PAST = A_PAST_CHUNKS * CHUNK
REL_MAX = 2 * CHUNK
CONV_C = 4
Q_LORA = 256
KV_LORA = 128
QK_NOPE = 64
QK_ROPE = 32
V_HEAD = 64
ROPE_THETA = 10000.0
MLA_SCALE = (QK_NOPE + QK_ROPE) ** -0.5
Q_BLOCK = 128
D_FF = 2816
CONV_F = 3
EPS = 1e-6
IN_SIZES = (GROUP_W,) * 3 + (GROUP_W,) * 4 + (N_GROUP_HEADS,) * 2 + (GROUP_W,) * 4 + (N_GROUP_HEADS,) * 2 + (Q_LORA, KV_LORA, QK_ROPE)
D_IN = sum(IN_SIZES)

kernel_name = 'hybrid_chunk_streaming_encoder_step'


def rmsnorm(x, g):
    xf = x.astype(jnp.float32)
    y = xf * lax.rsqrt(jnp.mean(xf * xf, -1, keepdims=True) + EPS)
    return (y * g.astype(jnp.float32)).astype(x.dtype)


def head_norm(o, g):
    of = o.astype(jnp.float32)
    y = of * lax.rsqrt(jnp.mean(of * of, -1, keepdims=True) + EPS) * g.astype(jnp.float32)
    return y.reshape(o.shape[0], o.shape[1], -1)


def l2norm(x):
    return x * lax.rsqrt(jnp.sum(x * x, -1, keepdims=True) + 1e-6)


def split_cols(h, sizes):
    idx = np.cumsum(sizes)[:-1].tolist()
    return jnp.split(h, idx, axis=-1)


def causal_dwconv(x, hist, w):
    T, W = x.shape[1], w.shape[0]
    xp = jnp.concatenate([hist.astype(x.dtype), x], axis=1)
    y = xp[:, 0:T] * w[0]
    for j in range(1, W):
        y = y + xp[:, j:j + T] * w[j]
    return y, xp[:, T:]


def rope(x, pos):
    half = x.shape[-1] // 2
    inv = ROPE_THETA ** (-jnp.arange(half, dtype=jnp.float32) / half)
    ang = pos.astype(jnp.float32)[:, None] * inv[None, :]
    cos = jnp.cos(ang)[None, :, None, :]
    sin = jnp.sin(ang)[None, :, None, :]
    x1 = x[..., :half].astype(jnp.float32)
    x2 = x[..., half:].astype(jnp.float32)
    return jnp.concatenate([x1 * cos - x2 * sin, x1 * sin + x2 * cos], -1).astype(x.dtype)


def rel_bias(table, n_past, n_q, n_k):
    d = n_past + jnp.arange(n_q)[:, None] - jnp.arange(n_k)[None, :]
    idx = jnp.clip(d, -REL_MAX, REL_MAX) + REL_MAX
    return table[:, idx]


def band_attn_prompt(q, k, v, table):
    B, T, H, Dh = q.shape
    nc, P = T // CHUNK, A_PAST_CHUNKS
    band = (P + 1) * CHUNK
    pad = lambda t: jnp.pad(t, ((0, 0), (P * CHUNK, 0), (0, 0), (0, 0))).reshape(B, nc + P, CHUNK, H, Dh)
    kp, vp = pad(k), pad(v)
    kb = jnp.concatenate([kp[:, j:j + nc] for j in range(P + 1)], axis=2)
    vb = jnp.concatenate([vp[:, j:j + nc] for j in range(P + 1)], axis=2)
    kpos = (jnp.arange(nc)[:, None] - P) * CHUNK + jnp.arange(band)[None, :]
    s = jnp.einsum('bnqhd,bnkhd->bnhqk', q.reshape(B, nc, CHUNK, H, Dh), kb).astype(jnp.float32) * Dh ** -0.5
    s = s + rel_bias(table, P * CHUNK, CHUNK, band)[None, None].astype(jnp.float32)
    s = jnp.where((kpos >= 0)[None, :, None, None, :], s, -jnp.inf)
    p = jax.nn.softmax(s, axis=-1)
    return jnp.einsum('bnhqk,bnkhd->bnqhd', p.astype(v.dtype), vb).reshape(B, T, H, Dh)


def band_attn_sample(q, k, v, ck, cv, table):
    L, S, Dh = ck.shape[1], q.shape[1], q.shape[-1]
    kk = jnp.concatenate([ck.astype(k.dtype), k], axis=1)
    vv = jnp.concatenate([cv.astype(v.dtype), v], axis=1)
    s = jnp.einsum('bqhd,bkhd->bhqk', q, kk).astype(jnp.float32) * Dh ** -0.5
    s = s + rel_bias(table, L, S, L + S)[None].astype(jnp.float32)
    p = jax.nn.softmax(s, axis=-1)
    return jnp.einsum('bhqk,bkhd->bqhd', p.astype(vv.dtype), vv)


def mlstm_block(carry, blk):
    C, n, m = carry
    q, k, v, ig, lf = blk
    L = q.shape[2]
    g = jnp.cumsum(lf, axis=-1)
    causal = jnp.tril(jnp.ones((L, L), bool))
    lmat = jnp.where(causal, g[..., :, None] - g[..., None, :] + ig[..., None, :], -jnp.inf)
    linter = g + m[..., None]
    mt = jnp.maximum(linter, jnp.max(lmat, axis=-1))
    w = jnp.einsum('bhtd,bhsd->bhts', q, k) * jnp.exp(lmat - mt[..., None])
    inter = jnp.exp(linter - mt)
    num = jnp.einsum('bhts,bhsd->bhtd', w, v) + inter[..., None] * jnp.einsum('bhtd,bhde->bhte', q, C)
    den = jnp.sum(w, axis=-1) + inter * jnp.einsum('bhtd,bhd->bht', q, n)
    h = num / jnp.maximum(jnp.abs(den), jnp.exp(-mt))[..., None]
    m_new = mt[..., -1]
    ws = jnp.exp(g[..., -1:] - g + ig - m_new[..., None])
    dprev = jnp.exp(g[..., -1] + m - m_new)
    C_new = dprev[..., None, None] * C + jnp.einsum('bhs,bhsd,bhse->bhde', ws, k, v)
    n_new = dprev[..., None] * n + jnp.einsum('bhs,bhsd->bhd', ws, k)
    return (C_new, n_new, m_new), h


def gdn_block(S, blk):
    q, k, v, beta, g = blk
    L, Dv = q.shape[2], v.shape[-1]
    G = jnp.cumsum(g, axis=-1)
    incl = jnp.tril(jnp.ones((L, L), bool))
    strict = jnp.tril(jnp.ones((L, L), bool), -1)
    decay = jnp.exp(jnp.where(incl, G[..., :, None] - G[..., None, :], -jnp.inf))
    a_low = jnp.where(strict, beta[..., :, None] * jnp.einsum('bhid,bhjd->bhij', k, k) * decay, 0.0)
    mat = a_low + jnp.eye(L, dtype=a_low.dtype)
    eG = jnp.exp(G)[..., None]
    rhs = jnp.concatenate([v * beta[..., None], k * beta[..., None] * eG], axis=-1)
    sol = lax.linalg.triangular_solve(mat, rhs, left_side=True, lower=True, unit_diagonal=True)
    u, w = sol[..., :Dv], sol[..., Dv:]
    v_new = u - jnp.einsum('bhlk,bhkv->bhlv', w, S)
    o = jnp.einsum('bhlk,bhkv->bhlv', q * eG, S) + jnp.einsum('bhij,bhjv->bhiv', jnp.einsum('bhid,bhjd->bhij', q, k) * decay, v_new)
    GL = G[..., -1]
    S_new = jnp.exp(GL)[..., None, None] * S + jnp.einsum('bhl,bhlk,bhlv->bhkv', jnp.exp(GL[..., None] - G), k, v_new)
    return S_new, o


def blocked(step, carry, xs, T):
    if T <= CHUNK:
        return step(carry, xs)
    nb = T // CHUNK
    cut = lambda a: jnp.moveaxis(a.reshape(a.shape[:2] + (nb, CHUNK) + a.shape[3:]), 2, 0)
    carry, ys = lax.scan(step, carry, tuple(cut(a) for a in xs))
    ys = jnp.moveaxis(ys, 0, 2)
    return carry, ys.reshape(ys.shape[:2] + (T,) + ys.shape[4:])


def mla_attend(qn, qp, kn, kpe, v, mask):
    s = (jnp.einsum('bqhd,bkhd->bhqk', qn, kn) + jnp.einsum('bqhr,bkr->bhqk', qp, kpe)).astype(jnp.float32) * MLA_SCALE
    if mask is not None:
        s = jnp.where(mask, s, -jnp.inf)
    p = jax.nn.softmax(s, axis=-1)
    return jnp.einsum('bhqk,bkhd->bqhd', p.astype(v.dtype), v)


def mla_prompt(qn, qp, kn, kpe, v):
    B, T, H, _ = qn.shape
    nb = T // Q_BLOCK
    kchunk = jnp.arange(T) // CHUNK
    to_blocks = lambda t: jnp.moveaxis(t.reshape((B, nb, Q_BLOCK) + t.shape[2:]), 1, 0)

    def one(args):
        qn_b, qp_b, bi = args
        qchunk = (bi * Q_BLOCK + jnp.arange(Q_BLOCK)) // CHUNK
        return mla_attend(qn_b, qp_b, kn, kpe, v, kchunk[None, :] <= qchunk[:, None])

    o = lax.map(one, (to_blocks(qn), to_blocks(qp), jnp.arange(nb)))
    return jnp.moveaxis(o, 0, 1).reshape(B, T, H, V_HEAD)


def layer(x, offset, st, lw):
    (g_mix, w_in, a_rel_bias, b_i_bias, b_f_bias, c_conv_w, c_a_log, c_dt_bias,
     d_g_q, d_w_q_up, d_g_kv, d_w_kv_up, g_head, w_out, g_ffn, w_up, f_conv_w, w_down) = lw
    B, T, _ = x.shape
    H, Dh = N_GROUP_HEADS, HEAD_DIM
    f32 = jnp.float32
    first = st is None
    pos = jnp.arange(T, dtype=jnp.int32) + offset

    proj = rmsnorm(x, g_mix) @ w_in
    (a_q, a_k, a_v, b_q, b_k, b_v, b_o, b_i, b_f,
     c_q, c_k, c_v, c_z, c_b, c_a, d_q, d_kv, d_kr) = split_cols(proj, IN_SIZES)
    heads = lambda t: t.reshape(B, T, H, Dh)
    bh = lambda t: jnp.swapaxes(heads(t), 1, 2).astype(f32)

    qa, ka, va = heads(a_q), heads(a_k), heads(a_v)
    if first:
        oa = band_attn_prompt(qa, ka, va, a_rel_bias)
        keep = min(A_PAST, T)
        new_ak, new_av = ka[:, T - keep:], va[:, T - keep:]
    else:
        oa = band_attn_sample(qa, ka, va, st[0], st[1], a_rel_bias)
        new_ak, new_av = ka, va

    ig = jnp.swapaxes((b_i + b_i_bias).astype(f32), 1, 2)
    lf = jax.nn.log_sigmoid(jnp.swapaxes((b_f + b_f_bias).astype(f32), 1, 2))
    if first:
        carry_b = (jnp.zeros((B, H, Dh, Dh), f32), jnp.zeros((B, H, Dh), f32), jnp.zeros((B, H), f32))
    else:
        carry_b = (st[2].astype(f32), st[3].astype(f32), st[4].astype(f32))
    (new_bc, new_bn, new_bm), hb = blocked(mlstm_block, carry_b, (bh(b_q), bh(b_k) * Dh ** -0.5, bh(b_v), ig, lf), T)
    ob = jnp.swapaxes(hb, 1, 2) * jax.nn.sigmoid(heads(b_o).astype(f32))

    hist_c = jnp.zeros((B, CONV_C - 1, 3 * GROUP_W), x.dtype) if first else st[6]
    cqkv, new_cconv = causal_dwconv(jnp.concatenate([c_q, c_k, c_v], axis=-1), hist_c, c_conv_w)
    cq, ck, cv = jnp.split(jax.nn.silu(cqkv), 3, axis=-1)
    qc = l2norm(bh(cq)) * Dh ** -0.5
    kc = l2norm(bh(ck))
    vc = bh(cv)
    beta = jnp.swapaxes(jax.nn.sigmoid(c_b.astype(f32)), 1, 2)
    gdec = jnp.swapaxes(-jnp.exp(c_a_log.astype(f32)) * jax.nn.softplus(c_a.astype(f32) + c_dt_bias.astype(f32)), 1, 2)
    s0 = jnp.zeros((B, H, Dh, Dh), f32) if first else st[5].astype(f32)
    new_cs, hc = blocked(gdn_block, s0, (qc, kc, vc, beta, gdec), T)
    oc = jnp.swapaxes(hc, 1, 2)

    ckv = rmsnorm(d_kv, d_g_kv)
    kpe = rope(d_kr[:, :, None, :], pos)[:, :, 0]
    qd = (rmsnorm(d_q, d_g_q) @ d_w_q_up).reshape(B, T, H, QK_NOPE + QK_ROPE)
    qn, qp = qd[..., :QK_NOPE], rope(qd[..., QK_NOPE:], pos)
    if first:
        ckv_all, kpe_all = ckv, kpe
    else:
        ckv_all = jnp.concatenate([st[7].astype(ckv.dtype), ckv], axis=1)
        kpe_all = jnp.concatenate([st[8].astype(kpe.dtype), kpe], axis=1)
    kv = (ckv_all @ d_w_kv_up).reshape(B, ckv_all.shape[1], H, QK_NOPE + V_HEAD)
    kn, vd = kv[..., :QK_NOPE], kv[..., QK_NOPE:]
    od = mla_prompt(qn, qp, kn, kpe_all, vd) if first else mla_attend(qn, qp, kn, kpe_all, vd, None)

    gh = g_head.reshape(N_GROUPS, H, Dh)
    mixed = jnp.concatenate([
        head_norm(oa, gh[0]),
        head_norm(ob, gh[1]),
        head_norm(oc, gh[2]) * jax.nn.silu(c_z.astype(f32)),
        head_norm(od, gh[3])], axis=-1).astype(x.dtype)
    x = x + mixed @ w_out

    up = rmsnorm(x, g_ffn) @ w_up
    ga, u = up[..., :D_FF], up[..., D_FF:]
    hist_f = jnp.zeros((B, CONV_F - 1, D_FF), x.dtype) if first else st[9]
    ga, new_fconv = causal_dwconv(ga, hist_f, f_conv_w)
    x = x + (jax.nn.silu(ga) * u) @ w_down
    return x, (new_ak, new_av, new_bc, new_bn, new_bm, new_cs, new_cconv, ckv, kpe, new_fconv)


def setup_inputs(seed: int = 0) -> dict:
    key = jax.random.key(seed)
    keys = jax.random.split(key, 40)
    cnt = [0]

    def nk():
        cnt[0] += 1
        return keys[cnt[0] - 1]

    f32 = jnp.float32
    nrm = lambda shape, scale: jax.random.normal(nk(), shape, f32) * scale
    gain = lambda shape: 1.0 + 0.02 * jax.random.normal(nk(), shape, f32)
    H, Dh = N_GROUP_HEADS, HEAD_DIM
    a_cache = min(A_PAST, PAST_LEN)
    x_prompt = nrm((BATCH, SEQ, D_MODEL), 1.0)
    x_sample = nrm((DEC_BATCH, DEC_SEQ, D_MODEL), 1.0)
    cache_a_k = nrm((DEPTH, DEC_BATCH, a_cache, H, Dh), 1.0)
    cache_a_v = nrm((DEPTH, DEC_BATCH, a_cache, H, Dh), 1.0)
    state_b_c = nrm((DEPTH, DEC_BATCH, H, Dh, Dh), 0.1)
    state_b_n = nrm((DEPTH, DEC_BATCH, H, Dh), 0.1)
    state_b_m = jax.random.uniform(nk(), (DEPTH, DEC_BATCH, H), f32, 0.0, 2.0)
    state_c_s = nrm((DEPTH, DEC_BATCH, H, Dh, Dh), 0.1)
    cache_c_conv = nrm((DEPTH, DEC_BATCH, CONV_C - 1, 3 * GROUP_W), 1.0)
    cache_d_ckv = nrm((DEPTH, DEC_BATCH, PAST_LEN, KV_LORA), 1.0)
    cache_d_kpe = nrm((DEPTH, DEC_BATCH, PAST_LEN, QK_ROPE), 1.0)
    cache_ffn_conv = nrm((DEPTH, DEC_BATCH, CONV_F - 1, D_FF), 1.0)
    g_mix = gain((DEPTH, D_MODEL))
    w_in = nrm((DEPTH, D_MODEL, D_IN), D_MODEL ** -0.5)
    a_rel_bias = nrm((DEPTH, H, 2 * REL_MAX + 1), 0.5)
    b_i_bias = nrm((DEPTH, H), 0.1)
    b_f_bias = 3.0 + 3.0 * jax.random.uniform(nk(), (DEPTH, H), f32)
    c_conv_w = nrm((DEPTH, CONV_C, 3 * GROUP_W), CONV_C ** -0.5)
    c_a_log = jnp.log(jax.random.uniform(nk(), (DEPTH, H), f32, 1.0, 16.0))
    dt = jnp.exp(jax.random.uniform(nk(), (DEPTH, H), f32, math.log(1e-3), math.log(1e-1)))
    c_dt_bias = dt + jnp.log(-jnp.expm1(-dt))
    d_g_q = gain((DEPTH, Q_LORA))
    d_w_q_up = nrm((DEPTH, Q_LORA, H * (QK_NOPE + QK_ROPE)), Q_LORA ** -0.5)
    d_g_kv = gain((DEPTH, KV_LORA))
    d_w_kv_up = nrm((DEPTH, KV_LORA, H * (QK_NOPE + V_HEAD)), KV_LORA ** -0.5)
    g_head = gain((DEPTH, MIX_W))
    w_out = nrm((DEPTH, MIX_W, D_MODEL), MIX_W ** -0.5)
    g_ffn = gain((DEPTH, D_MODEL))
    w_up = nrm((DEPTH, D_MODEL, 2 * D_FF), D_MODEL ** -0.5)
    f_conv_w = nrm((DEPTH, CONV_F, D_FF), CONV_F ** -0.5)
    w_down = nrm((DEPTH, D_FF, D_MODEL), D_FF ** -0.5)
    g_final = gain((D_MODEL,))
    return {'x_prompt': x_prompt, 'x_sample': x_sample,
            'cache_a_k': cache_a_k, 'cache_a_v': cache_a_v,
            'state_b_c': state_b_c, 'state_b_n': state_b_n, 'state_b_m': state_b_m,
            'state_c_s': state_c_s, 'cache_c_conv': cache_c_conv,
            'cache_d_ckv': cache_d_ckv, 'cache_d_kpe': cache_d_kpe,
            'cache_ffn_conv': cache_ffn_conv,
            'g_mix': g_mix, 'w_in': w_in, 'a_rel_bias': a_rel_bias,
            'b_i_bias': b_i_bias, 'b_f_bias': b_f_bias,
            'c_conv_w': c_conv_w, 'c_a_log': c_a_log, 'c_dt_bias': c_dt_bias,
            'd_g_q': d_g_q, 'd_w_q_up': d_w_q_up, 'd_g_kv': d_g_kv, 'd_w_kv_up': d_w_kv_up,
            'g_head': g_head, 'w_out': w_out,
            'g_ffn': g_ffn, 'w_up': w_up, 'f_conv_w': f_conv_w, 'w_down': w_down,
            'g_final': g_final}


def reference(x_prompt, x_sample, cache_a_k, cache_a_v, state_b_c, state_b_n, state_b_m,
              state_c_s, cache_c_conv, cache_d_ckv, cache_d_kpe, cache_ffn_conv,
              g_mix, w_in, a_rel_bias, b_i_bias, b_f_bias, c_conv_w, c_a_log, c_dt_bias,
              d_g_q, d_w_q_up, d_g_kv, d_w_kv_up, g_head, w_out, g_ffn, w_up, f_conv_w, w_down,
              g_final):
    layer_w = (g_mix, w_in, a_rel_bias, b_i_bias, b_f_bias, c_conv_w, c_a_log, c_dt_bias,
               d_g_q, d_w_q_up, d_g_kv, d_w_kv_up, g_head, w_out, g_ffn, w_up, f_conv_w, w_down)
    past = cache_d_ckv.shape[2]
    xp, xs = x_prompt, x_sample
    new_p, new_s = [], []
    for l in range(DEPTH):
        lw = tuple(w[l] for w in layer_w)
        xp, sp_l = layer(xp, 0, None, lw)
        st = (cache_a_k[l], cache_a_v[l], state_b_c[l], state_b_n[l], state_b_m[l],
              state_c_s[l], cache_c_conv[l], cache_d_ckv[l], cache_d_kpe[l], cache_ffn_conv[l])
        xs, ss_l = layer(xs, past, st, lw)
        new_p.append(sp_l)
        new_s.append(ss_l)
    y_prompt = rmsnorm(xp, g_final)
    y_sample = rmsnorm(xs, g_final)
    sp = lambda i: jnp.stack([s[i] for s in new_p])
    ss = lambda i: jnp.stack([s[i] for s in new_s])
    a_k_prompt, a_k_sample = sp(0), ss(0)
    a_v_prompt, a_v_sample = sp(1), ss(1)
    b_c_prompt, b_c_sample = sp(2), ss(2)
    b_n_prompt, b_n_sample = sp(3), ss(3)
    b_m_prompt, b_m_sample = sp(4), ss(4)
    c_s_prompt, c_s_sample = sp(5), ss(5)
    c_conv_prompt, c_conv_sample = sp(6), ss(6)
    d_ckv_prompt, d_ckv_sample = sp(7), ss(7)
    d_kpe_prompt, d_kpe_sample = sp(8), ss(8)
    ffn_conv_prompt, ffn_conv_sample = sp(9), ss(9)
    return (y_prompt, y_sample,
            a_k_prompt, a_k_sample, a_v_prompt, a_v_sample,
            b_c_prompt, b_c_sample, b_n_prompt, b_n_sample, b_m_prompt, b_m_sample,
            c_s_prompt, c_s_sample, c_conv_prompt, c_conv_sample,
            d_ckv_prompt, d_ckv_sample, d_kpe_prompt, d_kpe_sample,
            ffn_conv_prompt, ffn_conv_sample)
```

```python
import functools
import math

import jax
import jax.numpy as jnp
from jax import lax
from jax.experimental import pallas as pl
from jax.experimental.pallas import tpu as pltpu

F32 = jnp.float32
BF16 = jnp.bfloat16

D_MODEL = 1024
CHUNK = 64
H = 4
DH = 64
GW = H * DH
A_PAST = 8 * CHUNK
REL_MAX = 2 * CHUNK
Q_LORA = 256
KV_LORA = 128
QK_NOPE = 64
QK_ROPE = 32
ROPE_THETA = 10000.0
MLA_SCALE = (QK_NOPE + QK_ROPE) ** -0.5
D_FF = 2816
EPS = 1e-6

COL_CX = 0
COL_A = 3 * GW
COL_TAIL = 6 * GW
TAIL_W = 512
COL_B = COL_TAIL + TAIL_W
COL_CZ = COL_B + 4 * GW
PROJ_W = COL_CZ + GW
GATE_BLK = (COL_TAIL + 384) // 128
GATE_OFF = 32
N_GATES = 16

LANES = 128
SUBLANES = 8
VMEM_LIMIT = 56 * 1024 * 1024

NEG_INF = float("-inf")


def _cparams(*sem):
    return pltpu.CompilerParams(dimension_semantics=sem, vmem_limit_bytes=VMEM_LIMIT)


def _dot(a, b):
    return jnp.dot(a, b, preferred_element_type=F32)


def _dot_nt(a, b):
    return lax.dot_general(a, b, (((1,), (1,)), ((), ())), preferred_element_type=F32)


def _dot_tn(a, b):
    return lax.dot_general(a, b, (((0,), (0,)), ((), ())), preferred_element_type=F32)


def _dot_hp(a, b):
    return jnp.dot(a, b, preferred_element_type=F32, precision=lax.Precision.HIGHEST)


def _split3(x):
    hi = x.astype(BF16)
    r1 = x - hi.astype(F32)
    mid = r1.astype(BF16)
    lo = (r1 - mid.astype(F32)).astype(BF16)
    return hi, mid, lo


def _rms(x, g):
    return x * lax.rsqrt(jnp.mean(x * x, axis=-1, keepdims=True) + EPS) * g


def _log_sigmoid(x):
    return jnp.minimum(x, 0.0) - jnp.log1p(jnp.exp(-jnp.abs(x)))


def _softplus(x):
    return jnp.maximum(x, 0.0) + jnp.log1p(jnp.exp(-jnp.abs(x)))


def _sigmoid(x):
    return 1.0 / (1.0 + jnp.exp(-x))


def _iota(shape, dim):
    return lax.broadcasted_iota(jnp.int32, shape, dim)


def _inproj_kernel(x_ref, g_ref, w_ref, wgt_ref, proj_ref, gt_ref, h_scr):
    @pl.when(pl.program_id(1) == 0)
    def _():
        h = _rms(x_ref[...], g_ref[...]).astype(BF16)
        h_scr[...] = h
        gt_ref[...] = _dot_nt(wgt_ref[...], h)

    proj_ref[...] = _dot(h_scr[...], w_ref[...])


def _inproj(x, g, w, wgt, tm, tn):
    n = x.shape[0]
    return pl.pallas_call(
        _inproj_kernel,
        grid=(n // tm, PROJ_W // tn),
        in_specs=[
            pl.BlockSpec((tm, D_MODEL), lambda i, j: (i, 0)),
            pl.BlockSpec((1, D_MODEL), lambda i, j: (0, 0)),
            pl.BlockSpec((D_MODEL, tn), lambda i, j: (0, j)),
            pl.BlockSpec((N_GATES, D_MODEL), lambda i, j: (0, 0)),
        ],
        out_specs=[
            pl.BlockSpec((tm, tn), lambda i, j: (i, j)),
            pl.BlockSpec((N_GATES, tm), lambda i, j: (0, i)),
        ],
        out_shape=[
            jax.ShapeDtypeStruct((n, PROJ_W), F32),
            jax.ShapeDtypeStruct((N_GATES, n), F32),
        ],
        scratch_shapes=[pltpu.VMEM((tm, D_MODEL), BF16)],
        compiler_params=_cparams("parallel", "arbitrary"),
        name="inproj",
    )(x, g, w, wgt)


def _head_norm_store(out_ref, rows, h, o, gh_ref):
    g = gh_ref[:, h * DH:(h + 1) * DH]
    y = o * lax.rsqrt(jnp.mean(o * o, axis=-1, keepdims=True) + EPS) * g
    out_ref[rows, h * DH:(h + 1) * DH] = y.astype(out_ref.dtype)


def _band_prompt_kernel(q_ref, kp_ref, kc_ref, vp_ref, vc_ref, bias_ref, gh_ref, out_ref, *, qb):
    first = pl.program_id(0) == 0
    band = A_PAST + CHUNK
    nchunks = qb // CHUNK
    jj = _iota((CHUNK, band), 1)
    for h in range(H):
        hs = slice(h * DH, (h + 1) * DH)
        kcat = jnp.concatenate([kp_ref[:, hs], kc_ref[:, hs]], axis=0).astype(BF16)
        vcat = jnp.concatenate([vp_ref[:, hs], vc_ref[:, hs]], axis=0).astype(BF16)
        bias = bias_ref[h]
        for c in range(nchunks):
            rows = slice(c * CHUNK, (c + 1) * CHUNK)
            q = q_ref[rows, hs].astype(BF16)
            kb = kcat[c * CHUNK:c * CHUNK + band]
            vb = vcat[c * CHUNK:c * CHUNK + band]
            s = _dot_nt(q, kb) * (DH ** -0.5) + bias
            if c < A_PAST // CHUNK:
                valid = jnp.logical_or(jnp.logical_not(first), jj >= A_PAST - c * CHUNK)
                s = jnp.where(valid, s, NEG_INF)
            m = jnp.max(s, axis=-1, keepdims=True)
            p = jnp.exp(s - m)
            l = jnp.sum(p, axis=-1, keepdims=True)
            o = _dot(p.astype(BF16), vb) / l
            _head_norm_store(out_ref, rows, h, o, gh_ref)


def _band_prompt(proj, bias, gh, qb=A_PAST):
    t = proj.shape[0]
    assert qb == A_PAST and t % qb == 0
    prev = lambda i: jnp.maximum(i - 1, 0)
    cq = COL_A // GW
    return pl.pallas_call(
        functools.partial(_band_prompt_kernel, qb=qb),
        grid=(t // qb,),
        in_specs=[
            pl.BlockSpec((qb, GW), lambda i: (i, cq)),
            pl.BlockSpec((qb, GW), lambda i: (prev(i), cq + 1)),
            pl.BlockSpec((qb, GW), lambda i: (i, cq + 1)),
            pl.BlockSpec((qb, GW), lambda i: (prev(i), cq + 2)),
            pl.BlockSpec((qb, GW), lambda i: (i, cq + 2)),
            pl.BlockSpec((H, CHUNK, A_PAST + CHUNK), lambda i: (0, 0, 0)),
            pl.BlockSpec((1, GW), lambda i: (0, 0)),
        ],
        out_specs=pl.BlockSpec((qb, GW), lambda i: (i, 0)),
        out_shape=jax.ShapeDtypeStruct((t, GW), BF16),
        compiler_params=_cparams("parallel"),
        name="band_prompt",
    )(proj, proj, proj, proj, proj, bias, gh)


def _band_sample_kernel(q_ref, k_ref, v_ref, ck_ref, cv_ref, bias_ref, gh_ref, out_ref):
    npast = ck_ref.shape[0]
    rows = slice(0, q_ref.shape[0])
    for h in range(H):
        hs = slice(h * DH, (h + 1) * DH)
        q = q_ref[:, hs].astype(BF16)
        s1 = _dot_nt(q, ck_ref[:, hs].astype(BF16)) * (DH ** -0.5) + bias_ref[h, :, :npast]
        s2 = _dot_nt(q, k_ref[:, hs].astype(BF16)) * (DH ** -0.5) + bias_ref[h, :, npast:]
        m = jnp.maximum(jnp.max(s1, axis=-1, keepdims=True), jnp.max(s2, axis=-1, keepdims=True))
        p1 = jnp.exp(s1 - m)
        p2 = jnp.exp(s2 - m)
        l = jnp.sum(p1, axis=-1, keepdims=True) + jnp.sum(p2, axis=-1, keepdims=True)
        o = (_dot(p1.astype(BF16), cv_ref[:, hs].astype(BF16))
             + _dot(p2.astype(BF16), v_ref[:, hs].astype(BF16))) / l
        _head_norm_store(out_ref, rows, h, o, gh_ref)


def _band_sample(proj3, ck, cv, bias, gh):
    b, s, _ = proj3.shape
    npast = ck.shape[1]
    return pl.pallas_call(
        _band_sample_kernel,
        grid=(b,),
        in_specs=[
            pl.BlockSpec((None, s, GW), lambda i: (i, 0, COL_A // GW)),
            pl.BlockSpec((None, s, GW), lambda i: (i, 0, COL_A // GW + 1)),
            pl.BlockSpec((None, s, GW), lambda i: (i, 0, COL_A // GW + 2)),
            pl.BlockSpec((None, npast, GW), lambda i: (i, 0, 0)),
            pl.BlockSpec((None, npast, GW), lambda i: (i, 0, 0)),
            pl.BlockSpec((H, s, npast + s), lambda i: (0, 0, 0)),
            pl.BlockSpec((1, GW), lambda i: (0, 0)),
        ],
        out_specs=pl.BlockSpec((None, s, GW), lambda i: (i, 0, 0)),
        out_shape=jax.ShapeDtypeStruct((b, s, GW), BF16),
        compiler_params=_cparams("parallel"),
        name="band_sample",
    )(proj3, proj3, proj3, ck, cv, bias, gh)


def _cumsum_cols(x, lo_tri):
    return sum(_dot(lo_tri, part) for part in _split3(x))


def _cumsum_rows(x, up_tri):
    return sum(_dot(part, up_tri) for part in _split3(x))


def _tri_masks(l):
    r = _iota((l, l), 0)
    c = _iota((l, l), 1)
    return r >= c, r > c


def _mlstm_kernel(q_ref, k_ref, v_ref, o_ref, gc_ref, gr_ref, bc_ref, br_ref, gh_ref,
                  c0_ref, n0_ref, m0_ref, out_ref, c_ref, n_ref, m_ref, *, l, nck):
    @pl.when(pl.program_id(1) == 0)
    def _():
        c_ref[...] = c0_ref[...]
        n_ref[...] = n0_ref[...]
        m_ref[...] = m0_ref[...]

    incl, _ = _tri_masks(l)
    lo_tri = incl.astype(BF16)
    up_tri = (_iota((l, l), 0) <= _iota((l, l), 1)).astype(BF16)
    for ck in range(nck):
        rows = slice(ck * l, (ck + 1) * l)
        gcol = gc_ref[rows, :] + bc_ref[...]
        grow = gr_ref[ck] + br_ref[...]
        gcs = _cumsum_cols(_log_sigmoid(gcol), lo_tri)
        grs = _cumsum_rows(_log_sigmoid(grow), up_tri)
        for h in range(H):
            hs = slice(h * DH, (h + 1) * DH)
            ig_c = gcol[:, GATE_OFF + h:GATE_OFF + h + 1]
            g_c = gcs[:, GATE_OFF + H + h:GATE_OFF + H + h + 1]
            ig_r = grow[h:h + 1, :]
            g_r = grs[H + h:H + h + 1, :]
            cst = c_ref[h]
            nst = n_ref[h:h + 1, :]
            mst = m_ref[:, h:h + 1]
            q = q_ref[rows, hs]
            kf = k_ref[rows, hs] * (DH ** -0.5)
            qb, kb, vb = q.astype(BF16), kf.astype(BF16), v_ref[rows, hs].astype(BF16)

            lmat = jnp.where(incl, g_c - g_r + ig_r, NEG_INF)
            linter = g_c + mst
            mt = jnp.maximum(linter, jnp.max(lmat, axis=-1, keepdims=True))
            w = _dot_nt(qb, kb) * jnp.exp(lmat - mt)
            inter = jnp.exp(linter - mt)
            num = _dot(w.astype(BF16), vb) + inter * _dot(qb, cst.astype(BF16))
            den = jnp.sum(w, axis=-1, keepdims=True) + inter * jnp.sum(q * nst, axis=-1, keepdims=True)
            hout = num / jnp.maximum(jnp.abs(den), jnp.exp(-mt))
            g_last = g_c[l - 1:l, :]
            m_new = mt[l - 1:l, :]
            ws = jnp.exp(g_last - g_c + ig_c - m_new)
            dprev = jnp.exp(g_last + mst - m_new)
            kw = kf * ws
            c_ref[h] = dprev * cst + _dot_tn(kw.astype(BF16), vb)
            n_ref[h:h + 1, :] = dprev * nst + jnp.sum(kw, axis=0, keepdims=True)
            m_ref[:, h:h + 1] = m_new

            ob = hout * _sigmoid(o_ref[rows, hs])
            _head_norm_store(out_ref, rows, h, ob, gh_ref)


def _mlstm(proj3, gt3, bias_c, bias_r, gh, c0, n0, m0, l, nck):
    b, t, _ = proj3.shape
    steps = t // (l * nck)
    blk = l * nck
    col = lambda j: pl.BlockSpec((None, blk, GW), lambda bi, s: (bi, s, j))
    state = lambda shp: pl.BlockSpec((None,) + shp, lambda bi, s: (bi,) + (0,) * len(shp))
    return pl.pallas_call(
        functools.partial(_mlstm_kernel, l=l, nck=nck),
        grid=(b, steps),
        in_specs=[
            col(COL_B // GW), col(COL_B // GW + 1), col(COL_B // GW + 2), col(COL_B // GW + 3),
            pl.BlockSpec((None, blk, LANES), lambda bi, s: (bi, s, GATE_BLK)),
            pl.BlockSpec((None, nck, N_GATES, l), lambda bi, s: (bi, s, 0, 0)),
            pl.BlockSpec((1, LANES), lambda bi, s: (0, 0)),
            pl.BlockSpec((N_GATES, 1), lambda bi, s: (0, 0)),
            pl.BlockSpec((1, GW), lambda bi, s: (0, 0)),
            state((H, DH, DH)), state((H, DH)), state((1, H)),
        ],
        out_specs=[
            pl.BlockSpec((None, blk, GW), lambda bi, s: (bi, s, 0)),
            state((H, DH, DH)), state((H, DH)), state((1, H)),
        ],
        out_shape=[
            jax.ShapeDtypeStruct((b, t, GW), BF16),
            jax.ShapeDtypeStruct((b, H, DH, DH), F32),
            jax.ShapeDtypeStruct((b, H, DH), F32),
            jax.ShapeDtypeStruct((b, 1, H), F32),
        ],
        compiler_params=_cparams("parallel", "arbitrary"),
        name="mlstm",
    )(proj3, proj3, proj3, proj3, proj3, gt3, bias_c, bias_r, gh, c0, n0, m0)


def _unit_lower_inverse(nmat, l):
    eye = (_iota((l, l), 0) == _iota((l, l), 1)).astype(F32)
    p = eye - nmat
    q = _dot_hp(nmat, nmat)
    power = 2
    while power < l:
        p = p + _dot_hp(p, q)
        power *= 2
        if power < l:
            q = _dot_hp(q, q)
    return p


def _l2norm(x):
    return x * lax.rsqrt(jnp.sum(x * x, axis=-1, keepdims=True) + 1e-6)


def _gdn_kernel(x_ref, z_ref, gc_ref, gr_ref, hist_ref, cw_ref, ac_ref, ar_ref, dc_ref, dr_ref,
                gh_ref, s0_ref, out_ref, s_ref, carry_scr, *, l, nck):
    @pl.when(pl.program_id(1) == 0)
    def _():
        s_ref[...] = s0_ref[...]
        carry_scr[...] = hist_ref[...]

    blk = l * nck
    x = x_ref[...]
    ext = jnp.concatenate([carry_scr[...], x], axis=0)
    carry_scr[...] = x[blk - SUBLANES:, :]
    y = x * cw_ref[3:4, :]
    for j in range(1, 4):
        y = y + ext[SUBLANES - j:SUBLANES - j + blk, :] * cw_ref[3 - j:4 - j, :]
    y = y * _sigmoid(y)

    incl, strict = _tri_masks(l)
    lo_tri = incl.astype(BF16)
    up_tri = (_iota((l, l), 0) <= _iota((l, l), 1)).astype(BF16)
    for ck in range(nck):
        rows = slice(ck * l, (ck + 1) * l)
        gcol = gc_ref[rows, :]
        grow = gr_ref[ck]
        beta_cs = _sigmoid(gcol)
        dec_c = -jnp.exp(ac_ref[...]) * _softplus(gcol + dc_ref[...])
        dec_r = -jnp.exp(ar_ref[...]) * _softplus(grow + dr_ref[...])
        gcs = _cumsum_cols(dec_c, lo_tri)
        grs = _cumsum_rows(dec_r, up_tri)
        for h in range(H):
            hs = slice(h * DH, (h + 1) * DH)
            beta = beta_cs[:, GATE_OFF + 2 * H + h:GATE_OFF + 2 * H + h + 1]
            g_c = gcs[:, GATE_OFF + 3 * H + h:GATE_OFF + 3 * H + h + 1]
            g_r = grs[3 * H + h:3 * H + h + 1, :]
            sst = s_ref[h]
            q = _l2norm(y[rows, h * DH:(h + 1) * DH]) * (DH ** -0.5)
            k = _l2norm(y[rows, GW + h * DH:GW + (h + 1) * DH])
            v = y[rows, 2 * GW + h * DH:2 * GW + (h + 1) * DH]
            qb, kb = q.astype(BF16), k.astype(BF16)

            decay = jnp.exp(jnp.where(incl, g_c - g_r, NEG_INF))
            eg = jnp.exp(g_c)
            a_low = jnp.where(strict, beta * _dot_nt(kb, kb) * decay, 0.0)
            tinv = _unit_lower_inverse(a_low, l)
            rhs = jnp.concatenate([v * beta, k * (beta * eg)], axis=-1)
            sol = _dot_hp(tinv, rhs)
            u, w = sol[:, :DH], sol[:, DH:]
            sb = sst.astype(BF16)
            v_new = u - _dot(w.astype(BF16), sb)
            vnb = v_new.astype(BF16)
            o = _dot((q * eg).astype(BF16), sb) + _dot((_dot_nt(qb, kb) * decay).astype(BF16), vnb)
            g_last = g_c[l - 1:l, :]
            s_ref[h] = jnp.exp(g_last) * sst + _dot_tn((k * jnp.exp(g_last - g_c)).astype(BF16), vnb)

            zg = z_ref[rows, hs]
            gate = zg * _sigmoid(zg)
            g = gh_ref[:, hs]
            yo = o * lax.rsqrt(jnp.mean(o * o, axis=-1, keepdims=True) + EPS) * g * gate
            out_ref[rows, hs] = yo.astype(out_ref.dtype)


def _gdn(proj3, gt3, hist8, cw, a_c, a_r, dt_c, dt_r, gh, s0, l, nck):
    b, t, _ = proj3.shape
    blk = l * nck
    steps = t // blk
    state = lambda shp: pl.BlockSpec((None,) + shp, lambda bi, s: (bi,) + (0,) * len(shp))
    const = lambda shp: pl.BlockSpec(shp, lambda bi, s: (0,) * len(shp))
    return pl.pallas_call(
        functools.partial(_gdn_kernel, l=l, nck=nck),
        grid=(b, steps),
        in_specs=[
            pl.BlockSpec((None, blk, 3 * GW), lambda bi, s: (bi, s, COL_CX // (3 * GW))),
            pl.BlockSpec((None, blk, GW), lambda bi, s: (bi, s, COL_CZ // GW)),
            pl.BlockSpec((None, blk, LANES), lambda bi, s: (bi, s, GATE_BLK)),
            pl.BlockSpec((None, nck, N_GATES, l), lambda bi, s: (bi, s, 0, 0)),
            state((SUBLANES, 3 * GW)),
            const((4, 3 * GW)),
            const((1, LANES)), const((N_GATES, 1)), const((1, LANES)), const((N_GATES, 1)),
            const((1, GW)),
            state((H, DH, DH)),
        ],
        out_specs=[
            pl.BlockSpec((None, blk, GW), lambda bi, s: (bi, s, 0)),
            state((H, DH, DH)),
        ],
        out_shape=[
            jax.ShapeDtypeStruct((b, t, GW), BF16),
            jax.ShapeDtypeStruct((b, H, DH, DH), F32),
        ],
        scratch_shapes=[pltpu.VMEM((SUBLANES, 3 * GW), F32)],
        compiler_params=_cparams("parallel", "arbitrary"),
        name="gdn",
    )(proj3, proj3, proj3, gt3, hist8, cw, a_c, a_r, dt_c, dt_r, gh, s0)


HEAD_PAD = 128
DPAD = H * HEAD_PAD


def _tile_heads(t):
    return jnp.concatenate([t] * H, axis=-1)


def _dprep_kernel(tail_ref, gq_ref, gkv_ref, wq_ref, wqp_ref, qcos_ref, qsin_ref, ka_ref, kb_ref,
                  ckv_ref, kpe_ref, qc_ref):
    hq = _rms(tail_ref[:, :Q_LORA], gq_ref[...]).astype(BF16)
    qc = (_dot(hq, wq_ref[...]) * _tile_heads(qcos_ref[...])
          + _dot(hq, wqp_ref[...]) * _tile_heads(qsin_ref[...]))
    qc_ref[...] = qc.astype(BF16)
    ckv_ref[...] = _rms(tail_ref[:, Q_LORA:Q_LORA + KV_LORA], gkv_ref[...])
    kr = tail_ref[:, Q_LORA + KV_LORA:]
    kpe = kr * ka_ref[...] + pltpu.roll(kr, 64, axis=1) * kb_ref[...]
    kpe_ref[...] = kpe[:, :QK_ROPE]


def _dprep(proj, gq, gkv, wq, wqp, qcos, qsin, ka, kb, tm):
    n = proj.shape[0]
    row = lambda w: pl.BlockSpec((tm, w), lambda i: (i, 0))
    const = lambda a, b: pl.BlockSpec((a, b), lambda i: (0, 0))
    return pl.pallas_call(
        _dprep_kernel,
        grid=(n // tm,),
        in_specs=[
            pl.BlockSpec((tm, TAIL_W), lambda i: (i, COL_TAIL // TAIL_W)),
            const(1, Q_LORA), const(1, KV_LORA), const(Q_LORA, DPAD), const(Q_LORA, DPAD),
            row(HEAD_PAD), row(HEAD_PAD), row(LANES), row(LANES),
        ],
        out_specs=[row(KV_LORA), row(QK_ROPE), row(DPAD)],
        out_shape=[
            jax.ShapeDtypeStruct((n, KV_LORA), F32),
            jax.ShapeDtypeStruct((n, QK_ROPE), F32),
            jax.ShapeDtypeStruct((n, DPAD), BF16),
        ],
        compiler_params=_cparams("parallel"),
        name="dprep",
    )(proj, gq, gkv, wq, wqp, qcos, qsin, ka, kb)


def _kvup_kernel(ckv_ref, kpe_ref, wk_ref, wv_ref, pm_ref, kc_ref, vp_ref):
    c = ckv_ref[...].astype(BF16)
    kc_ref[...] = (_dot(c, wk_ref[...]) + _dot(kpe_ref[...].astype(BF16), pm_ref[...])).astype(BF16)
    vp_ref[...] = _dot(c, wv_ref[...]).astype(BF16)


def _kvup(ckv, kpe, wk, wv, pm, tm):
    m = ckv.shape[0]
    row = lambda w: pl.BlockSpec((tm, w), lambda i: (i, 0))
    const = lambda a, b: pl.BlockSpec((a, b), lambda i: (0, 0))
    return pl.pallas_call(
        _kvup_kernel,
        grid=(m // tm,),
        in_specs=[row(KV_LORA), row(QK_ROPE), const(KV_LORA, DPAD), const(KV_LORA, DPAD),
                  const(QK_ROPE, DPAD)],
        out_specs=[row(DPAD), row(DPAD)],
        out_shape=[jax.ShapeDtypeStruct((m, DPAD), BF16), jax.ShapeDtypeStruct((m, DPAD), BF16)],
        compiler_params=_cparams("parallel"),
        name="kvup",
    )(ckv, kpe, wk, wv, pm)


def _mla_finish(out_ref, gh_ref, h, acc, l):
    o = acc[:, :DH] / l
    rows = slice(0, o.shape[0])
    _head_norm_store(out_ref, rows, h, o, gh_ref)


def _mla_prompt_kernel(qi_ref, ki_ref, q_ref, k_ref, v_ref, gh_ref, out_ref, m_scr, l_scr, acc_scr,
                       *, blk):
    p = pl.program_id(0)
    q_i = qi_ref[p]
    k_i = ki_ref[p]

    @pl.when(k_i == 0)
    def _():
        m_scr[...] = jnp.full(m_scr.shape, NEG_INF, F32)
        l_scr[...] = jnp.zeros(l_scr.shape, F32)
        acc_scr[...] = jnp.zeros(acc_scr.shape, F32)

    def step(diag):
        if diag:
            allowed = (_iota((blk, blk), 1) // CHUNK) <= (_iota((blk, blk), 0) // CHUNK)
        for h in range(H):
            hs = slice(h * HEAD_PAD, (h + 1) * HEAD_PAD)
            s = _dot_nt(q_ref[:, hs], k_ref[:, hs]) * MLA_SCALE
            if diag:
                s = jnp.where(allowed, s, NEG_INF)
            m_prev = m_scr[h]
            m_new = jnp.maximum(m_prev, jnp.max(s, axis=-1, keepdims=True))
            alpha = jnp.exp(m_prev - m_new)
            pr = jnp.exp(s - m_new)
            l_new = alpha * l_scr[h] + jnp.sum(pr, axis=-1, keepdims=True)
            acc = alpha * acc_scr[h] + _dot(pr.astype(BF16), v_ref[:, hs])
            if diag:
                _mla_finish(out_ref, gh_ref, h, acc, l_new)
            else:
                m_scr[h] = m_new
                l_scr[h] = l_new
                acc_scr[h] = acc

    @pl.when(k_i < q_i)
    def _():
        step(False)

    @pl.when(k_i == q_i)
    def _():
        step(True)


def _mla_prompt(qc, kc, vp, gh, blk):
    t = qc.shape[0]
    nb = t // blk
    qi = jnp.asarray([i for i in range(nb) for _ in range(i + 1)], jnp.int32)
    ki = jnp.asarray([j for i in range(nb) for j in range(i + 1)], jnp.int32)
    grid_spec = pltpu.PrefetchScalarGridSpec(
        num_scalar_prefetch=2,
        grid=(int(qi.shape[0]),),
        in_specs=[
            pl.BlockSpec((blk, DPAD), lambda p, qi, ki: (qi[p], 0)),
            pl.BlockSpec((blk, DPAD), lambda p, qi, ki: (ki[p], 0)),
            pl.BlockSpec((blk, DPAD), lambda p, qi, ki: (ki[p], 0)),
            pl.BlockSpec((1, GW), lambda p, qi, ki: (0, 0)),
        ],
        out_specs=pl.BlockSpec((blk, GW), lambda p, qi, ki: (qi[p], 0)),
        scratch_shapes=[
            pltpu.VMEM((H, blk, 1), F32),
            pltpu.VMEM((H, blk, 1), F32),
            pltpu.VMEM((H, blk, HEAD_PAD), F32),
        ],
    )
    return pl.pallas_call(
        functools.partial(_mla_prompt_kernel, blk=blk),
        grid_spec=grid_spec,
        out_shape=jax.ShapeDtypeStruct((t, GW), BF16),
        compiler_params=_cparams("arbitrary"),
        name="mla_prompt",
    )(qi, ki, qc, kc, vp, gh)


def _mla_sample_kernel(q_ref, k_ref, v_ref, gh_ref, out_ref):
    for h in range(H):
        hs = slice(h * HEAD_PAD, (h + 1) * HEAD_PAD)
        s = _dot_nt(q_ref[:, hs], k_ref[:, hs]) * MLA_SCALE
        m = jnp.max(s, axis=-1, keepdims=True)
        pr = jnp.exp(s - m)
        l = jnp.sum(pr, axis=-1, keepdims=True)
        acc = _dot(pr.astype(BF16), v_ref[:, hs])
        _mla_finish(out_ref, gh_ref, h, acc, l)


def _mla_sample(qc3, kc3, vp3, gh):
    b, s, _ = qc3.shape
    nk = kc3.shape[1]
    return pl.pallas_call(
        _mla_sample_kernel,
        grid=(b,),
        in_specs=[
            pl.BlockSpec((None, s, DPAD), lambda i: (i, 0, 0)),
            pl.BlockSpec((None, nk, DPAD), lambda i: (i, 0, 0)),
            pl.BlockSpec((None, nk, DPAD), lambda i: (i, 0, 0)),
            pl.BlockSpec((1, GW), lambda i: (0, 0)),
        ],
        out_specs=pl.BlockSpec((None, s, GW), lambda i: (i, 0, 0)),
        out_shape=jax.ShapeDtypeStruct((b, s, GW), BF16),
        compiler_params=_cparams("parallel"),
        name="mla_sample",
    )(qc3, kc3, vp3, gh)


def _outproj_kernel(x_ref, a_ref, b_ref, c_ref, d_ref, w_ref, out_ref):
    acc = x_ref[...]
    for g, m_ref in enumerate((a_ref, b_ref, c_ref, d_ref)):
        acc = acc + _dot(m_ref[...], w_ref[g * GW:(g + 1) * GW, :])
    out_ref[...] = acc


def _outproj(x, ma, mb, mc, md, w, tm):
    n = x.shape[0]
    mix = pl.BlockSpec((tm, GW), lambda i: (i, 0))
    return pl.pallas_call(
        _outproj_kernel,
        grid=(n // tm,),
        in_specs=[pl.BlockSpec((tm, D_MODEL), lambda i: (i, 0)), mix, mix, mix, mix,
                  pl.BlockSpec((D_MODEL, D_MODEL), lambda i: (0, 0))],
        out_specs=pl.BlockSpec((tm, D_MODEL), lambda i: (i, 0)),
        out_shape=jax.ShapeDtypeStruct((n, D_MODEL), F32),
        compiler_params=_cparams("parallel"),
        name="outproj",
    )(x, ma, mb, mc, md, w)


def _ffn_kernel(*refs, seq_len, final_norm, nj):
    if seq_len is None:
        (x_ref, g_ref, wg_ref, wu_ref, cw_ref, wd_ref, gf_ref,
         out_ref, ga_ref, h_scr, acc_scr, carry_scr) = refs
    else:
        (x_ref, g_ref, wg_ref, wu_ref, cw_ref, wd_ref, gf_ref, h1_ref, h2_ref,
         out_ref, ga_ref, h_scr, acc_scr) = refs
    i = pl.program_id(0)
    j = pl.program_id(1)
    tm = x_ref.shape[0]

    @pl.when(j == 0)
    def _():
        h_scr[...] = _rms(x_ref[...], g_ref[...]).astype(BF16)
        acc_scr[...] = jnp.zeros(acc_scr.shape, F32)

    h = h_scr[...]
    ga = _dot(h, wg_ref[...])
    u = _dot(h, wu_ref[...])
    row = _iota(ga.shape, 0)
    r1 = pltpu.roll(ga, 1, axis=0)
    r2 = pltpu.roll(ga, 2, axis=0)
    if seq_len is None:
        @pl.when(i == 0)
        def _():
            carry_scr[j] = jnp.zeros(carry_scr.shape[1:], F32)

        c = carry_scr[j]
        c1 = c[SUBLANES - 1:SUBLANES, :]
        c2 = c[SUBLANES - 2:SUBLANES - 1, :]
        prev1 = jnp.where(row >= 1, r1, c1)
        prev2 = jnp.where(row >= 2, r2, jnp.where(row == 1, c1, c2))
        tail = ga[tm - SUBLANES:, :]
        carry_scr[j] = tail
        ga_ref[...] = tail
    else:
        t = row % seq_len
        prev1 = jnp.where(t >= 1, r1, h1_ref[...])
        prev2 = jnp.where(t >= 2, r2, h2_ref[...])
        ga_ref[...] = ga
    conv = prev2 * cw_ref[0:1, :] + prev1 * cw_ref[1:2, :] + ga * cw_ref[2:3, :]
    act = (conv * _sigmoid(conv) * u).astype(BF16)
    acc_scr[...] += _dot(act, wd_ref[...])

    @pl.when(j == nj - 1)
    def _():
        y = x_ref[...] + acc_scr[...]
        if final_norm:
            y = _rms(y, gf_ref[...])
        out_ref[...] = y


def _ffn(x, g, w_up, cw, w_down, gf, h1, h2, *, tm, tf, seq_len, final_norm):
    n = x.shape[0]
    nj = D_FF // tf
    ni = n // tm
    in_specs = [
        pl.BlockSpec((tm, D_MODEL), lambda i, j: (i, 0)),
        pl.BlockSpec((1, D_MODEL), lambda i, j: (0, 0)),
        pl.BlockSpec((D_MODEL, tf), lambda i, j: (0, j)),
        pl.BlockSpec((D_MODEL, tf), lambda i, j: (0, j + nj)),
        pl.BlockSpec((3, tf), lambda i, j: (0, j)),
        pl.BlockSpec((tf, D_MODEL), lambda i, j: (j, 0)),
        pl.BlockSpec((1, D_MODEL), lambda i, j: (0, 0)),
    ]
    args = [x, g, w_up, w_up, cw, w_down, gf]
    scratch = [pltpu.VMEM((tm, D_MODEL), BF16), pltpu.VMEM((tm, D_MODEL), F32)]
    if seq_len is None:
        ga_spec = pl.BlockSpec((None, SUBLANES, tf), lambda i, j: (i, 0, j))
        ga_shape = jax.ShapeDtypeStruct((ni, SUBLANES, D_FF), F32)
        scratch.append(pltpu.VMEM((nj, SUBLANES, tf), F32))
    else:
        in_specs += [pl.BlockSpec((tm, tf), lambda i, j: (i, j))] * 2
        args += [h1, h2]
        ga_spec = pl.BlockSpec((tm, tf), lambda i, j: (i, j))
        ga_shape = jax.ShapeDtypeStruct((n, D_FF), F32)
    return pl.pallas_call(
        functools.partial(_ffn_kernel, seq_len=seq_len, final_norm=final_norm, nj=nj),
        grid=(ni, nj),
        in_specs=in_specs,
        out_specs=[pl.BlockSpec((tm, D_MODEL), lambda i, j: (i, 0)), ga_spec],
        out_shape=[jax.ShapeDtypeStruct((n, D_MODEL), F32), ga_shape],
        scratch_shapes=scratch,
        compiler_params=_cparams("arbitrary", "arbitrary"),
        name="ffn",
    )(*args)


def _rope_tables(pos):
    half = QK_ROPE // 2
    inv = ROPE_THETA ** (-jnp.arange(half, dtype=F32) / half)
    ang = pos.astype(F32)[:, None] * inv[None, :]
    cos, sin = jnp.cos(ang), jnp.sin(ang)
    cos2 = jnp.concatenate([cos, cos], -1)
    sin2 = jnp.concatenate([-sin, sin], -1)
    n = pos.shape[0]
    z32 = jnp.zeros((n, 32), F32)
    qcos = jnp.concatenate([jnp.ones((n, QK_NOPE), F32), cos2, z32], -1)
    qsin = jnp.concatenate([jnp.zeros((n, QK_NOPE), F32), sin2, z32], -1)
    ka = jnp.concatenate([cos2, z32, sin2, z32], -1)
    kb = jnp.concatenate([sin2, z32, cos2, z32], -1)
    return qcos, qsin, ka, kb


def _rel_bias(table, n_past, n_q, n_k):
    d = n_past + jnp.arange(n_q)[:, None] - jnp.arange(n_k)[None, :]
    idx = jnp.clip(d, -REL_MAX, REL_MAX) + REL_MAX
    return table[:, idx]


def _swap_halves(w):
    half = w.shape[-1] // 2
    return jnp.concatenate([w[..., half:], w[..., :half]], -1)


def _layer_weights(lw):
    (g_mix, w_in, a_rel_bias, b_i_bias, b_f_bias, c_conv_w, c_a_log, c_dt_bias,
     d_g_q, d_w_q_up, d_g_kv, d_w_kv_up, g_head, w_out, g_ffn, w_up, f_conv_w, w_down) = lw
    o = 0
    cols = {}
    for name, size in (("a", 3 * GW), ("b", 4 * GW), ("bg", 2 * H), ("c", 3 * GW), ("cz", GW),
                       ("cg", 2 * H), ("dq", Q_LORA), ("dkv", KV_LORA), ("dkr", QK_ROPE)):
        cols[name] = w_in[:, o:o + size]
        o += size
    gates = jnp.concatenate([cols["bg"], cols["cg"]], -1)
    pad16 = jnp.zeros((D_MODEL, 16), F32)
    pad32 = jnp.zeros((D_MODEL, 32), F32)
    w_perm = jnp.concatenate([cols["c"], cols["a"], cols["dq"], cols["dkv"], cols["dkr"], gates, pad16,
                              _swap_halves(cols["dkr"]), pad32, cols["b"], cols["cz"]], -1)
    zc = lambda n: jnp.zeros((1, n), F32)
    zr = lambda n: jnp.zeros((n, 1), F32)
    bias_c = jnp.concatenate([zc(GATE_OFF), b_i_bias[None], b_f_bias[None], zc(LANES - GATE_OFF - 2 * H)], -1)
    bias_r = jnp.concatenate([b_i_bias[:, None], b_f_bias[:, None], zr(2 * H)], 0)
    alog_c = jnp.concatenate([zc(GATE_OFF + 3 * H), c_a_log[None], zc(LANES - GATE_OFF - 4 * H)], -1)
    alog_r = jnp.concatenate([zr(3 * H), c_a_log[:, None]], 0)
    dt_c = jnp.concatenate([zc(GATE_OFF + 3 * H), c_dt_bias[None], zc(LANES - GATE_OFF - 4 * H)], -1)
    dt_r = jnp.concatenate([zr(3 * H), c_dt_bias[:, None]], 0)

    wq = d_w_q_up.reshape(Q_LORA, H, QK_NOPE + QK_ROPE)
    z_h32 = jnp.zeros((Q_LORA, H, 32), F32)
    wq_full = jnp.concatenate([wq, z_h32], -1).reshape(Q_LORA, DPAD)
    wq_part = jnp.concatenate([jnp.zeros((Q_LORA, H, QK_NOPE), F32), _swap_halves(wq[..., QK_NOPE:]), z_h32],
                              -1).reshape(Q_LORA, DPAD)
    wkv = d_w_kv_up.reshape(KV_LORA, H, 2 * DH)
    z_h64 = jnp.zeros((KV_LORA, H, DH), F32)
    wk_full = jnp.concatenate([wkv[..., :DH], z_h64], -1).reshape(KV_LORA, DPAD)
    wv_full = jnp.concatenate([wkv[..., DH:], z_h64], -1).reshape(KV_LORA, DPAD)
    place = jnp.concatenate([jnp.zeros((QK_ROPE, QK_NOPE), F32), jnp.eye(QK_ROPE, dtype=F32),
                             jnp.zeros((QK_ROPE, 32), F32)], -1)
    pmat = jnp.concatenate([place] * H, -1)
    return dict(
        g_mix=g_mix[None], w_in=w_perm.astype(BF16), w_gt=gates.T.astype(BF16),
        table=a_rel_bias, bias_c=bias_c, bias_r=bias_r, alog_c=alog_c, alog_r=alog_r, dt_c=dt_c, dt_r=dt_r,
        c_conv_w=c_conv_w, g_q=d_g_q[None], g_kv=d_g_kv[None],
        wq=wq_full.astype(BF16), wqp=wq_part.astype(BF16), wk=wk_full.astype(BF16), wv=wv_full.astype(BF16),
        pmat=pmat.astype(BF16), g_head=g_head.reshape(4, 1, GW), w_out=w_out.astype(BF16),
        g_ffn=g_ffn[None], w_up=w_up.astype(BF16), f_conv_w=f_conv_w, w_down=w_down.astype(BF16))


def _gates_t3(gt, b, t, l):
    return gt.reshape(N_GATES, b, t // l, l).transpose(1, 2, 0, 3)


def _layer(x, offset, st, w, gf, final_norm, cfg):
    b, t, _ = x.shape
    n = b * t
    first = st is None
    x2 = x.reshape(n, D_MODEL)
    proj, gt = _inproj(x2, w["g_mix"], w["w_in"], w["w_gt"], cfg["tm"], cfg["tn"])
    proj3 = proj.reshape(b, t, PROJ_W)
    gh = w["g_head"]
    l = min(t, CHUNK)
    gt3 = _gates_t3(gt, b, t, l)

    new_ak = proj3[:, t - min(A_PAST, t):, COL_A + GW:COL_A + 2 * GW].reshape(b, -1, H, DH)
    new_av = proj3[:, t - min(A_PAST, t):, COL_A + 2 * GW:COL_A + 3 * GW].reshape(b, -1, H, DH)
    if first:
        bias = _rel_bias(w["table"], A_PAST, CHUNK, A_PAST + CHUNK)
        oa = _band_prompt(proj, bias, gh[0])
    else:
        npast = st[0].shape[1]
        bias = _rel_bias(w["table"], npast, t, npast + t)
        oa = _band_sample(proj3, st[0].reshape(b, npast, GW), st[1].reshape(b, npast, GW), bias, gh[0])
        oa = oa.reshape(n, GW)

    if first:
        c0 = jnp.zeros((b, H, DH, DH), F32)
        n0 = jnp.zeros((b, H, DH), F32)
        m0 = jnp.zeros((b, 1, H), F32)
    else:
        c0, n0, m0 = st[2], st[3], st[4][:, None, :]
    ob, new_bc, new_bn, new_bm = _mlstm(proj3, gt3, w["bias_c"], w["bias_r"], gh[1], c0, n0, m0,
                                        l, cfg["nck"])
    new_bm = new_bm[:, 0, :]

    if first:
        hist8 = jnp.zeros((b, SUBLANES, 3 * GW), F32)
        s0 = jnp.zeros((b, H, DH, DH), F32)
    else:
        hist8 = jnp.concatenate([jnp.zeros((b, SUBLANES - 3, 3 * GW), F32), st[6]], 1)
        s0 = st[5]
    oc, new_cs = _gdn(proj3, gt3, hist8, w["c_conv_w"], w["alog_c"], w["alog_r"], w["dt_c"], w["dt_r"],
                      gh[2], s0, l, cfg["nck"])
    new_cconv = proj3[:, t - 3:, COL_CX:COL_CX + 3 * GW]

    pos = jnp.arange(t, dtype=jnp.int32) + offset
    qcos, qsin, ka, kb = (jnp.tile(a, (b, 1)) for a in _rope_tables(pos))
    ckv, kpe, qc = _dprep(proj, w["g_q"], w["g_kv"], w["wq"], w["wqp"], qcos, qsin, ka, kb, cfg["tm_d"])
    if first:
        kc, vp = _kvup(ckv, kpe, w["wk"], w["wv"], w["pmat"], cfg["tm_kv"])
        od = _mla_prompt(qc, kc, vp, gh[3], cfg["mla_blk"])
    else:
        ckv_all = jnp.concatenate([st[7], ckv.reshape(b, t, KV_LORA)], 1)
        kpe_all = jnp.concatenate([st[8], kpe.reshape(b, t, QK_ROPE)], 1)
        nk = ckv_all.shape[1]
        kc, vp = _kvup(ckv_all.reshape(b * nk, KV_LORA), kpe_all.reshape(b * nk, QK_ROPE),
                       w["wk"], w["wv"], w["pmat"], nk)
        od = _mla_sample(qc.reshape(b, t, DPAD), kc.reshape(b, nk, DPAD), vp.reshape(b, nk, DPAD), gh[3])
        od = od.reshape(n, GW)

    x2 = _outproj(x2, oa, ob.reshape(n, GW), oc.reshape(n, GW), od, w["w_out"], cfg["tm"])

    if first:
        y, ga_tail = _ffn(x2, w["g_ffn"], w["w_up"], w["f_conv_w"], w["w_down"], gf, None, None,
                          tm=cfg["tm"], tf=cfg["tf"], seq_len=None, final_norm=final_norm)
        new_fconv = ga_tail[-1, SUBLANES - 2:, :][None]
    else:
        hist = st[9]
        zrow = jnp.zeros((b, t - 1, D_FF), F32)
        h1 = jnp.concatenate([hist[:, 1:2], zrow], 1).reshape(n, D_FF)
        h2 = jnp.concatenate([hist, zrow[:, 1:]], 1).reshape(n, D_FF)
        y, ga = _ffn(x2, w["g_ffn"], w["w_up"], w["f_conv_w"], w["w_down"], gf, h1, h2,
                     tm=cfg["tm"], tf=cfg["tf"], seq_len=t, final_norm=final_norm)
        new_fconv = ga.reshape(b, t, D_FF)[:, t - 2:]
    state = (new_ak, new_av, new_bc, new_bn, new_bm, new_cs, new_cconv,
             ckv.reshape(b, t, KV_LORA), kpe.reshape(b, t, QK_ROPE), new_fconv)
    return y.reshape(b, t, D_MODEL), state


def _config(b, t):
    n = b * t
    tm = min(n, 1024)
    return dict(tm=tm, tn=PROJ_W // 2, tf=256, nck=1 if t <= CHUNK else 2,
                tm_d=min(n, 1024), tm_kv=min(n, 2048), mla_blk=min(t, 512))


def kernel(x_prompt, x_sample, cache_a_k, cache_a_v, state_b_c, state_b_n, state_b_m, state_c_s, cache_c_conv, cache_d_ckv, cache_d_kpe, cache_ffn_conv, g_mix, w_in, a_rel_bias, b_i_bias, b_f_bias, c_conv_w, c_a_log, c_dt_bias, d_g_q, d_w_q_up, d_g_kv, d_w_kv_up, g_head, w_out, g_ffn, w_up, f_conv_w, w_down, g_final):
    layer_w = (g_mix, w_in, a_rel_bias, b_i_bias, b_f_bias, c_conv_w, c_a_log, c_dt_bias,
               d_g_q, d_w_q_up, d_g_kv, d_w_kv_up, g_head, w_out, g_ffn, w_up, f_conv_w, w_down)
    depth = g_mix.shape[0]
    past = cache_d_ckv.shape[2]
    xp, xs = x_prompt, x_sample
    cfg_p = _config(*x_prompt.shape[:2])
    cfg_s = _config(*x_sample.shape[:2])
    gf = g_final[None]
    new_p, new_s = [], []
    for l in range(depth):
        w = _layer_weights(tuple(a[l] for a in layer_w))
        last = l == depth - 1
        xp, sp_l = _layer(xp, 0, None, w, gf, last, cfg_p)
        st = (cache_a_k[l], cache_a_v[l], state_b_c[l], state_b_n[l], state_b_m[l],
              state_c_s[l], cache_c_conv[l], cache_d_ckv[l], cache_d_kpe[l], cache_ffn_conv[l])
        xs, ss_l = _layer(xs, past, st, w, gf, last, cfg_s)
        new_p.append(sp_l)
        new_s.append(ss_l)
    outs = [xp, xs]
    for i in range(10):
        outs.append(jnp.stack([s[i] for s in new_p]))
        outs.append(jnp.stack([s[i] for s in new_s]))
    return tuple(outs)
```

```python
import functools
import math

import jax
import jax.numpy as jnp
from jax import lax
from jax.experimental import pallas as pl
from jax.experimental.pallas import tpu as pltpu

F32 = jnp.float32
BF16 = jnp.bfloat16

D_MODEL = 1024
CHUNK = 64
H = 4
DH = 64
GW = H * DH
A_PAST = 8 * CHUNK
REL_MAX = 2 * CHUNK
Q_LORA = 256
KV_LORA = 128
QK_NOPE = 64
QK_ROPE = 32
ROPE_THETA = 10000.0
MLA_SCALE = (QK_NOPE + QK_ROPE) ** -0.5
D_FF = 2816
EPS = 1e-6

COL_CX = 0
COL_A = 3 * GW
COL_TAIL = 6 * GW
TAIL_W = 512
COL_B = COL_TAIL + TAIL_W
COL_CZ = COL_B + 4 * GW
PROJ_W = COL_CZ + GW
GATE_BLK = (COL_TAIL + 384) // 128
GATE_OFF = 32
N_GATES = 16

LANES = 128
SUBLANES = 8
VMEM_LIMIT = 56 * 1024 * 1024

NEG_INF = float("-inf")


def _cparams(*sem):
    return pltpu.CompilerParams(dimension_semantics=sem, vmem_limit_bytes=VMEM_LIMIT)


def _dot(a, b):
    return jnp.dot(a, b, preferred_element_type=F32)


def _dot_nt(a, b):
    return lax.dot_general(a, b, (((1,), (1,)), ((), ())), preferred_element_type=F32)


def _dot_tn(a, b):
    return lax.dot_general(a, b, (((0,), (0,)), ((), ())), preferred_element_type=F32)


def _split3(x):
    hi = x.astype(BF16)
    r1 = x - hi.astype(F32)
    mid = r1.astype(BF16)
    lo = (r1 - mid.astype(F32)).astype(BF16)
    return hi, mid, lo


def _rms(x, g):
    return x * lax.rsqrt(jnp.mean(x * x, axis=-1, keepdims=True) + EPS) * g


def _log_sigmoid(x):
    return jnp.minimum(x, 0.0) - jnp.log1p(jnp.exp(-jnp.abs(x)))


def _softplus(x):
    return jnp.maximum(x, 0.0) + jnp.log1p(jnp.exp(-jnp.abs(x)))


def _sigmoid(x):
    return 1.0 / (1.0 + jnp.exp(-x))


def _iota(shape, dim):
    return lax.broadcasted_iota(jnp.int32, shape, dim)


def _inproj_kernel(x_ref, g_ref, w_ref, wgt_ref, proj_ref, gt_ref, h_scr):
    @pl.when(pl.program_id(1) == 0)
    def _():
        h = _rms(x_ref[...], g_ref[...]).astype(BF16)
        h_scr[...] = h
        gt_ref[...] = _dot_nt(wgt_ref[...], h)

    proj_ref[...] = _dot(h_scr[...], w_ref[...])


def _inproj(x, g, w, wgt, tm, tn):
    n = x.shape[0]
    return pl.pallas_call(
        _inproj_kernel,
        grid=(n // tm, PROJ_W // tn),
        in_specs=[
            pl.BlockSpec((tm, D_MODEL), lambda i, j: (i, 0)),
            pl.BlockSpec((1, D_MODEL), lambda i, j: (0, 0)),
            pl.BlockSpec((D_MODEL, tn), lambda i, j: (0, j)),
            pl.BlockSpec((N_GATES, D_MODEL), lambda i, j: (0, 0)),
        ],
        out_specs=[
            pl.BlockSpec((tm, tn), lambda i, j: (i, j)),
            pl.BlockSpec((N_GATES, tm), lambda i, j: (0, i)),
        ],
        out_shape=[
            jax.ShapeDtypeStruct((n, PROJ_W), F32),
            jax.ShapeDtypeStruct((N_GATES, n), F32),
        ],
        scratch_shapes=[pltpu.VMEM((tm, D_MODEL), BF16)],
        compiler_params=_cparams("parallel", "arbitrary"),
        name="inproj",
    )(x, g, w, wgt)


def _head_norm_store(out_ref, rows, h, o, gh_ref):
    g = gh_ref[:, h * DH:(h + 1) * DH]
    y = o * lax.rsqrt(jnp.mean(o * o, axis=-1, keepdims=True) + EPS) * g
    out_ref[rows, h * DH:(h + 1) * DH] = y.astype(out_ref.dtype)


def _band_prompt_kernel(q_ref, kp_ref, kc_ref, vp_ref, vc_ref, bias_ref, gh_ref, out_ref, *, qb):
    first = pl.program_id(0) == 0
    band = A_PAST + CHUNK
    nchunks = qb // CHUNK
    jj = _iota((CHUNK, band), 1)

    def scores(h):
        hs = slice(h * DH, (h + 1) * DH)
        kcat = jnp.concatenate([kp_ref[:, hs], kc_ref[:, hs]], axis=0).astype(BF16)
        return [_dot_nt(q_ref[c * CHUNK:(c + 1) * CHUNK, hs].astype(BF16), kcat[c * CHUNK:c * CHUNK + band])
                for c in range(nchunks)]

    def attend(h, ss):
        hs = slice(h * DH, (h + 1) * DH)
        vcat = jnp.concatenate([vp_ref[:, hs], vc_ref[:, hs]], axis=0).astype(BF16)
        bias = bias_ref[h]
        masked = []
        for c, s in enumerate(ss):
            s = s * (DH ** -0.5) + bias
            if c < A_PAST // CHUNK:
                valid = jnp.logical_or(jnp.logical_not(first), jj >= A_PAST - c * CHUNK)
                s = jnp.where(valid, s, NEG_INF)
            masked.append(s)
        ms = [jnp.max(s, axis=-1, keepdims=True) for s in masked]
        ps = [jnp.exp(s - m) for s, m in zip(masked, ms)]
        ls = [jnp.sum(p, axis=-1, keepdims=True) for p in ps]
        os_ = [_dot(p.astype(BF16), vcat[c * CHUNK:c * CHUNK + band]) for c, p in enumerate(ps)]
        os_ = [o / l for o, l in zip(os_, ls)]
        msq = [jnp.mean(o * o, axis=-1, keepdims=True) for o in os_]
        for c, (o, m2) in enumerate(zip(os_, msq)):
            out_ref[c * CHUNK:(c + 1) * CHUNK, hs] = (o * lax.rsqrt(m2 + EPS) * gh_ref[:, hs]).astype(out_ref.dtype)

    pending = scores(0)
    for h in range(H):
        nxt = scores(h + 1) if h + 1 < H else None
        attend(h, pending)
        pending = nxt


def _band_prompt(proj, bias, gh, qb=A_PAST):
    t = proj.shape[0]
    assert qb == A_PAST and t % qb == 0
    prev = lambda i: jnp.maximum(i - 1, 0)
    cq = COL_A // GW
    return pl.pallas_call(
        functools.partial(_band_prompt_kernel, qb=qb),
        grid=(t // qb,),
        in_specs=[
            pl.BlockSpec((qb, GW), lambda i: (i, cq)),
            pl.BlockSpec((qb, GW), lambda i: (prev(i), cq + 1)),
            pl.BlockSpec((qb, GW), lambda i: (i, cq + 1)),
            pl.BlockSpec((qb, GW), lambda i: (prev(i), cq + 2)),
            pl.BlockSpec((qb, GW), lambda i: (i, cq + 2)),
            pl.BlockSpec((H, CHUNK, A_PAST + CHUNK), lambda i: (0, 0, 0)),
            pl.BlockSpec((1, GW), lambda i: (0, 0)),
        ],
        out_specs=pl.BlockSpec((qb, GW), lambda i: (i, 0)),
        out_shape=jax.ShapeDtypeStruct((t, GW), BF16),
        compiler_params=_cparams("parallel"),
        name="band_prompt",
    )(proj, proj, proj, proj, proj, bias, gh)


def _band_sample_kernel(q_ref, k_ref, v_ref, ck_ref, cv_ref, bias_ref, gh_ref, out_ref):
    npast = ck_ref.shape[0]
    rows = slice(0, q_ref.shape[0])
    for h in range(H):
        hs = slice(h * DH, (h + 1) * DH)
        q = q_ref[:, hs].astype(BF16)
        s1 = _dot_nt(q, ck_ref[:, hs].astype(BF16)) * (DH ** -0.5) + bias_ref[h, :, :npast]
        s2 = _dot_nt(q, k_ref[:, hs].astype(BF16)) * (DH ** -0.5) + bias_ref[h, :, npast:]
        m = jnp.maximum(jnp.max(s1, axis=-1, keepdims=True), jnp.max(s2, axis=-1, keepdims=True))
        p1 = jnp.exp(s1 - m)
        p2 = jnp.exp(s2 - m)
        l = jnp.sum(p1, axis=-1, keepdims=True) + jnp.sum(p2, axis=-1, keepdims=True)
        o = (_dot(p1.astype(BF16), cv_ref[:, hs].astype(BF16))
             + _dot(p2.astype(BF16), v_ref[:, hs].astype(BF16))) / l
        _head_norm_store(out_ref, rows, h, o, gh_ref)


def _band_sample(proj3, ck, cv, bias, gh):
    b, s, _ = proj3.shape
    npast = ck.shape[1]
    return pl.pallas_call(
        _band_sample_kernel,
        grid=(b,),
        in_specs=[
            pl.BlockSpec((None, s, GW), lambda i: (i, 0, COL_A // GW)),
            pl.BlockSpec((None, s, GW), lambda i: (i, 0, COL_A // GW + 1)),
            pl.BlockSpec((None, s, GW), lambda i: (i, 0, COL_A // GW + 2)),
            pl.BlockSpec((None, npast, GW), lambda i: (i, 0, 0)),
            pl.BlockSpec((None, npast, GW), lambda i: (i, 0, 0)),
            pl.BlockSpec((H, s, npast + s), lambda i: (0, 0, 0)),
            pl.BlockSpec((1, GW), lambda i: (0, 0)),
        ],
        out_specs=pl.BlockSpec((None, s, GW), lambda i: (i, 0, 0)),
        out_shape=jax.ShapeDtypeStruct((b, s, GW), BF16),
        compiler_params=_cparams("parallel"),
        name="band_sample",
    )(proj3, proj3, proj3, ck, cv, bias, gh)


def _cumsum_cols(x, lo_tri):
    return sum(_dot(lo_tri, part) for part in _split3(x))


def _cumsum_rows(x, up_tri):
    return sum(_dot(part, up_tri) for part in _split3(x))


def _tri_masks(l):
    r = _iota((l, l), 0)
    c = _iota((l, l), 1)
    return r >= c, r > c


def _mlstm_kernel(q_ref, k_ref, v_ref, o_ref, gc_ref, gr_ref, bc_ref, br_ref, gh_ref,
                  c0_ref, n0_ref, m0_ref, out_ref, c_ref, n_ref, m_ref, *, l, nck):
    @pl.when(pl.program_id(1) == 0)
    def _():
        c_ref[...] = c0_ref[...]
        n_ref[...] = n0_ref[...]
        m_ref[...] = m0_ref[...]

    incl, _ = _tri_masks(l)
    lo_tri = incl.astype(BF16)
    up_tri = (_iota((l, l), 0) <= _iota((l, l), 1)).astype(BF16)

    probs = []
    for ck in range(nck):
        rows = slice(ck * l, (ck + 1) * l)
        gcol = gc_ref[rows, :] + bc_ref[...]
        grow = gr_ref[ck] + br_ref[...]
        gcs = _cumsum_cols(_log_sigmoid(gcol), lo_tri)
        grs = _cumsum_rows(_log_sigmoid(grow), up_tri)
        for h in range(H):
            hs = slice(h * DH, (h + 1) * DH)
            ig_c = gcol[:, GATE_OFF + h:GATE_OFF + h + 1]
            g_c = gcs[:, GATE_OFF + H + h:GATE_OFF + H + h + 1]
            ig_r = grow[h:h + 1, :]
            g_r = grs[H + h:H + h + 1, :]
            q = q_ref[rows, hs]
            kf = k_ref[rows, hs] * (DH ** -0.5)
            lmat = jnp.where(incl, g_c - g_r + ig_r, NEG_INF)
            probs.append(dict(
                rows=rows, h=h, q=q, kf=kf, qb=q.astype(BF16), kb=kf.astype(BF16),
                vb=v_ref[rows, hs].astype(BF16), lmat=lmat, lmax=jnp.max(lmat, axis=-1, keepdims=True),
                g_c=g_c, ig_c=ig_c, g_last=g_c[l - 1:l, :]))
    qks = [_dot_nt(p["qb"], p["kb"]) for p in probs]

    ms = [m_ref[:, h:h + 1] for h in range(H)]
    for p in probs:
        m_old = ms[p["h"]]
        p["linter"] = p["g_c"] + m_old
        p["mt"] = jnp.maximum(p["linter"], p["lmax"])
        m_new = p["mt"][l - 1:l, :]
        p["dprev"] = jnp.exp(p["g_last"] + m_old - m_new)
        p["kw"] = p["kf"] * jnp.exp(p["g_last"] - p["g_c"] + p["ig_c"] - m_new)
        ms[p["h"]] = m_new
    ws_ = [qk * jnp.exp(p["lmat"] - p["mt"]) for p, qk in zip(probs, qks)]
    wvs = [_dot(w.astype(BF16), p["vb"]) for p, w in zip(probs, ws_)]
    upds = [_dot_tn(p["kw"].astype(BF16), p["vb"]) for p in probs]

    cs = [c_ref[h] for h in range(H)]
    ns = [n_ref[h:h + 1, :] for h in range(H)]
    qcs, qns = [], []
    for p, upd in zip(probs, upds):
        h = p["h"]
        qcs.append(_dot(p["qb"], cs[h].astype(BF16)))
        qns.append(jnp.sum(p["q"] * ns[h], axis=-1, keepdims=True))
        cs[h] = p["dprev"] * cs[h] + upd
        ns[h] = p["dprev"] * ns[h] + jnp.sum(p["kw"], axis=0, keepdims=True)
    for h in range(H):
        c_ref[h] = cs[h]
        n_ref[h:h + 1, :] = ns[h]
        m_ref[:, h:h + 1] = ms[h]

    wsums = [jnp.sum(w, axis=-1, keepdims=True) for w in ws_]
    obs = []
    for p, wsum, wv, qc, qn in zip(probs, wsums, wvs, qcs, qns):
        hs = slice(p["h"] * DH, (p["h"] + 1) * DH)
        inter = jnp.exp(p["linter"] - p["mt"])
        den = wsum + inter * qn
        hout = (wv + inter * qc) / jnp.maximum(jnp.abs(den), jnp.exp(-p["mt"]))
        obs.append(hout * _sigmoid(o_ref[p["rows"], hs]))
    msq = [jnp.mean(ob * ob, axis=-1, keepdims=True) for ob in obs]
    for p, ob, ms_ in zip(probs, obs, msq):
        hs = slice(p["h"] * DH, (p["h"] + 1) * DH)
        out_ref[p["rows"], hs] = (ob * lax.rsqrt(ms_ + EPS) * gh_ref[:, hs]).astype(out_ref.dtype)


def _mlstm(proj3, gt3, bias_c, bias_r, gh, c0, n0, m0, l, nck):
    b, t, _ = proj3.shape
    steps = t // (l * nck)
    blk = l * nck
    col = lambda j: pl.BlockSpec((None, blk, GW), lambda bi, s: (bi, s, j))
    state = lambda shp: pl.BlockSpec((None,) + shp, lambda bi, s: (bi,) + (0,) * len(shp))
    return pl.pallas_call(
        functools.partial(_mlstm_kernel, l=l, nck=nck),
        grid=(b, steps),
        in_specs=[
            col(COL_B // GW), col(COL_B // GW + 1), col(COL_B // GW + 2), col(COL_B // GW + 3),
            pl.BlockSpec((None, blk, LANES), lambda bi, s: (bi, s, GATE_BLK)),
            pl.BlockSpec((None, nck, N_GATES, l), lambda bi, s: (bi, s, 0, 0)),
            pl.BlockSpec((1, LANES), lambda bi, s: (0, 0)),
            pl.BlockSpec((N_GATES, 1), lambda bi, s: (0, 0)),
            pl.BlockSpec((1, GW), lambda bi, s: (0, 0)),
            state((H, DH, DH)), state((H, DH)), state((1, H)),
        ],
        out_specs=[
            pl.BlockSpec((None, blk, GW), lambda bi, s: (bi, s, 0)),
            state((H, DH, DH)), state((H, DH)), state((1, H)),
        ],
        out_shape=[
            jax.ShapeDtypeStruct((b, t, GW), BF16),
            jax.ShapeDtypeStruct((b, H, DH, DH), F32),
            jax.ShapeDtypeStruct((b, H, DH), F32),
            jax.ShapeDtypeStruct((b, 1, H), F32),
        ],
        compiler_params=_cparams("parallel", "arbitrary"),
        name="mlstm",
    )(proj3, proj3, proj3, proj3, proj3, gt3, bias_c, bias_r, gh, c0, n0, m0)


def _split2(x):
    hi = x.astype(BF16)
    lo = (x - hi.astype(F32)).astype(BF16)
    return hi, lo


def _dot_sp(a, b):
    return _dot(a[0], b[0]) + (_dot(a[0], b[1]) + _dot(a[1], b[0]))


def _unit_lower_inverses(nmats, l):
    eye = (_iota((l, l), 0) == _iota((l, l), 1)).astype(F32)
    ps = [eye - n for n in nmats]
    qs = [_split2(n) for n in nmats]
    qs = [_split2(_dot_sp(q, q)) for q in qs]
    power = 2
    while power < l:
        ps = [p + _dot_sp(_split2(p), q) for p, q in zip(ps, qs)]
        power *= 2
        if power < l:
            qs = [_split2(_dot_sp(q, q)) for q in qs]
    return ps


def _l2norm(x):
    return x * lax.rsqrt(jnp.sum(x * x, axis=-1, keepdims=True) + 1e-6)


def _gdn_kernel(x_ref, z_ref, gc_ref, gr_ref, hist_ref, cw_ref, ac_ref, ar_ref, dc_ref, dr_ref,
                gh_ref, s0_ref, out_ref, s_ref, carry_scr, *, l, nck):
    @pl.when(pl.program_id(1) == 0)
    def _():
        s_ref[...] = s0_ref[...]
        carry_scr[...] = hist_ref[...]

    blk = l * nck
    x = x_ref[...]
    ext = jnp.concatenate([carry_scr[...], x], axis=0)
    carry_scr[...] = x[blk - SUBLANES:, :]
    y = x * cw_ref[3:4, :]
    for j in range(1, 4):
        y = y + ext[SUBLANES - j:SUBLANES - j + blk, :] * cw_ref[3 - j:4 - j, :]
    y = y * _sigmoid(y)

    incl, strict = _tri_masks(l)
    lo_tri = incl.astype(BF16)
    up_tri = (_iota((l, l), 0) <= _iota((l, l), 1)).astype(BF16)

    qraw = [y[ck * l:(ck + 1) * l, h * DH:(h + 1) * DH] for ck in range(nck) for h in range(H)]
    kraw = [y[ck * l:(ck + 1) * l, GW + h * DH:GW + (h + 1) * DH] for ck in range(nck) for h in range(H)]
    vraw = [y[ck * l:(ck + 1) * l, 2 * GW + h * DH:2 * GW + (h + 1) * DH] for ck in range(nck) for h in range(H)]
    qnorm = [_l2norm(a) * (DH ** -0.5) for a in qraw]
    knorm = [_l2norm(a) for a in kraw]
    probs = []
    for ck in range(nck):
        rows = slice(ck * l, (ck + 1) * l)
        gcol = gc_ref[rows, :]
        grow = gr_ref[ck]
        beta_cs = _sigmoid(gcol)
        dec_c = -jnp.exp(ac_ref[...]) * _softplus(gcol + dc_ref[...])
        dec_r = -jnp.exp(ar_ref[...]) * _softplus(grow + dr_ref[...])
        gcs = _cumsum_cols(dec_c, lo_tri)
        grs = _cumsum_rows(dec_r, up_tri)
        for h in range(H):
            beta = beta_cs[:, GATE_OFF + 2 * H + h:GATE_OFF + 2 * H + h + 1]
            g_c = gcs[:, GATE_OFF + 3 * H + h:GATE_OFF + 3 * H + h + 1]
            g_r = grs[3 * H + h:3 * H + h + 1, :]
            q, k, v = qnorm[ck * H + h], knorm[ck * H + h], vraw[ck * H + h]
            decay = jnp.exp(jnp.where(incl, g_c - g_r, NEG_INF))
            eg = jnp.exp(g_c)
            g_last = g_c[l - 1:l, :]
            probs.append(dict(
                rows=rows, h=h, qb=q.astype(BF16), kb=k.astype(BF16), beta=beta, decay=decay,
                rhs=jnp.concatenate([v * beta, k * (beta * eg)], axis=-1),
                qeg=(q * eg).astype(BF16), kdec=(k * jnp.exp(g_last - g_c)).astype(BF16),
                sdec=jnp.exp(g_last)))
    kks = [_dot_nt(p["kb"], p["kb"]) for p in probs]
    qks = [_dot_nt(p["qb"], p["kb"]) for p in probs]
    a_lows = [jnp.where(strict, p["beta"] * kk * p["decay"], 0.0) for p, kk in zip(probs, kks)]
    attns = [(qk * p["decay"]).astype(BF16) for p, qk in zip(probs, qks)]
    tinvs = _unit_lower_inverses(a_lows, l)
    sols = [_dot_sp(_split2(t), _split2(p["rhs"])) for t, p in zip(tinvs, probs)]

    states = [s_ref[h] for h in range(H)]
    for ck in range(nck):
        ps = probs[ck * H:(ck + 1) * H]
        ss = sols[ck * H:(ck + 1) * H]
        at = attns[ck * H:(ck + 1) * H]
        sbs = [s.astype(BF16) for s in states]
        wss = [_dot(sol[:, DH:].astype(BF16), sb) for sol, sb in zip(ss, sbs)]
        qss = [_dot(p["qeg"], sb) for p, sb in zip(ps, sbs)]
        vnbs = [(sol[:, :DH] - ws).astype(BF16) for sol, ws in zip(ss, wss)]
        os_ = [qs + _dot(a, vnb) for qs, a, vnb in zip(qss, at, vnbs)]
        states = [p["sdec"] * s + _dot_tn(p["kdec"], vnb) for p, s, vnb in zip(ps, states, vnbs)]
        for p, o in zip(ps, os_):
            hs = slice(p["h"] * DH, (p["h"] + 1) * DH)
            zg = z_ref[p["rows"], hs]
            yo = (o * lax.rsqrt(jnp.mean(o * o, axis=-1, keepdims=True) + EPS) * gh_ref[:, hs]
                  * (zg * _sigmoid(zg)))
            out_ref[p["rows"], hs] = yo.astype(out_ref.dtype)
    for h in range(H):
        s_ref[h] = states[h]


def _gdn(proj3, gt3, hist8, cw, a_c, a_r, dt_c, dt_r, gh, s0, l, nck):
    b, t, _ = proj3.shape
    blk = l * nck
    steps = t // blk
    state = lambda shp: pl.BlockSpec((None,) + shp, lambda bi, s: (bi,) + (0,) * len(shp))
    const = lambda shp: pl.BlockSpec(shp, lambda bi, s: (0,) * len(shp))
    return pl.pallas_call(
        functools.partial(_gdn_kernel, l=l, nck=nck),
        grid=(b, steps),
        in_specs=[
            pl.BlockSpec((None, blk, 3 * GW), lambda bi, s: (bi, s, COL_CX // (3 * GW))),
            pl.BlockSpec((None, blk, GW), lambda bi, s: (bi, s, COL_CZ // GW)),
            pl.BlockSpec((None, blk, LANES), lambda bi, s: (bi, s, GATE_BLK)),
            pl.BlockSpec((None, nck, N_GATES, l), lambda bi, s: (bi, s, 0, 0)),
            state((SUBLANES, 3 * GW)),
            const((4, 3 * GW)),
            const((1, LANES)), const((N_GATES, 1)), const((1, LANES)), const((N_GATES, 1)),
            const((1, GW)),
            state((H, DH, DH)),
        ],
        out_specs=[
            pl.BlockSpec((None, blk, GW), lambda bi, s: (bi, s, 0)),
            state((H, DH, DH)),
        ],
        out_shape=[
            jax.ShapeDtypeStruct((b, t, GW), BF16),
            jax.ShapeDtypeStruct((b, H, DH, DH), F32),
        ],
        scratch_shapes=[pltpu.VMEM((SUBLANES, 3 * GW), F32)],
        compiler_params=_cparams("parallel", "arbitrary"),
        name="gdn",
    )(proj3, proj3, proj3, gt3, hist8, cw, a_c, a_r, dt_c, dt_r, gh, s0)


HEAD_PAD = 128
DPAD = H * HEAD_PAD


def _tile_heads(t):
    return jnp.concatenate([t] * H, axis=-1)


def _dprep_kernel(tail_ref, gq_ref, gkv_ref, wq_ref, wqp_ref, qcos_ref, qsin_ref, ka_ref, kb_ref,
                  ckv_ref, kpe_ref, qc_ref):
    hq = _rms(tail_ref[:, :Q_LORA], gq_ref[...]).astype(BF16)
    qc = (_dot(hq, wq_ref[...]) * _tile_heads(qcos_ref[...])
          + _dot(hq, wqp_ref[...]) * _tile_heads(qsin_ref[...]))
    qc_ref[...] = qc.astype(BF16)
    ckv_ref[...] = _rms(tail_ref[:, Q_LORA:Q_LORA + KV_LORA], gkv_ref[...])
    kr = tail_ref[:, Q_LORA + KV_LORA:]
    kpe = kr * ka_ref[...] + pltpu.roll(kr, 64, axis=1) * kb_ref[...]
    kpe_ref[...] = kpe[:, :QK_ROPE]


def _dprep(proj, gq, gkv, wq, wqp, qcos, qsin, ka, kb, tm):
    n = proj.shape[0]
    row = lambda w: pl.BlockSpec((tm, w), lambda i: (i, 0))
    const = lambda a, b: pl.BlockSpec((a, b), lambda i: (0, 0))
    return pl.pallas_call(
        _dprep_kernel,
        grid=(n // tm,),
        in_specs=[
            pl.BlockSpec((tm, TAIL_W), lambda i: (i, COL_TAIL // TAIL_W)),
            const(1, Q_LORA), const(1, KV_LORA), const(Q_LORA, DPAD), const(Q_LORA, DPAD),
            row(HEAD_PAD), row(HEAD_PAD), row(LANES), row(LANES),
        ],
        out_specs=[row(KV_LORA), row(QK_ROPE), row(DPAD)],
        out_shape=[
            jax.ShapeDtypeStruct((n, KV_LORA), F32),
            jax.ShapeDtypeStruct((n, QK_ROPE), F32),
            jax.ShapeDtypeStruct((n, DPAD), BF16),
        ],
        compiler_params=_cparams("parallel"),
        name="dprep",
    )(proj, gq, gkv, wq, wqp, qcos, qsin, ka, kb)


def _kvup_kernel(ckv_ref, kpe_ref, wk_ref, wv_ref, pm_ref, one_ref, kc_ref, vp_ref):
    c = ckv_ref[...].astype(BF16)
    kc_ref[...] = (_dot(c, wk_ref[...]) + _dot(kpe_ref[...].astype(BF16), pm_ref[...])).astype(BF16)
    vp_ref[...] = (_dot(c, wv_ref[...]) + one_ref[...]).astype(BF16)


def _kvup(ckv, kpe, wk, wv, pm, ones, tm):
    m = ckv.shape[0]
    row = lambda w: pl.BlockSpec((tm, w), lambda i: (i, 0))
    const = lambda a, b: pl.BlockSpec((a, b), lambda i: (0, 0))
    return pl.pallas_call(
        _kvup_kernel,
        grid=(m // tm,),
        in_specs=[row(KV_LORA), row(QK_ROPE), const(KV_LORA, DPAD), const(KV_LORA, DPAD),
                  const(QK_ROPE, DPAD), const(1, DPAD)],
        out_specs=[row(DPAD), row(DPAD)],
        out_shape=[jax.ShapeDtypeStruct((m, DPAD), BF16), jax.ShapeDtypeStruct((m, DPAD), BF16)],
        compiler_params=_cparams("parallel"),
        name="kvup",
    )(ckv, kpe, wk, wv, pm, ones)


def _mla_finish(out_ref, gh_ref, h, acc):
    o = acc[:, :DH] / acc[:, DH:DH + 1]
    rows = slice(0, o.shape[0])
    _head_norm_store(out_ref, rows, h, o, gh_ref)


LOG2E = 1.4426950408889634


def _lane_tile_max(s):
    mx = s[:, :LANES]
    for t in range(1, s.shape[1] // LANES):
        mx = jnp.maximum(mx, s[:, t * LANES:(t + 1) * LANES])
    return mx


def _mla_prompt_kernel(qi_ref, ki_ref, q_ref, k_ref, v_ref, gh_ref, out_ref, m_scr, acc_scr, *, blk):
    p = pl.program_id(0)
    q_i = qi_ref[p]
    k_i = ki_ref[p]

    @pl.when(k_i == 0)
    def _():
        m_scr[...] = jnp.full(m_scr.shape, NEG_INF, F32)
        acc_scr[...] = jnp.zeros(acc_scr.shape, F32)

    def step(diag):
        if diag:
            allowed = (_iota((blk, blk), 1) // CHUNK) <= (_iota((blk, blk), 0) // CHUNK)

        def scores(h):
            hs = slice(h * HEAD_PAD, (h + 1) * HEAD_PAD)
            return _dot_nt(q_ref[:, hs], k_ref[:, hs])

        def update(h, s):
            hs = slice(h * HEAD_PAD, (h + 1) * HEAD_PAD)
            s = s * (MLA_SCALE * LOG2E)
            if diag:
                s = jnp.where(allowed, s, NEG_INF)
            m_prev = m_scr[h]
            m_new = jnp.maximum(m_prev, jnp.max(_lane_tile_max(s), axis=-1, keepdims=True))
            alpha = jnp.exp2(m_prev - m_new)
            pr = jnp.exp2(s - jnp.concatenate([m_new] * (blk // LANES), axis=-1))
            acc = alpha * acc_scr[h] + _dot(pr.astype(BF16), v_ref[:, hs])
            if diag:
                _mla_finish(out_ref, gh_ref, h, acc)
            else:
                m_scr[h] = m_new
                acc_scr[h] = acc

        pending = scores(0)
        for h in range(H):
            nxt = scores(h + 1) if h + 1 < H else None
            update(h, pending)
            pending = nxt

    @pl.when(k_i < q_i)
    def _():
        step(False)

    @pl.when(k_i == q_i)
    def _():
        step(True)


def _mla_prompt(qc, kc, vp, gh, blk):
    t = qc.shape[0]
    nb = t // blk
    qi = jnp.asarray([i for i in range(nb) for _ in range(i + 1)], jnp.int32)
    ki = jnp.asarray([j for i in range(nb) for j in range(i + 1)], jnp.int32)
    grid_spec = pltpu.PrefetchScalarGridSpec(
        num_scalar_prefetch=2,
        grid=(int(qi.shape[0]),),
        in_specs=[
            pl.BlockSpec((blk, DPAD), lambda p, qi, ki: (qi[p], 0)),
            pl.BlockSpec((blk, DPAD), lambda p, qi, ki: (ki[p], 0)),
            pl.BlockSpec((blk, DPAD), lambda p, qi, ki: (ki[p], 0)),
            pl.BlockSpec((1, GW), lambda p, qi, ki: (0, 0)),
        ],
        out_specs=pl.BlockSpec((blk, GW), lambda p, qi, ki: (qi[p], 0)),
        scratch_shapes=[
            pltpu.VMEM((H, blk, LANES), F32),
            pltpu.VMEM((H, blk, HEAD_PAD), F32),
        ],
    )
    return pl.pallas_call(
        functools.partial(_mla_prompt_kernel, blk=blk),
        grid_spec=grid_spec,
        out_shape=jax.ShapeDtypeStruct((t, GW), BF16),
        compiler_params=_cparams("arbitrary"),
        name="mla_prompt",
    )(qi, ki, qc, kc, vp, gh)


def _mla_sample_kernel(q_ref, k_ref, v_ref, gh_ref, out_ref):
    for h in range(H):
        hs = slice(h * HEAD_PAD, (h + 1) * HEAD_PAD)
        s = _dot_nt(q_ref[:, hs], k_ref[:, hs]) * MLA_SCALE
        m = jnp.max(s, axis=-1, keepdims=True)
        pr = jnp.exp(s - m)
        acc = _dot(pr.astype(BF16), v_ref[:, hs])
        _mla_finish(out_ref, gh_ref, h, acc)


def _mla_sample(qc3, kc3, vp3, gh):
    b, s, _ = qc3.shape
    nk = kc3.shape[1]
    return pl.pallas_call(
        _mla_sample_kernel,
        grid=(b,),
        in_specs=[
            pl.BlockSpec((None, s, DPAD), lambda i: (i, 0, 0)),
            pl.BlockSpec((None, nk, DPAD), lambda i: (i, 0, 0)),
            pl.BlockSpec((None, nk, DPAD), lambda i: (i, 0, 0)),
            pl.BlockSpec((1, GW), lambda i: (0, 0)),
        ],
        out_specs=pl.BlockSpec((None, s, GW), lambda i: (i, 0, 0)),
        out_shape=jax.ShapeDtypeStruct((b, s, GW), BF16),
        compiler_params=_cparams("parallel"),
        name="mla_sample",
    )(qc3, kc3, vp3, gh)


def _outproj_kernel(x_ref, a_ref, b_ref, c_ref, d_ref, w_ref, out_ref):
    acc = x_ref[...]
    for g, m_ref in enumerate((a_ref, b_ref, c_ref, d_ref)):
        acc = acc + _dot(m_ref[...], w_ref[g * GW:(g + 1) * GW, :])
    out_ref[...] = acc


def _outproj(x, ma, mb, mc, md, w, tm):
    n = x.shape[0]
    mix = pl.BlockSpec((tm, GW), lambda i: (i, 0))
    return pl.pallas_call(
        _outproj_kernel,
        grid=(n // tm,),
        in_specs=[pl.BlockSpec((tm, D_MODEL), lambda i: (i, 0)), mix, mix, mix, mix,
                  pl.BlockSpec((D_MODEL, D_MODEL), lambda i: (0, 0))],
        out_specs=pl.BlockSpec((tm, D_MODEL), lambda i: (i, 0)),
        out_shape=jax.ShapeDtypeStruct((n, D_MODEL), F32),
        compiler_params=_cparams("parallel"),
        name="outproj",
    )(x, ma, mb, mc, md, w)


def _ffn_kernel(*refs, seq_len, final_norm, nj):
    if seq_len is None:
        (x_ref, g_ref, wg_ref, wu_ref, cw_ref, wd_ref, gf_ref,
         out_ref, ga_ref, h_scr, acc_scr, carry_scr) = refs
    else:
        (x_ref, g_ref, wg_ref, wu_ref, cw_ref, wd_ref, gf_ref, h1_ref, h2_ref,
         out_ref, ga_ref, h_scr, acc_scr) = refs
    i = pl.program_id(0)
    j = pl.program_id(1)
    tm = x_ref.shape[0]

    @pl.when(j == 0)
    def _():
        h_scr[...] = _rms(x_ref[...], g_ref[...]).astype(BF16)
        acc_scr[...] = jnp.zeros(acc_scr.shape, F32)

    h = h_scr[...]
    ga = _dot(h, wg_ref[...])
    u = _dot(h, wu_ref[...])
    row = _iota(ga.shape, 0)
    r1 = pltpu.roll(ga, 1, axis=0)
    r2 = pltpu.roll(ga, 2, axis=0)
    if seq_len is None:
        @pl.when(i == 0)
        def _():
            carry_scr[j] = jnp.zeros(carry_scr.shape[1:], F32)

        c = carry_scr[j]
        c1 = c[SUBLANES - 1:SUBLANES, :]
        c2 = c[SUBLANES - 2:SUBLANES - 1, :]
        prev1 = jnp.where(row >= 1, r1, c1)
        prev2 = jnp.where(row >= 2, r2, jnp.where(row == 1, c1, c2))
        tail = ga[tm - SUBLANES:, :]
        carry_scr[j] = tail
        ga_ref[...] = tail
    else:
        t = row % seq_len
        prev1 = jnp.where(t >= 1, r1, h1_ref[...])
        prev2 = jnp.where(t >= 2, r2, h2_ref[...])
        ga_ref[...] = ga
    conv = prev2 * cw_ref[0:1, :] + prev1 * cw_ref[1:2, :] + ga * cw_ref[2:3, :]
    act = (conv * _sigmoid(conv) * u).astype(BF16)
    acc_scr[...] += _dot(act, wd_ref[...])

    @pl.when(j == nj - 1)
    def _():
        y = x_ref[...] + acc_scr[...]
        if final_norm:
            y = _rms(y, gf_ref[...])
        out_ref[...] = y


def _ffn(x, g, w_up, cw, w_down, gf, h1, h2, *, tm, tf, seq_len, final_norm):
    n = x.shape[0]
    nj = D_FF // tf
    ni = n // tm
    in_specs = [
        pl.BlockSpec((tm, D_MODEL), lambda i, j: (i, 0)),
        pl.BlockSpec((1, D_MODEL), lambda i, j: (0, 0)),
        pl.BlockSpec((D_MODEL, tf), lambda i, j: (0, j)),
        pl.BlockSpec((D_MODEL, tf), lambda i, j: (0, j + nj)),
        pl.BlockSpec((3, tf), lambda i, j: (0, j)),
        pl.BlockSpec((tf, D_MODEL), lambda i, j: (j, 0)),
        pl.BlockSpec((1, D_MODEL), lambda i, j: (0, 0)),
    ]
    args = [x, g, w_up, w_up, cw, w_down, gf]
    scratch = [pltpu.VMEM((tm, D_MODEL), BF16), pltpu.VMEM((tm, D_MODEL), F32)]
    if seq_len is None:
        ga_spec = pl.BlockSpec((None, SUBLANES, tf), lambda i, j: (i, 0, j))
        ga_shape = jax.ShapeDtypeStruct((ni, SUBLANES, D_FF), F32)
        scratch.append(pltpu.VMEM((nj, SUBLANES, tf), F32))
    else:
        in_specs += [pl.BlockSpec((tm, tf), lambda i, j: (i, j))] * 2
        args += [h1, h2]
        ga_spec = pl.BlockSpec((tm, tf), lambda i, j: (i, j))
        ga_shape = jax.ShapeDtypeStruct((n, D_FF), F32)
    return pl.pallas_call(
        functools.partial(_ffn_kernel, seq_len=seq_len, final_norm=final_norm, nj=nj),
        grid=(ni, nj),
        in_specs=in_specs,
        out_specs=[pl.BlockSpec((tm, D_MODEL), lambda i, j: (i, 0)), ga_spec],
        out_shape=[jax.ShapeDtypeStruct((n, D_MODEL), F32), ga_shape],
        scratch_shapes=scratch,
        compiler_params=_cparams("arbitrary", "arbitrary"),
        name="ffn",
    )(*args)


def _rope_tables(pos):
    half = QK_ROPE // 2
    inv = ROPE_THETA ** (-jnp.arange(half, dtype=F32) / half)
    ang = pos.astype(F32)[:, None] * inv[None, :]
    cos, sin = jnp.cos(ang), jnp.sin(ang)
    cos2 = jnp.concatenate([cos, cos], -1)
    sin2 = jnp.concatenate([-sin, sin], -1)
    n = pos.shape[0]
    z32 = jnp.zeros((n, 32), F32)
    qcos = jnp.concatenate([jnp.ones((n, QK_NOPE), F32), cos2, z32], -1)
    qsin = jnp.concatenate([jnp.zeros((n, QK_NOPE), F32), sin2, z32], -1)
    ka = jnp.concatenate([cos2, z32, sin2, z32], -1)
    kb = jnp.concatenate([sin2, z32, cos2, z32], -1)
    return qcos, qsin, ka, kb


def _rel_bias(table, n_past, n_q, n_k):
    dmax = n_past + n_q - 1
    dmin = n_past - n_k + 1
    diag = table[:, jnp.clip(jnp.arange(dmax, dmin - 1, -1), -REL_MAX, REL_MAX) + REL_MAX]
    return jnp.stack([diag[:, n_q - 1 - i:n_q - 1 - i + n_k] for i in range(n_q)], axis=1)


def _swap_halves(w):
    half = w.shape[-1] // 2
    return jnp.concatenate([w[..., half:], w[..., :half]], -1)


def _layer_weights(lw):
    (g_mix, w_in, a_rel_bias, b_i_bias, b_f_bias, c_conv_w, c_a_log, c_dt_bias,
     d_g_q, d_w_q_up, d_g_kv, d_w_kv_up, g_head, w_out, g_ffn, w_up, f_conv_w, w_down) = lw
    o = 0
    cols = {}
    for name, size in (("a", 3 * GW), ("b", 4 * GW), ("bg", 2 * H), ("c", 3 * GW), ("cz", GW),
                       ("cg", 2 * H), ("dq", Q_LORA), ("dkv", KV_LORA), ("dkr", QK_ROPE)):
        cols[name] = w_in[:, o:o + size]
        o += size
    gates = jnp.concatenate([cols["bg"], cols["cg"]], -1)
    pad16 = jnp.zeros((D_MODEL, 16), F32)
    pad32 = jnp.zeros((D_MODEL, 32), F32)
    w_perm = jnp.concatenate([cols["c"], cols["a"], cols["dq"], cols["dkv"], cols["dkr"], gates, pad16,
                              _swap_halves(cols["dkr"]), pad32, cols["b"], cols["cz"]], -1)
    zc = lambda n: jnp.zeros((1, n), F32)
    zr = lambda n: jnp.zeros((n, 1), F32)
    bias_c = jnp.concatenate([zc(GATE_OFF), b_i_bias[None], b_f_bias[None], zc(LANES - GATE_OFF - 2 * H)], -1)
    bias_r = jnp.concatenate([b_i_bias[:, None], b_f_bias[:, None], zr(2 * H)], 0)
    alog_c = jnp.concatenate([zc(GATE_OFF + 3 * H), c_a_log[None], zc(LANES - GATE_OFF - 4 * H)], -1)
    alog_r = jnp.concatenate([zr(3 * H), c_a_log[:, None]], 0)
    dt_c = jnp.concatenate([zc(GATE_OFF + 3 * H), c_dt_bias[None], zc(LANES - GATE_OFF - 4 * H)], -1)
    dt_r = jnp.concatenate([zr(3 * H), c_dt_bias[:, None]], 0)

    wq = d_w_q_up.reshape(Q_LORA, H, QK_NOPE + QK_ROPE)
    z_h32 = jnp.zeros((Q_LORA, H, 32), F32)
    wq_full = jnp.concatenate([wq, z_h32], -1).reshape(Q_LORA, DPAD)
    wq_part = jnp.concatenate([jnp.zeros((Q_LORA, H, QK_NOPE), F32), _swap_halves(wq[..., QK_NOPE:]), z_h32],
                              -1).reshape(Q_LORA, DPAD)
    wkv = d_w_kv_up.reshape(KV_LORA, H, 2 * DH)
    z_h64 = jnp.zeros((KV_LORA, H, DH), F32)
    wk_full = jnp.concatenate([wkv[..., :DH], z_h64], -1).reshape(KV_LORA, DPAD)
    wv_full = jnp.concatenate([wkv[..., DH:], z_h64], -1).reshape(KV_LORA, DPAD)
    place = jnp.concatenate([jnp.zeros((QK_ROPE, QK_NOPE), F32), jnp.eye(QK_ROPE, dtype=F32),
                             jnp.zeros((QK_ROPE, 32), F32)], -1)
    pmat = jnp.concatenate([place] * H, -1)
    return dict(
        g_mix=g_mix[None], w_in=w_perm.astype(BF16), w_gt=gates.T.astype(BF16),
        table=a_rel_bias, bias_c=bias_c, bias_r=bias_r, alog_c=alog_c, alog_r=alog_r, dt_c=dt_c, dt_r=dt_r,
        c_conv_w=c_conv_w, g_q=d_g_q[None], g_kv=d_g_kv[None],
        wq=wq_full.astype(BF16), wqp=wq_part.astype(BF16), wk=wk_full.astype(BF16), wv=wv_full.astype(BF16),
        pmat=pmat.astype(BF16),
        vones=(jnp.arange(DPAD) % HEAD_PAD == DH).astype(F32)[None],
g_head=g_head.reshape(4, 1, GW), w_out=w_out.astype(BF16),
        g_ffn=g_ffn[None], w_up=w_up.astype(BF16), f_conv_w=f_conv_w, w_down=w_down.astype(BF16))


def _gates_t3(gt, b, t, l):
    return gt.reshape(N_GATES, b, t // l, l).transpose(1, 2, 0, 3)


def _layer(x, offset, st, w, gf, final_norm, cfg):
    b, t, _ = x.shape
    n = b * t
    first = st is None
    x2 = x.reshape(n, D_MODEL)
    proj, gt = _inproj(x2, w["g_mix"], w["w_in"], w["w_gt"], cfg["tm"], cfg["tn"])
    proj3 = proj.reshape(b, t, PROJ_W)
    gh = w["g_head"]
    l = min(t, CHUNK)
    gt3 = _gates_t3(gt, b, t, l)

    new_ak = proj3[:, t - min(A_PAST, t):, COL_A + GW:COL_A + 2 * GW].reshape(b, -1, H, DH)
    new_av = proj3[:, t - min(A_PAST, t):, COL_A + 2 * GW:COL_A + 3 * GW].reshape(b, -1, H, DH)
    if first:
        bias = _rel_bias(w["table"], A_PAST, CHUNK, A_PAST + CHUNK)
        oa = _band_prompt(proj, bias, gh[0])
    else:
        npast = st[0].shape[1]
        bias = _rel_bias(w["table"], npast, t, npast + t)
        oa = _band_sample(proj3, st[0].reshape(b, npast, GW), st[1].reshape(b, npast, GW), bias, gh[0])
        oa = oa.reshape(n, GW)

    if first:
        c0 = jnp.zeros((b, H, DH, DH), F32)
        n0 = jnp.zeros((b, H, DH), F32)
        m0 = jnp.zeros((b, 1, H), F32)
    else:
        c0, n0, m0 = st[2], st[3], st[4][:, None, :]
    ob, new_bc, new_bn, new_bm = _mlstm(proj3, gt3, w["bias_c"], w["bias_r"], gh[1], c0, n0, m0,
                                        l, cfg["nck"])
    new_bm = new_bm[:, 0, :]

    if first:
        hist8 = jnp.zeros((b, SUBLANES, 3 * GW), F32)
        s0 = jnp.zeros((b, H, DH, DH), F32)
    else:
        hist8 = jnp.concatenate([jnp.zeros((b, SUBLANES - 3, 3 * GW), F32), st[6]], 1)
        s0 = st[5]
    oc, new_cs = _gdn(proj3, gt3, hist8, w["c_conv_w"], w["alog_c"], w["alog_r"], w["dt_c"], w["dt_r"],
                      gh[2], s0, l, cfg["nck"])
    new_cconv = proj3[:, t - 3:, COL_CX:COL_CX + 3 * GW]

    pos = jnp.arange(t, dtype=jnp.int32) + offset
    qcos, qsin, ka, kb = (jnp.tile(a, (b, 1)) for a in _rope_tables(pos))
    ckv, kpe, qc = _dprep(proj, w["g_q"], w["g_kv"], w["wq"], w["wqp"], qcos, qsin, ka, kb, cfg["tm_d"])
    if first:
        kc, vp = _kvup(ckv, kpe, w["wk"], w["wv"], w["pmat"], w["vones"], cfg["tm_kv"])
        od = _mla_prompt(qc, kc, vp, gh[3], cfg["mla_blk"])
    else:
        ckv_all = jnp.concatenate([st[7], ckv.reshape(b, t, KV_LORA)], 1)
        kpe_all = jnp.concatenate([st[8], kpe.reshape(b, t, QK_ROPE)], 1)
        nk = ckv_all.shape[1]
        kc, vp = _kvup(ckv_all.reshape(b * nk, KV_LORA), kpe_all.reshape(b * nk, QK_ROPE),
                       w["wk"], w["wv"], w["pmat"], w["vones"], nk)
        od = _mla_sample(qc.reshape(b, t, DPAD), kc.reshape(b, nk, DPAD), vp.reshape(b, nk, DPAD), gh[3])
        od = od.reshape(n, GW)

    x2 = _outproj(x2, oa, ob.reshape(n, GW), oc.reshape(n, GW), od, w["w_out"], cfg["tm"])

    if first:
        y, ga_tail = _ffn(x2, w["g_ffn"], w["w_up"], w["f_conv_w"], w["w_down"], gf, None, None,
                          tm=cfg["tm"], tf=cfg["tf"], seq_len=None, final_norm=final_norm)
        new_fconv = ga_tail[-1, SUBLANES - 2:, :][None]
    else:
        hist = st[9]
        zrow = jnp.zeros((b, t - 1, D_FF), F32)
        h1 = jnp.concatenate([hist[:, 1:2], zrow], 1).reshape(n, D_FF)
        h2 = jnp.concatenate([hist, zrow[:, 1:]], 1).reshape(n, D_FF)
        y, ga = _ffn(x2, w["g_ffn"], w["w_up"], w["f_conv_w"], w["w_down"], gf, h1, h2,
                     tm=cfg["tm"], tf=cfg["tf"], seq_len=t, final_norm=final_norm)
        new_fconv = ga.reshape(b, t, D_FF)[:, t - 2:]
    state = (new_ak, new_av, new_bc, new_bn, new_bm, new_cs, new_cconv,
             ckv.reshape(b, t, KV_LORA), kpe.reshape(b, t, QK_ROPE), new_fconv)
    return y.reshape(b, t, D_MODEL), state


def _config(b, t):
    n = b * t
    tm = min(n, 1024)
    return dict(tm=tm, tn=PROJ_W // 2, tf=256, nck=1 if t <= CHUNK else 4,
                tm_d=min(n, 1024), tm_kv=min(n, 2048), mla_blk=min(t, 512))


def kernel(x_prompt, x_sample, cache_a_k, cache_a_v, state_b_c, state_b_n, state_b_m, state_c_s, cache_c_conv, cache_d_ckv, cache_d_kpe, cache_ffn_conv, g_mix, w_in, a_rel_bias, b_i_bias, b_f_bias, c_conv_w, c_a_log, c_dt_bias, d_g_q, d_w_q_up, d_g_kv, d_w_kv_up, g_head, w_out, g_ffn, w_up, f_conv_w, w_down, g_final):
    layer_w = (g_mix, w_in, a_rel_bias, b_i_bias, b_f_bias, c_conv_w, c_a_log, c_dt_bias,
               d_g_q, d_w_q_up, d_g_kv, d_w_kv_up, g_head, w_out, g_ffn, w_up, f_conv_w, w_down)
    depth = g_mix.shape[0]
    past = cache_d_ckv.shape[2]
    xp, xs = x_prompt, x_sample
    cfg_p = _config(*x_prompt.shape[:2])
    cfg_s = _config(*x_sample.shape[:2])
    gf = g_final[None]
    new_p, new_s = [], []
    for l in range(depth):
        w = _layer_weights(tuple(a[l] for a in layer_w))
        last = l == depth - 1
        xp, sp_l = _layer(xp, 0, None, w, gf, last, cfg_p)
        st = (cache_a_k[l], cache_a_v[l], state_b_c[l], state_b_n[l], state_b_m[l],
              state_c_s[l], cache_c_conv[l], cache_d_ckv[l], cache_d_kpe[l], cache_ffn_conv[l])
        xs, ss_l = _layer(xs, past, st, w, gf, last, cfg_s)
        new_p.append(sp_l)
        new_s.append(ss_l)
    outs = [xp, xs]
    for i in range(10):
        outs.append(jnp.stack([s[i] for s in new_p]))
        outs.append(jnp.stack([s[i] for s in new_s]))
    return tuple(outs)
```

```python
import functools
import math

import jax
import jax.numpy as jnp
from jax import lax
from jax.experimental import pallas as pl
from jax.experimental.pallas import tpu as pltpu

F32 = jnp.float32
BF16 = jnp.bfloat16

D_MODEL = 1024
CHUNK = 64
H = 4
DH = 64
GW = H * DH
A_PAST = 8 * CHUNK
REL_MAX = 2 * CHUNK
Q_LORA = 256
KV_LORA = 128
QK_NOPE = 64
QK_ROPE = 32
ROPE_THETA = 10000.0
MLA_SCALE = (QK_NOPE + QK_ROPE) ** -0.5
D_FF = 2816
EPS = 1e-6

COL_CX = 0
COL_A = 3 * GW
COL_TAIL = 6 * GW
TAIL_W = 512
COL_B = COL_TAIL + TAIL_W
COL_CZ = COL_B + 4 * GW
PROJ_W = COL_CZ + GW
GATE_BLK = (COL_TAIL + 384) // 128
GATE_OFF = 32
N_GATES = 16

LANES = 128
SUBLANES = 8
VMEM_LIMIT = 56 * 1024 * 1024

NEG_INF = float("-inf")


def _cparams(*sem):
    return pltpu.CompilerParams(dimension_semantics=sem, vmem_limit_bytes=VMEM_LIMIT)


def _dot(a, b):
    return jnp.dot(a, b, preferred_element_type=F32)


def _dot_nt(a, b):
    return lax.dot_general(a, b, (((1,), (1,)), ((), ())), preferred_element_type=F32)


def _dot_tn(a, b):
    return lax.dot_general(a, b, (((0,), (0,)), ((), ())), preferred_element_type=F32)


def _split3(x):
    hi = x.astype(BF16)
    r1 = x - hi.astype(F32)
    mid = r1.astype(BF16)
    lo = (r1 - mid.astype(F32)).astype(BF16)
    return hi, mid, lo


def _rms(x, g):
    return x * lax.rsqrt(jnp.mean(x * x, axis=-1, keepdims=True) + EPS) * g


def _log_sigmoid(x):
    return jnp.minimum(x, 0.0) - jnp.log1p(jnp.exp(-jnp.abs(x)))


def _softplus(x):
    return jnp.maximum(x, 0.0) + jnp.log1p(jnp.exp(-jnp.abs(x)))


def _sigmoid(x):
    return 1.0 / (1.0 + jnp.exp(-x))


def _iota(shape, dim):
    return lax.broadcasted_iota(jnp.int32, shape, dim)


def _inproj_kernel(x_ref, g_ref, w_ref, wgt_ref, proj_ref, gt_ref, h_scr):
    @pl.when(pl.program_id(1) == 0)
    def _():
        h = _rms(x_ref[...], g_ref[...]).astype(BF16)
        h_scr[...] = h
        gt_ref[...] = _dot_nt(wgt_ref[...], h)

    proj_ref[...] = _dot(h_scr[...], w_ref[...])


def _inproj(x, g, w, wgt, tm, tn):
    n = x.shape[0]
    return pl.pallas_call(
        _inproj_kernel,
        grid=(n // tm, PROJ_W // tn),
        in_specs=[
            pl.BlockSpec((tm, D_MODEL), lambda i, j: (i, 0)),
            pl.BlockSpec((1, D_MODEL), lambda i, j: (0, 0)),
            pl.BlockSpec((D_MODEL, tn), lambda i, j: (0, j)),
            pl.BlockSpec((N_GATES, D_MODEL), lambda i, j: (0, 0)),
        ],
        out_specs=[
            pl.BlockSpec((tm, tn), lambda i, j: (i, j)),
            pl.BlockSpec((N_GATES, tm), lambda i, j: (0, i)),
        ],
        out_shape=[
            jax.ShapeDtypeStruct((n, PROJ_W), F32),
            jax.ShapeDtypeStruct((N_GATES, n), F32),
        ],
        scratch_shapes=[pltpu.VMEM((tm, D_MODEL), BF16)],
        compiler_params=_cparams("parallel", "arbitrary"),
        name="inproj",
    )(x, g, w, wgt)


def _head_norm_store(out_ref, rows, h, o, gh_ref):
    g = gh_ref[:, h * DH:(h + 1) * DH]
    y = o * lax.rsqrt(jnp.mean(o * o, axis=-1, keepdims=True) + EPS) * g
    out_ref[rows, h * DH:(h + 1) * DH] = y.astype(out_ref.dtype)


def _band_prompt_kernel(q_ref, kp_ref, kc_ref, vp_ref, vc_ref, bias_ref, gh_ref, out_ref, *, qb):
    first = pl.program_id(0) == 0
    band = A_PAST + CHUNK
    nchunks = qb // CHUNK
    jj = _iota((CHUNK, band), 1)

    def scores(h):
        hs = slice(h * DH, (h + 1) * DH)
        kcat = jnp.concatenate([kp_ref[:, hs], kc_ref[:, hs]], axis=0).astype(BF16)
        return [_dot_nt(q_ref[c * CHUNK:(c + 1) * CHUNK, hs].astype(BF16), kcat[c * CHUNK:c * CHUNK + band])
                for c in range(nchunks)]

    def attend(h, ss):
        hs = slice(h * DH, (h + 1) * DH)
        vcat = jnp.concatenate([vp_ref[:, hs], vc_ref[:, hs]], axis=0).astype(BF16)
        bias = bias_ref[h]
        masked = []
        for c, s in enumerate(ss):
            s = s * (DH ** -0.5) + bias
            if c < A_PAST // CHUNK:
                valid = jnp.logical_or(jnp.logical_not(first), jj >= A_PAST - c * CHUNK)
                s = jnp.where(valid, s, NEG_INF)
            masked.append(s)
        ms = [jnp.max(s, axis=-1, keepdims=True) for s in masked]
        ps = [jnp.exp(s - m) for s, m in zip(masked, ms)]
        ls = [jnp.sum(p, axis=-1, keepdims=True) for p in ps]
        os_ = [_dot(p.astype(BF16), vcat[c * CHUNK:c * CHUNK + band]) for c, p in enumerate(ps)]
        os_ = [o / l for o, l in zip(os_, ls)]
        msq = [jnp.mean(o * o, axis=-1, keepdims=True) for o in os_]
        for c, (o, m2) in enumerate(zip(os_, msq)):
            out_ref[c * CHUNK:(c + 1) * CHUNK, hs] = (o * lax.rsqrt(m2 + EPS) * gh_ref[:, hs]).astype(out_ref.dtype)

    pending = scores(0)
    for h in range(H):
        nxt = scores(h + 1) if h + 1 < H else None
        attend(h, pending)
        pending = nxt


def _band_prompt(proj, bias, gh, qb=A_PAST):
    t = proj.shape[0]
    assert qb == A_PAST and t % qb == 0
    prev = lambda i: jnp.maximum(i - 1, 0)
    cq = COL_A // GW
    return pl.pallas_call(
        functools.partial(_band_prompt_kernel, qb=qb),
        grid=(t // qb,),
        in_specs=[
            pl.BlockSpec((qb, GW), lambda i: (i, cq)),
            pl.BlockSpec((qb, GW), lambda i: (prev(i), cq + 1)),
            pl.BlockSpec((qb, GW), lambda i: (i, cq + 1)),
            pl.BlockSpec((qb, GW), lambda i: (prev(i), cq + 2)),
            pl.BlockSpec((qb, GW), lambda i: (i, cq + 2)),
            pl.BlockSpec((H, CHUNK, A_PAST + CHUNK), lambda i: (0, 0, 0)),
            pl.BlockSpec((1, GW), lambda i: (0, 0)),
        ],
        out_specs=pl.BlockSpec((qb, GW), lambda i: (i, 0)),
        out_shape=jax.ShapeDtypeStruct((t, GW), BF16),
        compiler_params=_cparams("parallel"),
        name="band_prompt",
    )(proj, proj, proj, proj, proj, bias, gh)


def _band_sample_kernel(q_ref, k_ref, v_ref, ck_ref, cv_ref, bias_ref, gh_ref, out_ref):
    npast = ck_ref.shape[0]
    rows = slice(0, q_ref.shape[0])
    for h in range(H):
        hs = slice(h * DH, (h + 1) * DH)
        q = q_ref[:, hs].astype(BF16)
        s1 = _dot_nt(q, ck_ref[:, hs].astype(BF16)) * (DH ** -0.5) + bias_ref[h, :, :npast]
        s2 = _dot_nt(q, k_ref[:, hs].astype(BF16)) * (DH ** -0.5) + bias_ref[h, :, npast:]
        m = jnp.maximum(jnp.max(s1, axis=-1, keepdims=True), jnp.max(s2, axis=-1, keepdims=True))
        p1 = jnp.exp(s1 - m)
        p2 = jnp.exp(s2 - m)
        l = jnp.sum(p1, axis=-1, keepdims=True) + jnp.sum(p2, axis=-1, keepdims=True)
        o = (_dot(p1.astype(BF16), cv_ref[:, hs].astype(BF16))
             + _dot(p2.astype(BF16), v_ref[:, hs].astype(BF16))) / l
        _head_norm_store(out_ref, rows, h, o, gh_ref)


def _band_sample(proj3, ck, cv, bias, gh):
    b, s, _ = proj3.shape
    npast = ck.shape[1]
    return pl.pallas_call(
        _band_sample_kernel,
        grid=(b,),
        in_specs=[
            pl.BlockSpec((None, s, GW), lambda i: (i, 0, COL_A // GW)),
            pl.BlockSpec((None, s, GW), lambda i: (i, 0, COL_A // GW + 1)),
            pl.BlockSpec((None, s, GW), lambda i: (i, 0, COL_A // GW + 2)),
            pl.BlockSpec((None, npast, GW), lambda i: (i, 0, 0)),
            pl.BlockSpec((None, npast, GW), lambda i: (i, 0, 0)),
            pl.BlockSpec((H, s, npast + s), lambda i: (0, 0, 0)),
            pl.BlockSpec((1, GW), lambda i: (0, 0)),
        ],
        out_specs=pl.BlockSpec((None, s, GW), lambda i: (i, 0, 0)),
        out_shape=jax.ShapeDtypeStruct((b, s, GW), BF16),
        compiler_params=_cparams("parallel"),
        name="band_sample",
    )(proj3, proj3, proj3, ck, cv, bias, gh)


def _cumsum_cols(x, lo_tri):
    return sum(_dot(lo_tri, part) for part in _split3(x))


def _cumsum_rows(x, up_tri):
    return sum(_dot(part, up_tri) for part in _split3(x))


def _tri_masks(l):
    r = _iota((l, l), 0)
    c = _iota((l, l), 1)
    return r >= c, r > c


def _mlstm_kernel(q_ref, k_ref, v_ref, o_ref, gc_ref, gr_ref, bc_ref, br_ref, gh_ref,
                  c0_ref, n0_ref, m0_ref, out_ref, c_ref, n_ref, m_ref, *, l, nck):
    @pl.when(pl.program_id(1) == 0)
    def _():
        c_ref[...] = c0_ref[...]
        n_ref[...] = n0_ref[...]
        m_ref[...] = m0_ref[...]

    incl, _ = _tri_masks(l)
    lo_tri = incl.astype(BF16)
    up_tri = (_iota((l, l), 0) <= _iota((l, l), 1)).astype(BF16)

    probs = []
    for ck in range(nck):
        rows = slice(ck * l, (ck + 1) * l)
        gcol = gc_ref[rows, :] + bc_ref[...]
        grow = gr_ref[ck] + br_ref[...]
        gcs = _cumsum_cols(_log_sigmoid(gcol), lo_tri)
        grs = _cumsum_rows(_log_sigmoid(grow), up_tri)
        for h in range(H):
            hs = slice(h * DH, (h + 1) * DH)
            ig_c = gcol[:, GATE_OFF + h:GATE_OFF + h + 1]
            g_c = gcs[:, GATE_OFF + H + h:GATE_OFF + H + h + 1]
            ig_r = grow[h:h + 1, :]
            g_r = grs[H + h:H + h + 1, :]
            q = q_ref[rows, hs]
            kf = k_ref[rows, hs] * (DH ** -0.5)
            lmat = jnp.where(incl, g_c - g_r + ig_r, NEG_INF)
            probs.append(dict(
                rows=rows, h=h, q=q, kf=kf, qb=q.astype(BF16), kb=kf.astype(BF16),
                vb=v_ref[rows, hs].astype(BF16), lmat=lmat, lmax=jnp.max(lmat, axis=-1, keepdims=True),
                g_c=g_c, ig_c=ig_c, g_last=g_c[l - 1:l, :]))
    qks = [_dot_nt(p["qb"], p["kb"]) for p in probs]

    ms = [m_ref[:, h:h + 1] for h in range(H)]
    for p in probs:
        m_old = ms[p["h"]]
        p["linter"] = p["g_c"] + m_old
        p["mt"] = jnp.maximum(p["linter"], p["lmax"])
        m_new = p["mt"][l - 1:l, :]
        p["dprev"] = jnp.exp(p["g_last"] + m_old - m_new)
        p["kw"] = p["kf"] * jnp.exp(p["g_last"] - p["g_c"] + p["ig_c"] - m_new)
        ms[p["h"]] = m_new
    ws_ = [qk * jnp.exp(p["lmat"] - p["mt"]) for p, qk in zip(probs, qks)]
    wvs = [_dot(w.astype(BF16), p["vb"]) for p, w in zip(probs, ws_)]
    upds = [_dot_tn(p["kw"].astype(BF16), p["vb"]) for p in probs]

    cs = [c_ref[h] for h in range(H)]
    ns = [n_ref[h:h + 1, :] for h in range(H)]
    qcs, qns = [], []
    for p, upd in zip(probs, upds):
        h = p["h"]
        qcs.append(_dot(p["qb"], cs[h].astype(BF16)))
        qns.append(jnp.sum(p["q"] * ns[h], axis=-1, keepdims=True))
        cs[h] = p["dprev"] * cs[h] + upd
        ns[h] = p["dprev"] * ns[h] + jnp.sum(p["kw"], axis=0, keepdims=True)
    for h in range(H):
        c_ref[h] = cs[h]
        n_ref[h:h + 1, :] = ns[h]
        m_ref[:, h:h + 1] = ms[h]

    wsums = [jnp.sum(w, axis=-1, keepdims=True) for w in ws_]
    obs = []
    for p, wsum, wv, qc, qn in zip(probs, wsums, wvs, qcs, qns):
        hs = slice(p["h"] * DH, (p["h"] + 1) * DH)
        inter = jnp.exp(p["linter"] - p["mt"])
        den = wsum + inter * qn
        hout = (wv + inter * qc) / jnp.maximum(jnp.abs(den), jnp.exp(-p["mt"]))
        obs.append(hout * _sigmoid(o_ref[p["rows"], hs]))
    msq = [jnp.mean(ob * ob, axis=-1, keepdims=True) for ob in obs]
    for p, ob, ms_ in zip(probs, obs, msq):
        hs = slice(p["h"] * DH, (p["h"] + 1) * DH)
        out_ref[p["rows"], hs] = (ob * lax.rsqrt(ms_ + EPS) * gh_ref[:, hs]).astype(out_ref.dtype)


def _mlstm(proj3, gt3, bias_c, bias_r, gh, c0, n0, m0, l, nck):
    b, t, _ = proj3.shape
    steps = t // (l * nck)
    blk = l * nck
    col = lambda j: pl.BlockSpec((None, blk, GW), lambda bi, s: (bi, s, j))
    state = lambda shp: pl.BlockSpec((None,) + shp, lambda bi, s: (bi,) + (0,) * len(shp))
    return pl.pallas_call(
        functools.partial(_mlstm_kernel, l=l, nck=nck),
        grid=(b, steps),
        in_specs=[
            col(COL_B // GW), col(COL_B // GW + 1), col(COL_B // GW + 2), col(COL_B // GW + 3),
            pl.BlockSpec((None, blk, LANES), lambda bi, s: (bi, s, GATE_BLK)),
            pl.BlockSpec((None, nck, N_GATES, l), lambda bi, s: (bi, s, 0, 0)),
            pl.BlockSpec((1, LANES), lambda bi, s: (0, 0)),
            pl.BlockSpec((N_GATES, 1), lambda bi, s: (0, 0)),
            pl.BlockSpec((1, GW), lambda bi, s: (0, 0)),
            state((H, DH, DH)), state((H, DH)), state((1, H)),
        ],
        out_specs=[
            pl.BlockSpec((None, blk, GW), lambda bi, s: (bi, s, 0)),
            state((H, DH, DH)), state((H, DH)), state((1, H)),
        ],
        out_shape=[
            jax.ShapeDtypeStruct((b, t, GW), BF16),
            jax.ShapeDtypeStruct((b, H, DH, DH), F32),
            jax.ShapeDtypeStruct((b, H, DH), F32),
            jax.ShapeDtypeStruct((b, 1, H), F32),
        ],
        compiler_params=_cparams("parallel", "arbitrary"),
        name="mlstm",
    )(proj3, proj3, proj3, proj3, proj3, gt3, bias_c, bias_r, gh, c0, n0, m0)


def _split2(x):
    hi = x.astype(BF16)
    lo = (x - hi.astype(F32)).astype(BF16)
    return hi, lo


def _dot_sp(a, b):
    return _dot(a[0], b[0]) + (_dot(a[0], b[1]) + _dot(a[1], b[0]))


def _unit_lower_inverses(nmats, l):
    eye = (_iota((l, l), 0) == _iota((l, l), 1)).astype(F32)
    ps = [eye - n for n in nmats]
    qs = [_split2(n) for n in nmats]
    qs = [_split2(_dot_sp(q, q)) for q in qs]
    power = 2
    while power < l:
        ps = [p + _dot_sp(_split2(p), q) for p, q in zip(ps, qs)]
        power *= 2
        if power < l:
            qs = [_split2(_dot_sp(q, q)) for q in qs]
    return ps


def _l2norm(x):
    return x * lax.rsqrt(jnp.sum(x * x, axis=-1, keepdims=True) + 1e-6)


def _gdn_kernel(x_ref, z_ref, gc_ref, gr_ref, hist_ref, cw_ref, ac_ref, ar_ref, dc_ref, dr_ref,
                gh_ref, s0_ref, out_ref, s_ref, carry_scr, *, l, nck):
    @pl.when(pl.program_id(1) == 0)
    def _():
        s_ref[...] = s0_ref[...]
        carry_scr[...] = hist_ref[...]

    blk = l * nck
    x = x_ref[...]
    ext = jnp.concatenate([carry_scr[...], x], axis=0)
    carry_scr[...] = x[blk - SUBLANES:, :]
    y = x * cw_ref[3:4, :]
    for j in range(1, 4):
        y = y + ext[SUBLANES - j:SUBLANES - j + blk, :] * cw_ref[3 - j:4 - j, :]
    y = y * _sigmoid(y)

    incl, strict = _tri_masks(l)
    lo_tri = incl.astype(BF16)
    up_tri = (_iota((l, l), 0) <= _iota((l, l), 1)).astype(BF16)

    qraw = [y[ck * l:(ck + 1) * l, h * DH:(h + 1) * DH] for ck in range(nck) for h in range(H)]
    kraw = [y[ck * l:(ck + 1) * l, GW + h * DH:GW + (h + 1) * DH] for ck in range(nck) for h in range(H)]
    vraw = [y[ck * l:(ck + 1) * l, 2 * GW + h * DH:2 * GW + (h + 1) * DH] for ck in range(nck) for h in range(H)]
    qnorm = [_l2norm(a) * (DH ** -0.5) for a in qraw]
    knorm = [_l2norm(a) for a in kraw]
    probs = []
    for ck in range(nck):
        rows = slice(ck * l, (ck + 1) * l)
        gcol = gc_ref[rows, :]
        grow = gr_ref[ck]
        beta_cs = _sigmoid(gcol)
        dec_c = -jnp.exp(ac_ref[...]) * _softplus(gcol + dc_ref[...])
        dec_r = -jnp.exp(ar_ref[...]) * _softplus(grow + dr_ref[...])
        gcs = _cumsum_cols(dec_c, lo_tri)
        grs = _cumsum_rows(dec_r, up_tri)
        for h in range(H):
            beta = beta_cs[:, GATE_OFF + 2 * H + h:GATE_OFF + 2 * H + h + 1]
            g_c = gcs[:, GATE_OFF + 3 * H + h:GATE_OFF + 3 * H + h + 1]
            g_r = grs[3 * H + h:3 * H + h + 1, :]
            q, k, v = qnorm[ck * H + h], knorm[ck * H + h], vraw[ck * H + h]
            decay = jnp.exp(jnp.where(incl, g_c - g_r, NEG_INF))
            eg = jnp.exp(g_c)
            g_last = g_c[l - 1:l, :]
            probs.append(dict(
                rows=rows, h=h, qb=q.astype(BF16), kb=k.astype(BF16), beta=beta, decay=decay,
                rhs=jnp.concatenate([v * beta, k * (beta * eg)], axis=-1),
                qeg=(q * eg).astype(BF16), kdec=(k * jnp.exp(g_last - g_c)).astype(BF16),
                sdec=jnp.exp(g_last)))
    kks = [_dot_nt(p["kb"], p["kb"]) for p in probs]
    qks = [_dot_nt(p["qb"], p["kb"]) for p in probs]
    a_lows = [jnp.where(strict, p["beta"] * kk * p["decay"], 0.0) for p, kk in zip(probs, kks)]
    attns = [(qk * p["decay"]).astype(BF16) for p, qk in zip(probs, qks)]
    tinvs = _unit_lower_inverses(a_lows, l)
    sols = [_dot_sp(_split2(t), _split2(p["rhs"])) for t, p in zip(tinvs, probs)]

    states = [s_ref[h] for h in range(H)]
    for ck in range(nck):
        ps = probs[ck * H:(ck + 1) * H]
        ss = sols[ck * H:(ck + 1) * H]
        at = attns[ck * H:(ck + 1) * H]
        sbs = [s.astype(BF16) for s in states]
        wss = [_dot(sol[:, DH:].astype(BF16), sb) for sol, sb in zip(ss, sbs)]
        qss = [_dot(p["qeg"], sb) for p, sb in zip(ps, sbs)]
        vnbs = [(sol[:, :DH] - ws).astype(BF16) for sol, ws in zip(ss, wss)]
        os_ = [qs + _dot(a, vnb) for qs, a, vnb in zip(qss, at, vnbs)]
        states = [p["sdec"] * s + _dot_tn(p["kdec"], vnb) for p, s, vnb in zip(ps, states, vnbs)]
        for p, o in zip(ps, os_):
            hs = slice(p["h"] * DH, (p["h"] + 1) * DH)
            zg = z_ref[p["rows"], hs]
            yo = (o * lax.rsqrt(jnp.mean(o * o, axis=-1, keepdims=True) + EPS) * gh_ref[:, hs]
                  * (zg * _sigmoid(zg)))
            out_ref[p["rows"], hs] = yo.astype(out_ref.dtype)
    for h in range(H):
        s_ref[h] = states[h]


def _gdn(proj3, gt3, hist8, cw, a_c, a_r, dt_c, dt_r, gh, s0, l, nck):
    b, t, _ = proj3.shape
    blk = l * nck
    steps = t // blk
    state = lambda shp: pl.BlockSpec((None,) + shp, lambda bi, s: (bi,) + (0,) * len(shp))
    const = lambda shp: pl.BlockSpec(shp, lambda bi, s: (0,) * len(shp))
    return pl.pallas_call(
        functools.partial(_gdn_kernel, l=l, nck=nck),
        grid=(b, steps),
        in_specs=[
            pl.BlockSpec((None, blk, 3 * GW), lambda bi, s: (bi, s, COL_CX // (3 * GW))),
            pl.BlockSpec((None, blk, GW), lambda bi, s: (bi, s, COL_CZ // GW)),
            pl.BlockSpec((None, blk, LANES), lambda bi, s: (bi, s, GATE_BLK)),
            pl.BlockSpec((None, nck, N_GATES, l), lambda bi, s: (bi, s, 0, 0)),
            state((SUBLANES, 3 * GW)),
            const((4, 3 * GW)),
            const((1, LANES)), const((N_GATES, 1)), const((1, LANES)), const((N_GATES, 1)),
            const((1, GW)),
            state((H, DH, DH)),
        ],
        out_specs=[
            pl.BlockSpec((None, blk, GW), lambda bi, s: (bi, s, 0)),
            state((H, DH, DH)),
        ],
        out_shape=[
            jax.ShapeDtypeStruct((b, t, GW), BF16),
            jax.ShapeDtypeStruct((b, H, DH, DH), F32),
        ],
        scratch_shapes=[pltpu.VMEM((SUBLANES, 3 * GW), F32)],
        compiler_params=_cparams("parallel", "arbitrary"),
        name="gdn",
    )(proj3, proj3, proj3, gt3, hist8, cw, a_c, a_r, dt_c, dt_r, gh, s0)


HEAD_PAD = 128
DPAD = H * HEAD_PAD


def _tile_heads(t):
    return jnp.concatenate([t] * H, axis=-1)


def _dprep_kernel(tail_ref, gq_ref, gkv_ref, wq_ref, wqp_ref, qcos_ref, qsin_ref, ka_ref, kb_ref,
                  ckv_ref, kpe_ref, qc_ref):
    hq = _rms(tail_ref[:, :Q_LORA], gq_ref[...]).astype(BF16)
    qc = (_dot(hq, wq_ref[...]) * _tile_heads(qcos_ref[...])
          + _dot(hq, wqp_ref[...]) * _tile_heads(qsin_ref[...]))
    qc_ref[...] = (qc * (MLA_SCALE * LOG2E)).astype(BF16)
    ckv_ref[...] = _rms(tail_ref[:, Q_LORA:Q_LORA + KV_LORA], gkv_ref[...])
    kr = tail_ref[:, Q_LORA + KV_LORA:]
    kpe = kr * ka_ref[...] + pltpu.roll(kr, 64, axis=1) * kb_ref[...]
    kpe_ref[...] = kpe[:, :QK_ROPE]


def _dprep(proj, gq, gkv, wq, wqp, qcos, qsin, ka, kb, tm):
    n = proj.shape[0]
    row = lambda w: pl.BlockSpec((tm, w), lambda i: (i, 0))
    const = lambda a, b: pl.BlockSpec((a, b), lambda i: (0, 0))
    return pl.pallas_call(
        _dprep_kernel,
        grid=(n // tm,),
        in_specs=[
            pl.BlockSpec((tm, TAIL_W), lambda i: (i, COL_TAIL // TAIL_W)),
            const(1, Q_LORA), const(1, KV_LORA), const(Q_LORA, DPAD), const(Q_LORA, DPAD),
            row(HEAD_PAD), row(HEAD_PAD), row(LANES), row(LANES),
        ],
        out_specs=[row(KV_LORA), row(QK_ROPE), row(DPAD)],
        out_shape=[
            jax.ShapeDtypeStruct((n, KV_LORA), F32),
            jax.ShapeDtypeStruct((n, QK_ROPE), F32),
            jax.ShapeDtypeStruct((n, DPAD), BF16),
        ],
        compiler_params=_cparams("parallel"),
        name="dprep",
    )(proj, gq, gkv, wq, wqp, qcos, qsin, ka, kb)


def _kvup_kernel(ckv_ref, kpe_ref, wk_ref, wv_ref, pm_ref, one_ref, kc_ref, vp_ref):
    c = ckv_ref[...].astype(BF16)
    kc_ref[...] = (_dot(c, wk_ref[...]) + _dot(kpe_ref[...].astype(BF16), pm_ref[...])).astype(BF16)
    vp_ref[...] = (_dot(c, wv_ref[...]) + one_ref[...]).astype(BF16)


def _kvup(ckv, kpe, wk, wv, pm, ones, tm):
    m = ckv.shape[0]
    row = lambda w: pl.BlockSpec((tm, w), lambda i: (i, 0))
    const = lambda a, b: pl.BlockSpec((a, b), lambda i: (0, 0))
    return pl.pallas_call(
        _kvup_kernel,
        grid=(m // tm,),
        in_specs=[row(KV_LORA), row(QK_ROPE), const(KV_LORA, DPAD), const(KV_LORA, DPAD),
                  const(QK_ROPE, DPAD), const(1, DPAD)],
        out_specs=[row(DPAD), row(DPAD)],
        out_shape=[jax.ShapeDtypeStruct((m, DPAD), BF16), jax.ShapeDtypeStruct((m, DPAD), BF16)],
        compiler_params=_cparams("parallel"),
        name="kvup",
    )(ckv, kpe, wk, wv, pm, ones)


def _mla_finish(out_ref, gh_ref, h, acc):
    o = acc[:, :DH] / acc[:, DH:DH + 1]
    rows = slice(0, o.shape[0])
    _head_norm_store(out_ref, rows, h, o, gh_ref)


LOG2E = 1.4426950408889634


def _kvup_t_kernel(ckv_ref, kpe_ref, wk_ref, wvt_ref, pm_ref, onet_ref, kc_ref, vt_ref):
    c = ckv_ref[...].astype(BF16)
    kc_ref[...] = (_dot(c, wk_ref[...]) + _dot(kpe_ref[...].astype(BF16), pm_ref[...])).astype(BF16)
    vt_ref[...] = (_dot_nt(wvt_ref[...], c) + onet_ref[...]).astype(BF16)


def _kvup_t(ckv, kpe, wk, wvt, pm, onet, tm):
    m = ckv.shape[0]
    row = lambda w: pl.BlockSpec((tm, w), lambda i: (i, 0))
    const = lambda a, b: pl.BlockSpec((a, b), lambda i: (0, 0))
    return pl.pallas_call(
        _kvup_t_kernel,
        grid=(m // tm,),
        in_specs=[row(KV_LORA), row(QK_ROPE), const(KV_LORA, DPAD), const(DPAD, KV_LORA),
                  const(QK_ROPE, DPAD), const(DPAD, 1)],
        out_specs=[row(DPAD), pl.BlockSpec((DPAD, tm), lambda i: (0, i))],
        out_shape=[jax.ShapeDtypeStruct((m, DPAD), BF16), jax.ShapeDtypeStruct((DPAD, m), BF16)],
        compiler_params=_cparams("parallel"),
        name="kvup_t",
    )(ckv, kpe, wk, wvt, pm, onet)


def _mla_prompt_kernel(qi_ref, ki_ref, q_ref, k_ref, vt_ref, ghc_ref, out_ref, m_scr, acc_scr, *, bq, bk,
                       qw, ahead):
    p = pl.program_id(0)
    q_i = qi_ref[p]
    k_i = ki_ref[p]
    last = (q_i * bq) // bk

    @pl.when(k_i == 0)
    def _():
        m_scr[...] = jnp.full(m_scr.shape, NEG_INF, F32)
        acc_scr[...] = jnp.zeros(acc_scr.shape, F32)

    def step(diag):
        if diag:
            key_chunk = k_i * (bk // CHUNK) + _iota((bk, bq), 0) // CHUNK
            qry_chunk = q_i * (bq // CHUNK) + _iota((bk, bq), 1) // CHUNK
            allowed = key_chunk <= qry_chunk

        units = [(h, c) for h in range(H) for c in range(bq // qw)]
        rc = min(bk, 64)

        def scores(u):
            h, c = u
            hs = slice(h * HEAD_PAD, (h + 1) * HEAD_PAD)
            return _dot_nt(k_ref[:, hs], q_ref[c * qw:(c + 1) * qw, hs])

        def update(u, st):
            h, c = u
            hs = slice(h * HEAD_PAD, (h + 1) * HEAD_PAD)
            qs = slice(c * qw, (c + 1) * qw)
            if diag:
                st = jnp.where(allowed[:, qs], st, NEG_INF)
            m_prev = m_scr[h, :, qs]
            mx = st[:rc]
            for r in range(1, bk // rc):
                mx = jnp.maximum(mx, st[r * rc:(r + 1) * rc])
            m_new = jnp.maximum(m_prev, jnp.max(mx, axis=0, keepdims=True))
            alpha = jnp.exp2(m_prev - m_new)[0:1]
            m_row = m_new[0:1]
            pt = jnp.concatenate([jnp.exp2(st[r * rc:(r + 1) * rc] - m_row).astype(BF16)
                                  for r in range(bk // rc)], axis=0)
            acc = alpha * acc_scr[h, :, qs] + _dot(vt_ref[hs, :], pt)
            if not diag:
                m_scr[h, :, qs] = m_new
                acc_scr[h, :, qs] = acc
            return acc

        accs = []
        pending = [scores(u) for u in units[:ahead]]
        for idx, u in enumerate(units):
            if idx + ahead < len(units):
                pending.append(scores(units[idx + ahead]))
            accs.append(update(u, pending.pop(0)))
        per_head = bq // qw
        return [jnp.concatenate(accs[h * per_head:(h + 1) * per_head], axis=1) for h in range(H)]

    @pl.when(k_i < last)
    def _():
        step(False)

    @pl.when(k_i == last)
    def _():
        ys = []
        for h, acc in enumerate(step(True)):
            o = acc[:DH] / acc[DH:DH + 1]
            ms = jnp.mean(o * o, axis=0, keepdims=True)
            ys.append(o * lax.rsqrt(ms + EPS) * ghc_ref[h * DH:(h + 1) * DH, :])
        out_ref[...] = jnp.concatenate(ys, axis=0).T.astype(out_ref.dtype)


def _mla_prompt(qc, kc, vt, ghc, bq, bk):
    t = qc.shape[0]
    assert t % bq == 0 and t % bk == 0 and bk % bq == 0
    pairs = [(i, j) for i in range(t // bq) for j in range((i * bq) // bk + 1)]
    qi = jnp.asarray([i for i, _ in pairs], jnp.int32)
    ki = jnp.asarray([j for _, j in pairs], jnp.int32)
    grid_spec = pltpu.PrefetchScalarGridSpec(
        num_scalar_prefetch=2,
        grid=(len(pairs),),
        in_specs=[
            pl.BlockSpec((bq, DPAD), lambda p, qi, ki: (qi[p], 0)),
            pl.BlockSpec((bk, DPAD), lambda p, qi, ki: (ki[p], 0)),
            pl.BlockSpec((DPAD, bk), lambda p, qi, ki: (0, ki[p])),
            pl.BlockSpec((GW, 1), lambda p, qi, ki: (0, 0)),
        ],
        out_specs=pl.BlockSpec((bq, GW), lambda p, qi, ki: (qi[p], 0)),
        scratch_shapes=[
            pltpu.VMEM((H, SUBLANES, bq), F32),
            pltpu.VMEM((H, HEAD_PAD, bq), F32),
        ],
    )
    return pl.pallas_call(
        functools.partial(_mla_prompt_kernel, bq=bq, bk=bk, qw=min(bq, 256), ahead=2),
        grid_spec=grid_spec,
        out_shape=jax.ShapeDtypeStruct((t, GW), BF16),
        compiler_params=_cparams("arbitrary"),
        name="mla_prompt",
    )(qi, ki, qc, kc, vt, ghc)


def _mla_sample_kernel(q_ref, k_ref, v_ref, gh_ref, out_ref):
    for h in range(H):
        hs = slice(h * HEAD_PAD, (h + 1) * HEAD_PAD)
        s = _dot_nt(q_ref[:, hs], k_ref[:, hs])
        m = jnp.max(s, axis=-1, keepdims=True)
        pr = jnp.exp2(s - m)
        acc = _dot(pr.astype(BF16), v_ref[:, hs])
        _mla_finish(out_ref, gh_ref, h, acc)


def _mla_sample(qc3, kc3, vp3, gh):
    b, s, _ = qc3.shape
    nk = kc3.shape[1]
    return pl.pallas_call(
        _mla_sample_kernel,
        grid=(b,),
        in_specs=[
            pl.BlockSpec((None, s, DPAD), lambda i: (i, 0, 0)),
            pl.BlockSpec((None, nk, DPAD), lambda i: (i, 0, 0)),
            pl.BlockSpec((None, nk, DPAD), lambda i: (i, 0, 0)),
            pl.BlockSpec((1, GW), lambda i: (0, 0)),
        ],
        out_specs=pl.BlockSpec((None, s, GW), lambda i: (i, 0, 0)),
        out_shape=jax.ShapeDtypeStruct((b, s, GW), BF16),
        compiler_params=_cparams("parallel"),
        name="mla_sample",
    )(qc3, kc3, vp3, gh)


def _outproj_kernel(x_ref, a_ref, b_ref, c_ref, d_ref, w_ref, out_ref):
    acc = x_ref[...]
    for g, m_ref in enumerate((a_ref, b_ref, c_ref, d_ref)):
        acc = acc + _dot(m_ref[...], w_ref[g * GW:(g + 1) * GW, :])
    out_ref[...] = acc


def _outproj(x, ma, mb, mc, md, w, tm):
    n = x.shape[0]
    mix = pl.BlockSpec((tm, GW), lambda i: (i, 0))
    return pl.pallas_call(
        _outproj_kernel,
        grid=(n // tm,),
        in_specs=[pl.BlockSpec((tm, D_MODEL), lambda i: (i, 0)), mix, mix, mix, mix,
                  pl.BlockSpec((D_MODEL, D_MODEL), lambda i: (0, 0))],
        out_specs=pl.BlockSpec((tm, D_MODEL), lambda i: (i, 0)),
        out_shape=jax.ShapeDtypeStruct((n, D_MODEL), F32),
        compiler_params=_cparams("parallel"),
        name="outproj",
    )(x, ma, mb, mc, md, w)


def _ffn_kernel(*refs, seq_len, final_norm, tf):
    if seq_len is None:
        (x_ref, g_ref, wup_ref, cw_ref, wd_ref, gf_ref, out_ref, ga_ref, act_scr, carry_scr) = refs
    else:
        (x_ref, g_ref, wup_ref, cw_ref, wd_ref, gf_ref, h1_ref, h2_ref, out_ref, ga_ref, act_scr) = refs
    tm = x_ref.shape[0]
    nj = D_FF // tf
    h = _rms(x_ref[...], g_ref[...]).astype(BF16)
    row = _iota((tm, tf), 0)

    if seq_len is None:
        @pl.when(pl.program_id(0) == 0)
        def _():
            carry_scr[...] = jnp.zeros(carry_scr.shape, F32)

    def up(j):
        cols = slice(j * tf, (j + 1) * tf)
        ucols = slice(D_FF + j * tf, D_FF + (j + 1) * tf)
        return _dot(h, wup_ref[:, cols]), _dot(h, wup_ref[:, ucols])

    def gate(j, ga, u):
        cols = slice(j * tf, (j + 1) * tf)
        r1 = pltpu.roll(ga, 1, axis=0)
        r2 = pltpu.roll(ga, 2, axis=0)
        if seq_len is None:
            c1 = carry_scr[SUBLANES - 1:SUBLANES, cols]
            c2 = carry_scr[SUBLANES - 2:SUBLANES - 1, cols]
            prev1 = jnp.where(row >= 1, r1, c1)
            prev2 = jnp.where(row >= 2, r2, jnp.where(row == 1, c1, c2))
            tail = ga[tm - SUBLANES:, :]
            carry_scr[:, cols] = tail
            ga_ref[:, cols] = tail
        else:
            t = row % seq_len
            prev1 = jnp.where(t >= 1, r1, h1_ref[:, cols])
            prev2 = jnp.where(t >= 2, r2, h2_ref[:, cols])
            ga_ref[:, cols] = ga
        conv = prev2 * cw_ref[0:1, cols] + prev1 * cw_ref[1:2, cols] + ga * cw_ref[2:3, cols]
        act_scr[:, cols] = (conv * _sigmoid(conv) * u).astype(BF16)

    pending = up(0)
    for j in range(nj):
        nxt = up(j + 1) if j + 1 < nj else None
        gate(j, *pending)
        pending = nxt
    y = x_ref[...] + _dot(act_scr[...], wd_ref[...])
    if final_norm:
        y = _rms(y, gf_ref[...])
    out_ref[...] = y


def _ffn(x, g, w_up, cw, w_down, gf, h1, h2, *, tm, tf, seq_len, final_norm):
    n = x.shape[0]
    ni = n // tm
    resident = lambda a, b: pl.BlockSpec((a, b), lambda i: (0, 0), pipeline_mode=pl.Buffered(1))
    in_specs = [
        pl.BlockSpec((tm, D_MODEL), lambda i: (i, 0)),
        resident(1, D_MODEL),
        resident(D_MODEL, 2 * D_FF),
        resident(3, D_FF),
        resident(D_FF, D_MODEL),
        resident(1, D_MODEL),
    ]
    args = [x, g, w_up, cw, w_down, gf]
    scratch = [pltpu.VMEM((tm, D_FF), BF16)]
    if seq_len is None:
        ga_spec = pl.BlockSpec((None, SUBLANES, D_FF), lambda i: (i, 0, 0))
        ga_shape = jax.ShapeDtypeStruct((ni, SUBLANES, D_FF), F32)
        scratch.append(pltpu.VMEM((SUBLANES, D_FF), F32))
    else:
        in_specs += [pl.BlockSpec((tm, D_FF), lambda i: (i, 0))] * 2
        args += [h1, h2]
        ga_spec = pl.BlockSpec((tm, D_FF), lambda i: (i, 0))
        ga_shape = jax.ShapeDtypeStruct((n, D_FF), F32)
    return pl.pallas_call(
        functools.partial(_ffn_kernel, seq_len=seq_len, final_norm=final_norm, tf=tf),
        grid=(ni,),
        in_specs=in_specs,
        out_specs=[pl.BlockSpec((tm, D_MODEL), lambda i: (i, 0)), ga_spec],
        out_shape=[jax.ShapeDtypeStruct((n, D_MODEL), F32), ga_shape],
        scratch_shapes=scratch,
        compiler_params=_cparams("arbitrary"),
        name="ffn",
    )(*args)


def _rope_tables(pos):
    half = QK_ROPE // 2
    inv = ROPE_THETA ** (-jnp.arange(half, dtype=F32) / half)
    ang = pos.astype(F32)[:, None] * inv[None, :]
    cos, sin = jnp.cos(ang), jnp.sin(ang)
    cos2 = jnp.concatenate([cos, cos], -1)
    sin2 = jnp.concatenate([-sin, sin], -1)
    n = pos.shape[0]
    z32 = jnp.zeros((n, 32), F32)
    qcos = jnp.concatenate([jnp.ones((n, QK_NOPE), F32), cos2, z32], -1)
    qsin = jnp.concatenate([jnp.zeros((n, QK_NOPE), F32), sin2, z32], -1)
    ka = jnp.concatenate([cos2, z32, sin2, z32], -1)
    kb = jnp.concatenate([sin2, z32, cos2, z32], -1)
    return qcos, qsin, ka, kb


def _rel_bias(table, n_past, n_q, n_k):
    dmax = n_past + n_q - 1
    dmin = n_past - n_k + 1
    diag = table[:, jnp.clip(jnp.arange(dmax, dmin - 1, -1), -REL_MAX, REL_MAX) + REL_MAX]
    return jnp.stack([diag[:, n_q - 1 - i:n_q - 1 - i + n_k] for i in range(n_q)], axis=1)


def _swap_halves(w):
    half = w.shape[-1] // 2
    return jnp.concatenate([w[..., half:], w[..., :half]], -1)


def _layer_weights(lw):
    (g_mix, w_in, a_rel_bias, b_i_bias, b_f_bias, c_conv_w, c_a_log, c_dt_bias,
     d_g_q, d_w_q_up, d_g_kv, d_w_kv_up, g_head, w_out, g_ffn, w_up, f_conv_w, w_down) = lw
    o = 0
    cols = {}
    for name, size in (("a", 3 * GW), ("b", 4 * GW), ("bg", 2 * H), ("c", 3 * GW), ("cz", GW),
                       ("cg", 2 * H), ("dq", Q_LORA), ("dkv", KV_LORA), ("dkr", QK_ROPE)):
        cols[name] = w_in[:, o:o + size]
        o += size
    gates = jnp.concatenate([cols["bg"], cols["cg"]], -1)
    pad16 = jnp.zeros((D_MODEL, 16), F32)
    pad32 = jnp.zeros((D_MODEL, 32), F32)
    w_perm = jnp.concatenate([cols["c"], cols["a"], cols["dq"], cols["dkv"], cols["dkr"], gates, pad16,
                              _swap_halves(cols["dkr"]), pad32, cols["b"], cols["cz"]], -1)
    zc = lambda n: jnp.zeros((1, n), F32)
    zr = lambda n: jnp.zeros((n, 1), F32)
    bias_c = jnp.concatenate([zc(GATE_OFF), b_i_bias[None], b_f_bias[None], zc(LANES - GATE_OFF - 2 * H)], -1)
    bias_r = jnp.concatenate([b_i_bias[:, None], b_f_bias[:, None], zr(2 * H)], 0)
    alog_c = jnp.concatenate([zc(GATE_OFF + 3 * H), c_a_log[None], zc(LANES - GATE_OFF - 4 * H)], -1)
    alog_r = jnp.concatenate([zr(3 * H), c_a_log[:, None]], 0)
    dt_c = jnp.concatenate([zc(GATE_OFF + 3 * H), c_dt_bias[None], zc(LANES - GATE_OFF - 4 * H)], -1)
    dt_r = jnp.concatenate([zr(3 * H), c_dt_bias[:, None]], 0)

    wq = d_w_q_up.reshape(Q_LORA, H, QK_NOPE + QK_ROPE)
    z_h32 = jnp.zeros((Q_LORA, H, 32), F32)
    wq_full = jnp.concatenate([wq, z_h32], -1).reshape(Q_LORA, DPAD)
    wq_part = jnp.concatenate([jnp.zeros((Q_LORA, H, QK_NOPE), F32), _swap_halves(wq[..., QK_NOPE:]), z_h32],
                              -1).reshape(Q_LORA, DPAD)
    wkv = d_w_kv_up.reshape(KV_LORA, H, 2 * DH)
    z_h64 = jnp.zeros((KV_LORA, H, DH), F32)
    wk_full = jnp.concatenate([wkv[..., :DH], z_h64], -1).reshape(KV_LORA, DPAD)
    wv_full = jnp.concatenate([wkv[..., DH:], z_h64], -1).reshape(KV_LORA, DPAD)
    place = jnp.concatenate([jnp.zeros((QK_ROPE, QK_NOPE), F32), jnp.eye(QK_ROPE, dtype=F32),
                             jnp.zeros((QK_ROPE, 32), F32)], -1)
    pmat = jnp.concatenate([place] * H, -1)
    return dict(
        g_mix=g_mix[None], w_in=w_perm.astype(BF16), w_gt=gates.T.astype(BF16),
        table=a_rel_bias, bias_c=bias_c, bias_r=bias_r, alog_c=alog_c, alog_r=alog_r, dt_c=dt_c, dt_r=dt_r,
        c_conv_w=c_conv_w, g_q=d_g_q[None], g_kv=d_g_kv[None],
        wq=wq_full.astype(BF16), wqp=wq_part.astype(BF16), wk=wk_full.astype(BF16), wv=wv_full.astype(BF16),
        pmat=pmat.astype(BF16),
        vones=(jnp.arange(DPAD) % HEAD_PAD == DH).astype(F32)[None],
g_head=g_head.reshape(4, 1, GW), w_out=w_out.astype(BF16),
        g_ffn=g_ffn[None], w_up=w_up.astype(BF16), f_conv_w=f_conv_w, w_down=w_down.astype(BF16))


def _gates_t3(gt, b, t, l):
    return gt.reshape(N_GATES, b, t // l, l).transpose(1, 2, 0, 3)


def _layer(x, offset, st, w, gf, final_norm, cfg):
    b, t, _ = x.shape
    n = b * t
    first = st is None
    x2 = x.reshape(n, D_MODEL)
    proj, gt = _inproj(x2, w["g_mix"], w["w_in"], w["w_gt"], cfg["tm"], cfg["tn"])
    proj3 = proj.reshape(b, t, PROJ_W)
    gh = w["g_head"]
    l = min(t, CHUNK)
    gt3 = _gates_t3(gt, b, t, l)

    new_ak = proj3[:, t - min(A_PAST, t):, COL_A + GW:COL_A + 2 * GW].reshape(b, -1, H, DH)
    new_av = proj3[:, t - min(A_PAST, t):, COL_A + 2 * GW:COL_A + 3 * GW].reshape(b, -1, H, DH)
    if first:
        bias = _rel_bias(w["table"], A_PAST, CHUNK, A_PAST + CHUNK)
        oa = _band_prompt(proj, bias, gh[0])
    else:
        npast = st[0].shape[1]
        bias = _rel_bias(w["table"], npast, t, npast + t)
        oa = _band_sample(proj3, st[0].reshape(b, npast, GW), st[1].reshape(b, npast, GW), bias, gh[0])
        oa = oa.reshape(n, GW)

    if first:
        c0 = jnp.zeros((b, H, DH, DH), F32)
        n0 = jnp.zeros((b, H, DH), F32)
        m0 = jnp.zeros((b, 1, H), F32)
    else:
        c0, n0, m0 = st[2], st[3], st[4][:, None, :]
    ob, new_bc, new_bn, new_bm = _mlstm(proj3, gt3, w["bias_c"], w["bias_r"], gh[1], c0, n0, m0,
                                        l, cfg["nck"])
    new_bm = new_bm[:, 0, :]

    if first:
        hist8 = jnp.zeros((b, SUBLANES, 3 * GW), F32)
        s0 = jnp.zeros((b, H, DH, DH), F32)
    else:
        hist8 = jnp.concatenate([jnp.zeros((b, SUBLANES - 3, 3 * GW), F32), st[6]], 1)
        s0 = st[5]
    oc, new_cs = _gdn(proj3, gt3, hist8, w["c_conv_w"], w["alog_c"], w["alog_r"], w["dt_c"], w["dt_r"],
                      gh[2], s0, l, cfg["nck"])
    new_cconv = proj3[:, t - 3:, COL_CX:COL_CX + 3 * GW]

    pos = jnp.arange(t, dtype=jnp.int32) + offset
    qcos, qsin, ka, kb = (jnp.tile(a, (b, 1)) for a in _rope_tables(pos))
    ckv, kpe, qc = _dprep(proj, w["g_q"], w["g_kv"], w["wq"], w["wqp"], qcos, qsin, ka, kb, cfg["tm_d"])
    if first:
        kc, vt = _kvup_t(ckv, kpe, w["wk"], w["wv"].T, w["pmat"], w["vones"].T, cfg["tm_kv"])
        od = _mla_prompt(qc, kc, vt, gh[3].T, cfg["mla_bq"], cfg["mla_bk"])
    else:
        ckv_all = jnp.concatenate([st[7], ckv.reshape(b, t, KV_LORA)], 1)
        kpe_all = jnp.concatenate([st[8], kpe.reshape(b, t, QK_ROPE)], 1)
        nk = ckv_all.shape[1]
        kc, vp = _kvup(ckv_all.reshape(b * nk, KV_LORA), kpe_all.reshape(b * nk, QK_ROPE),
                       w["wk"], w["wv"], w["pmat"], w["vones"], nk)
        od = _mla_sample(qc.reshape(b, t, DPAD), kc.reshape(b, nk, DPAD), vp.reshape(b, nk, DPAD), gh[3])
        od = od.reshape(n, GW)

    x2 = _outproj(x2, oa, ob.reshape(n, GW), oc.reshape(n, GW), od, w["w_out"], cfg["tm"])

    if first:
        y, ga_tail = _ffn(x2, w["g_ffn"], w["w_up"], w["f_conv_w"], w["w_down"], gf, None, None,
                          tm=cfg["tm"], tf=cfg["tf"], seq_len=None, final_norm=final_norm)
        new_fconv = ga_tail[-1, SUBLANES - 2:, :][None]
    else:
        hist = st[9]
        zrow = jnp.zeros((b, t - 1, D_FF), F32)
        h1 = jnp.concatenate([hist[:, 1:2], zrow], 1).reshape(n, D_FF)
        h2 = jnp.concatenate([hist, zrow[:, 1:]], 1).reshape(n, D_FF)
        y, ga = _ffn(x2, w["g_ffn"], w["w_up"], w["f_conv_w"], w["w_down"], gf, h1, h2,
                     tm=cfg["tm"], tf=cfg["tf"], seq_len=t, final_norm=final_norm)
        new_fconv = ga.reshape(b, t, D_FF)[:, t - 2:]
    state = (new_ak, new_av, new_bc, new_bn, new_bm, new_cs, new_cconv,
             ckv.reshape(b, t, KV_LORA), kpe.reshape(b, t, QK_ROPE), new_fconv)
    return y.reshape(b, t, D_MODEL), state


def _config(b, t):
    n = b * t
    tm = min(n, 1024)
    return dict(tm=tm, tn=PROJ_W // 2, tf=256, nck=1 if t <= CHUNK else 4,
                tm_d=min(n, 1024), tm_kv=min(n, 2048), mla_bq=min(t, 512), mla_bk=min(t, 1024))


def kernel(x_prompt, x_sample, cache_a_k, cache_a_v, state_b_c, state_b_n, state_b_m, state_c_s, cache_c_conv, cache_d_ckv, cache_d_kpe, cache_ffn_conv, g_mix, w_in, a_rel_bias, b_i_bias, b_f_bias, c_conv_w, c_a_log, c_dt_bias, d_g_q, d_w_q_up, d_g_kv, d_w_kv_up, g_head, w_out, g_ffn, w_up, f_conv_w, w_down, g_final):
    layer_w = (g_mix, w_in, a_rel_bias, b_i_bias, b_f_bias, c_conv_w, c_a_log, c_dt_bias,
               d_g_q, d_w_q_up, d_g_kv, d_w_kv_up, g_head, w_out, g_ffn, w_up, f_conv_w, w_down)
    depth = g_mix.shape[0]
    past = cache_d_ckv.shape[2]
    xp, xs = x_prompt, x_sample
    cfg_p = _config(*x_prompt.shape[:2])
    cfg_s = _config(*x_sample.shape[:2])
    gf = g_final[None]
    new_p, new_s = [], []
    for l in range(depth):
        w = _layer_weights(tuple(a[l] for a in layer_w))
        last = l == depth - 1
        xp, sp_l = _layer(xp, 0, None, w, gf, last, cfg_p)
        st = (cache_a_k[l], cache_a_v[l], state_b_c[l], state_b_n[l], state_b_m[l],
              state_c_s[l], cache_c_conv[l], cache_d_ckv[l], cache_d_kpe[l], cache_ffn_conv[l])
        xs, ss_l = _layer(xs, past, st, w, gf, last, cfg_s)
        new_p.append(sp_l)
        new_s.append(ss_l)
    outs = [xp, xs]
    for i in range(10):
        outs.append(jnp.stack([s[i] for s in new_p]))
        outs.append(jnp.stack([s[i] for s in new_s]))
    return tuple(outs)
```

```python
import functools
import math

import jax
import jax.numpy as jnp
from jax import lax
from jax.experimental import pallas as pl
from jax.experimental.pallas import tpu as pltpu

F32 = jnp.float32
BF16 = jnp.bfloat16

D_MODEL = 1024
CHUNK = 64
H = 4
DH = 64
GW = H * DH
A_PAST = 8 * CHUNK
REL_MAX = 2 * CHUNK
Q_LORA = 256
KV_LORA = 128
QK_NOPE = 64
QK_ROPE = 32
ROPE_THETA = 10000.0
MLA_SCALE = (QK_NOPE + QK_ROPE) ** -0.5
D_FF = 2816
EPS = 1e-6

COL_CX = 0
COL_A = 3 * GW
COL_TAIL = 6 * GW
TAIL_W = 512
COL_B = COL_TAIL + TAIL_W
COL_CZ = COL_B + 4 * GW
PROJ_W = COL_CZ + GW
GATE_BLK = (COL_TAIL + 384) // 128
GATE_OFF = 32
N_GATES = 16

LANES = 128
SUBLANES = 8
VMEM_LIMIT = 56 * 1024 * 1024

NEG_INF = float("-inf")


def _cparams(*sem):
    return pltpu.CompilerParams(dimension_semantics=sem, vmem_limit_bytes=VMEM_LIMIT)


def _dot(a, b):
    return jnp.dot(a, b, preferred_element_type=F32)


def _dot_nt(a, b):
    return lax.dot_general(a, b, (((1,), (1,)), ((), ())), preferred_element_type=F32)


def _dot_tn(a, b):
    return lax.dot_general(a, b, (((0,), (0,)), ((), ())), preferred_element_type=F32)


def _split3(x):
    hi = x.astype(BF16)
    r1 = x - hi.astype(F32)
    mid = r1.astype(BF16)
    lo = (r1 - mid.astype(F32)).astype(BF16)
    return hi, mid, lo


def _rms(x, g):
    return x * lax.rsqrt(jnp.mean(x * x, axis=-1, keepdims=True) + EPS) * g


def _log_sigmoid(x):
    return jnp.minimum(x, 0.0) - jnp.log1p(jnp.exp(-jnp.abs(x)))


def _softplus(x):
    return jnp.maximum(x, 0.0) + jnp.log1p(jnp.exp(-jnp.abs(x)))


def _sigmoid(x):
    return 1.0 / (1.0 + jnp.exp(-x))


def _iota(shape, dim):
    return lax.broadcasted_iota(jnp.int32, shape, dim)


def _inproj_kernel(x_ref, g_ref, w_ref, wgt_ref, proj_ref, gt_ref, h_scr):
    @pl.when(pl.program_id(1) == 0)
    def _():
        h = _rms(x_ref[...], g_ref[...]).astype(BF16)
        h_scr[...] = h
        gt_ref[...] = _dot_nt(wgt_ref[...], h)

    proj_ref[...] = _dot(h_scr[...], w_ref[...])


def _inproj(x, g, w, wgt, tm, tn):
    n = x.shape[0]
    return pl.pallas_call(
        _inproj_kernel,
        grid=(n // tm, PROJ_W // tn),
        in_specs=[
            pl.BlockSpec((tm, D_MODEL), lambda i, j: (i, 0)),
            pl.BlockSpec((1, D_MODEL), lambda i, j: (0, 0)),
            pl.BlockSpec((D_MODEL, tn), lambda i, j: (0, j)),
            pl.BlockSpec((N_GATES, D_MODEL), lambda i, j: (0, 0)),
        ],
        out_specs=[
            pl.BlockSpec((tm, tn), lambda i, j: (i, j)),
            pl.BlockSpec((N_GATES, tm), lambda i, j: (0, i)),
        ],
        out_shape=[
            jax.ShapeDtypeStruct((n, PROJ_W), F32),
            jax.ShapeDtypeStruct((N_GATES, n), F32),
        ],
        scratch_shapes=[pltpu.VMEM((tm, D_MODEL), BF16)],
        compiler_params=_cparams("parallel", "arbitrary"),
        name="inproj",
    )(x, g, w, wgt)


def _head_norm_store(out_ref, rows, h, o, gh_ref):
    g = gh_ref[:, h * DH:(h + 1) * DH]
    y = o * lax.rsqrt(jnp.mean(o * o, axis=-1, keepdims=True) + EPS) * g
    out_ref[rows, h * DH:(h + 1) * DH] = y.astype(out_ref.dtype)


def _band_prompt_kernel(q_ref, kp_ref, kc_ref, vp_ref, vc_ref, bias_ref, gh_ref, out_ref, *, qb):
    first = pl.program_id(0) == 0
    band = A_PAST + CHUNK
    nchunks = qb // CHUNK
    jj = _iota((CHUNK, band), 1)

    def scores(h):
        hs = slice(h * DH, (h + 1) * DH)
        kcat = jnp.concatenate([kp_ref[:, hs], kc_ref[:, hs]], axis=0).astype(BF16)
        return [_dot_nt(q_ref[c * CHUNK:(c + 1) * CHUNK, hs].astype(BF16), kcat[c * CHUNK:c * CHUNK + band])
                for c in range(nchunks)]

    def attend(h, ss):
        hs = slice(h * DH, (h + 1) * DH)
        vcat = jnp.concatenate([vp_ref[:, hs], vc_ref[:, hs]], axis=0).astype(BF16)
        bias = bias_ref[h]
        masked = []
        for c, s in enumerate(ss):
            s = s * (DH ** -0.5) + bias
            if c < A_PAST // CHUNK:
                valid = jnp.logical_or(jnp.logical_not(first), jj >= A_PAST - c * CHUNK)
                s = jnp.where(valid, s, NEG_INF)
            masked.append(s)
        ms = [jnp.max(s, axis=-1, keepdims=True) for s in masked]
        ps = [jnp.exp(s - m) for s, m in zip(masked, ms)]
        ls = [jnp.sum(p, axis=-1, keepdims=True) for p in ps]
        os_ = [_dot(p.astype(BF16), vcat[c * CHUNK:c * CHUNK + band]) for c, p in enumerate(ps)]
        os_ = [o / l for o, l in zip(os_, ls)]
        msq = [jnp.mean(o * o, axis=-1, keepdims=True) for o in os_]
        for c, (o, m2) in enumerate(zip(os_, msq)):
            out_ref[c * CHUNK:(c + 1) * CHUNK, hs] = (o * lax.rsqrt(m2 + EPS) * gh_ref[:, hs]).astype(out_ref.dtype)

    pending = scores(0)
    for h in range(H):
        nxt = scores(h + 1) if h + 1 < H else None
        attend(h, pending)
        pending = nxt


def _band_prompt(proj, bias, gh, qb=A_PAST):
    t = proj.shape[0]
    assert qb == A_PAST and t % qb == 0
    prev = lambda i: jnp.maximum(i - 1, 0)
    cq = COL_A // GW
    return pl.pallas_call(
        functools.partial(_band_prompt_kernel, qb=qb),
        grid=(t // qb,),
        in_specs=[
            pl.BlockSpec((qb, GW), lambda i: (i, cq)),
            pl.BlockSpec((qb, GW), lambda i: (prev(i), cq + 1)),
            pl.BlockSpec((qb, GW), lambda i: (i, cq + 1)),
            pl.BlockSpec((qb, GW), lambda i: (prev(i), cq + 2)),
            pl.BlockSpec((qb, GW), lambda i: (i, cq + 2)),
            pl.BlockSpec((H, CHUNK, A_PAST + CHUNK), lambda i: (0, 0, 0)),
            pl.BlockSpec((1, GW), lambda i: (0, 0)),
        ],
        out_specs=pl.BlockSpec((qb, GW), lambda i: (i, 0)),
        out_shape=jax.ShapeDtypeStruct((t, GW), BF16),
        compiler_params=_cparams("parallel"),
        name="band_prompt",
    )(proj, proj, proj, proj, proj, bias, gh)


def _band_sample_kernel(q_ref, k_ref, v_ref, ck_ref, cv_ref, bias_ref, gh_ref, out_ref):
    npast = ck_ref.shape[0]
    rows = slice(0, q_ref.shape[0])
    for h in range(H):
        hs = slice(h * DH, (h + 1) * DH)
        q = q_ref[:, hs].astype(BF16)
        s1 = _dot_nt(q, ck_ref[:, hs].astype(BF16)) * (DH ** -0.5) + bias_ref[h, :, :npast]
        s2 = _dot_nt(q, k_ref[:, hs].astype(BF16)) * (DH ** -0.5) + bias_ref[h, :, npast:]
        m = jnp.maximum(jnp.max(s1, axis=-1, keepdims=True), jnp.max(s2, axis=-1, keepdims=True))
        p1 = jnp.exp(s1 - m)
        p2 = jnp.exp(s2 - m)
        l = jnp.sum(p1, axis=-1, keepdims=True) + jnp.sum(p2, axis=-1, keepdims=True)
        o = (_dot(p1.astype(BF16), cv_ref[:, hs].astype(BF16))
             + _dot(p2.astype(BF16), v_ref[:, hs].astype(BF16))) / l
        _head_norm_store(out_ref, rows, h, o, gh_ref)


def _band_sample(proj3, ck, cv, bias, gh):
    b, s, _ = proj3.shape
    npast = ck.shape[1]
    return pl.pallas_call(
        _band_sample_kernel,
        grid=(b,),
        in_specs=[
            pl.BlockSpec((None, s, GW), lambda i: (i, 0, COL_A // GW)),
            pl.BlockSpec((None, s, GW), lambda i: (i, 0, COL_A // GW + 1)),
            pl.BlockSpec((None, s, GW), lambda i: (i, 0, COL_A // GW + 2)),
            pl.BlockSpec((None, npast, GW), lambda i: (i, 0, 0)),
            pl.BlockSpec((None, npast, GW), lambda i: (i, 0, 0)),
            pl.BlockSpec((H, s, npast + s), lambda i: (0, 0, 0)),
            pl.BlockSpec((1, GW), lambda i: (0, 0)),
        ],
        out_specs=pl.BlockSpec((None, s, GW), lambda i: (i, 0, 0)),
        out_shape=jax.ShapeDtypeStruct((b, s, GW), BF16),
        compiler_params=_cparams("parallel"),
        name="band_sample",
    )(proj3, proj3, proj3, ck, cv, bias, gh)


def _cumsum_cols(x, lo_tri):
    return sum(_dot(lo_tri, part) for part in _split3(x))


def _cumsum_rows(x, up_tri):
    return sum(_dot(part, up_tri) for part in _split3(x))


def _tri_masks(l):
    r = _iota((l, l), 0)
    c = _iota((l, l), 1)
    return r >= c, r > c


def _mlstm_kernel(q_ref, k_ref, v_ref, o_ref, gc_ref, gr_ref, bc_ref, br_ref, gh_ref,
                  c0_ref, n0_ref, m0_ref, out_ref, c_ref, n_ref, m_ref, *, l, nck):
    @pl.when(pl.program_id(1) == 0)
    def _():
        c_ref[...] = c0_ref[...]
        n_ref[...] = n0_ref[...]
        m_ref[...] = m0_ref[...]

    incl, _ = _tri_masks(l)
    lo_tri = incl.astype(BF16)
    up_tri = (_iota((l, l), 0) <= _iota((l, l), 1)).astype(BF16)

    probs = []
    for ck in range(nck):
        rows = slice(ck * l, (ck + 1) * l)
        gcol = gc_ref[rows, :] + bc_ref[...]
        grow = gr_ref[ck] + br_ref[...]
        gcs = _cumsum_cols(_log_sigmoid(gcol), lo_tri)
        grs = _cumsum_rows(_log_sigmoid(grow), up_tri)
        for h in range(H):
            hs = slice(h * DH, (h + 1) * DH)
            ig_c = gcol[:, GATE_OFF + h:GATE_OFF + h + 1]
            g_c = gcs[:, GATE_OFF + H + h:GATE_OFF + H + h + 1]
            ig_r = grow[h:h + 1, :]
            g_r = grs[H + h:H + h + 1, :]
            q = q_ref[rows, hs]
            kf = k_ref[rows, hs] * (DH ** -0.5)
            lmat = jnp.where(incl, g_c - g_r + ig_r, NEG_INF)
            probs.append(dict(
                rows=rows, h=h, q=q, kf=kf, qb=q.astype(BF16), kb=kf.astype(BF16),
                vb=v_ref[rows, hs].astype(BF16), lmat=lmat, lmax=jnp.max(lmat, axis=-1, keepdims=True),
                g_c=g_c, ig_c=ig_c, g_last=g_c[l - 1:l, :]))
    qks = [_dot_nt(p["qb"], p["kb"]) for p in probs]

    ms = [m_ref[:, h:h + 1] for h in range(H)]
    for p in probs:
        m_old = ms[p["h"]]
        p["linter"] = p["g_c"] + m_old
        p["mt"] = jnp.maximum(p["linter"], p["lmax"])
        m_new = p["mt"][l - 1:l, :]
        p["dprev"] = jnp.exp(p["g_last"] + m_old - m_new)
        p["kw"] = p["kf"] * jnp.exp(p["g_last"] - p["g_c"] + p["ig_c"] - m_new)
        ms[p["h"]] = m_new
    ws_ = [qk * jnp.exp(p["lmat"] - p["mt"]) for p, qk in zip(probs, qks)]
    wvs = [_dot(w.astype(BF16), p["vb"]) for p, w in zip(probs, ws_)]
    upds = [_dot_tn(p["kw"].astype(BF16), p["vb"]) for p in probs]

    cs = [c_ref[h] for h in range(H)]
    ns = [n_ref[h:h + 1, :] for h in range(H)]
    qcs, qns = [], []
    for p, upd in zip(probs, upds):
        h = p["h"]
        qcs.append(_dot(p["qb"], cs[h].astype(BF16)))
        qns.append(jnp.sum(p["q"] * ns[h], axis=-1, keepdims=True))
        cs[h] = p["dprev"] * cs[h] + upd
        ns[h] = p["dprev"] * ns[h] + jnp.sum(p["kw"], axis=0, keepdims=True)
    for h in range(H):
        c_ref[h] = cs[h]
        n_ref[h:h + 1, :] = ns[h]
        m_ref[:, h:h + 1] = ms[h]

    wsums = [jnp.sum(w, axis=-1, keepdims=True) for w in ws_]
    obs = []
    for p, wsum, wv, qc, qn in zip(probs, wsums, wvs, qcs, qns):
        hs = slice(p["h"] * DH, (p["h"] + 1) * DH)
        inter = jnp.exp(p["linter"] - p["mt"])
        den = wsum + inter * qn
        hout = (wv + inter * qc) / jnp.maximum(jnp.abs(den), jnp.exp(-p["mt"]))
        obs.append(hout * _sigmoid(o_ref[p["rows"], hs]))
    msq = [jnp.mean(ob * ob, axis=-1, keepdims=True) for ob in obs]
    for p, ob, ms_ in zip(probs, obs, msq):
        hs = slice(p["h"] * DH, (p["h"] + 1) * DH)
        out_ref[p["rows"], hs] = (ob * lax.rsqrt(ms_ + EPS) * gh_ref[:, hs]).astype(out_ref.dtype)


def _mlstm(proj3, gt3, bias_c, bias_r, gh, c0, n0, m0, l, nck):
    b, t, _ = proj3.shape
    steps = t // (l * nck)
    blk = l * nck
    col = lambda j: pl.BlockSpec((None, blk, GW), lambda bi, s: (bi, s, j))
    state = lambda shp: pl.BlockSpec((None,) + shp, lambda bi, s: (bi,) + (0,) * len(shp))
    return pl.pallas_call(
        functools.partial(_mlstm_kernel, l=l, nck=nck),
        grid=(b, steps),
        in_specs=[
            col(COL_B // GW), col(COL_B // GW + 1), col(COL_B // GW + 2), col(COL_B // GW + 3),
            pl.BlockSpec((None, blk, LANES), lambda bi, s: (bi, s, GATE_BLK)),
            pl.BlockSpec((None, nck, N_GATES, l), lambda bi, s: (bi, s, 0, 0)),
            pl.BlockSpec((1, LANES), lambda bi, s: (0, 0)),
            pl.BlockSpec((N_GATES, 1), lambda bi, s: (0, 0)),
            pl.BlockSpec((1, GW), lambda bi, s: (0, 0)),
            state((H, DH, DH)), state((H, DH)), state((1, H)),
        ],
        out_specs=[
            pl.BlockSpec((None, blk, GW), lambda bi, s: (bi, s, 0)),
            state((H, DH, DH)), state((H, DH)), state((1, H)),
        ],
        out_shape=[
            jax.ShapeDtypeStruct((b, t, GW), BF16),
            jax.ShapeDtypeStruct((b, H, DH, DH), F32),
            jax.ShapeDtypeStruct((b, H, DH), F32),
            jax.ShapeDtypeStruct((b, 1, H), F32),
        ],
        compiler_params=_cparams("parallel", "arbitrary"),
        name="mlstm",
    )(proj3, proj3, proj3, proj3, proj3, gt3, bias_c, bias_r, gh, c0, n0, m0)


def _head_of(idx):
    return idx // DH


def _block_mask(n_rows, n_cols):
    return _head_of(_iota((n_rows, n_cols), 0)) == _head_of(_iota((n_rows, n_cols), 1))


def _expander(first_lane):
    r = _iota((LANES, GW), 0)
    c = _iota((LANES, GW), 1)
    return (r == first_lane + _head_of(c)).astype(BF16)


def _dot_stacked(parts, rhs):
    m = parts[0].shape[0]
    y = _dot(jnp.concatenate(parts, axis=0), rhs)
    return sum(y[i * m:(i + 1) * m] for i in range(len(parts)))


def _expand(x, e):
    return _dot_stacked(_split3(x), e)


def _head_sums(a, bones):
    return _dot_stacked(_split3(a), bones)


def _cumsum_cols_wide(x, lo_tri):
    w = x.shape[1]
    y = _dot(lo_tri, jnp.concatenate(_split3(x), axis=1))
    return y[:, :w] + y[:, w:2 * w] + y[:, 2 * w:]


def _row_select(x_t, first_row):
    r = _iota(x_t.shape, 0)
    c = _iota(x_t.shape, 1)
    return jnp.sum(jnp.where(r == first_row + _head_of(c), x_t, 0.0), axis=0, keepdims=True)


def _block_diag_rows(x, mask):
    return jnp.where(mask, jnp.concatenate([x] * H, axis=0), jnp.zeros((), x.dtype))


def _cummax_rows(x):
    rows = _iota(x.shape, 0)
    sh = 1
    while sh < x.shape[0]:
        x = jnp.maximum(x, jnp.where(rows >= sh, pltpu.roll(x, sh, axis=0), NEG_INF))
        sh *= 2
    return x


def _mlstm64_kernel(q_ref, k_ref, v_ref, o_ref, gc_ref, gr_ref, bc_ref, br_ref, gh_ref,
                    c0_ref, n0_ref, m0_ref, out_ref, c_ref, n_ref, m_ref, *, nck):
    l = CHUNK

    @pl.when(pl.program_id(1) == 0)
    def _():
        c_ref[...] = c0_ref[...]
        n_ref[...] = n0_ref[...]
        m_ref[...] = m0_ref[...]

    bmask = _block_mask(GW, GW)
    bones = bmask.astype(BF16)
    within = _iota((GW, GW), 0) % DH <= _iota((GW, GW), 1) % DH
    up_bd = jnp.logical_and(bmask, within).astype(BF16)
    lo_tri = (_iota((l, l), 0) >= _iota((l, l), 1)).astype(BF16)
    incl = _iota((l, GW), 0) >= _iota((l, GW), 1) % DH
    e_i = _expander(GATE_OFF)
    e_f = _expander(GATE_OFF + H)

    cks = []
    for ck in range(nck):
        rows = slice(ck * l, (ck + 1) * l)
        gcol = gc_ref[rows, :] + bc_ref[...]
        gcs = _cumsum_cols_wide(_log_sigmoid(gcol), lo_tri)
        g_c = _expand(gcs, e_f)
        i_c = _expand(gcol, e_i)
        grow = gr_ref[ck] + br_ref[...]
        grow_t = jnp.concatenate([grow] * H, axis=1)
        grs_t = _dot_stacked(_split3(_log_sigmoid(grow_t)), up_bd)
        a_r = _row_select(grow_t, 0) - _row_select(grs_t, H)
        lmat = jnp.where(incl, g_c + a_r, NEG_INF)
        lmax = g_c + _cummax_rows(i_c - g_c)
        q = q_ref[rows, :]
        kf = k_ref[rows, :] * (DH ** -0.5)
        cks.append(dict(rows=rows, g_c=g_c, i_c=i_c, lmat=lmat, lmax=lmax, q=q, kf=kf,
                        qb=q.astype(BF16), kb=kf.astype(BF16), vb=v_ref[rows, :].astype(BF16),
                        g_last=g_c[l - 1:l, :]))
    scs = [_dot_nt(p["qb"], _block_diag_rows(p["kb"], bmask)) for p in cks]

    m_run = m_ref[...]
    for p in cks:
        p["linter"] = p["g_c"] + m_run
        p["mt"] = jnp.maximum(p["linter"], p["lmax"])
        m_new = p["mt"][l - 1:l, :]
        p["dprev"] = jnp.exp(p["g_last"] + m_run - m_new)
        p["kw"] = p["kf"] * jnp.exp(p["g_last"] - p["g_c"] + p["i_c"] - m_new)
        m_run = m_new
    m_ref[...] = m_run
    wbs = [(s * jnp.exp(p["lmat"] - p["mt"])).astype(BF16) for p, s in zip(cks, scs)]
    nums = [_dot(w, _block_diag_rows(p["vb"], bmask)) for p, w in zip(cks, wbs)]
    wsums = [_dot(w, bones) for w in wbs]
    upds = [jnp.where(bmask, _dot_tn(p["kw"].astype(BF16), p["vb"]), 0.0) for p in cks]

    c_run = c_ref[...]
    n_run = n_ref[...]
    qcs, qns = [], []
    for p, upd in zip(cks, upds):
        qcs.append(_dot(p["qb"], c_run.astype(BF16)))
        qns.append(_dot((p["q"] * n_run).astype(BF16), bones))
        c_run = p["dprev"] * c_run + upd
        n_run = p["dprev"] * n_run + jnp.sum(p["kw"], axis=0, keepdims=True)
    c_ref[...] = c_run
    n_ref[...] = n_run

    obs = []
    for p, num, wsum, qc, qn in zip(cks, nums, wsums, qcs, qns):
        inter = jnp.exp(p["linter"] - p["mt"])
        den = wsum + inter * qn
        hout = (num + inter * qc) / jnp.maximum(jnp.abs(den), jnp.exp(-p["mt"]))
        obs.append(hout * _sigmoid(o_ref[p["rows"], :]))
    msq = [_head_sums(ob * ob, bones) * (1.0 / DH) for ob in obs]
    for p, ob, m2 in zip(cks, obs, msq):
        out_ref[p["rows"], :] = (ob * lax.rsqrt(m2 + EPS) * gh_ref[...]).astype(out_ref.dtype)


def _mlstm64(proj3, gt3, bias_c, bias_r, gh, c0, n0, m0, nck):
    b, t, _ = proj3.shape
    blk = CHUNK * nck
    steps = t // blk
    col = lambda j: pl.BlockSpec((None, blk, GW), lambda bi, s: (bi, s, j))
    state = lambda shp: pl.BlockSpec((None,) + shp, lambda bi, s: (bi,) + (0,) * len(shp))
    return pl.pallas_call(
        functools.partial(_mlstm64_kernel, nck=nck),
        grid=(b, steps),
        in_specs=[
            col(COL_B // GW), col(COL_B // GW + 1), col(COL_B // GW + 2), col(COL_B // GW + 3),
            pl.BlockSpec((None, blk, LANES), lambda bi, s: (bi, s, GATE_BLK)),
            pl.BlockSpec((None, nck, N_GATES, CHUNK), lambda bi, s: (bi, s, 0, 0)),
            pl.BlockSpec((1, LANES), lambda bi, s: (0, 0)),
            pl.BlockSpec((N_GATES, 1), lambda bi, s: (0, 0)),
            pl.BlockSpec((1, GW), lambda bi, s: (0, 0)),
            state((GW, GW)), state((1, GW)), state((1, GW)),
        ],
        out_specs=[
            pl.BlockSpec((None, blk, GW), lambda bi, s: (bi, s, 0)),
            state((GW, GW)), state((1, GW)), state((1, GW)),
        ],
        out_shape=[
            jax.ShapeDtypeStruct((b, t, GW), BF16),
            jax.ShapeDtypeStruct((b, GW, GW), F32),
            jax.ShapeDtypeStruct((b, 1, GW), F32),
            jax.ShapeDtypeStruct((b, 1, GW), F32),
        ],
        compiler_params=_cparams("parallel", "arbitrary"),
        name="mlstm64",
    )(proj3, proj3, proj3, proj3, proj3, gt3, bias_c, bias_r, gh, c0, n0, m0)


def _to_block_diag(c):
    b = c.shape[0]
    eye = jnp.eye(H, dtype=c.dtype)
    return jnp.einsum("bhde,hg->bhdge", c, eye).reshape(b, GW, GW)


def _from_block_diag(cbd):
    b = cbd.shape[0]
    c5 = cbd.reshape(b, H, DH, H, DH)
    return jnp.stack([c5[:, h, :, h, :] for h in range(H)], axis=1)


def _split2(x):
    hi = x.astype(BF16)
    lo = (x - hi.astype(F32)).astype(BF16)
    return hi, lo


def _dot_sp(a, b):
    return _dot(a[0], b[0]) + (_dot(a[0], b[1]) + _dot(a[1], b[0]))


def _unit_lower_inverses(nmats, l):
    eye = (_iota((l, l), 0) == _iota((l, l), 1)).astype(F32)
    ps = [eye - n for n in nmats]
    qs = [_split2(n) for n in nmats]
    qs = [_split2(_dot_sp(q, q)) for q in qs]
    power = 2
    while power < l:
        ps = [p + _dot_sp(_split2(p), q) for p, q in zip(ps, qs)]
        power *= 2
        if power < l:
            qs = [_split2(_dot_sp(q, q)) for q in qs]
    return ps


def _l2norm(x):
    return x * lax.rsqrt(jnp.sum(x * x, axis=-1, keepdims=True) + 1e-6)


def _gdn_kernel(x_ref, z_ref, gc_ref, gr_ref, hist_ref, cw_ref, ac_ref, ar_ref, dc_ref, dr_ref,
                gh_ref, s0_ref, out_ref, s_ref, carry_scr, *, l, nck):
    @pl.when(pl.program_id(1) == 0)
    def _():
        s_ref[...] = s0_ref[...]
        carry_scr[...] = hist_ref[...]

    blk = l * nck
    x = x_ref[...]
    ext = jnp.concatenate([carry_scr[...], x], axis=0)
    carry_scr[...] = x[blk - SUBLANES:, :]
    y = x * cw_ref[3:4, :]
    for j in range(1, 4):
        y = y + ext[SUBLANES - j:SUBLANES - j + blk, :] * cw_ref[3 - j:4 - j, :]
    y = y * _sigmoid(y)

    incl, strict = _tri_masks(l)
    lo_tri = incl.astype(BF16)
    up_tri = (_iota((l, l), 0) <= _iota((l, l), 1)).astype(BF16)

    qraw = [y[ck * l:(ck + 1) * l, h * DH:(h + 1) * DH] for ck in range(nck) for h in range(H)]
    kraw = [y[ck * l:(ck + 1) * l, GW + h * DH:GW + (h + 1) * DH] for ck in range(nck) for h in range(H)]
    vraw = [y[ck * l:(ck + 1) * l, 2 * GW + h * DH:2 * GW + (h + 1) * DH] for ck in range(nck) for h in range(H)]
    qnorm = [_l2norm(a) * (DH ** -0.5) for a in qraw]
    knorm = [_l2norm(a) for a in kraw]
    probs = []
    for ck in range(nck):
        rows = slice(ck * l, (ck + 1) * l)
        gcol = gc_ref[rows, :]
        grow = gr_ref[ck]
        beta_cs = _sigmoid(gcol)
        dec_c = -jnp.exp(ac_ref[...]) * _softplus(gcol + dc_ref[...])
        dec_r = -jnp.exp(ar_ref[...]) * _softplus(grow + dr_ref[...])
        gcs = _cumsum_cols(dec_c, lo_tri)
        grs = _cumsum_rows(dec_r, up_tri)
        for h in range(H):
            beta = beta_cs[:, GATE_OFF + 2 * H + h:GATE_OFF + 2 * H + h + 1]
            g_c = gcs[:, GATE_OFF + 3 * H + h:GATE_OFF + 3 * H + h + 1]
            g_r = grs[3 * H + h:3 * H + h + 1, :]
            q, k, v = qnorm[ck * H + h], knorm[ck * H + h], vraw[ck * H + h]
            decay = jnp.exp(jnp.where(incl, g_c - g_r, NEG_INF))
            eg = jnp.exp(g_c)
            g_last = g_c[l - 1:l, :]
            probs.append(dict(
                rows=rows, h=h, qb=q.astype(BF16), kb=k.astype(BF16), beta=beta, decay=decay,
                rhs=jnp.concatenate([v * beta, k * (beta * eg)], axis=-1),
                qeg=(q * eg).astype(BF16), kdec=(k * jnp.exp(g_last - g_c)).astype(BF16),
                sdec=jnp.exp(g_last)))
    kks = [_dot_nt(p["kb"], p["kb"]) for p in probs]
    qks = [_dot_nt(p["qb"], p["kb"]) for p in probs]
    a_lows = [jnp.where(strict, p["beta"] * kk * p["decay"], 0.0) for p, kk in zip(probs, kks)]
    attns = [(qk * p["decay"]).astype(BF16) for p, qk in zip(probs, qks)]
    tinvs = _unit_lower_inverses(a_lows, l)
    sols = [_dot_sp(_split2(t), _split2(p["rhs"])) for t, p in zip(tinvs, probs)]

    states = [s_ref[h] for h in range(H)]
    for ck in range(nck):
        ps = probs[ck * H:(ck + 1) * H]
        ss = sols[ck * H:(ck + 1) * H]
        at = attns[ck * H:(ck + 1) * H]
        sbs = [s.astype(BF16) for s in states]
        wss = [_dot(sol[:, DH:].astype(BF16), sb) for sol, sb in zip(ss, sbs)]
        qss = [_dot(p["qeg"], sb) for p, sb in zip(ps, sbs)]
        vnbs = [(sol[:, :DH] - ws).astype(BF16) for sol, ws in zip(ss, wss)]
        os_ = [qs + _dot(a, vnb) for qs, a, vnb in zip(qss, at, vnbs)]
        states = [p["sdec"] * s + _dot_tn(p["kdec"], vnb) for p, s, vnb in zip(ps, states, vnbs)]
        for p, o in zip(ps, os_):
            hs = slice(p["h"] * DH, (p["h"] + 1) * DH)
            zg = z_ref[p["rows"], hs]
            yo = (o * lax.rsqrt(jnp.mean(o * o, axis=-1, keepdims=True) + EPS) * gh_ref[:, hs]
                  * (zg * _sigmoid(zg)))
            out_ref[p["rows"], hs] = yo.astype(out_ref.dtype)
    for h in range(H):
        s_ref[h] = states[h]


def _gdn(proj3, gt3, hist8, cw, a_c, a_r, dt_c, dt_r, gh, s0, l, nck):
    b, t, _ = proj3.shape
    blk = l * nck
    steps = t // blk
    state = lambda shp: pl.BlockSpec((None,) + shp, lambda bi, s: (bi,) + (0,) * len(shp))
    const = lambda shp: pl.BlockSpec(shp, lambda bi, s: (0,) * len(shp))
    return pl.pallas_call(
        functools.partial(_gdn_kernel, l=l, nck=nck),
        grid=(b, steps),
        in_specs=[
            pl.BlockSpec((None, blk, 3 * GW), lambda bi, s: (bi, s, COL_CX // (3 * GW))),
            pl.BlockSpec((None, blk, GW), lambda bi, s: (bi, s, COL_CZ // GW)),
            pl.BlockSpec((None, blk, LANES), lambda bi, s: (bi, s, GATE_BLK)),
            pl.BlockSpec((None, nck, N_GATES, l), lambda bi, s: (bi, s, 0, 0)),
            state((SUBLANES, 3 * GW)),
            const((4, 3 * GW)),
            const((1, LANES)), const((N_GATES, 1)), const((1, LANES)), const((N_GATES, 1)),
            const((1, GW)),
            state((H, DH, DH)),
        ],
        out_specs=[
            pl.BlockSpec((None, blk, GW), lambda bi, s: (bi, s, 0)),
            state((H, DH, DH)),
        ],
        out_shape=[
            jax.ShapeDtypeStruct((b, t, GW), BF16),
            jax.ShapeDtypeStruct((b, H, DH, DH), F32),
        ],
        scratch_shapes=[pltpu.VMEM((SUBLANES, 3 * GW), F32)],
        compiler_params=_cparams("parallel", "arbitrary"),
        name="gdn",
    )(proj3, proj3, proj3, gt3, hist8, cw, a_c, a_r, dt_c, dt_r, gh, s0)


def _bd_split(x, mask):
    hi, lo = _split2(x)
    return _block_diag_rows(hi, mask), _block_diag_rows(lo, mask)


def _gdn64_kernel(x_ref, z_ref, gc_ref, gr_ref, hist_ref, cw_ref, ac_ref, ar_ref, dc_ref, dr_ref,
                  gh_ref, s0_ref, out_ref, s_ref, carry_scr, *, nck):
    l = CHUNK

    @pl.when(pl.program_id(1) == 0)
    def _():
        s_ref[...] = s0_ref[...]
        carry_scr[...] = hist_ref[...]

    blk = l * nck
    x = x_ref[...]
    ext = jnp.concatenate([carry_scr[...], x], axis=0)
    carry_scr[...] = x[blk - SUBLANES:, :]
    y = x * cw_ref[3:4, :]
    for j in range(1, 4):
        y = y + ext[SUBLANES - j:SUBLANES - j + blk, :] * cw_ref[3 - j:4 - j, :]
    y = y * _sigmoid(y)

    bmask = _block_mask(GW, GW)
    bones = bmask.astype(BF16)
    within = _iota((GW, GW), 0) % DH <= _iota((GW, GW), 1) % DH
    up_bd = jnp.logical_and(bmask, within).astype(BF16)
    lo_tri = (_iota((l, l), 0) >= _iota((l, l), 1)).astype(BF16)
    key_pos = _iota((l, GW), 1) % DH
    incl = _iota((l, GW), 0) >= key_pos
    strict = _iota((l, GW), 0) > key_pos
    eye_t = (_iota((l, GW), 0) == key_pos).astype(F32)
    e_b = _expander(GATE_OFF + 2 * H)
    e_a = _expander(GATE_OFF + 3 * H)

    def head_sums(a):
        return _head_sums(a, bones)

    def shared_rhs(lhs_splits, rhs_bd):
        n = len(lhs_splits)
        big = _dot(jnp.concatenate([part for sp in lhs_splits for part in sp], axis=0), rhs_bd[0])
        small = _dot(jnp.concatenate([sp[0] for sp in lhs_splits], axis=0), rhs_bd[1])
        return [big[2 * i * l:(2 * i + 1) * l] + big[(2 * i + 1) * l:(2 * i + 2) * l]
                + small[i * l:(i + 1) * l] for i in range(n)]

    yq, yk, yv = y[:, :GW], y[:, GW:2 * GW], y[:, 2 * GW:]
    qn_all = yq * lax.rsqrt(head_sums(yq * yq) + 1e-6) * (DH ** -0.5)
    kn_all = yk * lax.rsqrt(head_sums(yk * yk) + 1e-6)

    cks = []
    for ck in range(nck):
        rows = slice(ck * l, (ck + 1) * l)
        gcol = gc_ref[rows, :]
        dec_c = -jnp.exp(ac_ref[...]) * _softplus(gcol + dc_ref[...])
        beta = _expand(_sigmoid(gcol), e_b)
        g_c = _expand(_cumsum_cols_wide(dec_c, lo_tri), e_a)
        grow_t = jnp.concatenate([gr_ref[ck]] * H, axis=1)
        dec_r = -jnp.exp(ar_ref[...]) * _softplus(grow_t + dr_ref[...])
        g_r = _row_select(_dot_stacked(_split3(dec_r), up_bd), 3 * H)
        decay = jnp.exp(jnp.where(incl, g_c - g_r, NEG_INF))
        eg = jnp.exp(g_c)
        g_last = g_c[l - 1:l, :]
        q, k, v = qn_all[rows], kn_all[rows], yv[rows]
        cks.append(dict(rows=rows, qb=q.astype(BF16), kb=k.astype(BF16),
                        beta=beta, decay=decay, rhs_v=v * beta, rhs_k=k * (beta * eg),
                        qeg=(q * eg).astype(BF16), kdec=(k * jnp.exp(g_last - g_c)).astype(BF16),
                        sdec=jnp.exp(g_last)))
    kqs = [_dot_nt(jnp.concatenate([p["kb"], p["qb"]], axis=0), _block_diag_rows(p["kb"], bmask))
           for p in cks]
    nmats = [jnp.where(strict, p["beta"] * kq[:l] * p["decay"], 0.0) for p, kq in zip(cks, kqs)]
    attns = [(kq[l:] * p["decay"]).astype(BF16) for p, kq in zip(cks, kqs)]

    ps = [eye_t - n for n in nmats]
    qs = [shared_rhs([_split2(n)], _bd_split(n, bmask))[0] for n in nmats]
    power = 2
    while power < l:
        power *= 2
        if power < l:
            res = [shared_rhs([_split2(p), _split2(q)], _bd_split(q, bmask)) for p, q in zip(ps, qs)]
            ps = [p + r[0] for p, r in zip(ps, res)]
            qs = [r[1] for r in res]
        else:
            ps = [p + shared_rhs([_split2(p)], _bd_split(q, bmask))[0] for p, q in zip(ps, qs)]
    tsp = [_split2(p) for p in ps]
    us = [shared_rhs([t], _bd_split(p["rhs_v"], bmask))[0] for t, p in zip(tsp, cks)]
    ws = [shared_rhs([t], _bd_split(p["rhs_k"], bmask))[0].astype(BF16) for t, p in zip(tsp, cks)]

    wu = [jnp.concatenate([w, u.astype(BF16)], axis=1) for w, u in zip(ws, us)]
    kwu = [_dot_tn(p["kdec"], x) for p, x in zip(cks, wu)]
    awu = [_dot(at, jnp.concatenate([_block_diag_rows(x[:, :GW], bmask),
                                     _block_diag_rows(x[:, GW:], bmask)], axis=1))
           for at, x in zip(attns, wu)]
    gmats = [jnp.where(bmask, x[:, :GW], 0.0).astype(BF16) for x in kwu]
    bmats = [jnp.where(bmask, x[:, GW:], 0.0) for x in kwu]
    qts = [(p["qeg"].astype(F32) - x[:, :GW]).astype(BF16) for p, x in zip(cks, awu)]

    s_run = s_ref[...]
    outs = []
    for p, g, bm, qt, x in zip(cks, gmats, bmats, qts, awu):
        ys = _dot(jnp.concatenate([g, qt], axis=0), s_run.astype(BF16))
        outs.append(ys[GW:] + x[:, GW:])
        s_run = p["sdec"] * s_run - ys[:GW] + bm
    s_ref[...] = s_run

    msq = [head_sums(o * o) * (1.0 / DH) for o in outs]
    for p, o, m2 in zip(cks, outs, msq):
        zg = z_ref[p["rows"], :]
        yo = o * lax.rsqrt(m2 + EPS) * gh_ref[...] * (zg * _sigmoid(zg))
        out_ref[p["rows"], :] = yo.astype(out_ref.dtype)


def _gdn64(proj3, gt3, hist8, cw, a_c, a_r, dt_c, dt_r, gh, s0, nck):
    b, t, _ = proj3.shape
    blk = CHUNK * nck
    steps = t // blk
    state = lambda shp: pl.BlockSpec((None,) + shp, lambda bi, s: (bi,) + (0,) * len(shp))
    const = lambda shp: pl.BlockSpec(shp, lambda bi, s: (0,) * len(shp))
    return pl.pallas_call(
        functools.partial(_gdn64_kernel, nck=nck),
        grid=(b, steps),
        in_specs=[
            pl.BlockSpec((None, blk, 3 * GW), lambda bi, s: (bi, s, COL_CX // (3 * GW))),
            pl.BlockSpec((None, blk, GW), lambda bi, s: (bi, s, COL_CZ // GW)),
            pl.BlockSpec((None, blk, LANES), lambda bi, s: (bi, s, GATE_BLK)),
            pl.BlockSpec((None, nck, N_GATES, CHUNK), lambda bi, s: (bi, s, 0, 0)),
            state((SUBLANES, 3 * GW)),
            const((4, 3 * GW)),
            const((1, LANES)), const((N_GATES, 1)), const((1, LANES)), const((N_GATES, 1)),
            const((1, GW)),
            state((GW, GW)),
        ],
        out_specs=[
            pl.BlockSpec((None, blk, GW), lambda bi, s: (bi, s, 0)),
            state((GW, GW)),
        ],
        out_shape=[
            jax.ShapeDtypeStruct((b, t, GW), BF16),
            jax.ShapeDtypeStruct((b, GW, GW), F32),
        ],
        scratch_shapes=[pltpu.VMEM((SUBLANES, 3 * GW), F32)],
        compiler_params=_cparams("parallel", "arbitrary"),
        name="gdn64",
    )(proj3, proj3, proj3, gt3, hist8, cw, a_c, a_r, dt_c, dt_r, gh, s0)


HEAD_PAD = 128
DPAD = H * HEAD_PAD


def _tile_heads(t):
    return jnp.concatenate([t] * H, axis=-1)


def _dprep_kernel(tail_ref, gq_ref, gkv_ref, wq_ref, wqp_ref, qcos_ref, qsin_ref, ka_ref, kb_ref,
                  ckv_ref, kpe_ref, qc_ref):
    hq = _rms(tail_ref[:, :Q_LORA], gq_ref[...]).astype(BF16)
    qc = (_dot(hq, wq_ref[...]) * _tile_heads(qcos_ref[...])
          + _dot(hq, wqp_ref[...]) * _tile_heads(qsin_ref[...]))
    qc_ref[...] = (qc * (MLA_SCALE * LOG2E)).astype(BF16)
    ckv_ref[...] = _rms(tail_ref[:, Q_LORA:Q_LORA + KV_LORA], gkv_ref[...])
    kr = tail_ref[:, Q_LORA + KV_LORA:]
    kpe = kr * ka_ref[...] + pltpu.roll(kr, 64, axis=1) * kb_ref[...]
    kpe_ref[...] = kpe[:, :QK_ROPE]


def _dprep(proj, gq, gkv, wq, wqp, qcos, qsin, ka, kb, tm):
    n = proj.shape[0]
    row = lambda w: pl.BlockSpec((tm, w), lambda i: (i, 0))
    const = lambda a, b: pl.BlockSpec((a, b), lambda i: (0, 0))
    return pl.pallas_call(
        _dprep_kernel,
        grid=(n // tm,),
        in_specs=[
            pl.BlockSpec((tm, TAIL_W), lambda i: (i, COL_TAIL // TAIL_W)),
            const(1, Q_LORA), const(1, KV_LORA), const(Q_LORA, DPAD), const(Q_LORA, DPAD),
            row(HEAD_PAD), row(HEAD_PAD), row(LANES), row(LANES),
        ],
        out_specs=[row(KV_LORA), row(QK_ROPE), row(DPAD)],
        out_shape=[
            jax.ShapeDtypeStruct((n, KV_LORA), F32),
            jax.ShapeDtypeStruct((n, QK_ROPE), F32),
            jax.ShapeDtypeStruct((n, DPAD), BF16),
        ],
        compiler_params=_cparams("parallel"),
        name="dprep",
    )(proj, gq, gkv, wq, wqp, qcos, qsin, ka, kb)


def _kvup_kernel(ckv_ref, kpe_ref, wk_ref, wv_ref, pm_ref, one_ref, kc_ref, vp_ref):
    c = ckv_ref[...].astype(BF16)
    kc_ref[...] = (_dot(c, wk_ref[...]) + _dot(kpe_ref[...].astype(BF16), pm_ref[...])).astype(BF16)
    vp_ref[...] = (_dot(c, wv_ref[...]) + one_ref[...]).astype(BF16)


def _kvup(ckv, kpe, wk, wv, pm, ones, tm):
    m = ckv.shape[0]
    row = lambda w: pl.BlockSpec((tm, w), lambda i: (i, 0))
    const = lambda a, b: pl.BlockSpec((a, b), lambda i: (0, 0))
    return pl.pallas_call(
        _kvup_kernel,
        grid=(m // tm,),
        in_specs=[row(KV_LORA), row(QK_ROPE), const(KV_LORA, DPAD), const(KV_LORA, DPAD),
                  const(QK_ROPE, DPAD), const(1, DPAD)],
        out_specs=[row(DPAD), row(DPAD)],
        out_shape=[jax.ShapeDtypeStruct((m, DPAD), BF16), jax.ShapeDtypeStruct((m, DPAD), BF16)],
        compiler_params=_cparams("parallel"),
        name="kvup",
    )(ckv, kpe, wk, wv, pm, ones)


def _mla_finish(out_ref, gh_ref, h, acc):
    o = acc[:, :DH] / acc[:, DH:DH + 1]
    rows = slice(0, o.shape[0])
    _head_norm_store(out_ref, rows, h, o, gh_ref)


LOG2E = 1.4426950408889634


def _kvup_t_kernel(ckv_ref, kpe_ref, wk_ref, wvt_ref, pm_ref, onet_ref, kc_ref, vt_ref):
    c = ckv_ref[...].astype(BF16)
    kc_ref[...] = (_dot(c, wk_ref[...]) + _dot(kpe_ref[...].astype(BF16), pm_ref[...])).astype(BF16)
    vt_ref[...] = (_dot_nt(wvt_ref[...], c) + onet_ref[...]).astype(BF16)


def _kvup_t(ckv, kpe, wk, wvt, pm, onet, tm):
    m = ckv.shape[0]
    row = lambda w: pl.BlockSpec((tm, w), lambda i: (i, 0))
    const = lambda a, b: pl.BlockSpec((a, b), lambda i: (0, 0))
    return pl.pallas_call(
        _kvup_t_kernel,
        grid=(m // tm,),
        in_specs=[row(KV_LORA), row(QK_ROPE), const(KV_LORA, DPAD), const(DPAD, KV_LORA),
                  const(QK_ROPE, DPAD), const(DPAD, 1)],
        out_specs=[row(DPAD), pl.BlockSpec((DPAD, tm), lambda i: (0, i))],
        out_shape=[jax.ShapeDtypeStruct((m, DPAD), BF16), jax.ShapeDtypeStruct((DPAD, m), BF16)],
        compiler_params=_cparams("parallel"),
        name="kvup_t",
    )(ckv, kpe, wk, wvt, pm, onet)


def _mla_prompt_kernel(qi_ref, ki_ref, q_ref, k_ref, vt_ref, ghc_ref, out_ref, m_scr, acc_scr, *, bq, bk,
                       qw, ahead):
    p = pl.program_id(0)
    q_i = qi_ref[p]
    k_i = ki_ref[p]
    last = (q_i * bq) // bk

    @pl.when(k_i == 0)
    def _():
        m_scr[...] = jnp.full(m_scr.shape, NEG_INF, F32)
        acc_scr[...] = jnp.zeros(acc_scr.shape, F32)

    def step(diag):
        if diag:
            key_chunk = k_i * (bk // CHUNK) + _iota((bk, bq), 0) // CHUNK
            qry_chunk = q_i * (bq // CHUNK) + _iota((bk, bq), 1) // CHUNK
            allowed = key_chunk <= qry_chunk

        units = [(h, c) for h in range(H) for c in range(bq // qw)]
        rc = min(bk, 64)

        def scores(u):
            h, c = u
            hs = slice(h * HEAD_PAD, (h + 1) * HEAD_PAD)
            return _dot_nt(k_ref[:, hs], q_ref[c * qw:(c + 1) * qw, hs])

        def update(u, st):
            h, c = u
            hs = slice(h * HEAD_PAD, (h + 1) * HEAD_PAD)
            qs = slice(c * qw, (c + 1) * qw)
            if diag:
                st = jnp.where(allowed[:, qs], st, NEG_INF)
            m_prev = m_scr[h, :, qs]
            mx = st[:rc]
            for r in range(1, bk // rc):
                mx = jnp.maximum(mx, st[r * rc:(r + 1) * rc])
            m_new = jnp.maximum(m_prev, jnp.max(mx, axis=0, keepdims=True))
            alpha = jnp.exp2(m_prev - m_new)[0:1]
            m_row = m_new[0:1]
            pt = jnp.concatenate([jnp.exp2(st[r * rc:(r + 1) * rc] - m_row).astype(BF16)
                                  for r in range(bk // rc)], axis=0)
            acc = alpha * acc_scr[h, :, qs] + _dot(vt_ref[hs, :], pt)
            if not diag:
                m_scr[h, :, qs] = m_new
                acc_scr[h, :, qs] = acc
            return acc

        accs = []
        pending = [scores(u) for u in units[:ahead]]
        for idx, u in enumerate(units):
            if idx + ahead < len(units):
                pending.append(scores(units[idx + ahead]))
            accs.append(update(u, pending.pop(0)))
        per_head = bq // qw
        return [jnp.concatenate(accs[h * per_head:(h + 1) * per_head], axis=1) for h in range(H)]

    @pl.when(k_i < last)
    def _():
        step(False)

    @pl.when(k_i == last)
    def _():
        ys = []
        for h, acc in enumerate(step(True)):
            o = acc[:DH] / acc[DH:DH + 1]
            ms = jnp.mean(o * o, axis=0, keepdims=True)
            ys.append(o * lax.rsqrt(ms + EPS) * ghc_ref[h * DH:(h + 1) * DH, :])
        out_ref[...] = jnp.concatenate(ys, axis=0).T.astype(out_ref.dtype)


def _mla_prompt(qc, kc, vt, ghc, bq, bk):
    t = qc.shape[0]
    assert t % bq == 0 and t % bk == 0 and bk % bq == 0
    pairs = [(i, j) for i in range(t // bq) for j in range((i * bq) // bk + 1)]
    qi = jnp.asarray([i for i, _ in pairs], jnp.int32)
    ki = jnp.asarray([j for _, j in pairs], jnp.int32)
    grid_spec = pltpu.PrefetchScalarGridSpec(
        num_scalar_prefetch=2,
        grid=(len(pairs),),
        in_specs=[
            pl.BlockSpec((bq, DPAD), lambda p, qi, ki: (qi[p], 0)),
            pl.BlockSpec((bk, DPAD), lambda p, qi, ki: (ki[p], 0)),
            pl.BlockSpec((DPAD, bk), lambda p, qi, ki: (0, ki[p])),
            pl.BlockSpec((GW, 1), lambda p, qi, ki: (0, 0)),
        ],
        out_specs=pl.BlockSpec((bq, GW), lambda p, qi, ki: (qi[p], 0)),
        scratch_shapes=[
            pltpu.VMEM((H, SUBLANES, bq), F32),
            pltpu.VMEM((H, HEAD_PAD, bq), F32),
        ],
    )
    return pl.pallas_call(
        functools.partial(_mla_prompt_kernel, bq=bq, bk=bk, qw=min(bq, 256), ahead=2),
        grid_spec=grid_spec,
        out_shape=jax.ShapeDtypeStruct((t, GW), BF16),
        compiler_params=_cparams("arbitrary"),
        name="mla_prompt",
    )(qi, ki, qc, kc, vt, ghc)


def _mla_sample_kernel(q_ref, k_ref, v_ref, gh_ref, out_ref):
    for h in range(H):
        hs = slice(h * HEAD_PAD, (h + 1) * HEAD_PAD)
        s = _dot_nt(q_ref[:, hs], k_ref[:, hs])
        m = jnp.max(s, axis=-1, keepdims=True)
        pr = jnp.exp2(s - m)
        acc = _dot(pr.astype(BF16), v_ref[:, hs])
        _mla_finish(out_ref, gh_ref, h, acc)


def _mla_sample(qc3, kc3, vp3, gh):
    b, s, _ = qc3.shape
    nk = kc3.shape[1]
    return pl.pallas_call(
        _mla_sample_kernel,
        grid=(b,),
        in_specs=[
            pl.BlockSpec((None, s, DPAD), lambda i: (i, 0, 0)),
            pl.BlockSpec((None, nk, DPAD), lambda i: (i, 0, 0)),
            pl.BlockSpec((None, nk, DPAD), lambda i: (i, 0, 0)),
            pl.BlockSpec((1, GW), lambda i: (0, 0)),
        ],
        out_specs=pl.BlockSpec((None, s, GW), lambda i: (i, 0, 0)),
        out_shape=jax.ShapeDtypeStruct((b, s, GW), BF16),
        compiler_params=_cparams("parallel"),
        name="mla_sample",
    )(qc3, kc3, vp3, gh)


def _outproj_kernel(x_ref, a_ref, b_ref, c_ref, d_ref, w_ref, out_ref):
    acc = x_ref[...]
    for g, m_ref in enumerate((a_ref, b_ref, c_ref, d_ref)):
        acc = acc + _dot(m_ref[...], w_ref[g * GW:(g + 1) * GW, :])
    out_ref[...] = acc


def _outproj(x, ma, mb, mc, md, w, tm):
    n = x.shape[0]
    mix = pl.BlockSpec((tm, GW), lambda i: (i, 0))
    return pl.pallas_call(
        _outproj_kernel,
        grid=(n // tm,),
        in_specs=[pl.BlockSpec((tm, D_MODEL), lambda i: (i, 0)), mix, mix, mix, mix,
                  pl.BlockSpec((D_MODEL, D_MODEL), lambda i: (0, 0))],
        out_specs=pl.BlockSpec((tm, D_MODEL), lambda i: (i, 0)),
        out_shape=jax.ShapeDtypeStruct((n, D_MODEL), F32),
        compiler_params=_cparams("parallel"),
        name="outproj",
    )(x, ma, mb, mc, md, w)


def _ffn_kernel(*refs, seq_len, final_norm, tf):
    if seq_len is None:
        (x_ref, g_ref, wup_ref, cw_ref, wd_ref, gf_ref, out_ref, ga_ref, act_scr, carry_scr) = refs
    else:
        (x_ref, g_ref, wup_ref, cw_ref, wd_ref, gf_ref, h1_ref, h2_ref, out_ref, ga_ref, act_scr) = refs
    tm = x_ref.shape[0]
    nj = D_FF // tf
    h = _rms(x_ref[...], g_ref[...]).astype(BF16)
    row = _iota((tm, tf), 0)

    if seq_len is None:
        @pl.when(pl.program_id(0) == 0)
        def _():
            carry_scr[...] = jnp.zeros(carry_scr.shape, F32)

    def up(j):
        cols = slice(j * tf, (j + 1) * tf)
        ucols = slice(D_FF + j * tf, D_FF + (j + 1) * tf)
        return _dot(h, wup_ref[:, cols]), _dot(h, wup_ref[:, ucols])

    def gate(j, ga, u):
        cols = slice(j * tf, (j + 1) * tf)
        r1 = pltpu.roll(ga, 1, axis=0)
        r2 = pltpu.roll(ga, 2, axis=0)
        if seq_len is None:
            c1 = carry_scr[SUBLANES - 1:SUBLANES, cols]
            c2 = carry_scr[SUBLANES - 2:SUBLANES - 1, cols]
            prev1 = jnp.where(row >= 1, r1, c1)
            prev2 = jnp.where(row >= 2, r2, jnp.where(row == 1, c1, c2))
            tail = ga[tm - SUBLANES:, :]
            carry_scr[:, cols] = tail
            ga_ref[:, cols] = tail
        else:
            t = row % seq_len
            prev1 = jnp.where(t >= 1, r1, h1_ref[:, cols])
            prev2 = jnp.where(t >= 2, r2, h2_ref[:, cols])
            ga_ref[:, cols] = ga
        conv = prev2 * cw_ref[0:1, cols] + prev1 * cw_ref[1:2, cols] + ga * cw_ref[2:3, cols]
        act_scr[:, cols] = (conv * _sigmoid(conv) * u).astype(BF16)

    pending = up(0)
    for j in range(nj):
        nxt = up(j + 1) if j + 1 < nj else None
        gate(j, *pending)
        pending = nxt
    y = x_ref[...] + _dot(act_scr[...], wd_ref[...])
    if final_norm:
        y = _rms(y, gf_ref[...])
    out_ref[...] = y


def _ffn(x, g, w_up, cw, w_down, gf, h1, h2, *, tm, tf, seq_len, final_norm):
    n = x.shape[0]
    ni = n // tm
    resident = lambda a, b: pl.BlockSpec((a, b), lambda i: (0, 0), pipeline_mode=pl.Buffered(1))
    in_specs = [
        pl.BlockSpec((tm, D_MODEL), lambda i: (i, 0)),
        resident(1, D_MODEL),
        resident(D_MODEL, 2 * D_FF),
        resident(3, D_FF),
        resident(D_FF, D_MODEL),
        resident(1, D_MODEL),
    ]
    args = [x, g, w_up, cw, w_down, gf]
    scratch = [pltpu.VMEM((tm, D_FF), BF16)]
    if seq_len is None:
        ga_spec = pl.BlockSpec((None, SUBLANES, D_FF), lambda i: (i, 0, 0))
        ga_shape = jax.ShapeDtypeStruct((ni, SUBLANES, D_FF), F32)
        scratch.append(pltpu.VMEM((SUBLANES, D_FF), F32))
    else:
        in_specs += [pl.BlockSpec((tm, D_FF), lambda i: (i, 0))] * 2
        args += [h1, h2]
        ga_spec = pl.BlockSpec((tm, D_FF), lambda i: (i, 0))
        ga_shape = jax.ShapeDtypeStruct((n, D_FF), F32)
    return pl.pallas_call(
        functools.partial(_ffn_kernel, seq_len=seq_len, final_norm=final_norm, tf=tf),
        grid=(ni,),
        in_specs=in_specs,
        out_specs=[pl.BlockSpec((tm, D_MODEL), lambda i: (i, 0)), ga_spec],
        out_shape=[jax.ShapeDtypeStruct((n, D_MODEL), F32), ga_shape],
        scratch_shapes=scratch,
        compiler_params=_cparams("arbitrary"),
        name="ffn",
    )(*args)


def _rope_tables(pos):
    half = QK_ROPE // 2
    inv = ROPE_THETA ** (-jnp.arange(half, dtype=F32) / half)
    ang = pos.astype(F32)[:, None] * inv[None, :]
    cos, sin = jnp.cos(ang), jnp.sin(ang)
    cos2 = jnp.concatenate([cos, cos], -1)
    sin2 = jnp.concatenate([-sin, sin], -1)
    n = pos.shape[0]
    z32 = jnp.zeros((n, 32), F32)
    qcos = jnp.concatenate([jnp.ones((n, QK_NOPE), F32), cos2, z32], -1)
    qsin = jnp.concatenate([jnp.zeros((n, QK_NOPE), F32), sin2, z32], -1)
    ka = jnp.concatenate([cos2, z32, sin2, z32], -1)
    kb = jnp.concatenate([sin2, z32, cos2, z32], -1)
    return qcos, qsin, ka, kb


def _rel_bias(table, n_past, n_q, n_k):
    dmax = n_past + n_q - 1
    dmin = n_past - n_k + 1
    diag = table[:, jnp.clip(jnp.arange(dmax, dmin - 1, -1), -REL_MAX, REL_MAX) + REL_MAX]
    return jnp.stack([diag[:, n_q - 1 - i:n_q - 1 - i + n_k] for i in range(n_q)], axis=1)


def _swap_halves(w):
    half = w.shape[-1] // 2
    return jnp.concatenate([w[..., half:], w[..., :half]], -1)


def _layer_weights(lw):
    (g_mix, w_in, a_rel_bias, b_i_bias, b_f_bias, c_conv_w, c_a_log, c_dt_bias,
     d_g_q, d_w_q_up, d_g_kv, d_w_kv_up, g_head, w_out, g_ffn, w_up, f_conv_w, w_down) = lw
    o = 0
    cols = {}
    for name, size in (("a", 3 * GW), ("b", 4 * GW), ("bg", 2 * H), ("c", 3 * GW), ("cz", GW),
                       ("cg", 2 * H), ("dq", Q_LORA), ("dkv", KV_LORA), ("dkr", QK_ROPE)):
        cols[name] = w_in[:, o:o + size]
        o += size
    gates = jnp.concatenate([cols["bg"], cols["cg"]], -1)
    pad16 = jnp.zeros((D_MODEL, 16), F32)
    pad32 = jnp.zeros((D_MODEL, 32), F32)
    w_perm = jnp.concatenate([cols["c"], cols["a"], cols["dq"], cols["dkv"], cols["dkr"], gates, pad16,
                              _swap_halves(cols["dkr"]), pad32, cols["b"], cols["cz"]], -1)
    zc = lambda n: jnp.zeros((1, n), F32)
    zr = lambda n: jnp.zeros((n, 1), F32)
    bias_c = jnp.concatenate([zc(GATE_OFF), b_i_bias[None], b_f_bias[None], zc(LANES - GATE_OFF - 2 * H)], -1)
    bias_r = jnp.concatenate([b_i_bias[:, None], b_f_bias[:, None], zr(2 * H)], 0)
    alog_c = jnp.concatenate([zc(GATE_OFF + 3 * H), c_a_log[None], zc(LANES - GATE_OFF - 4 * H)], -1)
    alog_r = jnp.concatenate([zr(3 * H), c_a_log[:, None]], 0)
    dt_c = jnp.concatenate([zc(GATE_OFF + 3 * H), c_dt_bias[None], zc(LANES - GATE_OFF - 4 * H)], -1)
    dt_r = jnp.concatenate([zr(3 * H), c_dt_bias[:, None]], 0)

    wq = d_w_q_up.reshape(Q_LORA, H, QK_NOPE + QK_ROPE)
    z_h32 = jnp.zeros((Q_LORA, H, 32), F32)
    wq_full = jnp.concatenate([wq, z_h32], -1).reshape(Q_LORA, DPAD)
    wq_part = jnp.concatenate([jnp.zeros((Q_LORA, H, QK_NOPE), F32), _swap_halves(wq[..., QK_NOPE:]), z_h32],
                              -1).reshape(Q_LORA, DPAD)
    wkv = d_w_kv_up.reshape(KV_LORA, H, 2 * DH)
    z_h64 = jnp.zeros((KV_LORA, H, DH), F32)
    wk_full = jnp.concatenate([wkv[..., :DH], z_h64], -1).reshape(KV_LORA, DPAD)
    wv_full = jnp.concatenate([wkv[..., DH:], z_h64], -1).reshape(KV_LORA, DPAD)
    place = jnp.concatenate([jnp.zeros((QK_ROPE, QK_NOPE), F32), jnp.eye(QK_ROPE, dtype=F32),
                             jnp.zeros((QK_ROPE, 32), F32)], -1)
    pmat = jnp.concatenate([place] * H, -1)
    return dict(
        g_mix=g_mix[None], w_in=w_perm.astype(BF16), w_gt=gates.T.astype(BF16),
        table=a_rel_bias, bias_c=bias_c, bias_r=bias_r, alog_c=alog_c, alog_r=alog_r, dt_c=dt_c, dt_r=dt_r,
        c_conv_w=c_conv_w, g_q=d_g_q[None], g_kv=d_g_kv[None],
        wq=wq_full.astype(BF16), wqp=wq_part.astype(BF16), wk=wk_full.astype(BF16), wv=wv_full.astype(BF16),
        pmat=pmat.astype(BF16),
        vones=(jnp.arange(DPAD) % HEAD_PAD == DH).astype(F32)[None],
g_head=g_head.reshape(4, 1, GW), w_out=w_out.astype(BF16),
        g_ffn=g_ffn[None], w_up=w_up.astype(BF16), f_conv_w=f_conv_w, w_down=w_down.astype(BF16))


def _gates_t3(gt, b, t, l):
    return gt.reshape(N_GATES, b, t // l, l).transpose(1, 2, 0, 3)


def _layer(x, offset, st, w, gf, final_norm, cfg):
    b, t, _ = x.shape
    n = b * t
    first = st is None
    x2 = x.reshape(n, D_MODEL)
    proj, gt = _inproj(x2, w["g_mix"], w["w_in"], w["w_gt"], cfg["tm"], cfg["tn"])
    proj3 = proj.reshape(b, t, PROJ_W)
    gh = w["g_head"]
    l = min(t, CHUNK)
    gt3 = _gates_t3(gt, b, t, l)

    new_ak = proj3[:, t - min(A_PAST, t):, COL_A + GW:COL_A + 2 * GW].reshape(b, -1, H, DH)
    new_av = proj3[:, t - min(A_PAST, t):, COL_A + 2 * GW:COL_A + 3 * GW].reshape(b, -1, H, DH)
    if first:
        bias = _rel_bias(w["table"], A_PAST, CHUNK, A_PAST + CHUNK)
        oa = _band_prompt(proj, bias, gh[0])
    else:
        npast = st[0].shape[1]
        bias = _rel_bias(w["table"], npast, t, npast + t)
        oa = _band_sample(proj3, st[0].reshape(b, npast, GW), st[1].reshape(b, npast, GW), bias, gh[0])
        oa = oa.reshape(n, GW)

    if first:
        c0 = jnp.zeros((b, H, DH, DH), F32)
        n0 = jnp.zeros((b, H, DH), F32)
        m0 = jnp.zeros((b, 1, H), F32)
    else:
        c0, n0, m0 = st[2], st[3], st[4][:, None, :]
    if l == CHUNK:
        ob, cbd, nrow, mrow = _mlstm64(proj3, gt3, w["bias_c"], w["bias_r"], gh[1], _to_block_diag(c0),
                                       n0.reshape(b, 1, GW), jnp.repeat(m0, DH, axis=-1), cfg["nck"])
        new_bc, new_bn, new_bm = _from_block_diag(cbd), nrow.reshape(b, H, DH), mrow[:, 0, ::DH]
    else:
        ob, new_bc, new_bn, new_bm = _mlstm(proj3, gt3, w["bias_c"], w["bias_r"], gh[1], c0, n0, m0,
                                            l, cfg["nck"])
        new_bm = new_bm[:, 0, :]

    if first:
        hist8 = jnp.zeros((b, SUBLANES, 3 * GW), F32)
        s0 = jnp.zeros((b, H, DH, DH), F32)
    else:
        hist8 = jnp.concatenate([jnp.zeros((b, SUBLANES - 3, 3 * GW), F32), st[6]], 1)
        s0 = st[5]
    gdn_args = (proj3, gt3, hist8, w["c_conv_w"], w["alog_c"], w["alog_r"], w["dt_c"], w["dt_r"], gh[2])
    if l == CHUNK:
        oc, sbd = _gdn64(*gdn_args, _to_block_diag(s0), cfg["nck"])
        new_cs = _from_block_diag(sbd)
    else:
        oc, new_cs = _gdn(*gdn_args, s0, l, cfg["nck"])
    new_cconv = proj3[:, t - 3:, COL_CX:COL_CX + 3 * GW]

    pos = jnp.arange(t, dtype=jnp.int32) + offset
    qcos, qsin, ka, kb = (jnp.tile(a, (b, 1)) for a in _rope_tables(pos))
    ckv, kpe, qc = _dprep(proj, w["g_q"], w["g_kv"], w["wq"], w["wqp"], qcos, qsin, ka, kb, cfg["tm_d"])
    if first:
        kc, vt = _kvup_t(ckv, kpe, w["wk"], w["wv"].T, w["pmat"], w["vones"].T, cfg["tm_kv"])
        od = _mla_prompt(qc, kc, vt, gh[3].T, cfg["mla_bq"], cfg["mla_bk"])
    else:
        ckv_all = jnp.concatenate([st[7], ckv.reshape(b, t, KV_LORA)], 1)
        kpe_all = jnp.concatenate([st[8], kpe.reshape(b, t, QK_ROPE)], 1)
        nk = ckv_all.shape[1]
        kc, vp = _kvup(ckv_all.reshape(b * nk, KV_LORA), kpe_all.reshape(b * nk, QK_ROPE),
                       w["wk"], w["wv"], w["pmat"], w["vones"], nk)
        od = _mla_sample(qc.reshape(b, t, DPAD), kc.reshape(b, nk, DPAD), vp.reshape(b, nk, DPAD), gh[3])
        od = od.reshape(n, GW)

    x2 = _outproj(x2, oa, ob.reshape(n, GW), oc.reshape(n, GW), od, w["w_out"], cfg["tm"])

    if first:
        y, ga_tail = _ffn(x2, w["g_ffn"], w["w_up"], w["f_conv_w"], w["w_down"], gf, None, None,
                          tm=cfg["tm"], tf=cfg["tf"], seq_len=None, final_norm=final_norm)
        new_fconv = ga_tail[-1, SUBLANES - 2:, :][None]
    else:
        hist = st[9]
        zrow = jnp.zeros((b, t - 1, D_FF), F32)
        h1 = jnp.concatenate([hist[:, 1:2], zrow], 1).reshape(n, D_FF)
        h2 = jnp.concatenate([hist, zrow[:, 1:]], 1).reshape(n, D_FF)
        y, ga = _ffn(x2, w["g_ffn"], w["w_up"], w["f_conv_w"], w["w_down"], gf, h1, h2,
                     tm=cfg["tm"], tf=cfg["tf"], seq_len=t, final_norm=final_norm)
        new_fconv = ga.reshape(b, t, D_FF)[:, t - 2:]
    state = (new_ak, new_av, new_bc, new_bn, new_bm, new_cs, new_cconv,
             ckv.reshape(b, t, KV_LORA), kpe.reshape(b, t, QK_ROPE), new_fconv)
    return y.reshape(b, t, D_MODEL), state


def _config(b, t):
    n = b * t
    tm = min(n, 1024)
    return dict(tm=tm, tn=PROJ_W // 2, tf=256, nck=1 if t <= CHUNK else 8,
                tm_d=min(n, 1024), tm_kv=min(n, 2048), mla_bq=min(t, 512), mla_bk=min(t, 1024))


def kernel(x_prompt, x_sample, cache_a_k, cache_a_v, state_b_c, state_b_n, state_b_m, state_c_s, cache_c_conv, cache_d_ckv, cache_d_kpe, cache_ffn_conv, g_mix, w_in, a_rel_bias, b_i_bias, b_f_bias, c_conv_w, c_a_log, c_dt_bias, d_g_q, d_w_q_up, d_g_kv, d_w_kv_up, g_head, w_out, g_ffn, w_up, f_conv_w, w_down, g_final):
    layer_w = (g_mix, w_in, a_rel_bias, b_i_bias, b_f_bias, c_conv_w, c_a_log, c_dt_bias,
               d_g_q, d_w_q_up, d_g_kv, d_w_kv_up, g_head, w_out, g_ffn, w_up, f_conv_w, w_down)
    depth = g_mix.shape[0]
    past = cache_d_ckv.shape[2]
    xp, xs = x_prompt, x_sample
    cfg_p = _config(*x_prompt.shape[:2])
    cfg_s = _config(*x_sample.shape[:2])
    gf = g_final[None]
    new_p, new_s = [], []
    for l in range(depth):
        w = _layer_weights(tuple(a[l] for a in layer_w))
        last = l == depth - 1
        xp, sp_l = _layer(xp, 0, None, w, gf, last, cfg_p)
        st = (cache_a_k[l], cache_a_v[l], state_b_c[l], state_b_n[l], state_b_m[l],
              state_c_s[l], cache_c_conv[l], cache_d_ckv[l], cache_d_kpe[l], cache_ffn_conv[l])
        xs, ss_l = _layer(xs, past, st, w, gf, last, cfg_s)
        new_p.append(sp_l)
        new_s.append(ss_l)
    outs = [xp, xs]
    for i in range(10):
        outs.append(jnp.stack([s[i] for s in new_p]))
        outs.append(jnp.stack([s[i] for s in new_s]))
    return tuple(outs)
```

```python
import functools
import math

import jax
import jax.numpy as jnp
from jax import lax
from jax.experimental import pallas as pl
from jax.experimental.pallas import tpu as pltpu

F32 = jnp.float32
BF16 = jnp.bfloat16

D_MODEL = 1024
CHUNK = 64
H = 4
DH = 64
GW = H * DH
A_PAST = 8 * CHUNK
REL_MAX = 2 * CHUNK
Q_LORA = 256
KV_LORA = 128
QK_NOPE = 64
QK_ROPE = 32
ROPE_THETA = 10000.0
MLA_SCALE = (QK_NOPE + QK_ROPE) ** -0.5
D_FF = 2816
EPS = 1e-6

COL_CX = 0
COL_A = 3 * GW
COL_TAIL = 6 * GW
TAIL_W = 512
COL_B = COL_TAIL + TAIL_W
COL_CZ = COL_B + 4 * GW
PROJ_W = COL_CZ + GW
GATE_BLK = (COL_TAIL + 384) // 128
GATE_OFF = 32
N_GATES = 16

LANES = 128
SUBLANES = 8
VMEM_LIMIT = 56 * 1024 * 1024

NEG_INF = float("-inf")


def _cparams(*sem):
    return pltpu.CompilerParams(dimension_semantics=sem, vmem_limit_bytes=VMEM_LIMIT)


def _dot(a, b):
    return jnp.dot(a, b, preferred_element_type=F32)


def _dot_nt(a, b):
    return lax.dot_general(a, b, (((1,), (1,)), ((), ())), preferred_element_type=F32)


def _dot_tn(a, b):
    return lax.dot_general(a, b, (((0,), (0,)), ((), ())), preferred_element_type=F32)


def _split3(x):
    hi = x.astype(BF16)
    r1 = x - hi.astype(F32)
    mid = r1.astype(BF16)
    lo = (r1 - mid.astype(F32)).astype(BF16)
    return hi, mid, lo


def _rms(x, g):
    return x * lax.rsqrt(jnp.mean(x * x, axis=-1, keepdims=True) + EPS) * g


def _log_sigmoid(x):
    return jnp.minimum(x, 0.0) - jnp.log1p(jnp.exp(-jnp.abs(x)))


def _softplus(x):
    return jnp.maximum(x, 0.0) + jnp.log1p(jnp.exp(-jnp.abs(x)))


def _sigmoid(x):
    return 1.0 / (1.0 + jnp.exp(-x))


def _iota(shape, dim):
    return lax.broadcasted_iota(jnp.int32, shape, dim)


def _inproj_kernel(x_ref, g_ref, w_ref, wgt_ref, proj_ref, gt_ref):
    h = _rms(x_ref[...], g_ref[...]).astype(BF16)
    gt_ref[...] = _dot_nt(wgt_ref[...], h)
    proj_ref[...] = _dot(h, w_ref[...])


def _inproj(x, g, w, wgt, tm):
    n = x.shape[0]
    resident = lambda a, b: pl.BlockSpec((a, b), lambda i: (0, 0), pipeline_mode=pl.Buffered(1))
    return pl.pallas_call(
        _inproj_kernel,
        grid=(n // tm,),
        in_specs=[
            pl.BlockSpec((tm, D_MODEL), lambda i: (i, 0)),
            resident(1, D_MODEL),
            resident(D_MODEL, PROJ_W),
            resident(N_GATES, D_MODEL),
        ],
        out_specs=[
            pl.BlockSpec((tm, PROJ_W), lambda i: (i, 0)),
            pl.BlockSpec((N_GATES, tm), lambda i: (0, i)),
        ],
        out_shape=[
            jax.ShapeDtypeStruct((n, PROJ_W), F32),
            jax.ShapeDtypeStruct((N_GATES, n), F32),
        ],
        compiler_params=_cparams("parallel"),
        name="inproj",
    )(x, g, w, wgt)


def _head_norm_store(out_ref, rows, h, o, gh_ref):
    g = gh_ref[:, h * DH:(h + 1) * DH]
    y = o * lax.rsqrt(jnp.mean(o * o, axis=-1, keepdims=True) + EPS) * g
    out_ref[rows, h * DH:(h + 1) * DH] = y.astype(out_ref.dtype)


def _band_prompt_kernel(q_ref, kp_ref, kc_ref, vp_ref, vc_ref, bias_ref, gh_ref, out_ref, *, qb):
    first = pl.program_id(0) == 0
    band = A_PAST + CHUNK
    nchunks = qb // CHUNK
    jj = _iota((CHUNK, band), 1)

    def scores(h):
        hs = slice(h * DH, (h + 1) * DH)
        kcat = jnp.concatenate([kp_ref[:, hs], kc_ref[:, hs]], axis=0).astype(BF16)
        return [_dot_nt(q_ref[c * CHUNK:(c + 1) * CHUNK, hs].astype(BF16), kcat[c * CHUNK:c * CHUNK + band])
                for c in range(nchunks)]

    def attend(h, ss):
        hs = slice(h * DH, (h + 1) * DH)
        vcat = jnp.concatenate([vp_ref[:, hs], vc_ref[:, hs]], axis=0).astype(BF16)
        bias = bias_ref[h]
        masked = []
        for c, s in enumerate(ss):
            s = s * (DH ** -0.5) + bias
            if c < A_PAST // CHUNK:
                valid = jnp.logical_or(jnp.logical_not(first), jj >= A_PAST - c * CHUNK)
                s = jnp.where(valid, s, NEG_INF)
            masked.append(s)
        ms = [jnp.max(s, axis=-1, keepdims=True) for s in masked]
        ps = [jnp.exp(s - m) for s, m in zip(masked, ms)]
        ls = [jnp.sum(p, axis=-1, keepdims=True) for p in ps]
        os_ = [_dot(p.astype(BF16), vcat[c * CHUNK:c * CHUNK + band]) for c, p in enumerate(ps)]
        os_ = [o / l for o, l in zip(os_, ls)]
        msq = [jnp.mean(o * o, axis=-1, keepdims=True) for o in os_]
        for c, (o, m2) in enumerate(zip(os_, msq)):
            out_ref[c * CHUNK:(c + 1) * CHUNK, hs] = (o * lax.rsqrt(m2 + EPS) * gh_ref[:, hs]).astype(out_ref.dtype)

    pending = scores(0)
    for h in range(H):
        nxt = scores(h + 1) if h + 1 < H else None
        attend(h, pending)
        pending = nxt


def _band_prompt(proj, bias, gh, qb=A_PAST):
    t = proj.shape[0]
    assert qb == A_PAST and t % qb == 0
    prev = lambda i: jnp.maximum(i - 1, 0)
    cq = COL_A // GW
    return pl.pallas_call(
        functools.partial(_band_prompt_kernel, qb=qb),
        grid=(t // qb,),
        in_specs=[
            pl.BlockSpec((qb, GW), lambda i: (i, cq)),
            pl.BlockSpec((qb, GW), lambda i: (prev(i), cq + 1)),
            pl.BlockSpec((qb, GW), lambda i: (i, cq + 1)),
            pl.BlockSpec((qb, GW), lambda i: (prev(i), cq + 2)),
            pl.BlockSpec((qb, GW), lambda i: (i, cq + 2)),
            pl.BlockSpec((H, CHUNK, A_PAST + CHUNK), lambda i: (0, 0, 0)),
            pl.BlockSpec((1, GW), lambda i: (0, 0)),
        ],
        out_specs=pl.BlockSpec((qb, GW), lambda i: (i, 0)),
        out_shape=jax.ShapeDtypeStruct((t, GW), BF16),
        compiler_params=_cparams("parallel"),
        name="band_prompt",
    )(proj, proj, proj, proj, proj, bias, gh)


def _band_sample_kernel(q_ref, k_ref, v_ref, ck_ref, cv_ref, bias_ref, gh_ref, out_ref):
    npast = ck_ref.shape[0]
    rows = slice(0, q_ref.shape[0])
    for h in range(H):
        hs = slice(h * DH, (h + 1) * DH)
        q = q_ref[:, hs].astype(BF16)
        s1 = _dot_nt(q, ck_ref[:, hs].astype(BF16)) * (DH ** -0.5) + bias_ref[h, :, :npast]
        s2 = _dot_nt(q, k_ref[:, hs].astype(BF16)) * (DH ** -0.5) + bias_ref[h, :, npast:]
        m = jnp.maximum(jnp.max(s1, axis=-1, keepdims=True), jnp.max(s2, axis=-1, keepdims=True))
        p1 = jnp.exp(s1 - m)
        p2 = jnp.exp(s2 - m)
        l = jnp.sum(p1, axis=-1, keepdims=True) + jnp.sum(p2, axis=-1, keepdims=True)
        o = (_dot(p1.astype(BF16), cv_ref[:, hs].astype(BF16))
             + _dot(p2.astype(BF16), v_ref[:, hs].astype(BF16))) / l
        _head_norm_store(out_ref, rows, h, o, gh_ref)


def _band_sample(proj3, ck, cv, bias, gh):
    b, s, _ = proj3.shape
    npast = ck.shape[1]
    return pl.pallas_call(
        _band_sample_kernel,
        grid=(b,),
        in_specs=[
            pl.BlockSpec((None, s, GW), lambda i: (i, 0, COL_A // GW)),
            pl.BlockSpec((None, s, GW), lambda i: (i, 0, COL_A // GW + 1)),
            pl.BlockSpec((None, s, GW), lambda i: (i, 0, COL_A // GW + 2)),
            pl.BlockSpec((None, npast, GW), lambda i: (i, 0, 0)),
            pl.BlockSpec((None, npast, GW), lambda i: (i, 0, 0)),
            pl.BlockSpec((H, s, npast + s), lambda i: (0, 0, 0)),
            pl.BlockSpec((1, GW), lambda i: (0, 0)),
        ],
        out_specs=pl.BlockSpec((None, s, GW), lambda i: (i, 0, 0)),
        out_shape=jax.ShapeDtypeStruct((b, s, GW), BF16),
        compiler_params=_cparams("parallel"),
        name="band_sample",
    )(proj3, proj3, proj3, ck, cv, bias, gh)


def _cumsum_cols(x, lo_tri):
    return sum(_dot(lo_tri, part) for part in _split3(x))


def _cumsum_rows(x, up_tri):
    return sum(_dot(part, up_tri) for part in _split3(x))


def _tri_masks(l):
    r = _iota((l, l), 0)
    c = _iota((l, l), 1)
    return r >= c, r > c


def _mlstm_kernel(q_ref, k_ref, v_ref, o_ref, gc_ref, gr_ref, bc_ref, br_ref, gh_ref,
                  c0_ref, n0_ref, m0_ref, out_ref, c_ref, n_ref, m_ref, *, l, nck):
    @pl.when(pl.program_id(1) == 0)
    def _():
        c_ref[...] = c0_ref[...]
        n_ref[...] = n0_ref[...]
        m_ref[...] = m0_ref[...]

    incl, _ = _tri_masks(l)
    lo_tri = incl.astype(BF16)
    up_tri = (_iota((l, l), 0) <= _iota((l, l), 1)).astype(BF16)

    probs = []
    for ck in range(nck):
        rows = slice(ck * l, (ck + 1) * l)
        gcol = gc_ref[rows, :] + bc_ref[...]
        grow = gr_ref[ck] + br_ref[...]
        gcs = _cumsum_cols(_log_sigmoid(gcol), lo_tri)
        grs = _cumsum_rows(_log_sigmoid(grow), up_tri)
        for h in range(H):
            hs = slice(h * DH, (h + 1) * DH)
            ig_c = gcol[:, GATE_OFF + h:GATE_OFF + h + 1]
            g_c = gcs[:, GATE_OFF + H + h:GATE_OFF + H + h + 1]
            ig_r = grow[h:h + 1, :]
            g_r = grs[H + h:H + h + 1, :]
            q = q_ref[rows, hs]
            kf = k_ref[rows, hs] * (DH ** -0.5)
            lmat = jnp.where(incl, g_c - g_r + ig_r, NEG_INF)
            probs.append(dict(
                rows=rows, h=h, q=q, kf=kf, qb=q.astype(BF16), kb=kf.astype(BF16),
                vb=v_ref[rows, hs].astype(BF16), lmat=lmat, lmax=jnp.max(lmat, axis=-1, keepdims=True),
                g_c=g_c, ig_c=ig_c, g_last=g_c[l - 1:l, :]))
    qks = [_dot_nt(p["qb"], p["kb"]) for p in probs]

    ms = [m_ref[:, h:h + 1] for h in range(H)]
    for p in probs:
        m_old = ms[p["h"]]
        p["linter"] = p["g_c"] + m_old
        p["mt"] = jnp.maximum(p["linter"], p["lmax"])
        m_new = p["mt"][l - 1:l, :]
        p["dprev"] = jnp.exp(p["g_last"] + m_old - m_new)
        p["kw"] = p["kf"] * jnp.exp(p["g_last"] - p["g_c"] + p["ig_c"] - m_new)
        ms[p["h"]] = m_new
    ws_ = [qk * jnp.exp(p["lmat"] - p["mt"]) for p, qk in zip(probs, qks)]
    wvs = [_dot(w.astype(BF16), p["vb"]) for p, w in zip(probs, ws_)]
    upds = [_dot_tn(p["kw"].astype(BF16), p["vb"]) for p in probs]

    cs = [c_ref[h] for h in range(H)]
    ns = [n_ref[h:h + 1, :] for h in range(H)]
    qcs, qns = [], []
    for p, upd in zip(probs, upds):
        h = p["h"]
        qcs.append(_dot(p["qb"], cs[h].astype(BF16)))
        qns.append(jnp.sum(p["q"] * ns[h], axis=-1, keepdims=True))
        cs[h] = p["dprev"] * cs[h] + upd
        ns[h] = p["dprev"] * ns[h] + jnp.sum(p["kw"], axis=0, keepdims=True)
    for h in range(H):
        c_ref[h] = cs[h]
        n_ref[h:h + 1, :] = ns[h]
        m_ref[:, h:h + 1] = ms[h]

    wsums = [jnp.sum(w, axis=-1, keepdims=True) for w in ws_]
    obs = []
    for p, wsum, wv, qc, qn in zip(probs, wsums, wvs, qcs, qns):
        hs = slice(p["h"] * DH, (p["h"] + 1) * DH)
        inter = jnp.exp(p["linter"] - p["mt"])
        den = wsum + inter * qn
        hout = (wv + inter * qc) / jnp.maximum(jnp.abs(den), jnp.exp(-p["mt"]))
        obs.append(hout * _sigmoid(o_ref[p["rows"], hs]))
    msq = [jnp.mean(ob * ob, axis=-1, keepdims=True) for ob in obs]
    for p, ob, ms_ in zip(probs, obs, msq):
        hs = slice(p["h"] * DH, (p["h"] + 1) * DH)
        out_ref[p["rows"], hs] = (ob * lax.rsqrt(ms_ + EPS) * gh_ref[:, hs]).astype(out_ref.dtype)


def _mlstm(proj3, gt3, bias_c, bias_r, gh, c0, n0, m0, l, nck):
    b, t, _ = proj3.shape
    steps = t // (l * nck)
    blk = l * nck
    col = lambda j: pl.BlockSpec((None, blk, GW), lambda bi, s: (bi, s, j))
    state = lambda shp: pl.BlockSpec((None,) + shp, lambda bi, s: (bi,) + (0,) * len(shp))
    return pl.pallas_call(
        functools.partial(_mlstm_kernel, l=l, nck=nck),
        grid=(b, steps),
        in_specs=[
            col(COL_B // GW), col(COL_B // GW + 1), col(COL_B // GW + 2), col(COL_B // GW + 3),
            pl.BlockSpec((None, blk, LANES), lambda bi, s: (bi, s, GATE_BLK)),
            pl.BlockSpec((None, nck, N_GATES, l), lambda bi, s: (bi, s, 0, 0)),
            pl.BlockSpec((1, LANES), lambda bi, s: (0, 0)),
            pl.BlockSpec((N_GATES, 1), lambda bi, s: (0, 0)),
            pl.BlockSpec((1, GW), lambda bi, s: (0, 0)),
            state((H, DH, DH)), state((H, DH)), state((1, H)),
        ],
        out_specs=[
            pl.BlockSpec((None, blk, GW), lambda bi, s: (bi, s, 0)),
            state((H, DH, DH)), state((H, DH)), state((1, H)),
        ],
        out_shape=[
            jax.ShapeDtypeStruct((b, t, GW), BF16),
            jax.ShapeDtypeStruct((b, H, DH, DH), F32),
            jax.ShapeDtypeStruct((b, H, DH), F32),
            jax.ShapeDtypeStruct((b, 1, H), F32),
        ],
        compiler_params=_cparams("parallel", "arbitrary"),
        name="mlstm",
    )(proj3, proj3, proj3, proj3, proj3, gt3, bias_c, bias_r, gh, c0, n0, m0)


def _head_of(idx):
    return idx // DH


def _block_mask(n_rows, n_cols):
    return _head_of(_iota((n_rows, n_cols), 0)) == _head_of(_iota((n_rows, n_cols), 1))


def _expander(first_lane):
    r = _iota((LANES, GW), 0)
    c = _iota((LANES, GW), 1)
    return (r == first_lane + _head_of(c)).astype(BF16)


def _dot_stacked(parts, rhs):
    m = parts[0].shape[0]
    y = _dot(jnp.concatenate(parts, axis=0), rhs)
    return sum(y[i * m:(i + 1) * m] for i in range(len(parts)))


def _expand(x, e):
    return _dot_stacked(_split3(x), e)


def _head_sums(a, bones):
    return _dot_stacked(_split3(a), bones)


def _cumsum_cols_wide(x, lo_tri):
    w = x.shape[1]
    y = _dot(lo_tri, jnp.concatenate(_split3(x), axis=1))
    return y[:, :w] + y[:, w:2 * w] + y[:, 2 * w:]


def _row_select(x_t, first_row):
    r = _iota(x_t.shape, 0)
    c = _iota(x_t.shape, 1)
    return jnp.sum(jnp.where(r == first_row + _head_of(c), x_t, 0.0), axis=0, keepdims=True)


def _block_diag_rows(x, mask):
    return jnp.where(mask, jnp.concatenate([x] * H, axis=0), jnp.zeros((), x.dtype))


def _cummax_rows(x):
    rows = _iota(x.shape, 0)
    sh = 1
    while sh < x.shape[0]:
        x = jnp.maximum(x, jnp.where(rows >= sh, pltpu.roll(x, sh, axis=0), NEG_INF))
        sh *= 2
    return x


def _mlstm64_kernel(q_ref, k_ref, v_ref, o_ref, gc_ref, gr_ref, bc_ref, br_ref, gh_ref,
                    c0_ref, n0_ref, m0_ref, out_ref, c_ref, n_ref, m_ref, *, nck):
    l = CHUNK

    @pl.when(pl.program_id(1) == 0)
    def _():
        c_ref[...] = c0_ref[...]
        n_ref[...] = n0_ref[...]
        m_ref[...] = m0_ref[...]

    bmask = _block_mask(GW, GW)
    bones = bmask.astype(BF16)
    within = _iota((GW, GW), 0) % DH <= _iota((GW, GW), 1) % DH
    up_bd = jnp.logical_and(bmask, within).astype(BF16)
    lo_tri = (_iota((l, l), 0) >= _iota((l, l), 1)).astype(BF16)
    incl = _iota((l, GW), 0) >= _iota((l, GW), 1) % DH
    e_i = _expander(GATE_OFF)
    e_f = _expander(GATE_OFF + H)

    cks = []
    for ck in range(nck):
        rows = slice(ck * l, (ck + 1) * l)
        gcol = gc_ref[rows, :] + bc_ref[...]
        gcs = _cumsum_cols_wide(_log_sigmoid(gcol), lo_tri)
        g_c = _expand(gcs, e_f)
        i_c = _expand(gcol, e_i)
        grow = gr_ref[ck] + br_ref[...]
        grow_t = jnp.concatenate([grow] * H, axis=1)
        grs_t = _dot_stacked(_split3(_log_sigmoid(grow_t)), up_bd)
        a_r = _row_select(grow_t, 0) - _row_select(grs_t, H)
        lmat = jnp.where(incl, g_c + a_r, NEG_INF)
        lmax = g_c + _cummax_rows(i_c - g_c)
        q = q_ref[rows, :]
        kf = k_ref[rows, :] * (DH ** -0.5)
        cks.append(dict(rows=rows, g_c=g_c, i_c=i_c, lmat=lmat, lmax=lmax, q=q, kf=kf,
                        qb=q.astype(BF16), kb=kf.astype(BF16), vb=v_ref[rows, :].astype(BF16),
                        g_last=g_c[l - 1:l, :]))
    scs = [_dot_nt(p["qb"], _block_diag_rows(p["kb"], bmask)) for p in cks]

    m_run = m_ref[...]
    for p in cks:
        p["linter"] = p["g_c"] + m_run
        p["mt"] = jnp.maximum(p["linter"], p["lmax"])
        m_new = p["mt"][l - 1:l, :]
        p["dprev"] = jnp.exp(p["g_last"] + m_run - m_new)
        p["kw"] = p["kf"] * jnp.exp(p["g_last"] - p["g_c"] + p["i_c"] - m_new)
        m_run = m_new
    m_ref[...] = m_run
    wbs = [(s * jnp.exp(p["lmat"] - p["mt"])).astype(BF16) for p, s in zip(cks, scs)]
    nums = [_dot(w, _block_diag_rows(p["vb"], bmask)) for p, w in zip(cks, wbs)]
    wsums = [_dot(w, bones) for w in wbs]
    upds = [jnp.where(bmask, _dot_tn(p["kw"].astype(BF16), p["vb"]), 0.0) for p in cks]

    c_run = c_ref[...]
    n_run = n_ref[...]
    qcs, qns = [], []
    for p, upd in zip(cks, upds):
        qcs.append(_dot(p["qb"], c_run.astype(BF16)))
        qns.append(_dot((p["q"] * n_run).astype(BF16), bones))
        c_run = p["dprev"] * c_run + upd
        n_run = p["dprev"] * n_run + jnp.sum(p["kw"], axis=0, keepdims=True)
    c_ref[...] = c_run
    n_ref[...] = n_run

    obs = []
    for p, num, wsum, qc, qn in zip(cks, nums, wsums, qcs, qns):
        inter = jnp.exp(p["linter"] - p["mt"])
        den = wsum + inter * qn
        hout = (num + inter * qc) / jnp.maximum(jnp.abs(den), jnp.exp(-p["mt"]))
        obs.append(hout * _sigmoid(o_ref[p["rows"], :]))
    msq = [_head_sums(ob * ob, bones) * (1.0 / DH) for ob in obs]
    for p, ob, m2 in zip(cks, obs, msq):
        out_ref[p["rows"], :] = (ob * lax.rsqrt(m2 + EPS) * gh_ref[...]).astype(out_ref.dtype)


def _mlstm64(proj3, gt3, bias_c, bias_r, gh, c0, n0, m0, nck):
    b, t, _ = proj3.shape
    blk = CHUNK * nck
    steps = t // blk
    col = lambda j: pl.BlockSpec((None, blk, GW), lambda bi, s: (bi, s, j))
    state = lambda shp: pl.BlockSpec((None,) + shp, lambda bi, s: (bi,) + (0,) * len(shp))
    return pl.pallas_call(
        functools.partial(_mlstm64_kernel, nck=nck),
        grid=(b, steps),
        in_specs=[
            col(COL_B // GW), col(COL_B // GW + 1), col(COL_B // GW + 2), col(COL_B // GW + 3),
            pl.BlockSpec((None, blk, LANES), lambda bi, s: (bi, s, GATE_BLK)),
            pl.BlockSpec((None, nck, N_GATES, CHUNK), lambda bi, s: (bi, s, 0, 0)),
            pl.BlockSpec((1, LANES), lambda bi, s: (0, 0)),
            pl.BlockSpec((N_GATES, 1), lambda bi, s: (0, 0)),
            pl.BlockSpec((1, GW), lambda bi, s: (0, 0)),
            state((GW, GW)), state((1, GW)), state((1, GW)),
        ],
        out_specs=[
            pl.BlockSpec((None, blk, GW), lambda bi, s: (bi, s, 0)),
            state((GW, GW)), state((1, GW)), state((1, GW)),
        ],
        out_shape=[
            jax.ShapeDtypeStruct((b, t, GW), BF16),
            jax.ShapeDtypeStruct((b, GW, GW), F32),
            jax.ShapeDtypeStruct((b, 1, GW), F32),
            jax.ShapeDtypeStruct((b, 1, GW), F32),
        ],
        compiler_params=_cparams("parallel", "arbitrary"),
        name="mlstm64",
    )(proj3, proj3, proj3, proj3, proj3, gt3, bias_c, bias_r, gh, c0, n0, m0)


def _to_block_diag(c):
    b = c.shape[0]
    eye = jnp.eye(H, dtype=c.dtype)
    return jnp.einsum("bhde,hg->bhdge", c, eye).reshape(b, GW, GW)


def _from_block_diag(cbd):
    b = cbd.shape[0]
    c5 = cbd.reshape(b, H, DH, H, DH)
    return jnp.stack([c5[:, h, :, h, :] for h in range(H)], axis=1)


def _split2(x):
    hi = x.astype(BF16)
    lo = (x - hi.astype(F32)).astype(BF16)
    return hi, lo


def _dot_sp(a, b):
    return _dot(a[0], b[0]) + (_dot(a[0], b[1]) + _dot(a[1], b[0]))


def _unit_lower_inverses(nmats, l):
    eye = (_iota((l, l), 0) == _iota((l, l), 1)).astype(F32)
    ps = [eye - n for n in nmats]
    qs = [_split2(n) for n in nmats]
    qs = [_split2(_dot_sp(q, q)) for q in qs]
    power = 2
    while power < l:
        ps = [p + _dot_sp(_split2(p), q) for p, q in zip(ps, qs)]
        power *= 2
        if power < l:
            qs = [_split2(_dot_sp(q, q)) for q in qs]
    return ps


def _l2norm(x):
    return x * lax.rsqrt(jnp.sum(x * x, axis=-1, keepdims=True) + 1e-6)


def _gdn_kernel(x_ref, z_ref, gc_ref, gr_ref, hist_ref, cw_ref, ac_ref, ar_ref, dc_ref, dr_ref,
                gh_ref, s0_ref, out_ref, s_ref, carry_scr, *, l, nck):
    @pl.when(pl.program_id(1) == 0)
    def _():
        s_ref[...] = s0_ref[...]
        carry_scr[...] = hist_ref[...]

    blk = l * nck
    x = x_ref[...]
    ext = jnp.concatenate([carry_scr[...], x], axis=0)
    carry_scr[...] = x[blk - SUBLANES:, :]
    y = x * cw_ref[3:4, :]
    for j in range(1, 4):
        y = y + ext[SUBLANES - j:SUBLANES - j + blk, :] * cw_ref[3 - j:4 - j, :]
    y = y * _sigmoid(y)

    incl, strict = _tri_masks(l)
    lo_tri = incl.astype(BF16)
    up_tri = (_iota((l, l), 0) <= _iota((l, l), 1)).astype(BF16)

    qraw = [y[ck * l:(ck + 1) * l, h * DH:(h + 1) * DH] for ck in range(nck) for h in range(H)]
    kraw = [y[ck * l:(ck + 1) * l, GW + h * DH:GW + (h + 1) * DH] for ck in range(nck) for h in range(H)]
    vraw = [y[ck * l:(ck + 1) * l, 2 * GW + h * DH:2 * GW + (h + 1) * DH] for ck in range(nck) for h in range(H)]
    qnorm = [_l2norm(a) * (DH ** -0.5) for a in qraw]
    knorm = [_l2norm(a) for a in kraw]
    probs = []
    for ck in range(nck):
        rows = slice(ck * l, (ck + 1) * l)
        gcol = gc_ref[rows, :]
        grow = gr_ref[ck]
        beta_cs = _sigmoid(gcol)
        dec_c = -jnp.exp(ac_ref[...]) * _softplus(gcol + dc_ref[...])
        dec_r = -jnp.exp(ar_ref[...]) * _softplus(grow + dr_ref[...])
        gcs = _cumsum_cols(dec_c, lo_tri)
        grs = _cumsum_rows(dec_r, up_tri)
        for h in range(H):
            beta = beta_cs[:, GATE_OFF + 2 * H + h:GATE_OFF + 2 * H + h + 1]
            g_c = gcs[:, GATE_OFF + 3 * H + h:GATE_OFF + 3 * H + h + 1]
            g_r = grs[3 * H + h:3 * H + h + 1, :]
            q, k, v = qnorm[ck * H + h], knorm[ck * H + h], vraw[ck * H + h]
            decay = jnp.exp(jnp.where(incl, g_c - g_r, NEG_INF))
            eg = jnp.exp(g_c)
            g_last = g_c[l - 1:l, :]
            probs.append(dict(
                rows=rows, h=h, qb=q.astype(BF16), kb=k.astype(BF16), beta=beta, decay=decay,
                rhs=jnp.concatenate([v * beta, k * (beta * eg)], axis=-1),
                qeg=(q * eg).astype(BF16), kdec=(k * jnp.exp(g_last - g_c)).astype(BF16),
                sdec=jnp.exp(g_last)))
    kks = [_dot_nt(p["kb"], p["kb"]) for p in probs]
    qks = [_dot_nt(p["qb"], p["kb"]) for p in probs]
    a_lows = [jnp.where(strict, p["beta"] * kk * p["decay"], 0.0) for p, kk in zip(probs, kks)]
    attns = [(qk * p["decay"]).astype(BF16) for p, qk in zip(probs, qks)]
    tinvs = _unit_lower_inverses(a_lows, l)
    sols = [_dot_sp(_split2(t), _split2(p["rhs"])) for t, p in zip(tinvs, probs)]

    states = [s_ref[h] for h in range(H)]
    for ck in range(nck):
        ps = probs[ck * H:(ck + 1) * H]
        ss = sols[ck * H:(ck + 1) * H]
        at = attns[ck * H:(ck + 1) * H]
        sbs = [s.astype(BF16) for s in states]
        wss = [_dot(sol[:, DH:].astype(BF16), sb) for sol, sb in zip(ss, sbs)]
        qss = [_dot(p["qeg"], sb) for p, sb in zip(ps, sbs)]
        vnbs = [(sol[:, :DH] - ws).astype(BF16) for sol, ws in zip(ss, wss)]
        os_ = [qs + _dot(a, vnb) for qs, a, vnb in zip(qss, at, vnbs)]
        states = [p["sdec"] * s + _dot_tn(p["kdec"], vnb) for p, s, vnb in zip(ps, states, vnbs)]
        for p, o in zip(ps, os_):
            hs = slice(p["h"] * DH, (p["h"] + 1) * DH)
            zg = z_ref[p["rows"], hs]
            yo = (o * lax.rsqrt(jnp.mean(o * o, axis=-1, keepdims=True) + EPS) * gh_ref[:, hs]
                  * (zg * _sigmoid(zg)))
            out_ref[p["rows"], hs] = yo.astype(out_ref.dtype)
    for h in range(H):
        s_ref[h] = states[h]


def _gdn(proj3, gt3, hist8, cw, a_c, a_r, dt_c, dt_r, gh, s0, l, nck):
    b, t, _ = proj3.shape
    blk = l * nck
    steps = t // blk
    state = lambda shp: pl.BlockSpec((None,) + shp, lambda bi, s: (bi,) + (0,) * len(shp))
    const = lambda shp: pl.BlockSpec(shp, lambda bi, s: (0,) * len(shp))
    return pl.pallas_call(
        functools.partial(_gdn_kernel, l=l, nck=nck),
        grid=(b, steps),
        in_specs=[
            pl.BlockSpec((None, blk, 3 * GW), lambda bi, s: (bi, s, COL_CX // (3 * GW))),
            pl.BlockSpec((None, blk, GW), lambda bi, s: (bi, s, COL_CZ // GW)),
            pl.BlockSpec((None, blk, LANES), lambda bi, s: (bi, s, GATE_BLK)),
            pl.BlockSpec((None, nck, N_GATES, l), lambda bi, s: (bi, s, 0, 0)),
            state((SUBLANES, 3 * GW)),
            const((4, 3 * GW)),
            const((1, LANES)), const((N_GATES, 1)), const((1, LANES)), const((N_GATES, 1)),
            const((1, GW)),
            state((H, DH, DH)),
        ],
        out_specs=[
            pl.BlockSpec((None, blk, GW), lambda bi, s: (bi, s, 0)),
            state((H, DH, DH)),
        ],
        out_shape=[
            jax.ShapeDtypeStruct((b, t, GW), BF16),
            jax.ShapeDtypeStruct((b, H, DH, DH), F32),
        ],
        scratch_shapes=[pltpu.VMEM((SUBLANES, 3 * GW), F32)],
        compiler_params=_cparams("parallel", "arbitrary"),
        name="gdn",
    )(proj3, proj3, proj3, gt3, hist8, cw, a_c, a_r, dt_c, dt_r, gh, s0)


def _bd_split(x, mask):
    hi, lo = _split2(x)
    return _block_diag_rows(hi, mask), _block_diag_rows(lo, mask)


def _gdn64_kernel(x_ref, z_ref, gc_ref, gr_ref, hist_ref, cw_ref, ac_ref, ar_ref, dc_ref, dr_ref,
                  gh_ref, s0_ref, out_ref, s_ref, carry_scr, *, nck):
    l = CHUNK

    @pl.when(pl.program_id(1) == 0)
    def _():
        s_ref[...] = s0_ref[...]
        carry_scr[...] = hist_ref[...]

    blk = l * nck
    x = x_ref[...]
    ext = jnp.concatenate([carry_scr[...], x], axis=0)
    carry_scr[...] = x[blk - SUBLANES:, :]
    y = x * cw_ref[3:4, :]
    for j in range(1, 4):
        y = y + ext[SUBLANES - j:SUBLANES - j + blk, :] * cw_ref[3 - j:4 - j, :]
    y = y * _sigmoid(y)

    bmask = _block_mask(GW, GW)
    bones = bmask.astype(BF16)
    within = _iota((GW, GW), 0) % DH <= _iota((GW, GW), 1) % DH
    up_bd = jnp.logical_and(bmask, within).astype(BF16)
    lo_tri = (_iota((l, l), 0) >= _iota((l, l), 1)).astype(BF16)
    key_pos = _iota((l, GW), 1) % DH
    incl = _iota((l, GW), 0) >= key_pos
    strict = _iota((l, GW), 0) > key_pos
    eye_t = (_iota((l, GW), 0) == key_pos).astype(F32)
    e_b = _expander(GATE_OFF + 2 * H)
    e_a = _expander(GATE_OFF + 3 * H)

    def head_sums(a):
        return _head_sums(a, bones)

    def shared_rhs(lhs_splits, rhs_bd):
        n = len(lhs_splits)
        big = _dot(jnp.concatenate([part for sp in lhs_splits for part in sp], axis=0), rhs_bd[0])
        small = _dot(jnp.concatenate([sp[0] for sp in lhs_splits], axis=0), rhs_bd[1])
        return [big[2 * i * l:(2 * i + 1) * l] + big[(2 * i + 1) * l:(2 * i + 2) * l]
                + small[i * l:(i + 1) * l] for i in range(n)]

    yq, yk, yv = y[:, :GW], y[:, GW:2 * GW], y[:, 2 * GW:]
    qn_all = yq * lax.rsqrt(head_sums(yq * yq) + 1e-6) * (DH ** -0.5)
    kn_all = yk * lax.rsqrt(head_sums(yk * yk) + 1e-6)

    cks = []
    for ck in range(nck):
        rows = slice(ck * l, (ck + 1) * l)
        gcol = gc_ref[rows, :]
        dec_c = -jnp.exp(ac_ref[...]) * _softplus(gcol + dc_ref[...])
        beta = _expand(_sigmoid(gcol), e_b)
        g_c = _expand(_cumsum_cols_wide(dec_c, lo_tri), e_a)
        grow_t = jnp.concatenate([gr_ref[ck]] * H, axis=1)
        dec_r = -jnp.exp(ar_ref[...]) * _softplus(grow_t + dr_ref[...])
        g_r = _row_select(_dot_stacked(_split3(dec_r), up_bd), 3 * H)
        decay = jnp.exp(jnp.where(incl, g_c - g_r, NEG_INF))
        eg = jnp.exp(g_c)
        g_last = g_c[l - 1:l, :]
        q, k, v = qn_all[rows], kn_all[rows], yv[rows]
        cks.append(dict(rows=rows, qb=q.astype(BF16), kb=k.astype(BF16),
                        beta=beta, decay=decay, rhs_v=v * beta, rhs_k=k * (beta * eg),
                        qeg=(q * eg).astype(BF16), kdec=(k * jnp.exp(g_last - g_c)).astype(BF16),
                        sdec=jnp.exp(g_last)))
    kqs = [_dot_nt(jnp.concatenate([p["kb"], p["qb"]], axis=0), _block_diag_rows(p["kb"], bmask))
           for p in cks]
    nmats = [jnp.where(strict, p["beta"] * kq[:l] * p["decay"], 0.0) for p, kq in zip(cks, kqs)]
    attns = [(kq[l:] * p["decay"]).astype(BF16) for p, kq in zip(cks, kqs)]

    ps = [eye_t - n for n in nmats]
    qs = [shared_rhs([_split2(n)], _bd_split(n, bmask))[0] for n in nmats]
    power = 2
    while power < l:
        power *= 2
        if power < l:
            res = [shared_rhs([_split2(p), _split2(q)], _bd_split(q, bmask)) for p, q in zip(ps, qs)]
            ps = [p + r[0] for p, r in zip(ps, res)]
            qs = [r[1] for r in res]
        else:
            ps = [p + shared_rhs([_split2(p)], _bd_split(q, bmask))[0] for p, q in zip(ps, qs)]
    tsp = [_split2(p) for p in ps]
    us = [shared_rhs([t], _bd_split(p["rhs_v"], bmask))[0] for t, p in zip(tsp, cks)]
    ws = [shared_rhs([t], _bd_split(p["rhs_k"], bmask))[0].astype(BF16) for t, p in zip(tsp, cks)]

    wu = [jnp.concatenate([w, u.astype(BF16)], axis=1) for w, u in zip(ws, us)]
    kwu = [_dot_tn(p["kdec"], x) for p, x in zip(cks, wu)]
    awu = [_dot(at, jnp.concatenate([_block_diag_rows(x[:, :GW], bmask),
                                     _block_diag_rows(x[:, GW:], bmask)], axis=1))
           for at, x in zip(attns, wu)]
    gmats = [jnp.where(bmask, x[:, :GW], 0.0).astype(BF16) for x in kwu]
    bmats = [jnp.where(bmask, x[:, GW:], 0.0) for x in kwu]
    qts = [(p["qeg"].astype(F32) - x[:, :GW]).astype(BF16) for p, x in zip(cks, awu)]

    s_run = s_ref[...]
    outs = []
    for p, g, bm, qt, x in zip(cks, gmats, bmats, qts, awu):
        ys = _dot(jnp.concatenate([g, qt], axis=0), s_run.astype(BF16))
        outs.append(ys[GW:] + x[:, GW:])
        s_run = p["sdec"] * s_run - ys[:GW] + bm
    s_ref[...] = s_run

    msq = [head_sums(o * o) * (1.0 / DH) for o in outs]
    for p, o, m2 in zip(cks, outs, msq):
        zg = z_ref[p["rows"], :]
        yo = o * lax.rsqrt(m2 + EPS) * gh_ref[...] * (zg * _sigmoid(zg))
        out_ref[p["rows"], :] = yo.astype(out_ref.dtype)


def _gdn64(proj3, gt3, hist8, cw, a_c, a_r, dt_c, dt_r, gh, s0, nck):
    b, t, _ = proj3.shape
    blk = CHUNK * nck
    steps = t // blk
    state = lambda shp: pl.BlockSpec((None,) + shp, lambda bi, s: (bi,) + (0,) * len(shp))
    const = lambda shp: pl.BlockSpec(shp, lambda bi, s: (0,) * len(shp))
    return pl.pallas_call(
        functools.partial(_gdn64_kernel, nck=nck),
        grid=(b, steps),
        in_specs=[
            pl.BlockSpec((None, blk, 3 * GW), lambda bi, s: (bi, s, COL_CX // (3 * GW))),
            pl.BlockSpec((None, blk, GW), lambda bi, s: (bi, s, COL_CZ // GW)),
            pl.BlockSpec((None, blk, LANES), lambda bi, s: (bi, s, GATE_BLK)),
            pl.BlockSpec((None, nck, N_GATES, CHUNK), lambda bi, s: (bi, s, 0, 0)),
            state((SUBLANES, 3 * GW)),
            const((4, 3 * GW)),
            const((1, LANES)), const((N_GATES, 1)), const((1, LANES)), const((N_GATES, 1)),
            const((1, GW)),
            state((GW, GW)),
        ],
        out_specs=[
            pl.BlockSpec((None, blk, GW), lambda bi, s: (bi, s, 0)),
            state((GW, GW)),
        ],
        out_shape=[
            jax.ShapeDtypeStruct((b, t, GW), BF16),
            jax.ShapeDtypeStruct((b, GW, GW), F32),
        ],
        scratch_shapes=[pltpu.VMEM((SUBLANES, 3 * GW), F32)],
        compiler_params=_cparams("parallel", "arbitrary"),
        name="gdn64",
    )(proj3, proj3, proj3, gt3, hist8, cw, a_c, a_r, dt_c, dt_r, gh, s0)


HEAD_PAD = 128
DPAD = H * HEAD_PAD
BF16_ROWS = 16
VT_PAD = -(-(DH + 1) // BF16_ROWS) * BF16_ROWS
VT_ROWS = H * VT_PAD


def _tile_heads(t):
    return jnp.concatenate([t] * H, axis=-1)


def _dprep_kernel(tail_ref, gq_ref, gkv_ref, wq_ref, wqp_ref, ka_ref, ckv_ref, kpe_ref, qc_ref):
    ka = ka_ref[...]
    kb = pltpu.roll(ka, 64, axis=1)
    nope = _iota(ka.shape, 1) < QK_NOPE
    qcos = jnp.where(nope, 1.0, kb)
    qsin = jnp.where(nope, 0.0, ka)
    hq = _rms(tail_ref[:, :Q_LORA], gq_ref[...]).astype(BF16)
    qc = _dot(hq, wq_ref[...]) * _tile_heads(qcos) + _dot(hq, wqp_ref[...]) * _tile_heads(qsin)
    qc_ref[...] = (qc * (MLA_SCALE * LOG2E)).astype(BF16)
    ckv_ref[...] = _rms(tail_ref[:, Q_LORA:Q_LORA + KV_LORA], gkv_ref[...])
    kr = tail_ref[:, Q_LORA + KV_LORA:]
    kpe = kr * ka + pltpu.roll(kr, 64, axis=1) * kb
    kpe_ref[...] = kpe[:, :QK_ROPE]


def _dprep(proj, gq, gkv, wq, wqp, ka, tm):
    n = proj.shape[0]
    row = lambda w: pl.BlockSpec((tm, w), lambda i: (i, 0))
    const = lambda a, b: pl.BlockSpec((a, b), lambda i: (0, 0))
    return pl.pallas_call(
        _dprep_kernel,
        grid=(n // tm,),
        in_specs=[
            pl.BlockSpec((tm, TAIL_W), lambda i: (i, COL_TAIL // TAIL_W)),
            const(1, Q_LORA), const(1, KV_LORA), const(Q_LORA, DPAD), const(Q_LORA, DPAD),
            row(LANES),
        ],
        out_specs=[row(KV_LORA), row(QK_ROPE), row(DPAD)],
        out_shape=[
            jax.ShapeDtypeStruct((n, KV_LORA), F32),
            jax.ShapeDtypeStruct((n, QK_ROPE), F32),
            jax.ShapeDtypeStruct((n, DPAD), BF16),
        ],
        compiler_params=_cparams("parallel"),
        name="dprep",
    )(proj, gq, gkv, wq, wqp, ka)


def _kvup_kernel(ckv_ref, kpe_ref, wk_ref, wv_ref, pm_ref, one_ref, kc_ref, vp_ref):
    c = ckv_ref[...].astype(BF16)
    kc_ref[...] = (_dot(c, wk_ref[...]) + _dot(kpe_ref[...].astype(BF16), pm_ref[...])).astype(BF16)
    vp_ref[...] = (_dot(c, wv_ref[...]) + one_ref[...]).astype(BF16)


def _kvup(ckv, kpe, wk, wv, pm, ones, tm):
    m = ckv.shape[0]
    row = lambda w: pl.BlockSpec((tm, w), lambda i: (i, 0))
    const = lambda a, b: pl.BlockSpec((a, b), lambda i: (0, 0))
    return pl.pallas_call(
        _kvup_kernel,
        grid=(m // tm,),
        in_specs=[row(KV_LORA), row(QK_ROPE), const(KV_LORA, DPAD), const(KV_LORA, DPAD),
                  const(QK_ROPE, DPAD), const(1, DPAD)],
        out_specs=[row(DPAD), row(DPAD)],
        out_shape=[jax.ShapeDtypeStruct((m, DPAD), BF16), jax.ShapeDtypeStruct((m, DPAD), BF16)],
        compiler_params=_cparams("parallel"),
        name="kvup",
    )(ckv, kpe, wk, wv, pm, ones)


def _mla_finish(out_ref, gh_ref, h, acc):
    o = acc[:, :DH] / acc[:, DH:DH + 1]
    rows = slice(0, o.shape[0])
    _head_norm_store(out_ref, rows, h, o, gh_ref)


LOG2E = 1.4426950408889634


def _kvup_t_kernel(ckv_ref, kpe_ref, wk_ref, wvt_ref, pm_ref, onet_ref, kc_ref, vt_ref):
    c = ckv_ref[...].astype(BF16)
    kc_ref[...] = (_dot(c, wk_ref[...]) + _dot(kpe_ref[...].astype(BF16), pm_ref[...])).astype(BF16)
    vt_ref[...] = (_dot_nt(wvt_ref[...], c) + onet_ref[...]).astype(BF16)


def _kvup_t(ckv, kpe, wk, wvt, pm, onet, tm):
    m = ckv.shape[0]
    row = lambda w: pl.BlockSpec((tm, w), lambda i: (i, 0))
    const = lambda a, b: pl.BlockSpec((a, b), lambda i: (0, 0))
    return pl.pallas_call(
        _kvup_t_kernel,
        grid=(m // tm,),
        in_specs=[row(KV_LORA), row(QK_ROPE), const(KV_LORA, DPAD), const(VT_ROWS, KV_LORA),
                  const(QK_ROPE, DPAD), const(VT_ROWS, 1)],
        out_specs=[row(DPAD), pl.BlockSpec((VT_ROWS, tm), lambda i: (0, i))],
        out_shape=[jax.ShapeDtypeStruct((m, DPAD), BF16), jax.ShapeDtypeStruct((VT_ROWS, m), BF16)],
        compiler_params=_cparams("parallel"),
        name="kvup_t",
    )(ckv, kpe, wk, wvt, pm, onet)


def _mla_prompt_kernel(qi_ref, ki_ref, q_ref, k_ref, vt_ref, ghc_ref, out_ref, m_scr, acc_scr, *, bq, bk,
                       qw, ahead):
    p = pl.program_id(0)
    q_i = qi_ref[p]
    k_i = ki_ref[p]
    last = (q_i * bq) // bk

    @pl.when(k_i == 0)
    def _():
        m_scr[...] = jnp.full(m_scr.shape, NEG_INF, F32)
        acc_scr[...] = jnp.zeros(acc_scr.shape, F32)

    def step(diag):
        if diag:
            key_chunk = k_i * (bk // CHUNK) + _iota((bk, bq), 0) // CHUNK
            qry_chunk = q_i * (bq // CHUNK) + _iota((bk, bq), 1) // CHUNK
            allowed = key_chunk <= qry_chunk

        units = [(h, c) for h in range(H) for c in range(bq // qw)]
        rc = min(bk, 64)

        def scores(u):
            h, c = u
            hs = slice(h * HEAD_PAD, (h + 1) * HEAD_PAD)
            return _dot_nt(k_ref[:, hs], q_ref[c * qw:(c + 1) * qw, hs])

        def update(u, st):
            h, c = u
            hs = slice(h * HEAD_PAD, (h + 1) * HEAD_PAD)
            qs = slice(c * qw, (c + 1) * qw)
            if diag:
                st = jnp.where(allowed[:, qs], st, NEG_INF)
            m_prev = m_scr[h, :, qs]
            mx = st[:rc]
            for r in range(1, bk // rc):
                mx = jnp.maximum(mx, st[r * rc:(r + 1) * rc])
            m_new = jnp.maximum(m_prev, jnp.max(mx, axis=0, keepdims=True))
            alpha = jnp.exp2(m_prev - m_new)[0:1]
            m_row = m_new[0:1]
            pt = jnp.concatenate([jnp.exp2(st[r * rc:(r + 1) * rc] - m_row).astype(BF16)
                                  for r in range(bk // rc)], axis=0)
            acc = alpha * acc_scr[h, :, qs] + _dot(vt_ref[h * VT_PAD:(h + 1) * VT_PAD, :], pt)
            if not diag:
                m_scr[h, :, qs] = m_new
                acc_scr[h, :, qs] = acc
            return acc

        accs = []
        pending = [scores(u) for u in units[:ahead]]
        for idx, u in enumerate(units):
            if idx + ahead < len(units):
                pending.append(scores(units[idx + ahead]))
            accs.append(update(u, pending.pop(0)))
        per_head = bq // qw
        return [jnp.concatenate(accs[h * per_head:(h + 1) * per_head], axis=1) for h in range(H)]

    @pl.when(k_i < last)
    def _():
        step(False)

    @pl.when(k_i == last)
    def _():
        ys = []
        for h, acc in enumerate(step(True)):
            o = acc[:DH] / acc[DH:DH + 1]
            ms = jnp.mean(o * o, axis=0, keepdims=True)
            ys.append(o * lax.rsqrt(ms + EPS) * ghc_ref[h * DH:(h + 1) * DH, :])
        out_ref[...] = jnp.concatenate(ys, axis=0).T.astype(out_ref.dtype)


def _mla_prompt(qc, kc, vt, ghc, bq, bk):
    t = qc.shape[0]
    assert t % bq == 0 and t % bk == 0 and bk % bq == 0
    pairs = [(i, j) for i in range(t // bq) for j in range((i * bq) // bk + 1)]
    qi = jnp.asarray([i for i, _ in pairs], jnp.int32)
    ki = jnp.asarray([j for _, j in pairs], jnp.int32)
    grid_spec = pltpu.PrefetchScalarGridSpec(
        num_scalar_prefetch=2,
        grid=(len(pairs),),
        in_specs=[
            pl.BlockSpec((bq, DPAD), lambda p, qi, ki: (qi[p], 0)),
            pl.BlockSpec((bk, DPAD), lambda p, qi, ki: (ki[p], 0)),
            pl.BlockSpec((VT_ROWS, bk), lambda p, qi, ki: (0, ki[p])),
            pl.BlockSpec((GW, 1), lambda p, qi, ki: (0, 0)),
        ],
        out_specs=pl.BlockSpec((bq, GW), lambda p, qi, ki: (qi[p], 0)),
        scratch_shapes=[
            pltpu.VMEM((H, SUBLANES, bq), F32),
            pltpu.VMEM((H, VT_PAD, bq), F32),
        ],
    )
    return pl.pallas_call(
        functools.partial(_mla_prompt_kernel, bq=bq, bk=bk, qw=min(bq, 256), ahead=2),
        grid_spec=grid_spec,
        out_shape=jax.ShapeDtypeStruct((t, GW), BF16),
        compiler_params=_cparams("arbitrary"),
        name="mla_prompt",
    )(qi, ki, qc, kc, vt, ghc)


def _mla_sample_kernel(q_ref, kp_ref, vp_ref, kn_ref, vn_ref, gh_ref, out_ref):
    for h in range(H):
        hs = slice(h * HEAD_PAD, (h + 1) * HEAD_PAD)
        q = q_ref[:, hs]
        s1 = _dot_nt(q, kp_ref[:, hs])
        s2 = _dot_nt(q, kn_ref[:, hs])
        m = jnp.maximum(jnp.max(s1, axis=-1, keepdims=True), jnp.max(s2, axis=-1, keepdims=True))
        acc = (_dot(jnp.exp2(s1 - m).astype(BF16), vp_ref[:, hs])
               + _dot(jnp.exp2(s2 - m).astype(BF16), vn_ref[:, hs]))
        _mla_finish(out_ref, gh_ref, h, acc)


def _mla_sample(qc3, kcp, vpp, kcn, vpn, gh):
    b, s, _ = qc3.shape
    npast = kcp.shape[1]
    new = pl.BlockSpec((None, s, DPAD), lambda i: (i, 0, 0))
    past = pl.BlockSpec((None, npast, DPAD), lambda i: (i, 0, 0))
    return pl.pallas_call(
        _mla_sample_kernel,
        grid=(b,),
        in_specs=[new, past, past, new, new, pl.BlockSpec((1, GW), lambda i: (0, 0))],
        out_specs=pl.BlockSpec((None, s, GW), lambda i: (i, 0, 0)),
        out_shape=jax.ShapeDtypeStruct((b, s, GW), BF16),
        compiler_params=_cparams("parallel"),
        name="mla_sample",
    )(qc3, kcp, vpp, kcn, vpn, gh)


def _outproj_kernel(x_ref, a_ref, b_ref, c_ref, d_ref, w_ref, out_ref):
    acc = x_ref[...]
    for g, m_ref in enumerate((a_ref, b_ref, c_ref, d_ref)):
        acc = acc + _dot(m_ref[...], w_ref[g * GW:(g + 1) * GW, :])
    out_ref[...] = acc


def _outproj(x, ma, mb, mc, md, w, tm):
    n = x.shape[0]
    mix = pl.BlockSpec((tm, GW), lambda i: (i, 0))
    return pl.pallas_call(
        _outproj_kernel,
        grid=(n // tm,),
        in_specs=[pl.BlockSpec((tm, D_MODEL), lambda i: (i, 0)), mix, mix, mix, mix,
                  pl.BlockSpec((D_MODEL, D_MODEL), lambda i: (0, 0))],
        out_specs=pl.BlockSpec((tm, D_MODEL), lambda i: (i, 0)),
        out_shape=jax.ShapeDtypeStruct((n, D_MODEL), F32),
        compiler_params=_cparams("parallel"),
        name="outproj",
    )(x, ma, mb, mc, md, w)


def _ffn_kernel(*refs, seq_len, final_norm, tf):
    if seq_len is None:
        (x_ref, g_ref, wup_ref, cw_ref, wd_ref, gf_ref, out_ref, ga_ref, act_scr, carry_scr) = refs
    else:
        (x_ref, g_ref, wup_ref, cw_ref, wd_ref, gf_ref, h1_ref, h2_ref, out_ref, ga_ref, act_scr) = refs
    tm = x_ref.shape[0]
    nj = D_FF // tf
    h = _rms(x_ref[...], g_ref[...]).astype(BF16)
    row = _iota((tm, tf), 0)

    if seq_len is None:
        @pl.when(pl.program_id(0) == 0)
        def _():
            carry_scr[...] = jnp.zeros(carry_scr.shape, F32)

    def up(j):
        cols = slice(j * tf, (j + 1) * tf)
        ucols = slice(D_FF + j * tf, D_FF + (j + 1) * tf)
        return _dot(h, wup_ref[:, cols]), _dot(h, wup_ref[:, ucols])

    def gate(j, ga, u):
        cols = slice(j * tf, (j + 1) * tf)
        r1 = pltpu.roll(ga, 1, axis=0)
        r2 = pltpu.roll(ga, 2, axis=0)
        if seq_len is None:
            c1 = carry_scr[SUBLANES - 1:SUBLANES, cols]
            c2 = carry_scr[SUBLANES - 2:SUBLANES - 1, cols]
            prev1 = jnp.where(row >= 1, r1, c1)
            prev2 = jnp.where(row >= 2, r2, jnp.where(row == 1, c1, c2))
            tail = ga[tm - SUBLANES:, :]
            carry_scr[:, cols] = tail
            ga_ref[:, cols] = tail
        else:
            t = row % seq_len
            prev1 = jnp.where(t >= 1, r1, h1_ref[:, cols])
            prev2 = jnp.where(t >= 2, r2, h2_ref[:, cols])
            ga_ref[:, cols] = ga
        conv = prev2 * cw_ref[0:1, cols] + prev1 * cw_ref[1:2, cols] + ga * cw_ref[2:3, cols]
        act_scr[:, cols] = (conv * _sigmoid(conv) * u).astype(BF16)

    pending = up(0)
    for j in range(nj):
        nxt = up(j + 1) if j + 1 < nj else None
        gate(j, *pending)
        pending = nxt
    y = x_ref[...] + _dot(act_scr[...], wd_ref[...])
    if final_norm:
        y = _rms(y, gf_ref[...])
    out_ref[...] = y


def _ffn(x, g, w_up, cw, w_down, gf, h1, h2, *, tm, tf, seq_len, final_norm):
    n = x.shape[0]
    ni = n // tm
    resident = lambda a, b: pl.BlockSpec((a, b), lambda i: (0, 0), pipeline_mode=pl.Buffered(1))
    in_specs = [
        pl.BlockSpec((tm, D_MODEL), lambda i: (i, 0)),
        resident(1, D_MODEL),
        resident(D_MODEL, 2 * D_FF),
        resident(3, D_FF),
        resident(D_FF, D_MODEL),
        resident(1, D_MODEL),
    ]
    args = [x, g, w_up, cw, w_down, gf]
    scratch = [pltpu.VMEM((tm, D_FF), BF16)]
    if seq_len is None:
        ga_spec = pl.BlockSpec((None, SUBLANES, D_FF), lambda i: (i, 0, 0))
        ga_shape = jax.ShapeDtypeStruct((ni, SUBLANES, D_FF), F32)
        scratch.append(pltpu.VMEM((SUBLANES, D_FF), F32))
    else:
        in_specs += [pl.BlockSpec((tm, D_FF), lambda i: (i, 0))] * 2
        args += [h1, h2]
        ga_spec = pl.BlockSpec((tm, D_FF), lambda i: (i, 0))
        ga_shape = jax.ShapeDtypeStruct((n, D_FF), F32)
    return pl.pallas_call(
        functools.partial(_ffn_kernel, seq_len=seq_len, final_norm=final_norm, tf=tf),
        grid=(ni,),
        in_specs=in_specs,
        out_specs=[pl.BlockSpec((tm, D_MODEL), lambda i: (i, 0)), ga_spec],
        out_shape=[jax.ShapeDtypeStruct((n, D_MODEL), F32), ga_shape],
        scratch_shapes=scratch,
        compiler_params=_cparams("arbitrary"),
        name="ffn",
    )(*args)


def _rope_tables(pos):
    half = QK_ROPE // 2
    inv = ROPE_THETA ** (-jnp.arange(half, dtype=F32) / half)
    ang = pos.astype(F32)[:, None] * inv[None, :]
    cos, sin = jnp.cos(ang), jnp.sin(ang)
    z32 = jnp.zeros((pos.shape[0], 32), F32)
    return jnp.concatenate([cos, cos, z32, -sin, sin, z32], -1)


def _rel_bias(table, n_past, n_q, n_k):
    dmax = n_past + n_q - 1
    dmin = n_past - n_k + 1
    diag = table[:, jnp.clip(jnp.arange(dmax, dmin - 1, -1), -REL_MAX, REL_MAX) + REL_MAX]
    return jnp.stack([diag[:, n_q - 1 - i:n_q - 1 - i + n_k] for i in range(n_q)], axis=1)


def _swap_halves(w):
    half = w.shape[-1] // 2
    return jnp.concatenate([w[..., half:], w[..., :half]], -1)


def _layer_weights(lw):
    (g_mix, w_in, a_rel_bias, b_i_bias, b_f_bias, c_conv_w, c_a_log, c_dt_bias,
     d_g_q, d_w_q_up, d_g_kv, d_w_kv_up, g_head, w_out, g_ffn, w_up, f_conv_w, w_down) = lw
    o = 0
    cols = {}
    for name, size in (("a", 3 * GW), ("b", 4 * GW), ("bg", 2 * H), ("c", 3 * GW), ("cz", GW),
                       ("cg", 2 * H), ("dq", Q_LORA), ("dkv", KV_LORA), ("dkr", QK_ROPE)):
        cols[name] = w_in[:, o:o + size]
        o += size
    gates = jnp.concatenate([cols["bg"], cols["cg"]], -1)
    pad16 = jnp.zeros((D_MODEL, 16), F32)
    pad32 = jnp.zeros((D_MODEL, 32), F32)
    w_perm = jnp.concatenate([cols["c"], cols["a"], cols["dq"], cols["dkv"], cols["dkr"], gates, pad16,
                              _swap_halves(cols["dkr"]), pad32, cols["b"], cols["cz"]], -1)
    zc = lambda n: jnp.zeros((1, n), F32)
    zr = lambda n: jnp.zeros((n, 1), F32)
    bias_c = jnp.concatenate([zc(GATE_OFF), b_i_bias[None], b_f_bias[None], zc(LANES - GATE_OFF - 2 * H)], -1)
    bias_r = jnp.concatenate([b_i_bias[:, None], b_f_bias[:, None], zr(2 * H)], 0)
    alog_c = jnp.concatenate([zc(GATE_OFF + 3 * H), c_a_log[None], zc(LANES - GATE_OFF - 4 * H)], -1)
    alog_r = jnp.concatenate([zr(3 * H), c_a_log[:, None]], 0)
    dt_c = jnp.concatenate([zc(GATE_OFF + 3 * H), c_dt_bias[None], zc(LANES - GATE_OFF - 4 * H)], -1)
    dt_r = jnp.concatenate([zr(3 * H), c_dt_bias[:, None]], 0)

    wq = d_w_q_up.reshape(Q_LORA, H, QK_NOPE + QK_ROPE)
    z_h32 = jnp.zeros((Q_LORA, H, 32), F32)
    wq_full = jnp.concatenate([wq, z_h32], -1).reshape(Q_LORA, DPAD)
    wq_part = jnp.concatenate([jnp.zeros((Q_LORA, H, QK_NOPE), F32), _swap_halves(wq[..., QK_NOPE:]), z_h32],
                              -1).reshape(Q_LORA, DPAD)
    wkv = d_w_kv_up.reshape(KV_LORA, H, 2 * DH)
    z_h64 = jnp.zeros((KV_LORA, H, DH), F32)
    wk_full = jnp.concatenate([wkv[..., :DH], z_h64], -1).reshape(KV_LORA, DPAD)
    wv_full = jnp.concatenate([wkv[..., DH:], z_h64], -1).reshape(KV_LORA, DPAD)
    place = jnp.concatenate([jnp.zeros((QK_ROPE, QK_NOPE), F32), jnp.eye(QK_ROPE, dtype=F32),
                             jnp.zeros((QK_ROPE, 32), F32)], -1)
    pmat = jnp.concatenate([place] * H, -1)
    return dict(
        g_mix=g_mix[None], w_in=w_perm.astype(BF16), w_gt=gates.T.astype(BF16),
        table=a_rel_bias, bias_c=bias_c, bias_r=bias_r, alog_c=alog_c, alog_r=alog_r, dt_c=dt_c, dt_r=dt_r,
        c_conv_w=c_conv_w, g_q=d_g_q[None], g_kv=d_g_kv[None],
        wq=wq_full.astype(BF16), wqp=wq_part.astype(BF16), wk=wk_full.astype(BF16), wv=wv_full.astype(BF16),
        pmat=pmat.astype(BF16),
        vones=(jnp.arange(DPAD) % HEAD_PAD == DH).astype(F32)[None],
        wvt=jnp.concatenate([wkv[..., DH:], jnp.zeros((KV_LORA, H, VT_PAD - DH), F32)], -1)
        .reshape(KV_LORA, VT_ROWS).T.astype(BF16),
        vonest=(jnp.arange(VT_ROWS) % VT_PAD == DH).astype(F32)[:, None],
        g_head=g_head.reshape(4, 1, GW), w_out=w_out.astype(BF16),
        g_ffn=g_ffn[None], w_up=w_up.astype(BF16), f_conv_w=f_conv_w, w_down=w_down.astype(BF16))


def _gates_t3(gt, b, t, l):
    return gt.reshape(N_GATES, b, t // l, l).transpose(1, 2, 0, 3)


def _layer(x, offset, st, w, gf, final_norm, cfg):
    b, t, _ = x.shape
    n = b * t
    first = st is None
    x2 = x.reshape(n, D_MODEL)
    proj, gt = _inproj(x2, w["g_mix"], w["w_in"], w["w_gt"], cfg["tm"])
    proj3 = proj.reshape(b, t, PROJ_W)
    gh = w["g_head"]
    l = min(t, CHUNK)
    gt3 = _gates_t3(gt, b, t, l)

    new_ak = proj3[:, t - min(A_PAST, t):, COL_A + GW:COL_A + 2 * GW].reshape(b, -1, H, DH)
    new_av = proj3[:, t - min(A_PAST, t):, COL_A + 2 * GW:COL_A + 3 * GW].reshape(b, -1, H, DH)
    if first:
        bias = _rel_bias(w["table"], A_PAST, CHUNK, A_PAST + CHUNK)
        oa = _band_prompt(proj, bias, gh[0])
    else:
        npast = st[0].shape[1]
        bias = _rel_bias(w["table"], npast, t, npast + t)
        oa = _band_sample(proj3, st[0].reshape(b, npast, GW), st[1].reshape(b, npast, GW), bias, gh[0])
        oa = oa.reshape(n, GW)

    if first:
        c0 = jnp.zeros((b, H, DH, DH), F32)
        n0 = jnp.zeros((b, H, DH), F32)
        m0 = jnp.zeros((b, 1, H), F32)
    else:
        c0, n0, m0 = st[2], st[3], st[4][:, None, :]
    if l == CHUNK:
        ob, cbd, nrow, mrow = _mlstm64(proj3, gt3, w["bias_c"], w["bias_r"], gh[1], _to_block_diag(c0),
                                       n0.reshape(b, 1, GW), jnp.repeat(m0, DH, axis=-1), cfg["nck"])
        new_bc, new_bn, new_bm = _from_block_diag(cbd), nrow.reshape(b, H, DH), mrow[:, 0, ::DH]
    else:
        ob, new_bc, new_bn, new_bm = _mlstm(proj3, gt3, w["bias_c"], w["bias_r"], gh[1], c0, n0, m0,
                                            l, cfg["nck"])
        new_bm = new_bm[:, 0, :]

    if first:
        hist8 = jnp.zeros((b, SUBLANES, 3 * GW), F32)
        s0 = jnp.zeros((b, H, DH, DH), F32)
    else:
        hist8 = jnp.concatenate([jnp.zeros((b, SUBLANES - 3, 3 * GW), F32), st[6]], 1)
        s0 = st[5]
    gdn_args = (proj3, gt3, hist8, w["c_conv_w"], w["alog_c"], w["alog_r"], w["dt_c"], w["dt_r"], gh[2])
    if l == CHUNK:
        oc, sbd = _gdn64(*gdn_args, _to_block_diag(s0), cfg["nck"])
        new_cs = _from_block_diag(sbd)
    else:
        oc, new_cs = _gdn(*gdn_args, s0, l, cfg["nck"])
    new_cconv = proj3[:, t - 3:, COL_CX:COL_CX + 3 * GW]

    pos = jnp.arange(t, dtype=jnp.int32) + offset
    ka = jnp.tile(_rope_tables(pos), (b, 1))
    ckv, kpe, qc = _dprep(proj, w["g_q"], w["g_kv"], w["wq"], w["wqp"], ka, cfg["tm_d"])
    if first:
        kc, vt = _kvup_t(ckv, kpe, w["wk"], w["wvt"], w["pmat"], w["vonest"], cfg["tm_kv"])
        od = _mla_prompt(qc, kc, vt, gh[3].T, cfg["mla_bq"], cfg["mla_bk"])
    else:
        npast = st[7].shape[1]
        kv_w = (w["wk"], w["wv"], w["pmat"], w["vones"])
        kcp, vpp = _kvup(st[7].reshape(b * npast, KV_LORA), st[8].reshape(b * npast, QK_ROPE), *kv_w, npast)
        kcn, vpn = _kvup(ckv, kpe, *kv_w, n)
        od = _mla_sample(qc.reshape(b, t, DPAD), kcp.reshape(b, npast, DPAD), vpp.reshape(b, npast, DPAD),
                         kcn.reshape(b, t, DPAD), vpn.reshape(b, t, DPAD), gh[3])
        od = od.reshape(n, GW)

    x2 = _outproj(x2, oa, ob.reshape(n, GW), oc.reshape(n, GW), od, w["w_out"], cfg["tm"])

    if first:
        y, ga_tail = _ffn(x2, w["g_ffn"], w["w_up"], w["f_conv_w"], w["w_down"], gf, None, None,
                          tm=cfg["tm"], tf=cfg["tf"], seq_len=None, final_norm=final_norm)
        new_fconv = ga_tail[-1, SUBLANES - 2:, :][None]
    else:
        hist = st[9]
        zrow = jnp.zeros((b, t - 1, D_FF), F32)
        h1 = jnp.concatenate([hist[:, 1:2], zrow], 1).reshape(n, D_FF)
        h2 = jnp.concatenate([hist, zrow[:, 1:]], 1).reshape(n, D_FF)
        y, ga = _ffn(x2, w["g_ffn"], w["w_up"], w["f_conv_w"], w["w_down"], gf, h1, h2,
                     tm=cfg["tm"], tf=cfg["tf"], seq_len=t, final_norm=final_norm)
        new_fconv = ga.reshape(b, t, D_FF)[:, t - 2:]
    state = (new_ak, new_av, new_bc, new_bn, new_bm, new_cs, new_cconv,
             ckv.reshape(b, t, KV_LORA), kpe.reshape(b, t, QK_ROPE), new_fconv)
    return y.reshape(b, t, D_MODEL), state


def _config(b, t):
    n = b * t
    tm = min(n, 1024)
    return dict(tm=tm, tf=256, nck=1 if t <= CHUNK else 8,
                tm_d=min(n, 1024), tm_kv=min(n, 2048), mla_bq=min(t, 512), mla_bk=min(t, 1024))


def kernel(x_prompt, x_sample, cache_a_k, cache_a_v, state_b_c, state_b_n, state_b_m, state_c_s, cache_c_conv, cache_d_ckv, cache_d_kpe, cache_ffn_conv, g_mix, w_in, a_rel_bias, b_i_bias, b_f_bias, c_conv_w, c_a_log, c_dt_bias, d_g_q, d_w_q_up, d_g_kv, d_w_kv_up, g_head, w_out, g_ffn, w_up, f_conv_w, w_down, g_final):
    layer_w = (g_mix, w_in, a_rel_bias, b_i_bias, b_f_bias, c_conv_w, c_a_log, c_dt_bias,
               d_g_q, d_w_q_up, d_g_kv, d_w_kv_up, g_head, w_out, g_ffn, w_up, f_conv_w, w_down)
    depth = g_mix.shape[0]
    past = cache_d_ckv.shape[2]
    xp, xs = x_prompt, x_sample
    cfg_p = _config(*x_prompt.shape[:2])
    cfg_s = _config(*x_sample.shape[:2])
    gf = g_final[None]
    new_p, new_s = [], []
    for l in range(depth):
        w = _layer_weights(tuple(a[l] for a in layer_w))
        last = l == depth - 1
        xp, sp_l = _layer(xp, 0, None, w, gf, last, cfg_p)
        st = (cache_a_k[l], cache_a_v[l], state_b_c[l], state_b_n[l], state_b_m[l],
              state_c_s[l], cache_c_conv[l], cache_d_ckv[l], cache_d_kpe[l], cache_ffn_conv[l])
        xs, ss_l = _layer(xs, past, st, w, gf, last, cfg_s)
        new_p.append(sp_l)
        new_s.append(ss_l)
    outs = [xp, xs]
    for i in range(10):
        outs.append(jnp.stack([s[i] for s in new_p]))
        outs.append(jnp.stack([s[i] for s in new_s]))
    return tuple(outs)
```

```python
import functools
import math

import jax
import jax.numpy as jnp
from jax import lax
from jax.experimental import pallas as pl
from jax.experimental.pallas import tpu as pltpu

F32 = jnp.float32
BF16 = jnp.bfloat16

D_MODEL = 1024
CHUNK = 64
H = 4
DH = 64
GW = H * DH
A_PAST = 8 * CHUNK
REL_MAX = 2 * CHUNK
Q_LORA = 256
KV_LORA = 128
QK_NOPE = 64
QK_ROPE = 32
ROPE_THETA = 10000.0
MLA_SCALE = (QK_NOPE + QK_ROPE) ** -0.5
D_FF = 2816
EPS = 1e-6

COL_CX = 0
COL_A = 3 * GW
COL_TAIL = 6 * GW
TAIL_W = 512
COL_B = COL_TAIL + TAIL_W
COL_CZ = COL_B + 4 * GW
PROJ_W = COL_CZ + GW
GATE_BLK = (COL_TAIL + 384) // 128
GATE_OFF = 32
N_GATES = 16

LANES = 128
SUBLANES = 8
VMEM_LIMIT = 56 * 1024 * 1024

NEG_INF = float("-inf")


def _cparams(*sem):
    return pltpu.CompilerParams(dimension_semantics=sem, vmem_limit_bytes=VMEM_LIMIT)


def _dot(a, b):
    return jnp.dot(a, b, preferred_element_type=F32)


def _dot_nt(a, b):
    return lax.dot_general(a, b, (((1,), (1,)), ((), ())), preferred_element_type=F32)


def _dot_tn(a, b):
    return lax.dot_general(a, b, (((0,), (0,)), ((), ())), preferred_element_type=F32)


def _split3(x):
    hi = x.astype(BF16)
    r1 = x - hi.astype(F32)
    mid = r1.astype(BF16)
    lo = (r1 - mid.astype(F32)).astype(BF16)
    return hi, mid, lo


def _rms(x, g):
    return x * lax.rsqrt(jnp.mean(x * x, axis=-1, keepdims=True) + EPS) * g


def _log_sigmoid(x):
    return jnp.minimum(x, 0.0) - jnp.log1p(jnp.exp(-jnp.abs(x)))


def _softplus(x):
    return jnp.maximum(x, 0.0) + jnp.log1p(jnp.exp(-jnp.abs(x)))


def _sigmoid(x):
    return 1.0 / (1.0 + jnp.exp(-x))


def _iota(shape, dim):
    return lax.broadcasted_iota(jnp.int32, shape, dim)


def _inproj_kernel(x_ref, g_ref, w_ref, wgt_ref, proj_ref, gt_ref):
    h = _rms(x_ref[...], g_ref[...]).astype(BF16)
    gt_ref[...] = _dot_nt(wgt_ref[...], h)
    proj_ref[...] = _dot(h, w_ref[...])


def _inproj(x, g, w, wgt, tm):
    n = x.shape[0]
    resident = lambda a, b: pl.BlockSpec((a, b), lambda i: (0, 0), pipeline_mode=pl.Buffered(1))
    return pl.pallas_call(
        _inproj_kernel,
        grid=(n // tm,),
        in_specs=[
            pl.BlockSpec((tm, D_MODEL), lambda i: (i, 0)),
            resident(1, D_MODEL),
            resident(D_MODEL, PROJ_W),
            resident(N_GATES, D_MODEL),
        ],
        out_specs=[
            pl.BlockSpec((tm, PROJ_W), lambda i: (i, 0)),
            pl.BlockSpec((N_GATES, tm), lambda i: (0, i)),
        ],
        out_shape=[
            jax.ShapeDtypeStruct((n, PROJ_W), F32),
            jax.ShapeDtypeStruct((N_GATES, n), F32),
        ],
        compiler_params=_cparams("parallel"),
        name="inproj",
    )(x, g, w, wgt)


def _head_norm_store(out_ref, rows, h, o, gh_ref):
    g = gh_ref[:, h * DH:(h + 1) * DH]
    y = o * lax.rsqrt(jnp.mean(o * o, axis=-1, keepdims=True) + EPS) * g
    out_ref[rows, h * DH:(h + 1) * DH] = y.astype(out_ref.dtype)


def _band_prompt_kernel(q_ref, kp_ref, kc_ref, vp_ref, vc_ref, biast_ref, ghc_ref, out_ref, *, qb):
    nk = 2 * qb
    one_lane = (_iota((nk, DH), 1) == 0).astype(BF16)

    def run(first):
        def scores(h):
            hs = slice(h * DH, (h + 1) * DH)
            kcat = jnp.concatenate([kp_ref[:, hs], kc_ref[:, hs]], axis=0).astype(BF16)
            return _dot_nt(kcat, (q_ref[:, hs] * (DH ** -0.5)).astype(BF16))

        def attend(h, st):
            hs = slice(h * DH, (h + 1) * DH)
            st = st + biast_ref[h]
            if first:
                st = jnp.where(_iota((nk, qb), 0) >= qb, st, NEG_INF)
            p = jnp.exp(st - jnp.max(st, axis=0, keepdims=True)).astype(BF16)
            vcat = jnp.concatenate([vp_ref[:, hs], vc_ref[:, hs]], axis=0).astype(BF16)
            acc = _dot_tn(jnp.concatenate([vcat, one_lane], axis=1), p)
            o = acc[:DH] / acc[DH:DH + 1]
            ms = jnp.mean(o * o, axis=0, keepdims=True)
            return o * lax.rsqrt(ms + EPS) * ghc_ref[hs, :]

        ys = []
        pending = scores(0)
        for h in range(H):
            nxt = scores(h + 1) if h + 1 < H else None
            ys.append(attend(h, pending))
            pending = nxt
        out_ref[...] = jnp.concatenate(ys, axis=0).T.astype(out_ref.dtype)

    @pl.when(pl.program_id(0) == 0)
    def _():
        run(True)

    @pl.when(pl.program_id(0) > 0)
    def _():
        run(False)


def _band_bias_t(bias, qb):
    band = bias.shape[-1]
    per_chunk = [jnp.pad(bias, ((0, 0), (0, 0), (c * CHUNK, 2 * qb - band - c * CHUNK)),
                         constant_values=NEG_INF) for c in range(qb // CHUNK)]
    return jnp.concatenate(per_chunk, axis=1).transpose(0, 2, 1)


def _band_prompt(proj, bias, ghc, qb=A_PAST):
    t = proj.shape[0]
    assert qb == A_PAST and t % qb == 0
    prev = lambda i: jnp.maximum(i - 1, 0)
    cq = COL_A // GW
    return pl.pallas_call(
        functools.partial(_band_prompt_kernel, qb=qb),
        grid=(t // qb,),
        in_specs=[
            pl.BlockSpec((qb, GW), lambda i: (i, cq)),
            pl.BlockSpec((qb, GW), lambda i: (prev(i), cq + 1)),
            pl.BlockSpec((qb, GW), lambda i: (i, cq + 1)),
            pl.BlockSpec((qb, GW), lambda i: (prev(i), cq + 2)),
            pl.BlockSpec((qb, GW), lambda i: (i, cq + 2)),
            pl.BlockSpec((H, 2 * qb, qb), lambda i: (0, 0, 0), pipeline_mode=pl.Buffered(1)),
            pl.BlockSpec((GW, 1), lambda i: (0, 0)),
        ],
        out_specs=pl.BlockSpec((qb, GW), lambda i: (i, 0)),
        out_shape=jax.ShapeDtypeStruct((t, GW), BF16),
        compiler_params=_cparams("parallel"),
        name="band_prompt",
    )(proj, proj, proj, proj, proj, _band_bias_t(bias, qb), ghc)


def _band_sample_kernel(q_ref, k_ref, v_ref, ck_ref, cv_ref, bias_ref, gh_ref, out_ref):
    npast = ck_ref.shape[0]
    rows = slice(0, q_ref.shape[0])
    for h in range(H):
        hs = slice(h * DH, (h + 1) * DH)
        q = q_ref[:, hs].astype(BF16)
        s1 = _dot_nt(q, ck_ref[:, hs].astype(BF16)) * (DH ** -0.5) + bias_ref[h, :, :npast]
        s2 = _dot_nt(q, k_ref[:, hs].astype(BF16)) * (DH ** -0.5) + bias_ref[h, :, npast:]
        m = jnp.maximum(jnp.max(s1, axis=-1, keepdims=True), jnp.max(s2, axis=-1, keepdims=True))
        p1 = jnp.exp(s1 - m)
        p2 = jnp.exp(s2 - m)
        l = jnp.sum(p1, axis=-1, keepdims=True) + jnp.sum(p2, axis=-1, keepdims=True)
        o = (_dot(p1.astype(BF16), cv_ref[:, hs].astype(BF16))
             + _dot(p2.astype(BF16), v_ref[:, hs].astype(BF16))) / l
        _head_norm_store(out_ref, rows, h, o, gh_ref)


def _band_sample(proj3, ck, cv, bias, gh):
    b, s, _ = proj3.shape
    npast = ck.shape[1]
    return pl.pallas_call(
        _band_sample_kernel,
        grid=(b,),
        in_specs=[
            pl.BlockSpec((None, s, GW), lambda i: (i, 0, COL_A // GW)),
            pl.BlockSpec((None, s, GW), lambda i: (i, 0, COL_A // GW + 1)),
            pl.BlockSpec((None, s, GW), lambda i: (i, 0, COL_A // GW + 2)),
            pl.BlockSpec((None, npast, GW), lambda i: (i, 0, 0)),
            pl.BlockSpec((None, npast, GW), lambda i: (i, 0, 0)),
            pl.BlockSpec((H, s, npast + s), lambda i: (0, 0, 0)),
            pl.BlockSpec((1, GW), lambda i: (0, 0)),
        ],
        out_specs=pl.BlockSpec((None, s, GW), lambda i: (i, 0, 0)),
        out_shape=jax.ShapeDtypeStruct((b, s, GW), BF16),
        compiler_params=_cparams("parallel"),
        name="band_sample",
    )(proj3, proj3, proj3, ck, cv, bias, gh)


def _cumsum_cols(x, lo_tri):
    return sum(_dot(lo_tri, part) for part in _split3(x))


def _cumsum_rows(x, up_tri):
    return sum(_dot(part, up_tri) for part in _split3(x))


def _tri_masks(l):
    r = _iota((l, l), 0)
    c = _iota((l, l), 1)
    return r >= c, r > c


def _mlstm_kernel(q_ref, k_ref, v_ref, o_ref, gc_ref, gr_ref, bc_ref, br_ref, gh_ref,
                  c0_ref, n0_ref, m0_ref, out_ref, c_ref, n_ref, m_ref, *, l, nck):
    @pl.when(pl.program_id(1) == 0)
    def _():
        c_ref[...] = c0_ref[...]
        n_ref[...] = n0_ref[...]
        m_ref[...] = m0_ref[...]

    incl, _ = _tri_masks(l)
    lo_tri = incl.astype(BF16)
    up_tri = (_iota((l, l), 0) <= _iota((l, l), 1)).astype(BF16)

    probs = []
    for ck in range(nck):
        rows = slice(ck * l, (ck + 1) * l)
        gcol = gc_ref[rows, :] + bc_ref[...]
        grow = gr_ref[ck] + br_ref[...]
        gcs = _cumsum_cols(_log_sigmoid(gcol), lo_tri)
        grs = _cumsum_rows(_log_sigmoid(grow), up_tri)
        for h in range(H):
            hs = slice(h * DH, (h + 1) * DH)
            ig_c = gcol[:, GATE_OFF + h:GATE_OFF + h + 1]
            g_c = gcs[:, GATE_OFF + H + h:GATE_OFF + H + h + 1]
            ig_r = grow[h:h + 1, :]
            g_r = grs[H + h:H + h + 1, :]
            q = q_ref[rows, hs]
            kf = k_ref[rows, hs] * (DH ** -0.5)
            lmat = jnp.where(incl, g_c - g_r + ig_r, NEG_INF)
            probs.append(dict(
                rows=rows, h=h, q=q, kf=kf, qb=q.astype(BF16), kb=kf.astype(BF16),
                vb=v_ref[rows, hs].astype(BF16), lmat=lmat, lmax=jnp.max(lmat, axis=-1, keepdims=True),
                g_c=g_c, ig_c=ig_c, g_last=g_c[l - 1:l, :]))
    qks = [_dot_nt(p["qb"], p["kb"]) for p in probs]

    ms = [m_ref[:, h:h + 1] for h in range(H)]
    for p in probs:
        m_old = ms[p["h"]]
        p["linter"] = p["g_c"] + m_old
        p["mt"] = jnp.maximum(p["linter"], p["lmax"])
        m_new = p["mt"][l - 1:l, :]
        p["dprev"] = jnp.exp(p["g_last"] + m_old - m_new)
        p["kw"] = p["kf"] * jnp.exp(p["g_last"] - p["g_c"] + p["ig_c"] - m_new)
        ms[p["h"]] = m_new
    ws_ = [qk * jnp.exp(p["lmat"] - p["mt"]) for p, qk in zip(probs, qks)]
    wvs = [_dot(w.astype(BF16), p["vb"]) for p, w in zip(probs, ws_)]
    upds = [_dot_tn(p["kw"].astype(BF16), p["vb"]) for p in probs]

    cs = [c_ref[h] for h in range(H)]
    ns = [n_ref[h:h + 1, :] for h in range(H)]
    qcs, qns = [], []
    for p, upd in zip(probs, upds):
        h = p["h"]
        qcs.append(_dot(p["qb"], cs[h].astype(BF16)))
        qns.append(jnp.sum(p["q"] * ns[h], axis=-1, keepdims=True))
        cs[h] = p["dprev"] * cs[h] + upd
        ns[h] = p["dprev"] * ns[h] + jnp.sum(p["kw"], axis=0, keepdims=True)
    for h in range(H):
        c_ref[h] = cs[h]
        n_ref[h:h + 1, :] = ns[h]
        m_ref[:, h:h + 1] = ms[h]

    wsums = [jnp.sum(w, axis=-1, keepdims=True) for w in ws_]
    obs = []
    for p, wsum, wv, qc, qn in zip(probs, wsums, wvs, qcs, qns):
        hs = slice(p["h"] * DH, (p["h"] + 1) * DH)
        inter = jnp.exp(p["linter"] - p["mt"])
        den = wsum + inter * qn
        hout = (wv + inter * qc) / jnp.maximum(jnp.abs(den), jnp.exp(-p["mt"]))
        obs.append(hout * _sigmoid(o_ref[p["rows"], hs]))
    msq = [jnp.mean(ob * ob, axis=-1, keepdims=True) for ob in obs]
    for p, ob, ms_ in zip(probs, obs, msq):
        hs = slice(p["h"] * DH, (p["h"] + 1) * DH)
        out_ref[p["rows"], hs] = (ob * lax.rsqrt(ms_ + EPS) * gh_ref[:, hs]).astype(out_ref.dtype)


def _mlstm(proj3, gt3, bias_c, bias_r, gh, c0, n0, m0, l, nck):
    b, t, _ = proj3.shape
    steps = t // (l * nck)
    blk = l * nck
    col = lambda j: pl.BlockSpec((None, blk, GW), lambda bi, s: (bi, s, j))
    state = lambda shp: pl.BlockSpec((None,) + shp, lambda bi, s: (bi,) + (0,) * len(shp))
    return pl.pallas_call(
        functools.partial(_mlstm_kernel, l=l, nck=nck),
        grid=(b, steps),
        in_specs=[
            col(COL_B // GW), col(COL_B // GW + 1), col(COL_B // GW + 2), col(COL_B // GW + 3),
            pl.BlockSpec((None, blk, LANES), lambda bi, s: (bi, s, GATE_BLK)),
            pl.BlockSpec((None, nck, N_GATES, l), lambda bi, s: (bi, s, 0, 0)),
            pl.BlockSpec((1, LANES), lambda bi, s: (0, 0)),
            pl.BlockSpec((N_GATES, 1), lambda bi, s: (0, 0)),
            pl.BlockSpec((1, GW), lambda bi, s: (0, 0)),
            state((H, DH, DH)), state((H, DH)), state((1, H)),
        ],
        out_specs=[
            pl.BlockSpec((None, blk, GW), lambda bi, s: (bi, s, 0)),
            state((H, DH, DH)), state((H, DH)), state((1, H)),
        ],
        out_shape=[
            jax.ShapeDtypeStruct((b, t, GW), BF16),
            jax.ShapeDtypeStruct((b, H, DH, DH), F32),
            jax.ShapeDtypeStruct((b, H, DH), F32),
            jax.ShapeDtypeStruct((b, 1, H), F32),
        ],
        compiler_params=_cparams("parallel", "arbitrary"),
        name="mlstm",
    )(proj3, proj3, proj3, proj3, proj3, gt3, bias_c, bias_r, gh, c0, n0, m0)


def _head_of(idx):
    return idx // DH


def _block_mask(n_rows, n_cols):
    return _head_of(_iota((n_rows, n_cols), 0)) == _head_of(_iota((n_rows, n_cols), 1))


def _expander(first_lane):
    r = _iota((LANES, GW), 0)
    c = _iota((LANES, GW), 1)
    return (r == first_lane + _head_of(c)).astype(BF16)


def _dot_stacked(parts, rhs):
    m = parts[0].shape[0]
    y = _dot(jnp.concatenate(parts, axis=0), rhs)
    return sum(y[i * m:(i + 1) * m] for i in range(len(parts)))


def _expand(x, e):
    return _dot_stacked(_split3(x), e)


def _head_sums(a, bones):
    return _dot_stacked(_split3(a), bones)


def _cumsum_cols_wide(x, lo_tri):
    w = x.shape[1]
    y = _dot(lo_tri, jnp.concatenate(_split3(x), axis=1))
    return y[:, :w] + y[:, w:2 * w] + y[:, 2 * w:]


def _row_select(x_t, first_row):
    r = _iota(x_t.shape, 0)
    c = _iota(x_t.shape, 1)
    return jnp.sum(jnp.where(r == first_row + _head_of(c), x_t, 0.0), axis=0, keepdims=True)


def _block_diag_rows(x, mask):
    return jnp.where(mask, jnp.concatenate([x] * H, axis=0), jnp.zeros((), x.dtype))


def _cummax_rows(x):
    rows = _iota(x.shape, 0)
    sh = 1
    while sh < x.shape[0]:
        x = jnp.maximum(x, jnp.where(rows >= sh, pltpu.roll(x, sh, axis=0), NEG_INF))
        sh *= 2
    return x


def _mlstm64_kernel(q_ref, k_ref, v_ref, o_ref, gc_ref, gr_ref, bc_ref, br_ref, gh_ref,
                    c0_ref, n0_ref, m0_ref, out_ref, c_ref, n_ref, m_ref, *, nck):
    l = CHUNK

    @pl.when(pl.program_id(1) == 0)
    def _():
        c_ref[...] = c0_ref[...]
        n_ref[...] = n0_ref[...]
        m_ref[...] = m0_ref[...]

    bmask = _block_mask(GW, GW)
    bones = bmask.astype(BF16)
    within = _iota((GW, GW), 0) % DH <= _iota((GW, GW), 1) % DH
    up_bd = jnp.logical_and(bmask, within).astype(BF16)
    lo_tri = (_iota((l, l), 0) >= _iota((l, l), 1)).astype(BF16)
    incl = _iota((l, GW), 0) >= _iota((l, GW), 1) % DH
    e_i = _expander(GATE_OFF)
    e_f = _expander(GATE_OFF + H)

    cks = []
    for ck in range(nck):
        rows = slice(ck * l, (ck + 1) * l)
        gcol = gc_ref[rows, :] + bc_ref[...]
        gcs = _cumsum_cols_wide(_log_sigmoid(gcol), lo_tri)
        g_c = _expand(gcs, e_f)
        i_c = _expand(gcol, e_i)
        grow = gr_ref[ck] + br_ref[...]
        grow_t = jnp.concatenate([grow] * H, axis=1)
        grs_t = _dot_stacked(_split3(_log_sigmoid(grow_t)), up_bd)
        a_r = _row_select(grow_t, 0) - _row_select(grs_t, H)
        lmat = jnp.where(incl, g_c + a_r, NEG_INF)
        lmax = g_c + _cummax_rows(i_c - g_c)
        q = q_ref[rows, :]
        kf = k_ref[rows, :] * (DH ** -0.5)
        cks.append(dict(rows=rows, g_c=g_c, i_c=i_c, lmat=lmat, lmax=lmax, q=q, kf=kf,
                        qb=q.astype(BF16), kb=kf.astype(BF16), vb=v_ref[rows, :].astype(BF16),
                        g_last=g_c[l - 1:l, :]))
    scs = [_dot_nt(p["qb"], _block_diag_rows(p["kb"], bmask)) for p in cks]

    m_run = m_ref[...]
    for p in cks:
        p["linter"] = p["g_c"] + m_run
        p["mt"] = jnp.maximum(p["linter"], p["lmax"])
        m_new = p["mt"][l - 1:l, :]
        p["dprev"] = jnp.exp(p["g_last"] + m_run - m_new)
        p["kw"] = p["kf"] * jnp.exp(p["g_last"] - p["g_c"] + p["i_c"] - m_new)
        m_run = m_new
    m_ref[...] = m_run
    wbs = [(s * jnp.exp(p["lmat"] - p["mt"])).astype(BF16) for p, s in zip(cks, scs)]
    nums = [_dot(w, _block_diag_rows(p["vb"], bmask)) for p, w in zip(cks, wbs)]
    wsums = [_dot(w, bones) for w in wbs]
    upds = [jnp.where(bmask, _dot_tn(p["kw"].astype(BF16), p["vb"]), 0.0) for p in cks]

    c_run = c_ref[...]
    n_run = n_ref[...]
    qcs, qns = [], []
    for p, upd in zip(cks, upds):
        qcs.append(_dot(p["qb"], c_run.astype(BF16)))
        qns.append(_dot((p["q"] * n_run).astype(BF16), bones))
        c_run = p["dprev"] * c_run + upd
        n_run = p["dprev"] * n_run + jnp.sum(p["kw"], axis=0, keepdims=True)
    c_ref[...] = c_run
    n_ref[...] = n_run

    obs = []
    for p, num, wsum, qc, qn in zip(cks, nums, wsums, qcs, qns):
        inter = jnp.exp(p["linter"] - p["mt"])
        den = wsum + inter * qn
        hout = (num + inter * qc) / jnp.maximum(jnp.abs(den), jnp.exp(-p["mt"]))
        obs.append(hout * _sigmoid(o_ref[p["rows"], :]))
    msq = [_head_sums(ob * ob, bones) * (1.0 / DH) for ob in obs]
    for p, ob, m2 in zip(cks, obs, msq):
        out_ref[p["rows"], :] = (ob * lax.rsqrt(m2 + EPS) * gh_ref[...]).astype(out_ref.dtype)


def _mlstm64(proj3, gt3, bias_c, bias_r, gh, c0, n0, m0, nck):
    b, t, _ = proj3.shape
    blk = CHUNK * nck
    steps = t // blk
    col = lambda j: pl.BlockSpec((None, blk, GW), lambda bi, s: (bi, s, j))
    state = lambda shp: pl.BlockSpec((None,) + shp, lambda bi, s: (bi,) + (0,) * len(shp))
    return pl.pallas_call(
        functools.partial(_mlstm64_kernel, nck=nck),
        grid=(b, steps),
        in_specs=[
            col(COL_B // GW), col(COL_B // GW + 1), col(COL_B // GW + 2), col(COL_B // GW + 3),
            pl.BlockSpec((None, blk, LANES), lambda bi, s: (bi, s, GATE_BLK)),
            pl.BlockSpec((None, nck, N_GATES, CHUNK), lambda bi, s: (bi, s, 0, 0)),
            pl.BlockSpec((1, LANES), lambda bi, s: (0, 0)),
            pl.BlockSpec((N_GATES, 1), lambda bi, s: (0, 0)),
            pl.BlockSpec((1, GW), lambda bi, s: (0, 0)),
            state((GW, GW)), state((1, GW)), state((1, GW)),
        ],
        out_specs=[
            pl.BlockSpec((None, blk, GW), lambda bi, s: (bi, s, 0)),
            state((GW, GW)), state((1, GW)), state((1, GW)),
        ],
        out_shape=[
            jax.ShapeDtypeStruct((b, t, GW), BF16),
            jax.ShapeDtypeStruct((b, GW, GW), F32),
            jax.ShapeDtypeStruct((b, 1, GW), F32),
            jax.ShapeDtypeStruct((b, 1, GW), F32),
        ],
        compiler_params=_cparams("parallel", "arbitrary"),
        name="mlstm64",
    )(proj3, proj3, proj3, proj3, proj3, gt3, bias_c, bias_r, gh, c0, n0, m0)


def _to_block_diag(c):
    b = c.shape[0]
    eye = jnp.eye(H, dtype=c.dtype)
    return jnp.einsum("bhde,hg->bhdge", c, eye).reshape(b, GW, GW)


def _from_block_diag(cbd):
    b = cbd.shape[0]
    c5 = cbd.reshape(b, H, DH, H, DH)
    return jnp.stack([c5[:, h, :, h, :] for h in range(H)], axis=1)


def _split2(x):
    hi = x.astype(BF16)
    lo = (x - hi.astype(F32)).astype(BF16)
    return hi, lo


def _dot_sp(a, b):
    return _dot(a[0], b[0]) + (_dot(a[0], b[1]) + _dot(a[1], b[0]))


def _unit_lower_inverses(nmats, l):
    eye = (_iota((l, l), 0) == _iota((l, l), 1)).astype(F32)
    ps = [eye - n for n in nmats]
    qs = [_split2(n) for n in nmats]
    qs = [_split2(_dot_sp(q, q)) for q in qs]
    power = 2
    while power < l:
        ps = [p + _dot_sp(_split2(p), q) for p, q in zip(ps, qs)]
        power *= 2
        if power < l:
            qs = [_split2(_dot_sp(q, q)) for q in qs]
    return ps


def _l2norm(x):
    return x * lax.rsqrt(jnp.sum(x * x, axis=-1, keepdims=True) + 1e-6)


def _gdn_kernel(x_ref, z_ref, gc_ref, gr_ref, hist_ref, cw_ref, ac_ref, ar_ref, dc_ref, dr_ref,
                gh_ref, s0_ref, out_ref, s_ref, carry_scr, *, l, nck):
    @pl.when(pl.program_id(1) == 0)
    def _():
        s_ref[...] = s0_ref[...]
        carry_scr[...] = hist_ref[...]

    blk = l * nck
    x = x_ref[...]
    ext = jnp.concatenate([carry_scr[...], x], axis=0)
    carry_scr[...] = x[blk - SUBLANES:, :]
    y = x * cw_ref[3:4, :]
    for j in range(1, 4):
        y = y + ext[SUBLANES - j:SUBLANES - j + blk, :] * cw_ref[3 - j:4 - j, :]
    y = y * _sigmoid(y)

    incl, strict = _tri_masks(l)
    lo_tri = incl.astype(BF16)
    up_tri = (_iota((l, l), 0) <= _iota((l, l), 1)).astype(BF16)

    qraw = [y[ck * l:(ck + 1) * l, h * DH:(h + 1) * DH] for ck in range(nck) for h in range(H)]
    kraw = [y[ck * l:(ck + 1) * l, GW + h * DH:GW + (h + 1) * DH] for ck in range(nck) for h in range(H)]
    vraw = [y[ck * l:(ck + 1) * l, 2 * GW + h * DH:2 * GW + (h + 1) * DH] for ck in range(nck) for h in range(H)]
    qnorm = [_l2norm(a) * (DH ** -0.5) for a in qraw]
    knorm = [_l2norm(a) for a in kraw]
    probs = []
    for ck in range(nck):
        rows = slice(ck * l, (ck + 1) * l)
        gcol = gc_ref[rows, :]
        grow = gr_ref[ck]
        beta_cs = _sigmoid(gcol)
        dec_c = -jnp.exp(ac_ref[...]) * _softplus(gcol + dc_ref[...])
        dec_r = -jnp.exp(ar_ref[...]) * _softplus(grow + dr_ref[...])
        gcs = _cumsum_cols(dec_c, lo_tri)
        grs = _cumsum_rows(dec_r, up_tri)
        for h in range(H):
            beta = beta_cs[:, GATE_OFF + 2 * H + h:GATE_OFF + 2 * H + h + 1]
            g_c = gcs[:, GATE_OFF + 3 * H + h:GATE_OFF + 3 * H + h + 1]
            g_r = grs[3 * H + h:3 * H + h + 1, :]
            q, k, v = qnorm[ck * H + h], knorm[ck * H + h], vraw[ck * H + h]
            decay = jnp.exp(jnp.where(incl, g_c - g_r, NEG_INF))
            eg = jnp.exp(g_c)
            g_last = g_c[l - 1:l, :]
            probs.append(dict(
                rows=rows, h=h, qb=q.astype(BF16), kb=k.astype(BF16), beta=beta, decay=decay,
                rhs=jnp.concatenate([v * beta, k * (beta * eg)], axis=-1),
                qeg=(q * eg).astype(BF16), kdec=(k * jnp.exp(g_last - g_c)).astype(BF16),
                sdec=jnp.exp(g_last)))
    kks = [_dot_nt(p["kb"], p["kb"]) for p in probs]
    qks = [_dot_nt(p["qb"], p["kb"]) for p in probs]
    a_lows = [jnp.where(strict, p["beta"] * kk * p["decay"], 0.0) for p, kk in zip(probs, kks)]
    attns = [(qk * p["decay"]).astype(BF16) for p, qk in zip(probs, qks)]
    tinvs = _unit_lower_inverses(a_lows, l)
    sols = [_dot_sp(_split2(t), _split2(p["rhs"])) for t, p in zip(tinvs, probs)]

    states = [s_ref[h] for h in range(H)]
    for ck in range(nck):
        ps = probs[ck * H:(ck + 1) * H]
        ss = sols[ck * H:(ck + 1) * H]
        at = attns[ck * H:(ck + 1) * H]
        sbs = [s.astype(BF16) for s in states]
        wss = [_dot(sol[:, DH:].astype(BF16), sb) for sol, sb in zip(ss, sbs)]
        qss = [_dot(p["qeg"], sb) for p, sb in zip(ps, sbs)]
        vnbs = [(sol[:, :DH] - ws).astype(BF16) for sol, ws in zip(ss, wss)]
        os_ = [qs + _dot(a, vnb) for qs, a, vnb in zip(qss, at, vnbs)]
        states = [p["sdec"] * s + _dot_tn(p["kdec"], vnb) for p, s, vnb in zip(ps, states, vnbs)]
        for p, o in zip(ps, os_):
            hs = slice(p["h"] * DH, (p["h"] + 1) * DH)
            zg = z_ref[p["rows"], hs]
            yo = (o * lax.rsqrt(jnp.mean(o * o, axis=-1, keepdims=True) + EPS) * gh_ref[:, hs]
                  * (zg * _sigmoid(zg)))
            out_ref[p["rows"], hs] = yo.astype(out_ref.dtype)
    for h in range(H):
        s_ref[h] = states[h]


def _gdn(proj3, gt3, hist8, cw, a_c, a_r, dt_c, dt_r, gh, s0, l, nck):
    b, t, _ = proj3.shape
    blk = l * nck
    steps = t // blk
    state = lambda shp: pl.BlockSpec((None,) + shp, lambda bi, s: (bi,) + (0,) * len(shp))
    const = lambda shp: pl.BlockSpec(shp, lambda bi, s: (0,) * len(shp))
    return pl.pallas_call(
        functools.partial(_gdn_kernel, l=l, nck=nck),
        grid=(b, steps),
        in_specs=[
            pl.BlockSpec((None, blk, 3 * GW), lambda bi, s: (bi, s, COL_CX // (3 * GW))),
            pl.BlockSpec((None, blk, GW), lambda bi, s: (bi, s, COL_CZ // GW)),
            pl.BlockSpec((None, blk, LANES), lambda bi, s: (bi, s, GATE_BLK)),
            pl.BlockSpec((None, nck, N_GATES, l), lambda bi, s: (bi, s, 0, 0)),
            state((SUBLANES, 3 * GW)),
            const((4, 3 * GW)),
            const((1, LANES)), const((N_GATES, 1)), const((1, LANES)), const((N_GATES, 1)),
            const((1, GW)),
            state((H, DH, DH)),
        ],
        out_specs=[
            pl.BlockSpec((None, blk, GW), lambda bi, s: (bi, s, 0)),
            state((H, DH, DH)),
        ],
        out_shape=[
            jax.ShapeDtypeStruct((b, t, GW), BF16),
            jax.ShapeDtypeStruct((b, H, DH, DH), F32),
        ],
        scratch_shapes=[pltpu.VMEM((SUBLANES, 3 * GW), F32)],
        compiler_params=_cparams("parallel", "arbitrary"),
        name="gdn",
    )(proj3, proj3, proj3, gt3, hist8, cw, a_c, a_r, dt_c, dt_r, gh, s0)


def _bd_split(x, mask):
    hi, lo = _split2(x)
    return _block_diag_rows(hi, mask), _block_diag_rows(lo, mask)


def _gdn64_kernel(x_ref, z_ref, gc_ref, gr_ref, hist_ref, cw_ref, ac_ref, ar_ref, dc_ref, dr_ref,
                  gh_ref, s0_ref, out_ref, s_ref, carry_scr, *, nck):
    l = CHUNK

    @pl.when(pl.program_id(1) == 0)
    def _():
        s_ref[...] = s0_ref[...]
        carry_scr[...] = hist_ref[...]

    blk = l * nck
    x = x_ref[...]
    ext = jnp.concatenate([carry_scr[...], x], axis=0)
    carry_scr[...] = x[blk - SUBLANES:, :]
    y = x * cw_ref[3:4, :]
    for j in range(1, 4):
        y = y + ext[SUBLANES - j:SUBLANES - j + blk, :] * cw_ref[3 - j:4 - j, :]
    y = y * _sigmoid(y)

    bmask = _block_mask(GW, GW)
    bones = bmask.astype(BF16)
    within = _iota((GW, GW), 0) % DH <= _iota((GW, GW), 1) % DH
    up_bd = jnp.logical_and(bmask, within).astype(BF16)
    lo_tri = (_iota((l, l), 0) >= _iota((l, l), 1)).astype(BF16)
    key_pos = _iota((l, GW), 1) % DH
    incl = _iota((l, GW), 0) >= key_pos
    strict = _iota((l, GW), 0) > key_pos
    eye_t = (_iota((l, GW), 0) == key_pos).astype(F32)
    e_b = _expander(GATE_OFF + 2 * H)
    e_a = _expander(GATE_OFF + 3 * H)

    def head_sums(a):
        return _head_sums(a, bones)

    def shared_rhs(lhs_splits, rhs_bd):
        n = len(lhs_splits)
        big = _dot(jnp.concatenate([part for sp in lhs_splits for part in sp], axis=0), rhs_bd[0])
        small = _dot(jnp.concatenate([sp[0] for sp in lhs_splits], axis=0), rhs_bd[1])
        return [big[2 * i * l:(2 * i + 1) * l] + big[(2 * i + 1) * l:(2 * i + 2) * l]
                + small[i * l:(i + 1) * l] for i in range(n)]

    yq, yk, yv = y[:, :GW], y[:, GW:2 * GW], y[:, 2 * GW:]
    qn_all = yq * lax.rsqrt(head_sums(yq * yq) + 1e-6) * (DH ** -0.5)
    kn_all = yk * lax.rsqrt(head_sums(yk * yk) + 1e-6)

    cks = []
    for ck in range(nck):
        rows = slice(ck * l, (ck + 1) * l)
        gcol = gc_ref[rows, :]
        dec_c = -jnp.exp(ac_ref[...]) * _softplus(gcol + dc_ref[...])
        beta = _expand(_sigmoid(gcol), e_b)
        g_c = _expand(_cumsum_cols_wide(dec_c, lo_tri), e_a)
        grow_t = jnp.concatenate([gr_ref[ck]] * H, axis=1)
        dec_r = -jnp.exp(ar_ref[...]) * _softplus(grow_t + dr_ref[...])
        g_r = _row_select(_dot_stacked(_split3(dec_r), up_bd), 3 * H)
        decay = jnp.exp(jnp.where(incl, g_c - g_r, NEG_INF))
        eg = jnp.exp(g_c)
        g_last = g_c[l - 1:l, :]
        q, k, v = qn_all[rows], kn_all[rows], yv[rows]
        cks.append(dict(rows=rows, qb=q.astype(BF16), kb=k.astype(BF16),
                        beta=beta, decay=decay, rhs_v=v * beta, rhs_k=k * (beta * eg),
                        qeg=(q * eg).astype(BF16), kdec=(k * jnp.exp(g_last - g_c)).astype(BF16),
                        sdec=jnp.exp(g_last)))
    kqs = [_dot_nt(jnp.concatenate([p["kb"], p["qb"]], axis=0), _block_diag_rows(p["kb"], bmask))
           for p in cks]
    nmats = [jnp.where(strict, p["beta"] * kq[:l] * p["decay"], 0.0) for p, kq in zip(cks, kqs)]
    attns = [(kq[l:] * p["decay"]).astype(BF16) for p, kq in zip(cks, kqs)]

    ps = [eye_t - n for n in nmats]
    qs = [shared_rhs([_split2(n)], _bd_split(n, bmask))[0] for n in nmats]
    power = 2
    while power < l:
        power *= 2
        if power < l:
            res = [shared_rhs([_split2(p), _split2(q)], _bd_split(q, bmask)) for p, q in zip(ps, qs)]
            ps = [p + r[0] for p, r in zip(ps, res)]
            qs = [r[1] for r in res]
        else:
            ps = [p + shared_rhs([_split2(p)], _bd_split(q, bmask))[0] for p, q in zip(ps, qs)]
    tsp = [_split2(p) for p in ps]
    us = [shared_rhs([t], _bd_split(p["rhs_v"], bmask))[0] for t, p in zip(tsp, cks)]
    ws = [shared_rhs([t], _bd_split(p["rhs_k"], bmask))[0].astype(BF16) for t, p in zip(tsp, cks)]

    wu = [jnp.concatenate([w, u.astype(BF16)], axis=1) for w, u in zip(ws, us)]
    kwu = [_dot_tn(p["kdec"], x) for p, x in zip(cks, wu)]
    awu = [_dot(at, jnp.concatenate([_block_diag_rows(x[:, :GW], bmask),
                                     _block_diag_rows(x[:, GW:], bmask)], axis=1))
           for at, x in zip(attns, wu)]
    gmats = [jnp.where(bmask, x[:, :GW], 0.0).astype(BF16) for x in kwu]
    bmats = [jnp.where(bmask, x[:, GW:], 0.0) for x in kwu]
    qts = [(p["qeg"].astype(F32) - x[:, :GW]).astype(BF16) for p, x in zip(cks, awu)]

    s_run = s_ref[...]
    outs = []
    for p, g, bm, qt, x in zip(cks, gmats, bmats, qts, awu):
        ys = _dot(jnp.concatenate([g, qt], axis=0), s_run.astype(BF16))
        outs.append(ys[GW:] + x[:, GW:])
        s_run = p["sdec"] * s_run - ys[:GW] + bm
    s_ref[...] = s_run

    msq = [head_sums(o * o) * (1.0 / DH) for o in outs]
    for p, o, m2 in zip(cks, outs, msq):
        zg = z_ref[p["rows"], :]
        yo = o * lax.rsqrt(m2 + EPS) * gh_ref[...] * (zg * _sigmoid(zg))
        out_ref[p["rows"], :] = yo.astype(out_ref.dtype)


def _gdn64(proj3, gt3, hist8, cw, a_c, a_r, dt_c, dt_r, gh, s0, nck):
    b, t, _ = proj3.shape
    blk = CHUNK * nck
    steps = t // blk
    state = lambda shp: pl.BlockSpec((None,) + shp, lambda bi, s: (bi,) + (0,) * len(shp))
    const = lambda shp: pl.BlockSpec(shp, lambda bi, s: (0,) * len(shp))
    return pl.pallas_call(
        functools.partial(_gdn64_kernel, nck=nck),
        grid=(b, steps),
        in_specs=[
            pl.BlockSpec((None, blk, 3 * GW), lambda bi, s: (bi, s, COL_CX // (3 * GW))),
            pl.BlockSpec((None, blk, GW), lambda bi, s: (bi, s, COL_CZ // GW)),
            pl.BlockSpec((None, blk, LANES), lambda bi, s: (bi, s, GATE_BLK)),
            pl.BlockSpec((None, nck, N_GATES, CHUNK), lambda bi, s: (bi, s, 0, 0)),
            state((SUBLANES, 3 * GW)),
            const((4, 3 * GW)),
            const((1, LANES)), const((N_GATES, 1)), const((1, LANES)), const((N_GATES, 1)),
            const((1, GW)),
            state((GW, GW)),
        ],
        out_specs=[
            pl.BlockSpec((None, blk, GW), lambda bi, s: (bi, s, 0)),
            state((GW, GW)),
        ],
        out_shape=[
            jax.ShapeDtypeStruct((b, t, GW), BF16),
            jax.ShapeDtypeStruct((b, GW, GW), F32),
        ],
        scratch_shapes=[pltpu.VMEM((SUBLANES, 3 * GW), F32)],
        compiler_params=_cparams("parallel", "arbitrary"),
        name="gdn64",
    )(proj3, proj3, proj3, gt3, hist8, cw, a_c, a_r, dt_c, dt_r, gh, s0)


HEAD_PAD = 128
DPAD = H * HEAD_PAD
BF16_ROWS = 16
VT_PAD = -(-(DH + 1) // BF16_ROWS) * BF16_ROWS
VT_ROWS = H * VT_PAD


def _tile_heads(t):
    return jnp.concatenate([t] * H, axis=-1)


def _dprep_kernel(tail_ref, gq_ref, gkv_ref, wq_ref, wqp_ref, ka_ref, ckv_ref, kpe_ref, qc_ref):
    ka = ka_ref[...]
    kb = pltpu.roll(ka, 64, axis=1)
    nope = _iota(ka.shape, 1) < QK_NOPE
    qcos = jnp.where(nope, 1.0, kb)
    qsin = jnp.where(nope, 0.0, ka)
    hq = _rms(tail_ref[:, :Q_LORA], gq_ref[...]).astype(BF16)
    qc = _dot(hq, wq_ref[...]) * _tile_heads(qcos) + _dot(hq, wqp_ref[...]) * _tile_heads(qsin)
    qc_ref[...] = (qc * (MLA_SCALE * LOG2E)).astype(BF16)
    ckv_ref[...] = _rms(tail_ref[:, Q_LORA:Q_LORA + KV_LORA], gkv_ref[...])
    kr = tail_ref[:, Q_LORA + KV_LORA:]
    kpe = kr * ka + pltpu.roll(kr, 64, axis=1) * kb
    kpe_ref[...] = kpe[:, :QK_ROPE]


def _dprep(proj, gq, gkv, wq, wqp, ka, tm):
    n = proj.shape[0]
    row = lambda w: pl.BlockSpec((tm, w), lambda i: (i, 0))
    const = lambda a, b: pl.BlockSpec((a, b), lambda i: (0, 0))
    return pl.pallas_call(
        _dprep_kernel,
        grid=(n // tm,),
        in_specs=[
            pl.BlockSpec((tm, TAIL_W), lambda i: (i, COL_TAIL // TAIL_W)),
            const(1, Q_LORA), const(1, KV_LORA), const(Q_LORA, DPAD), const(Q_LORA, DPAD),
            row(LANES),
        ],
        out_specs=[row(KV_LORA), row(QK_ROPE), row(DPAD)],
        out_shape=[
            jax.ShapeDtypeStruct((n, KV_LORA), F32),
            jax.ShapeDtypeStruct((n, QK_ROPE), F32),
            jax.ShapeDtypeStruct((n, DPAD), BF16),
        ],
        compiler_params=_cparams("parallel"),
        name="dprep",
    )(proj, gq, gkv, wq, wqp, ka)


LOG2E = 1.4426950408889634


def _kvup_t_kernel(ckv_ref, kpe_ref, wk_ref, wvt_ref, pm_ref, onet_ref, kc_ref, vt_ref):
    c = ckv_ref[...].astype(BF16)
    kc_ref[...] = (_dot(c, wk_ref[...]) + _dot(kpe_ref[...].astype(BF16), pm_ref[...])).astype(BF16)
    vt_ref[...] = (_dot_nt(wvt_ref[...], c) + onet_ref[...]).astype(BF16)


def _kvup_t(ckv, kpe, wk, wvt, pm, onet, tm):
    m = ckv.shape[0]
    row = lambda w: pl.BlockSpec((tm, w), lambda i: (i, 0))
    const = lambda a, b: pl.BlockSpec((a, b), lambda i: (0, 0))
    return pl.pallas_call(
        _kvup_t_kernel,
        grid=(m // tm,),
        in_specs=[row(KV_LORA), row(QK_ROPE), const(KV_LORA, DPAD), const(VT_ROWS, KV_LORA),
                  const(QK_ROPE, DPAD), const(VT_ROWS, 1)],
        out_specs=[row(DPAD), pl.BlockSpec((VT_ROWS, tm), lambda i: (0, i))],
        out_shape=[jax.ShapeDtypeStruct((m, DPAD), BF16), jax.ShapeDtypeStruct((VT_ROWS, m), BF16)],
        compiler_params=_cparams("parallel"),
        name="kvup_t",
    )(ckv, kpe, wk, wvt, pm, onet)


def _mla_prompt_kernel(qi_ref, ki_ref, q_ref, k_ref, vt_ref, ghc_ref, out_ref, m_scr, acc_scr, *, bq, bk,
                       qw, ahead):
    p = pl.program_id(0)
    q_i = qi_ref[p]
    k_i = ki_ref[p]
    last = (q_i * bq) // bk

    @pl.when(k_i == 0)
    def _():
        m_scr[...] = jnp.full(m_scr.shape, NEG_INF, F32)
        acc_scr[...] = jnp.zeros(acc_scr.shape, F32)

    def step(diag):
        if diag:
            key_chunk = k_i * (bk // CHUNK) + _iota((bk, bq), 0) // CHUNK
            qry_chunk = q_i * (bq // CHUNK) + _iota((bk, bq), 1) // CHUNK
            allowed = key_chunk <= qry_chunk

        units = [(h, c) for h in range(H) for c in range(bq // qw)]
        rc = min(bk, 64)

        def scores(u):
            h, c = u
            hs = slice(h * HEAD_PAD, (h + 1) * HEAD_PAD)
            return _dot_nt(k_ref[:, hs], q_ref[c * qw:(c + 1) * qw, hs])

        def update(u, st):
            h, c = u
            hs = slice(h * HEAD_PAD, (h + 1) * HEAD_PAD)
            qs = slice(c * qw, (c + 1) * qw)
            if diag:
                st = jnp.where(allowed[:, qs], st, NEG_INF)
            m_prev = m_scr[h, :, qs]
            mx = st[:rc]
            for r in range(1, bk // rc):
                mx = jnp.maximum(mx, st[r * rc:(r + 1) * rc])
            m_new = jnp.maximum(m_prev, jnp.max(mx, axis=0, keepdims=True))
            alpha = jnp.exp2(m_prev - m_new)[0:1]
            m_row = m_new[0:1]
            pt = jnp.concatenate([jnp.exp2(st[r * rc:(r + 1) * rc] - m_row).astype(BF16)
                                  for r in range(bk // rc)], axis=0)
            acc = alpha * acc_scr[h, :, qs] + _dot(vt_ref[h * VT_PAD:(h + 1) * VT_PAD, :], pt)
            if not diag:
                m_scr[h, :, qs] = m_new
                acc_scr[h, :, qs] = acc
            return acc

        accs = []
        pending = [scores(u) for u in units[:ahead]]
        for idx, u in enumerate(units):
            if idx + ahead < len(units):
                pending.append(scores(units[idx + ahead]))
            accs.append(update(u, pending.pop(0)))
        per_head = bq // qw
        return [jnp.concatenate(accs[h * per_head:(h + 1) * per_head], axis=1) for h in range(H)]

    @pl.when(k_i < last)
    def _():
        step(False)

    @pl.when(k_i == last)
    def _():
        ys = []
        for h, acc in enumerate(step(True)):
            o = acc[:DH] / acc[DH:DH + 1]
            ms = jnp.mean(o * o, axis=0, keepdims=True)
            ys.append(o * lax.rsqrt(ms + EPS) * ghc_ref[h * DH:(h + 1) * DH, :])
        out_ref[...] = jnp.concatenate(ys, axis=0).T.astype(out_ref.dtype)


def _mla_prompt(qc, kc, vt, ghc, bq, bk):
    t = qc.shape[0]
    assert t % bq == 0 and t % bk == 0 and bk % bq == 0
    pairs = [(i, j) for i in range(t // bq) for j in range((i * bq) // bk + 1)]
    qi = jnp.asarray([i for i, _ in pairs], jnp.int32)
    ki = jnp.asarray([j for _, j in pairs], jnp.int32)
    grid_spec = pltpu.PrefetchScalarGridSpec(
        num_scalar_prefetch=2,
        grid=(len(pairs),),
        in_specs=[
            pl.BlockSpec((bq, DPAD), lambda p, qi, ki: (qi[p], 0)),
            pl.BlockSpec((bk, DPAD), lambda p, qi, ki: (ki[p], 0)),
            pl.BlockSpec((VT_ROWS, bk), lambda p, qi, ki: (0, ki[p])),
            pl.BlockSpec((GW, 1), lambda p, qi, ki: (0, 0)),
        ],
        out_specs=pl.BlockSpec((bq, GW), lambda p, qi, ki: (qi[p], 0)),
        scratch_shapes=[
            pltpu.VMEM((H, SUBLANES, bq), F32),
            pltpu.VMEM((H, VT_PAD, bq), F32),
        ],
    )
    return pl.pallas_call(
        functools.partial(_mla_prompt_kernel, bq=bq, bk=bk, qw=min(bq, 256), ahead=2),
        grid_spec=grid_spec,
        out_shape=jax.ShapeDtypeStruct((t, GW), BF16),
        compiler_params=_cparams("arbitrary"),
        name="mla_prompt",
    )(qi, ki, qc, kc, vt, ghc)


LAT_W = 2 * LANES
ONE_LANE = KV_LORA + QK_ROPE


def _mla_sample_kernel(q_ref, ckvp_ref, kpep_ref, ckvn_ref, kpen_ref, wabs_ref, wv_ref, gh_ref, out_ref):
    s_len = q_ref.shape[0]

    def latent_rows(ckv_ref, kpe_ref):
        n = ckv_ref.shape[0]
        tail = jnp.concatenate([kpe_ref[...].astype(BF16), jnp.zeros((n, LANES - QK_ROPE), BF16)], axis=1)
        tail = jnp.where(_iota((n, LANES), 1) == QK_ROPE, jnp.ones((), BF16), tail)
        return jnp.concatenate([ckv_ref[...].astype(BF16), tail], axis=1)

    kvp = latent_rows(ckvp_ref, kpep_ref)
    kvn = latent_rows(ckvn_ref, kpen_ref)
    qabs = jnp.concatenate([_dot(q_ref[:, h * HEAD_PAD:(h + 1) * HEAD_PAD], wabs_ref[h]) for h in range(H)],
                           axis=0).astype(BF16)
    s1 = _dot_nt(qabs, kvp)
    s2 = _dot_nt(qabs, kvn)
    m = jnp.maximum(jnp.max(s1, axis=-1, keepdims=True), jnp.max(s2, axis=-1, keepdims=True))
    acc = _dot(jnp.exp2(s1 - m).astype(BF16), kvp) + _dot(jnp.exp2(s2 - m).astype(BF16), kvn)
    olat = (acc[:, :KV_LORA] / acc[:, ONE_LANE:ONE_LANE + 1]).astype(BF16)
    rows = slice(0, s_len)
    for h in range(H):
        o = _dot(olat[h * s_len:(h + 1) * s_len], wv_ref[h])
        _head_norm_store(out_ref, rows, h, o, gh_ref)


def _mla_sample(qc3, ckvp, kpep, ckvn, kpen, wabs, wv, gh):
    b, s, _ = qc3.shape
    npast = ckvp.shape[1]
    per_b = lambda r, w: pl.BlockSpec((None, r, w), lambda i: (i, 0, 0))
    const = lambda shp: pl.BlockSpec(shp, lambda i: (0,) * len(shp))
    return pl.pallas_call(
        _mla_sample_kernel,
        grid=(b,),
        in_specs=[per_b(s, DPAD), per_b(npast, KV_LORA), per_b(npast, QK_ROPE), per_b(s, KV_LORA),
                  per_b(s, QK_ROPE), const((H, HEAD_PAD, LAT_W)), const((H, KV_LORA, DH)), const((1, GW))],
        out_specs=per_b(s, GW),
        out_shape=jax.ShapeDtypeStruct((b, s, GW), BF16),
        compiler_params=_cparams("parallel"),
        name="mla_sample",
    )(qc3, ckvp, kpep, ckvn, kpen, wabs, wv, gh)


def _outproj_kernel(x_ref, a_ref, b_ref, c_ref, d_ref, w_ref, out_ref):
    acc = x_ref[...]
    for g, m_ref in enumerate((a_ref, b_ref, c_ref, d_ref)):
        acc = acc + _dot(m_ref[...], w_ref[g * GW:(g + 1) * GW, :])
    out_ref[...] = acc


def _outproj(x, ma, mb, mc, md, w, tm):
    n = x.shape[0]
    mix = pl.BlockSpec((tm, GW), lambda i: (i, 0))
    return pl.pallas_call(
        _outproj_kernel,
        grid=(n // tm,),
        in_specs=[pl.BlockSpec((tm, D_MODEL), lambda i: (i, 0)), mix, mix, mix, mix,
                  pl.BlockSpec((D_MODEL, D_MODEL), lambda i: (0, 0))],
        out_specs=pl.BlockSpec((tm, D_MODEL), lambda i: (i, 0)),
        out_shape=jax.ShapeDtypeStruct((n, D_MODEL), F32),
        compiler_params=_cparams("parallel"),
        name="outproj",
    )(x, ma, mb, mc, md, w)


def _ffn_kernel(*refs, seq_len, final_norm, tf):
    if seq_len is None:
        (x_ref, g_ref, wup_ref, cw_ref, wd_ref, gf_ref, out_ref, ga_ref, act_scr, carry_scr) = refs
    else:
        (x_ref, g_ref, wup_ref, cw_ref, wd_ref, gf_ref, h1_ref, h2_ref, out_ref, ga_ref, act_scr) = refs
    tm = x_ref.shape[0]
    nj = D_FF // tf
    h = _rms(x_ref[...], g_ref[...]).astype(BF16)
    row = _iota((tm, tf), 0)

    if seq_len is None:
        @pl.when(pl.program_id(0) == 0)
        def _():
            carry_scr[...] = jnp.zeros(carry_scr.shape, F32)

    def up(j):
        cols = slice(j * tf, (j + 1) * tf)
        ucols = slice(D_FF + j * tf, D_FF + (j + 1) * tf)
        return _dot(h, wup_ref[:, cols]), _dot(h, wup_ref[:, ucols])

    def gate(j, ga, u):
        cols = slice(j * tf, (j + 1) * tf)
        r1 = pltpu.roll(ga, 1, axis=0)
        r2 = pltpu.roll(ga, 2, axis=0)
        if seq_len is None:
            c1 = carry_scr[SUBLANES - 1:SUBLANES, cols]
            c2 = carry_scr[SUBLANES - 2:SUBLANES - 1, cols]
            prev1 = jnp.where(row >= 1, r1, c1)
            prev2 = jnp.where(row >= 2, r2, jnp.where(row == 1, c1, c2))
            tail = ga[tm - SUBLANES:, :]
            carry_scr[:, cols] = tail
            ga_ref[:, cols] = tail
        else:
            t = row % seq_len
            prev1 = jnp.where(t >= 1, r1, h1_ref[:, cols])
            prev2 = jnp.where(t >= 2, r2, h2_ref[:, cols])
            ga_ref[:, cols] = ga
        conv = prev2 * cw_ref[0:1, cols] + prev1 * cw_ref[1:2, cols] + ga * cw_ref[2:3, cols]
        act_scr[:, cols] = (conv * _sigmoid(conv) * u).astype(BF16)

    pending = up(0)
    for j in range(nj):
        nxt = up(j + 1) if j + 1 < nj else None
        gate(j, *pending)
        pending = nxt
    y = x_ref[...] + _dot(act_scr[...], wd_ref[...])
    if final_norm:
        y = _rms(y, gf_ref[...])
    out_ref[...] = y


def _ffn(x, g, w_up, cw, w_down, gf, h1, h2, *, tm, tf, seq_len, final_norm):
    n = x.shape[0]
    ni = n // tm
    resident = lambda a, b: pl.BlockSpec((a, b), lambda i: (0, 0), pipeline_mode=pl.Buffered(1))
    in_specs = [
        pl.BlockSpec((tm, D_MODEL), lambda i: (i, 0)),
        resident(1, D_MODEL),
        resident(D_MODEL, 2 * D_FF),
        resident(3, D_FF),
        resident(D_FF, D_MODEL),
        resident(1, D_MODEL),
    ]
    args = [x, g, w_up, cw, w_down, gf]
    scratch = [pltpu.VMEM((tm, D_FF), BF16)]
    if seq_len is None:
        ga_spec = pl.BlockSpec((None, SUBLANES, D_FF), lambda i: (i, 0, 0))
        ga_shape = jax.ShapeDtypeStruct((ni, SUBLANES, D_FF), F32)
        scratch.append(pltpu.VMEM((SUBLANES, D_FF), F32))
    else:
        in_specs += [pl.BlockSpec((tm, D_FF), lambda i: (i, 0))] * 2
        args += [h1, h2]
        ga_spec = pl.BlockSpec((tm, D_FF), lambda i: (i, 0))
        ga_shape = jax.ShapeDtypeStruct((n, D_FF), F32)
    return pl.pallas_call(
        functools.partial(_ffn_kernel, seq_len=seq_len, final_norm=final_norm, tf=tf),
        grid=(ni,),
        in_specs=in_specs,
        out_specs=[pl.BlockSpec((tm, D_MODEL), lambda i: (i, 0)), ga_spec],
        out_shape=[jax.ShapeDtypeStruct((n, D_MODEL), F32), ga_shape],
        scratch_shapes=scratch,
        compiler_params=_cparams("arbitrary"),
        name="ffn",
    )(*args)


def _rope_tables(offset, t):
    half = QK_ROPE // 2
    per_row = LANES // half
    assert t % per_row == 0
    inv = ROPE_THETA ** (-jnp.arange(half, dtype=F32) / half)
    pos = offset + per_row * _iota((t // per_row, LANES), 0) + _iota((t // per_row, LANES), 1) // half
    ang = pos.astype(F32) * jnp.tile(inv, per_row)[None, :]
    cos, sin = jnp.cos(ang).reshape(t, half), jnp.sin(ang).reshape(t, half)
    z32 = jnp.zeros((t, 32), F32)
    return jnp.concatenate([cos, cos, z32, -sin, sin, z32], -1)


def _rel_bias(table, n_past, n_q, n_k):
    dmax = n_past + n_q - 1
    dmin = n_past - n_k + 1
    diag = table[:, jnp.clip(jnp.arange(dmax, dmin - 1, -1), -REL_MAX, REL_MAX) + REL_MAX]
    return jnp.stack([diag[:, n_q - 1 - i:n_q - 1 - i + n_k] for i in range(n_q)], axis=1)


def _swap_halves(w):
    half = w.shape[-1] // 2
    return jnp.concatenate([w[..., half:], w[..., :half]], -1)


def _layer_weights(lw):
    (g_mix, w_in, a_rel_bias, b_i_bias, b_f_bias, c_conv_w, c_a_log, c_dt_bias,
     d_g_q, d_w_q_up, d_g_kv, d_w_kv_up, g_head, w_out, g_ffn, w_up, f_conv_w, w_down) = lw
    o = 0
    cols = {}
    w_in = w_in.astype(BF16)
    for name, size in (("a", 3 * GW), ("b", 4 * GW), ("bg", 2 * H), ("c", 3 * GW), ("cz", GW),
                       ("cg", 2 * H), ("dq", Q_LORA), ("dkv", KV_LORA), ("dkr", QK_ROPE)):
        cols[name] = w_in[:, o:o + size]
        o += size
    gates = jnp.concatenate([cols["bg"], cols["cg"]], -1)
    pad16 = jnp.zeros((D_MODEL, 16), BF16)
    pad32 = jnp.zeros((D_MODEL, 32), BF16)
    w_perm = jnp.concatenate([cols["c"], cols["a"], cols["dq"], cols["dkv"], cols["dkr"], gates, pad16,
                              _swap_halves(cols["dkr"]), pad32, cols["b"], cols["cz"]], -1)
    zc = lambda n: jnp.zeros((1, n), F32)
    zr = lambda n: jnp.zeros((n, 1), F32)
    bias_c = jnp.concatenate([zc(GATE_OFF), b_i_bias[None], b_f_bias[None], zc(LANES - GATE_OFF - 2 * H)], -1)
    bias_r = jnp.concatenate([b_i_bias[:, None], b_f_bias[:, None], zr(2 * H)], 0)
    alog_c = jnp.concatenate([zc(GATE_OFF + 3 * H), c_a_log[None], zc(LANES - GATE_OFF - 4 * H)], -1)
    alog_r = jnp.concatenate([zr(3 * H), c_a_log[:, None]], 0)
    dt_c = jnp.concatenate([zc(GATE_OFF + 3 * H), c_dt_bias[None], zc(LANES - GATE_OFF - 4 * H)], -1)
    dt_r = jnp.concatenate([zr(3 * H), c_dt_bias[:, None]], 0)

    wq = d_w_q_up.reshape(Q_LORA, H, QK_NOPE + QK_ROPE)
    z_h32 = jnp.zeros((Q_LORA, H, 32), F32)
    wq_full = jnp.concatenate([wq, z_h32], -1).reshape(Q_LORA, DPAD)
    wq_part = jnp.concatenate([jnp.zeros((Q_LORA, H, QK_NOPE), F32), _swap_halves(wq[..., QK_NOPE:]), z_h32],
                              -1).reshape(Q_LORA, DPAD)
    wkv = d_w_kv_up.reshape(KV_LORA, H, 2 * DH)
    z_h64 = jnp.zeros((KV_LORA, H, DH), F32)
    wk_full = jnp.concatenate([wkv[..., :DH], z_h64], -1).reshape(KV_LORA, DPAD)
    place = jnp.concatenate([jnp.zeros((QK_ROPE, QK_NOPE), F32), jnp.eye(QK_ROPE, dtype=F32),
                             jnp.zeros((QK_ROPE, 32), F32)], -1)
    pmat = jnp.concatenate([place] * H, -1)
    wk_t = jnp.transpose(wkv[..., :DH], (1, 2, 0))
    rope_rows = jnp.concatenate([jnp.zeros((QK_ROPE, KV_LORA), F32), jnp.eye(QK_ROPE, dtype=F32),
                                 jnp.zeros((QK_ROPE, LAT_W - KV_LORA - QK_ROPE), F32)], -1)
    wabs = jnp.concatenate([
        jnp.concatenate([wk_t, jnp.zeros((H, QK_NOPE, LAT_W - KV_LORA), F32)], -1),
        jnp.broadcast_to(rope_rows, (H, QK_ROPE, LAT_W)),
        jnp.zeros((H, HEAD_PAD - QK_NOPE - QK_ROPE, LAT_W), F32)], 1)
    return dict(
        g_mix=g_mix[None], w_in=w_perm.astype(BF16), w_gt=gates.T.astype(BF16),
        table=a_rel_bias, bias_c=bias_c, bias_r=bias_r, alog_c=alog_c, alog_r=alog_r, dt_c=dt_c, dt_r=dt_r,
        c_conv_w=c_conv_w, g_q=d_g_q[None], g_kv=d_g_kv[None],
        wq=wq_full.astype(BF16), wqp=wq_part.astype(BF16), wk=wk_full.astype(BF16),
        pmat=pmat.astype(BF16), wabs=wabs.astype(BF16),
        wv_heads=jnp.transpose(wkv[..., DH:], (1, 0, 2)).astype(BF16),
        wvt=jnp.concatenate([wkv[..., DH:], jnp.zeros((KV_LORA, H, VT_PAD - DH), F32)], -1)
        .reshape(KV_LORA, VT_ROWS).T.astype(BF16),
        vonest=(jnp.arange(VT_ROWS) % VT_PAD == DH).astype(F32)[:, None],
        g_head=g_head.reshape(4, 1, GW), w_out=w_out.astype(BF16),
        g_ffn=g_ffn[None], w_up=w_up.astype(BF16), f_conv_w=f_conv_w, w_down=w_down.astype(BF16))


def _gates_t3(gt, b, t, l):
    return gt.reshape(N_GATES, b, t // l, l).transpose(1, 2, 0, 3)


def _layer(x, offset, st, w, gf, final_norm, cfg):
    b, t, _ = x.shape
    n = b * t
    first = st is None
    x2 = x.reshape(n, D_MODEL)
    proj, gt = _inproj(x2, w["g_mix"], w["w_in"], w["w_gt"], cfg["tm"])
    proj3 = proj.reshape(b, t, PROJ_W)
    gh = w["g_head"]
    l = min(t, CHUNK)
    gt3 = _gates_t3(gt, b, t, l)

    new_ak = proj3[:, t - min(A_PAST, t):, COL_A + GW:COL_A + 2 * GW].reshape(b, -1, H, DH)
    new_av = proj3[:, t - min(A_PAST, t):, COL_A + 2 * GW:COL_A + 3 * GW].reshape(b, -1, H, DH)
    if first:
        bias = _rel_bias(w["table"], A_PAST, CHUNK, A_PAST + CHUNK)
        oa = _band_prompt(proj, bias, gh[0].T)
    else:
        npast = st[0].shape[1]
        bias = _rel_bias(w["table"], npast, t, npast + t)
        oa = _band_sample(proj3, st[0].reshape(b, npast, GW), st[1].reshape(b, npast, GW), bias, gh[0])
        oa = oa.reshape(n, GW)

    if first:
        c0 = jnp.zeros((b, H, DH, DH), F32)
        n0 = jnp.zeros((b, H, DH), F32)
        m0 = jnp.zeros((b, 1, H), F32)
    else:
        c0, n0, m0 = st[2], st[3], st[4][:, None, :]
    if l == CHUNK:
        ob, cbd, nrow, mrow = _mlstm64(proj3, gt3, w["bias_c"], w["bias_r"], gh[1], _to_block_diag(c0),
                                       n0.reshape(b, 1, GW), jnp.repeat(m0, DH, axis=-1), cfg["nck"])
        new_bc, new_bn, new_bm = _from_block_diag(cbd), nrow.reshape(b, H, DH), mrow[:, 0, ::DH]
    else:
        ob, new_bc, new_bn, new_bm = _mlstm(proj3, gt3, w["bias_c"], w["bias_r"], gh[1], c0, n0, m0,
                                            l, cfg["nck"])
        new_bm = new_bm[:, 0, :]

    if first:
        hist8 = jnp.zeros((b, SUBLANES, 3 * GW), F32)
        s0 = jnp.zeros((b, H, DH, DH), F32)
    else:
        hist8 = jnp.concatenate([jnp.zeros((b, SUBLANES - 3, 3 * GW), F32), st[6]], 1)
        s0 = st[5]
    gdn_args = (proj3, gt3, hist8, w["c_conv_w"], w["alog_c"], w["alog_r"], w["dt_c"], w["dt_r"], gh[2])
    if l == CHUNK:
        oc, sbd = _gdn64(*gdn_args, _to_block_diag(s0), cfg["nck"])
        new_cs = _from_block_diag(sbd)
    else:
        oc, new_cs = _gdn(*gdn_args, s0, l, cfg["nck"])
    new_cconv = proj3[:, t - 3:, COL_CX:COL_CX + 3 * GW]

    ka = jnp.tile(_rope_tables(offset, t), (b, 1))
    ckv, kpe, qc = _dprep(proj, w["g_q"], w["g_kv"], w["wq"], w["wqp"], ka, cfg["tm_d"])
    if first:
        kc, vt = _kvup_t(ckv, kpe, w["wk"], w["wvt"], w["pmat"], w["vonest"], cfg["tm_kv"])
        od = _mla_prompt(qc, kc, vt, gh[3].T, cfg["mla_bq"], cfg["mla_bk"])
    else:
        od = _mla_sample(qc.reshape(b, t, DPAD), st[7], st[8], ckv.reshape(b, t, KV_LORA),
                         kpe.reshape(b, t, QK_ROPE), w["wabs"], w["wv_heads"], gh[3])
        od = od.reshape(n, GW)

    x2 = _outproj(x2, oa, ob.reshape(n, GW), oc.reshape(n, GW), od, w["w_out"], cfg["tm"])

    if first:
        y, ga_tail = _ffn(x2, w["g_ffn"], w["w_up"], w["f_conv_w"], w["w_down"], gf, None, None,
                          tm=cfg["tm"], tf=cfg["tf"], seq_len=None, final_norm=final_norm)
        new_fconv = ga_tail[-1, SUBLANES - 2:, :][None]
    else:
        hist = st[9]
        zrow = jnp.zeros((b, t - 1, D_FF), F32)
        h1 = jnp.concatenate([hist[:, 1:2], zrow], 1).reshape(n, D_FF)
        h2 = jnp.concatenate([hist, zrow[:, 1:]], 1).reshape(n, D_FF)
        y, ga = _ffn(x2, w["g_ffn"], w["w_up"], w["f_conv_w"], w["w_down"], gf, h1, h2,
                     tm=cfg["tm"], tf=cfg["tf"], seq_len=t, final_norm=final_norm)
        new_fconv = ga.reshape(b, t, D_FF)[:, t - 2:]
    state = (new_ak, new_av, new_bc, new_bn, new_bm, new_cs, new_cconv,
             ckv.reshape(b, t, KV_LORA), kpe.reshape(b, t, QK_ROPE), new_fconv)
    return y.reshape(b, t, D_MODEL), state


def _config(b, t):
    n = b * t
    tm = min(n, 1024)
    return dict(tm=tm, tf=256, nck=1 if t <= CHUNK else 8,
                tm_d=min(n, 1024), tm_kv=min(n, 2048), mla_bq=min(t, 512), mla_bk=min(t, 1024))


def kernel(x_prompt, x_sample, cache_a_k, cache_a_v, state_b_c, state_b_n, state_b_m, state_c_s, cache_c_conv, cache_d_ckv, cache_d_kpe, cache_ffn_conv, g_mix, w_in, a_rel_bias, b_i_bias, b_f_bias, c_conv_w, c_a_log, c_dt_bias, d_g_q, d_w_q_up, d_g_kv, d_w_kv_up, g_head, w_out, g_ffn, w_up, f_conv_w, w_down, g_final):
    layer_w = (g_mix, w_in, a_rel_bias, b_i_bias, b_f_bias, c_conv_w, c_a_log, c_dt_bias,
               d_g_q, d_w_q_up, d_g_kv, d_w_kv_up, g_head, w_out, g_ffn, w_up, f_conv_w, w_down)
    depth = g_mix.shape[0]
    past = cache_d_ckv.shape[2]
    xp, xs = x_prompt, x_sample
    cfg_p = _config(*x_prompt.shape[:2])
    cfg_s = _config(*x_sample.shape[:2])
    gf = g_final[None]
    new_p, new_s = [], []
    for l in range(depth):
        w = _layer_weights(tuple(a[l] for a in layer_w))
        last = l == depth - 1
        xp, sp_l = _layer(xp, 0, None, w, gf, last, cfg_p)
        st = (cache_a_k[l], cache_a_v[l], state_b_c[l], state_b_n[l], state_b_m[l],
              state_c_s[l], cache_c_conv[l], cache_d_ckv[l], cache_d_kpe[l], cache_ffn_conv[l])
        xs, ss_l = _layer(xs, past, st, w, gf, last, cfg_s)
        new_p.append(sp_l)
        new_s.append(ss_l)
    outs = [xp, xs]
    for i in range(10):
        outs.append(jnp.stack([s[i] for s in new_p]))
        outs.append(jnp.stack([s[i] for s in new_s]))
    return tuple(outs)
```

```python
import functools
import math

import jax
import jax.numpy as jnp
from jax import lax
from jax.experimental import pallas as pl
from jax.experimental.pallas import tpu as pltpu

F32 = jnp.float32
BF16 = jnp.bfloat16

D_MODEL = 1024
CHUNK = 64
H = 4
DH = 64
GW = H * DH
A_PAST = 8 * CHUNK
REL_MAX = 2 * CHUNK
Q_LORA = 256
KV_LORA = 128
QK_NOPE = 64
QK_ROPE = 32
ROPE_THETA = 10000.0
MLA_SCALE = (QK_NOPE + QK_ROPE) ** -0.5
D_FF = 2816
EPS = 1e-6

COL_CX = 0
COL_A = 3 * GW
COL_TAIL = 6 * GW
TAIL_W = 512
COL_B = COL_TAIL + TAIL_W
COL_CZ = COL_B + 4 * GW
PROJ_W = COL_CZ + GW
GATE_BLK = (COL_TAIL + 384) // 128
GATE_OFF = 32
N_GATES = 16

LANES = 128
SUBLANES = 8
VMEM_LIMIT = 56 * 1024 * 1024

NEG_INF = float("-inf")


def _cparams(*sem):
    return pltpu.CompilerParams(dimension_semantics=sem, vmem_limit_bytes=VMEM_LIMIT)


def _dot(a, b):
    return jnp.dot(a, b, preferred_element_type=F32)


def _dot_nt(a, b):
    return lax.dot_general(a, b, (((1,), (1,)), ((), ())), preferred_element_type=F32)


def _dot_tn(a, b):
    return lax.dot_general(a, b, (((0,), (0,)), ((), ())), preferred_element_type=F32)


def _split3(x):
    hi = x.astype(BF16)
    r1 = x - hi.astype(F32)
    mid = r1.astype(BF16)
    lo = (r1 - mid.astype(F32)).astype(BF16)
    return hi, mid, lo


def _rms(x, g):
    return x * lax.rsqrt(jnp.mean(x * x, axis=-1, keepdims=True) + EPS) * g


def _log_sigmoid(x):
    return jnp.minimum(x, 0.0) - jnp.log1p(jnp.exp(-jnp.abs(x)))


def _softplus(x):
    return jnp.maximum(x, 0.0) + jnp.log1p(jnp.exp(-jnp.abs(x)))


def _sigmoid(x):
    return 1.0 / (1.0 + jnp.exp(-x))


def _iota(shape, dim):
    return lax.broadcasted_iota(jnp.int32, shape, dim)


def _inproj_kernel(x_ref, g_ref, w_ref, wgt_ref, proj_ref, gt_ref):
    h = _rms(x_ref[...], g_ref[...]).astype(BF16)
    gt_ref[...] = _dot_nt(wgt_ref[...], h)
    proj_ref[...] = _dot(h, w_ref[...])


def _inproj(x, g, w, wgt, tm):
    n = x.shape[0]
    resident = lambda a, b: pl.BlockSpec((a, b), lambda i: (0, 0), pipeline_mode=pl.Buffered(1))
    return pl.pallas_call(
        _inproj_kernel,
        grid=(n // tm,),
        in_specs=[
            pl.BlockSpec((tm, D_MODEL), lambda i: (i, 0)),
            resident(1, D_MODEL),
            resident(D_MODEL, PROJ_W),
            resident(N_GATES, D_MODEL),
        ],
        out_specs=[
            pl.BlockSpec((tm, PROJ_W), lambda i: (i, 0)),
            pl.BlockSpec((N_GATES, tm), lambda i: (0, i)),
        ],
        out_shape=[
            jax.ShapeDtypeStruct((n, PROJ_W), F32),
            jax.ShapeDtypeStruct((N_GATES, n), F32),
        ],
        compiler_params=_cparams("parallel"),
        name="inproj",
    )(x, g, w, wgt)


def _head_norm_store(out_ref, rows, h, o, gh_ref):
    g = gh_ref[:, h * DH:(h + 1) * DH]
    y = o * lax.rsqrt(jnp.mean(o * o, axis=-1, keepdims=True) + EPS) * g
    out_ref[rows, h * DH:(h + 1) * DH] = y.astype(out_ref.dtype)


def _band_prompt_kernel(q_ref, kp_ref, kc_ref, vp_ref, vc_ref, bt2_ref, ghc_ref, out_ref, biast_ref, *, qb):
    nk = 2 * qb
    one_lane = (_iota((nk, DH), 1) == 0).astype(BF16)

    @pl.when(pl.program_id(0) == 0)
    def _():
        band = bt2_ref.shape[1]
        left = _iota((nk, LANES), 1) < CHUNK
        def placed(bt, top):
            ninf = lambda n: [jnp.full((n, LANES), NEG_INF, F32)] if n else []
            return jnp.concatenate(ninf(top) + [bt] + ninf(nk - band - top), axis=0)

        for h in range(H):
            bt = bt2_ref[h]
            shifted = [placed(bt, c * CHUNK) for c in range(qb // CHUNK)]
            for t2 in range(qb // LANES):
                biast_ref[h, :, t2 * LANES:(t2 + 1) * LANES] = jnp.where(left, shifted[2 * t2], shifted[2 * t2 + 1])

    def run(first):
        def scores(h):
            hs = slice(h * DH, (h + 1) * DH)
            kcat = jnp.concatenate([kp_ref[:, hs], kc_ref[:, hs]], axis=0).astype(BF16)
            return _dot_nt(kcat, (q_ref[:, hs] * (DH ** -0.5)).astype(BF16))

        def attend(h, st):
            hs = slice(h * DH, (h + 1) * DH)
            st = st + biast_ref[h]
            if first:
                st = jnp.where(_iota((nk, qb), 0) >= qb, st, NEG_INF)
            p = jnp.exp(st - jnp.max(st, axis=0, keepdims=True)).astype(BF16)
            vcat = jnp.concatenate([vp_ref[:, hs], vc_ref[:, hs]], axis=0).astype(BF16)
            acc = _dot_tn(jnp.concatenate([vcat, one_lane], axis=1), p)
            o = acc[:DH] / acc[DH:DH + 1]
            ms = jnp.mean(o * o, axis=0, keepdims=True)
            return o * lax.rsqrt(ms + EPS) * ghc_ref[hs, :]

        ys = []
        pending = scores(0)
        for h in range(H):
            nxt = scores(h + 1) if h + 1 < H else None
            ys.append(attend(h, pending))
            pending = nxt
        out_ref[...] = jnp.concatenate(ys, axis=0).T.astype(out_ref.dtype)

    @pl.when(pl.program_id(0) == 0)
    def _():
        run(True)

    @pl.when(pl.program_id(0) > 0)
    def _():
        run(False)


def _band_prompt(proj, bias, ghc, qb=A_PAST):
    t = proj.shape[0]
    assert qb == A_PAST and t % qb == 0
    prev = lambda i: jnp.maximum(i - 1, 0)
    cq = COL_A // GW
    band = bias.shape[-1]
    bias_t = bias.transpose(0, 2, 1)
    bt2 = jnp.concatenate([bias_t, bias_t], axis=-1)
    return pl.pallas_call(
        functools.partial(_band_prompt_kernel, qb=qb),
        grid=(t // qb,),
        in_specs=[
            pl.BlockSpec((qb, GW), lambda i: (i, cq)),
            pl.BlockSpec((qb, GW), lambda i: (prev(i), cq + 1)),
            pl.BlockSpec((qb, GW), lambda i: (i, cq + 1)),
            pl.BlockSpec((qb, GW), lambda i: (prev(i), cq + 2)),
            pl.BlockSpec((qb, GW), lambda i: (i, cq + 2)),
            pl.BlockSpec((H, band, LANES), lambda i: (0, 0, 0)),
            pl.BlockSpec((GW, 1), lambda i: (0, 0)),
        ],
        out_specs=pl.BlockSpec((qb, GW), lambda i: (i, 0)),
        out_shape=jax.ShapeDtypeStruct((t, GW), BF16),
        scratch_shapes=[pltpu.VMEM((H, 2 * qb, qb), F32)],
        compiler_params=_cparams("arbitrary"),
        name="band_prompt",
    )(proj, proj, proj, proj, proj, bt2, ghc)


def _band_sample_kernel(q_ref, k_ref, v_ref, ck_ref, cv_ref, bias_ref, gh_ref, out_ref):
    npast = ck_ref.shape[0]
    rows = slice(0, q_ref.shape[0])
    for h in range(H):
        hs = slice(h * DH, (h + 1) * DH)
        q = q_ref[:, hs].astype(BF16)
        s1 = _dot_nt(q, ck_ref[:, hs].astype(BF16)) * (DH ** -0.5) + bias_ref[h, :, :npast]
        s2 = _dot_nt(q, k_ref[:, hs].astype(BF16)) * (DH ** -0.5) + bias_ref[h, :, npast:]
        m = jnp.maximum(jnp.max(s1, axis=-1, keepdims=True), jnp.max(s2, axis=-1, keepdims=True))
        p1 = jnp.exp(s1 - m)
        p2 = jnp.exp(s2 - m)
        l = jnp.sum(p1, axis=-1, keepdims=True) + jnp.sum(p2, axis=-1, keepdims=True)
        o = (_dot(p1.astype(BF16), cv_ref[:, hs].astype(BF16))
             + _dot(p2.astype(BF16), v_ref[:, hs].astype(BF16))) / l
        _head_norm_store(out_ref, rows, h, o, gh_ref)


def _band_sample(proj3, ck, cv, bias, gh):
    b, s, _ = proj3.shape
    npast = ck.shape[1]
    return pl.pallas_call(
        _band_sample_kernel,
        grid=(b,),
        in_specs=[
            pl.BlockSpec((None, s, GW), lambda i: (i, 0, COL_A // GW)),
            pl.BlockSpec((None, s, GW), lambda i: (i, 0, COL_A // GW + 1)),
            pl.BlockSpec((None, s, GW), lambda i: (i, 0, COL_A // GW + 2)),
            pl.BlockSpec((None, npast, GW), lambda i: (i, 0, 0)),
            pl.BlockSpec((None, npast, GW), lambda i: (i, 0, 0)),
            pl.BlockSpec((H, s, npast + s), lambda i: (0, 0, 0)),
            pl.BlockSpec((1, GW), lambda i: (0, 0)),
        ],
        out_specs=pl.BlockSpec((None, s, GW), lambda i: (i, 0, 0)),
        out_shape=jax.ShapeDtypeStruct((b, s, GW), BF16),
        compiler_params=_cparams("parallel"),
        name="band_sample",
    )(proj3, proj3, proj3, ck, cv, bias, gh)


def _cumsum_cols(x, lo_tri):
    return sum(_dot(lo_tri, part) for part in _split3(x))


def _cumsum_rows(x, up_tri):
    return sum(_dot(part, up_tri) for part in _split3(x))


def _tri_masks(l):
    r = _iota((l, l), 0)
    c = _iota((l, l), 1)
    return r >= c, r > c


def _mlstm_kernel(q_ref, k_ref, v_ref, o_ref, gc_ref, gr_ref, bc_ref, br_ref, gh_ref,
                  c0_ref, n0_ref, m0_ref, out_ref, c_ref, n_ref, m_ref, *, l, nck):
    @pl.when(pl.program_id(1) == 0)
    def _():
        c_ref[...] = c0_ref[...]
        n_ref[...] = n0_ref[...]
        m_ref[...] = m0_ref[...]

    incl, _ = _tri_masks(l)
    lo_tri = incl.astype(BF16)
    up_tri = (_iota((l, l), 0) <= _iota((l, l), 1)).astype(BF16)

    probs = []
    for ck in range(nck):
        rows = slice(ck * l, (ck + 1) * l)
        gcol = gc_ref[rows, :] + bc_ref[...]
        grow = gr_ref[ck] + br_ref[...]
        gcs = _cumsum_cols(_log_sigmoid(gcol), lo_tri)
        grs = _cumsum_rows(_log_sigmoid(grow), up_tri)
        for h in range(H):
            hs = slice(h * DH, (h + 1) * DH)
            ig_c = gcol[:, GATE_OFF + h:GATE_OFF + h + 1]
            g_c = gcs[:, GATE_OFF + H + h:GATE_OFF + H + h + 1]
            ig_r = grow[h:h + 1, :]
            g_r = grs[H + h:H + h + 1, :]
            q = q_ref[rows, hs]
            kf = k_ref[rows, hs] * (DH ** -0.5)
            lmat = jnp.where(incl, g_c - g_r + ig_r, NEG_INF)
            probs.append(dict(
                rows=rows, h=h, q=q, kf=kf, qb=q.astype(BF16), kb=kf.astype(BF16),
                vb=v_ref[rows, hs].astype(BF16), lmat=lmat, lmax=jnp.max(lmat, axis=-1, keepdims=True),
                g_c=g_c, ig_c=ig_c, g_last=g_c[l - 1:l, :]))
    qks = [_dot_nt(p["qb"], p["kb"]) for p in probs]

    ms = [m_ref[:, h:h + 1] for h in range(H)]
    for p in probs:
        m_old = ms[p["h"]]
        p["linter"] = p["g_c"] + m_old
        p["mt"] = jnp.maximum(p["linter"], p["lmax"])
        m_new = p["mt"][l - 1:l, :]
        p["dprev"] = jnp.exp(p["g_last"] + m_old - m_new)
        p["kw"] = p["kf"] * jnp.exp(p["g_last"] - p["g_c"] + p["ig_c"] - m_new)
        ms[p["h"]] = m_new
    ws_ = [qk * jnp.exp(p["lmat"] - p["mt"]) for p, qk in zip(probs, qks)]
    wvs = [_dot(w.astype(BF16), p["vb"]) for p, w in zip(probs, ws_)]
    upds = [_dot_tn(p["kw"].astype(BF16), p["vb"]) for p in probs]

    cs = [c_ref[h] for h in range(H)]
    ns = [n_ref[h:h + 1, :] for h in range(H)]
    qcs, qns = [], []
    for p, upd in zip(probs, upds):
        h = p["h"]
        qcs.append(_dot(p["qb"], cs[h].astype(BF16)))
        qns.append(jnp.sum(p["q"] * ns[h], axis=-1, keepdims=True))
        cs[h] = p["dprev"] * cs[h] + upd
        ns[h] = p["dprev"] * ns[h] + jnp.sum(p["kw"], axis=0, keepdims=True)
    for h in range(H):
        c_ref[h] = cs[h]
        n_ref[h:h + 1, :] = ns[h]
        m_ref[:, h:h + 1] = ms[h]

    wsums = [jnp.sum(w, axis=-1, keepdims=True) for w in ws_]
    obs = []
    for p, wsum, wv, qc, qn in zip(probs, wsums, wvs, qcs, qns):
        hs = slice(p["h"] * DH, (p["h"] + 1) * DH)
        inter = jnp.exp(p["linter"] - p["mt"])
        den = wsum + inter * qn
        hout = (wv + inter * qc) / jnp.maximum(jnp.abs(den), jnp.exp(-p["mt"]))
        obs.append(hout * _sigmoid(o_ref[p["rows"], hs]))
    msq = [jnp.mean(ob * ob, axis=-1, keepdims=True) for ob in obs]
    for p, ob, ms_ in zip(probs, obs, msq):
        hs = slice(p["h"] * DH, (p["h"] + 1) * DH)
        out_ref[p["rows"], hs] = (ob * lax.rsqrt(ms_ + EPS) * gh_ref[:, hs]).astype(out_ref.dtype)


def _mlstm(proj3, gt3, bias_c, bias_r, gh, c0, n0, m0, l, nck):
    b, t, _ = proj3.shape
    steps = t // (l * nck)
    blk = l * nck
    col = lambda j: pl.BlockSpec((None, blk, GW), lambda bi, s: (bi, s, j))
    state = lambda shp: pl.BlockSpec((None,) + shp, lambda bi, s: (bi,) + (0,) * len(shp))
    return pl.pallas_call(
        functools.partial(_mlstm_kernel, l=l, nck=nck),
        grid=(b, steps),
        in_specs=[
            col(COL_B // GW), col(COL_B // GW + 1), col(COL_B // GW + 2), col(COL_B // GW + 3),
            pl.BlockSpec((None, blk, LANES), lambda bi, s: (bi, s, GATE_BLK)),
            pl.BlockSpec((None, nck, N_GATES, l), lambda bi, s: (bi, s, 0, 0)),
            pl.BlockSpec((1, LANES), lambda bi, s: (0, 0)),
            pl.BlockSpec((N_GATES, 1), lambda bi, s: (0, 0)),
            pl.BlockSpec((1, GW), lambda bi, s: (0, 0)),
            state((H, DH, DH)), state((H, DH)), state((1, H)),
        ],
        out_specs=[
            pl.BlockSpec((None, blk, GW), lambda bi, s: (bi, s, 0)),
            state((H, DH, DH)), state((H, DH)), state((1, H)),
        ],
        out_shape=[
            jax.ShapeDtypeStruct((b, t, GW), BF16),
            jax.ShapeDtypeStruct((b, H, DH, DH), F32),
            jax.ShapeDtypeStruct((b, H, DH), F32),
            jax.ShapeDtypeStruct((b, 1, H), F32),
        ],
        compiler_params=_cparams("parallel", "arbitrary"),
        name="mlstm",
    )(proj3, proj3, proj3, proj3, proj3, gt3, bias_c, bias_r, gh, c0, n0, m0)


def _head_of(idx):
    return idx // DH


def _block_mask(n_rows, n_cols):
    return _head_of(_iota((n_rows, n_cols), 0)) == _head_of(_iota((n_rows, n_cols), 1))


def _expander(first_lane):
    r = _iota((LANES, GW), 0)
    c = _iota((LANES, GW), 1)
    return (r == first_lane + _head_of(c)).astype(BF16)


def _dot_stacked(parts, rhs):
    m = parts[0].shape[0]
    y = _dot(jnp.concatenate(parts, axis=0), rhs)
    return sum(y[i * m:(i + 1) * m] for i in range(len(parts)))


def _expand(x, e):
    return _dot_stacked(_split3(x), e)


def _head_sums(a, bones):
    return _dot_stacked(_split3(a), bones)


def _cumsum_cols_wide(x, lo_tri):
    w = x.shape[1]
    y = _dot(lo_tri, jnp.concatenate(_split3(x), axis=1))
    return y[:, :w] + y[:, w:2 * w] + y[:, 2 * w:]


def _row_select(x_t, first_row):
    r = _iota(x_t.shape, 0)
    c = _iota(x_t.shape, 1)
    return jnp.sum(jnp.where(r == first_row + _head_of(c), x_t, 0.0), axis=0, keepdims=True)


def _block_diag_rows(x, mask):
    return jnp.where(mask, jnp.concatenate([x] * H, axis=0), jnp.zeros((), x.dtype))


def _cummax_rows(x):
    rows = _iota(x.shape, 0)
    sh = 1
    while sh < x.shape[0]:
        x = jnp.maximum(x, jnp.where(rows >= sh, pltpu.roll(x, sh, axis=0), NEG_INF))
        sh *= 2
    return x


def _mlstm64_kernel(q_ref, k_ref, v_ref, o_ref, gc_ref, gr_ref, bc_ref, br_ref, gh_ref,
                    c0_ref, n0_ref, m0_ref, out_ref, c_ref, n_ref, m_ref, *, nck):
    l = CHUNK

    @pl.when(pl.program_id(1) == 0)
    def _():
        c_ref[...] = c0_ref[...]
        n_ref[...] = n0_ref[...]
        m_ref[...] = m0_ref[...]

    bmask = _block_mask(GW, GW)
    bones = bmask.astype(BF16)
    within = _iota((GW, GW), 0) % DH <= _iota((GW, GW), 1) % DH
    up_bd = jnp.logical_and(bmask, within).astype(BF16)
    lo_tri = (_iota((l, l), 0) >= _iota((l, l), 1)).astype(BF16)
    incl = _iota((l, GW), 0) >= _iota((l, GW), 1) % DH
    e_i = _expander(GATE_OFF)
    e_f = _expander(GATE_OFF + H)

    cks = []
    for ck in range(nck):
        rows = slice(ck * l, (ck + 1) * l)
        gcol = gc_ref[rows, :] + bc_ref[...]
        gcs = _cumsum_cols_wide(_log_sigmoid(gcol), lo_tri)
        g_c = _expand(gcs, e_f)
        i_c = _expand(gcol, e_i)
        grow = gr_ref[ck] + br_ref[...]
        grow_t = jnp.concatenate([grow] * H, axis=1)
        grs_t = _dot_stacked(_split3(_log_sigmoid(grow_t)), up_bd)
        a_r = _row_select(grow_t, 0) - _row_select(grs_t, H)
        lmat = jnp.where(incl, g_c + a_r, NEG_INF)
        lmax = g_c + _cummax_rows(i_c - g_c)
        q = q_ref[rows, :]
        kf = k_ref[rows, :] * (DH ** -0.5)
        cks.append(dict(rows=rows, g_c=g_c, i_c=i_c, lmat=lmat, lmax=lmax, q=q, kf=kf,
                        qb=q.astype(BF16), kb=kf.astype(BF16), vb=v_ref[rows, :].astype(BF16),
                        g_last=g_c[l - 1:l, :]))
    scs = [_dot_nt(p["qb"], _block_diag_rows(p["kb"], bmask)) for p in cks]

    m_run = m_ref[...]
    for p in cks:
        p["linter"] = p["g_c"] + m_run
        p["mt"] = jnp.maximum(p["linter"], p["lmax"])
        m_new = p["mt"][l - 1:l, :]
        p["dprev"] = jnp.exp(p["g_last"] + m_run - m_new)
        p["kw"] = p["kf"] * jnp.exp(p["g_last"] - p["g_c"] + p["i_c"] - m_new)
        m_run = m_new
    m_ref[...] = m_run
    wbs = [(s * jnp.exp(p["lmat"] - p["mt"])).astype(BF16) for p, s in zip(cks, scs)]
    nums = [_dot(w, _block_diag_rows(p["vb"], bmask)) for p, w in zip(cks, wbs)]
    wsums = [_dot(w, bones) for w in wbs]
    upds = [jnp.where(bmask, _dot_tn(p["kw"].astype(BF16), p["vb"]), 0.0) for p in cks]

    c_run = c_ref[...]
    n_run = n_ref[...]
    qcs, qns = [], []
    for p, upd in zip(cks, upds):
        qcs.append(_dot(p["qb"], c_run.astype(BF16)))
        qns.append(_dot((p["q"] * n_run).astype(BF16), bones))
        c_run = p["dprev"] * c_run + upd
        n_run = p["dprev"] * n_run + jnp.sum(p["kw"], axis=0, keepdims=True)
    c_ref[...] = c_run
    n_ref[...] = n_run

    obs = []
    for p, num, wsum, qc, qn in zip(cks, nums, wsums, qcs, qns):
        inter = jnp.exp(p["linter"] - p["mt"])
        den = wsum + inter * qn
        hout = (num + inter * qc) / jnp.maximum(jnp.abs(den), jnp.exp(-p["mt"]))
        obs.append(hout * _sigmoid(o_ref[p["rows"], :]))
    msq = [_head_sums(ob * ob, bones) * (1.0 / DH) for ob in obs]
    for p, ob, m2 in zip(cks, obs, msq):
        out_ref[p["rows"], :] = (ob * lax.rsqrt(m2 + EPS) * gh_ref[...]).astype(out_ref.dtype)


def _mlstm64(proj3, gt3, bias_c, bias_r, gh, c0, n0, m0, nck):
    b, t, _ = proj3.shape
    blk = CHUNK * nck
    steps = t // blk
    col = lambda j: pl.BlockSpec((None, blk, GW), lambda bi, s: (bi, s, j))
    state = lambda shp: pl.BlockSpec((None,) + shp, lambda bi, s: (bi,) + (0,) * len(shp))
    return pl.pallas_call(
        functools.partial(_mlstm64_kernel, nck=nck),
        grid=(b, steps),
        in_specs=[
            col(COL_B // GW), col(COL_B // GW + 1), col(COL_B // GW + 2), col(COL_B // GW + 3),
            pl.BlockSpec((None, blk, LANES), lambda bi, s: (bi, s, GATE_BLK)),
            pl.BlockSpec((None, nck, N_GATES, CHUNK), lambda bi, s: (bi, s, 0, 0)),
            pl.BlockSpec((1, LANES), lambda bi, s: (0, 0)),
            pl.BlockSpec((N_GATES, 1), lambda bi, s: (0, 0)),
            pl.BlockSpec((1, GW), lambda bi, s: (0, 0)),
            state((GW, GW)), state((1, GW)), state((1, GW)),
        ],
        out_specs=[
            pl.BlockSpec((None, blk, GW), lambda bi, s: (bi, s, 0)),
            state((GW, GW)), state((1, GW)), state((1, GW)),
        ],
        out_shape=[
            jax.ShapeDtypeStruct((b, t, GW), BF16),
            jax.ShapeDtypeStruct((b, GW, GW), F32),
            jax.ShapeDtypeStruct((b, 1, GW), F32),
            jax.ShapeDtypeStruct((b, 1, GW), F32),
        ],
        compiler_params=_cparams("parallel", "arbitrary"),
        name="mlstm64",
    )(proj3, proj3, proj3, proj3, proj3, gt3, bias_c, bias_r, gh, c0, n0, m0)


def _to_block_diag(c):
    b = c.shape[0]
    eye = jnp.eye(H, dtype=c.dtype)
    return jnp.einsum("bhde,hg->bhdge", c, eye).reshape(b, GW, GW)


def _from_block_diag(cbd):
    b = cbd.shape[0]
    c5 = cbd.reshape(b, H, DH, H, DH)
    return jnp.stack([c5[:, h, :, h, :] for h in range(H)], axis=1)


def _split2(x):
    hi = x.astype(BF16)
    lo = (x - hi.astype(F32)).astype(BF16)
    return hi, lo


def _dot_sp(a, b):
    return _dot(a[0], b[0]) + (_dot(a[0], b[1]) + _dot(a[1], b[0]))


def _unit_lower_inverses(nmats, l):
    eye = (_iota((l, l), 0) == _iota((l, l), 1)).astype(F32)
    ps = [eye - n for n in nmats]
    qs = [_split2(n) for n in nmats]
    qs = [_split2(_dot_sp(q, q)) for q in qs]
    power = 2
    while power < l:
        ps = [p + _dot_sp(_split2(p), q) for p, q in zip(ps, qs)]
        power *= 2
        if power < l:
            qs = [_split2(_dot_sp(q, q)) for q in qs]
    return ps


def _l2norm(x):
    return x * lax.rsqrt(jnp.sum(x * x, axis=-1, keepdims=True) + 1e-6)


def _gdn_kernel(x_ref, z_ref, gc_ref, gr_ref, hist_ref, cw_ref, ac_ref, ar_ref, dc_ref, dr_ref,
                gh_ref, s0_ref, out_ref, s_ref, carry_scr, *, l, nck):
    @pl.when(pl.program_id(1) == 0)
    def _():
        s_ref[...] = s0_ref[...]
        carry_scr[...] = hist_ref[...]

    blk = l * nck
    x = x_ref[...]
    ext = jnp.concatenate([carry_scr[...], x], axis=0)
    carry_scr[...] = x[blk - SUBLANES:, :]
    y = x * cw_ref[3:4, :]
    for j in range(1, 4):
        y = y + ext[SUBLANES - j:SUBLANES - j + blk, :] * cw_ref[3 - j:4 - j, :]
    y = y * _sigmoid(y)

    incl, strict = _tri_masks(l)
    lo_tri = incl.astype(BF16)
    up_tri = (_iota((l, l), 0) <= _iota((l, l), 1)).astype(BF16)

    qraw = [y[ck * l:(ck + 1) * l, h * DH:(h + 1) * DH] for ck in range(nck) for h in range(H)]
    kraw = [y[ck * l:(ck + 1) * l, GW + h * DH:GW + (h + 1) * DH] for ck in range(nck) for h in range(H)]
    vraw = [y[ck * l:(ck + 1) * l, 2 * GW + h * DH:2 * GW + (h + 1) * DH] for ck in range(nck) for h in range(H)]
    qnorm = [_l2norm(a) * (DH ** -0.5) for a in qraw]
    knorm = [_l2norm(a) for a in kraw]
    probs = []
    for ck in range(nck):
        rows = slice(ck * l, (ck + 1) * l)
        gcol = gc_ref[rows, :]
        grow = gr_ref[ck]
        beta_cs = _sigmoid(gcol)
        dec_c = -jnp.exp(ac_ref[...]) * _softplus(gcol + dc_ref[...])
        dec_r = -jnp.exp(ar_ref[...]) * _softplus(grow + dr_ref[...])
        gcs = _cumsum_cols(dec_c, lo_tri)
        grs = _cumsum_rows(dec_r, up_tri)
        for h in range(H):
            beta = beta_cs[:, GATE_OFF + 2 * H + h:GATE_OFF + 2 * H + h + 1]
            g_c = gcs[:, GATE_OFF + 3 * H + h:GATE_OFF + 3 * H + h + 1]
            g_r = grs[3 * H + h:3 * H + h + 1, :]
            q, k, v = qnorm[ck * H + h], knorm[ck * H + h], vraw[ck * H + h]
            decay = jnp.exp(jnp.where(incl, g_c - g_r, NEG_INF))
            eg = jnp.exp(g_c)
            g_last = g_c[l - 1:l, :]
            probs.append(dict(
                rows=rows, h=h, qb=q.astype(BF16), kb=k.astype(BF16), beta=beta, decay=decay,
                rhs=jnp.concatenate([v * beta, k * (beta * eg)], axis=-1),
                qeg=(q * eg).astype(BF16), kdec=(k * jnp.exp(g_last - g_c)).astype(BF16),
                sdec=jnp.exp(g_last)))
    kks = [_dot_nt(p["kb"], p["kb"]) for p in probs]
    qks = [_dot_nt(p["qb"], p["kb"]) for p in probs]
    a_lows = [jnp.where(strict, p["beta"] * kk * p["decay"], 0.0) for p, kk in zip(probs, kks)]
    attns = [(qk * p["decay"]).astype(BF16) for p, qk in zip(probs, qks)]
    tinvs = _unit_lower_inverses(a_lows, l)
    sols = [_dot_sp(_split2(t), _split2(p["rhs"])) for t, p in zip(tinvs, probs)]

    states = [s_ref[h] for h in range(H)]
    for ck in range(nck):
        ps = probs[ck * H:(ck + 1) * H]
        ss = sols[ck * H:(ck + 1) * H]
        at = attns[ck * H:(ck + 1) * H]
        sbs = [s.astype(BF16) for s in states]
        wss = [_dot(sol[:, DH:].astype(BF16), sb) for sol, sb in zip(ss, sbs)]
        qss = [_dot(p["qeg"], sb) for p, sb in zip(ps, sbs)]
        vnbs = [(sol[:, :DH] - ws).astype(BF16) for sol, ws in zip(ss, wss)]
        os_ = [qs + _dot(a, vnb) for qs, a, vnb in zip(qss, at, vnbs)]
        states = [p["sdec"] * s + _dot_tn(p["kdec"], vnb) for p, s, vnb in zip(ps, states, vnbs)]
        for p, o in zip(ps, os_):
            hs = slice(p["h"] * DH, (p["h"] + 1) * DH)
            zg = z_ref[p["rows"], hs]
            yo = (o * lax.rsqrt(jnp.mean(o * o, axis=-1, keepdims=True) + EPS) * gh_ref[:, hs]
                  * (zg * _sigmoid(zg)))
            out_ref[p["rows"], hs] = yo.astype(out_ref.dtype)
    for h in range(H):
        s_ref[h] = states[h]


def _gdn(proj3, gt3, hist8, cw, a_c, a_r, dt_c, dt_r, gh, s0, l, nck):
    b, t, _ = proj3.shape
    blk = l * nck
    steps = t // blk
    state = lambda shp: pl.BlockSpec((None,) + shp, lambda bi, s: (bi,) + (0,) * len(shp))
    const = lambda shp: pl.BlockSpec(shp, lambda bi, s: (0,) * len(shp))
    return pl.pallas_call(
        functools.partial(_gdn_kernel, l=l, nck=nck),
        grid=(b, steps),
        in_specs=[
            pl.BlockSpec((None, blk, 3 * GW), lambda bi, s: (bi, s, COL_CX // (3 * GW))),
            pl.BlockSpec((None, blk, GW), lambda bi, s: (bi, s, COL_CZ // GW)),
            pl.BlockSpec((None, blk, LANES), lambda bi, s: (bi, s, GATE_BLK)),
            pl.BlockSpec((None, nck, N_GATES, l), lambda bi, s: (bi, s, 0, 0)),
            state((SUBLANES, 3 * GW)),
            const((4, 3 * GW)),
            const((1, LANES)), const((N_GATES, 1)), const((1, LANES)), const((N_GATES, 1)),
            const((1, GW)),
            state((H, DH, DH)),
        ],
        out_specs=[
            pl.BlockSpec((None, blk, GW), lambda bi, s: (bi, s, 0)),
            state((H, DH, DH)),
        ],
        out_shape=[
            jax.ShapeDtypeStruct((b, t, GW), BF16),
            jax.ShapeDtypeStruct((b, H, DH, DH), F32),
        ],
        scratch_shapes=[pltpu.VMEM((SUBLANES, 3 * GW), F32)],
        compiler_params=_cparams("parallel", "arbitrary"),
        name="gdn",
    )(proj3, proj3, proj3, gt3, hist8, cw, a_c, a_r, dt_c, dt_r, gh, s0)


def _bd_split(x, mask):
    hi, lo = _split2(x)
    return _block_diag_rows(hi, mask), _block_diag_rows(lo, mask)


def _gdn64_kernel(x_ref, z_ref, gc_ref, gr_ref, hist_ref, cw_ref, ac_ref, ar_ref, dc_ref, dr_ref,
                  gh_ref, s0_ref, out_ref, s_ref, carry_scr, *, nck):
    l = CHUNK

    @pl.when(pl.program_id(1) == 0)
    def _():
        s_ref[...] = s0_ref[...]
        carry_scr[...] = hist_ref[...]

    blk = l * nck
    x = x_ref[...]
    ext = jnp.concatenate([carry_scr[...], x], axis=0)
    carry_scr[...] = x[blk - SUBLANES:, :]
    y = x * cw_ref[3:4, :]
    for j in range(1, 4):
        y = y + ext[SUBLANES - j:SUBLANES - j + blk, :] * cw_ref[3 - j:4 - j, :]
    y = y * _sigmoid(y)

    bmask = _block_mask(GW, GW)
    bones = bmask.astype(BF16)
    within = _iota((GW, GW), 0) % DH <= _iota((GW, GW), 1) % DH
    up_bd = jnp.logical_and(bmask, within).astype(BF16)
    lo_tri = (_iota((l, l), 0) >= _iota((l, l), 1)).astype(BF16)
    key_pos = _iota((l, GW), 1) % DH
    incl = _iota((l, GW), 0) >= key_pos
    strict = _iota((l, GW), 0) > key_pos
    eye_t = (_iota((l, GW), 0) == key_pos).astype(F32)
    e_b = _expander(GATE_OFF + 2 * H)
    e_a = _expander(GATE_OFF + 3 * H)

    def head_sums(a):
        return _head_sums(a, bones)

    def shared_rhs(lhs_splits, rhs_bd):
        n = len(lhs_splits)
        big = _dot(jnp.concatenate([part for sp in lhs_splits for part in sp], axis=0), rhs_bd[0])
        small = _dot(jnp.concatenate([sp[0] for sp in lhs_splits], axis=0), rhs_bd[1])
        return [big[2 * i * l:(2 * i + 1) * l] + big[(2 * i + 1) * l:(2 * i + 2) * l]
                + small[i * l:(i + 1) * l] for i in range(n)]

    yq, yk, yv = y[:, :GW], y[:, GW:2 * GW], y[:, 2 * GW:]
    qn_all = yq * lax.rsqrt(head_sums(yq * yq) + 1e-6) * (DH ** -0.5)
    kn_all = yk * lax.rsqrt(head_sums(yk * yk) + 1e-6)

    cks = []
    for ck in range(nck):
        rows = slice(ck * l, (ck + 1) * l)
        gcol = gc_ref[rows, :]
        dec_c = -jnp.exp(ac_ref[...]) * _softplus(gcol + dc_ref[...])
        beta = _expand(_sigmoid(gcol), e_b)
        g_c = _expand(_cumsum_cols_wide(dec_c, lo_tri), e_a)
        grow_t = jnp.concatenate([gr_ref[ck]] * H, axis=1)
        dec_r = -jnp.exp(ar_ref[...]) * _softplus(grow_t + dr_ref[...])
        g_r = _row_select(_dot_stacked(_split3(dec_r), up_bd), 3 * H)
        decay = jnp.exp(jnp.where(incl, g_c - g_r, NEG_INF))
        eg = jnp.exp(g_c)
        g_last = g_c[l - 1:l, :]
        q, k, v = qn_all[rows], kn_all[rows], yv[rows]
        cks.append(dict(rows=rows, qb=q.astype(BF16), kb=k.astype(BF16),
                        beta=beta, decay=decay, rhs_v=v * beta, rhs_k=k * (beta * eg),
                        qeg=(q * eg).astype(BF16), kdec=(k * jnp.exp(g_last - g_c)).astype(BF16),
                        sdec=jnp.exp(g_last)))
    kqs = [_dot_nt(jnp.concatenate([p["kb"], p["qb"]], axis=0), _block_diag_rows(p["kb"], bmask))
           for p in cks]
    nmats = [jnp.where(strict, p["beta"] * kq[:l] * p["decay"], 0.0) for p, kq in zip(cks, kqs)]
    attns = [(kq[l:] * p["decay"]).astype(BF16) for p, kq in zip(cks, kqs)]

    ps = [eye_t - n for n in nmats]
    qs = [shared_rhs([_split2(n)], _bd_split(n, bmask))[0] for n in nmats]
    power = 2
    while power < l:
        power *= 2
        if power < l:
            res = [shared_rhs([_split2(p), _split2(q)], _bd_split(q, bmask)) for p, q in zip(ps, qs)]
            ps = [p + r[0] for p, r in zip(ps, res)]
            qs = [r[1] for r in res]
        else:
            ps = [p + shared_rhs([_split2(p)], _bd_split(q, bmask))[0] for p, q in zip(ps, qs)]
    tsp = [_split2(p) for p in ps]
    us = [shared_rhs([t], _bd_split(p["rhs_v"], bmask))[0] for t, p in zip(tsp, cks)]
    ws = [shared_rhs([t], _bd_split(p["rhs_k"], bmask))[0].astype(BF16) for t, p in zip(tsp, cks)]

    wu = [jnp.concatenate([w, u.astype(BF16)], axis=1) for w, u in zip(ws, us)]
    kwu = [_dot_tn(p["kdec"], x) for p, x in zip(cks, wu)]
    awu = [_dot(at, jnp.concatenate([_block_diag_rows(x[:, :GW], bmask),
                                     _block_diag_rows(x[:, GW:], bmask)], axis=1))
           for at, x in zip(attns, wu)]
    gmats = [jnp.where(bmask, x[:, :GW], 0.0).astype(BF16) for x in kwu]
    bmats = [jnp.where(bmask, x[:, GW:], 0.0) for x in kwu]
    qts = [(p["qeg"].astype(F32) - x[:, :GW]).astype(BF16) for p, x in zip(cks, awu)]

    s_run = s_ref[...]
    outs = []
    for p, g, bm, qt, x in zip(cks, gmats, bmats, qts, awu):
        ys = _dot(jnp.concatenate([g, qt], axis=0), s_run.astype(BF16))
        outs.append(ys[GW:] + x[:, GW:])
        s_run = p["sdec"] * s_run - ys[:GW] + bm
    s_ref[...] = s_run

    msq = [head_sums(o * o) * (1.0 / DH) for o in outs]
    for p, o, m2 in zip(cks, outs, msq):
        zg = z_ref[p["rows"], :]
        yo = o * lax.rsqrt(m2 + EPS) * gh_ref[...] * (zg * _sigmoid(zg))
        out_ref[p["rows"], :] = yo.astype(out_ref.dtype)


def _gdn64(proj3, gt3, hist8, cw, a_c, a_r, dt_c, dt_r, gh, s0, nck):
    b, t, _ = proj3.shape
    blk = CHUNK * nck
    steps = t // blk
    state = lambda shp: pl.BlockSpec((None,) + shp, lambda bi, s: (bi,) + (0,) * len(shp))
    const = lambda shp: pl.BlockSpec(shp, lambda bi, s: (0,) * len(shp))
    return pl.pallas_call(
        functools.partial(_gdn64_kernel, nck=nck),
        grid=(b, steps),
        in_specs=[
            pl.BlockSpec((None, blk, 3 * GW), lambda bi, s: (bi, s, COL_CX // (3 * GW))),
            pl.BlockSpec((None, blk, GW), lambda bi, s: (bi, s, COL_CZ // GW)),
            pl.BlockSpec((None, blk, LANES), lambda bi, s: (bi, s, GATE_BLK)),
            pl.BlockSpec((None, nck, N_GATES, CHUNK), lambda bi, s: (bi, s, 0, 0)),
            state((SUBLANES, 3 * GW)),
            const((4, 3 * GW)),
            const((1, LANES)), const((N_GATES, 1)), const((1, LANES)), const((N_GATES, 1)),
            const((1, GW)),
            state((GW, GW)),
        ],
        out_specs=[
            pl.BlockSpec((None, blk, GW), lambda bi, s: (bi, s, 0)),
            state((GW, GW)),
        ],
        out_shape=[
            jax.ShapeDtypeStruct((b, t, GW), BF16),
            jax.ShapeDtypeStruct((b, GW, GW), F32),
        ],
        scratch_shapes=[pltpu.VMEM((SUBLANES, 3 * GW), F32)],
        compiler_params=_cparams("parallel", "arbitrary"),
        name="gdn64",
    )(proj3, proj3, proj3, gt3, hist8, cw, a_c, a_r, dt_c, dt_r, gh, s0)


HEAD_PAD = 128
DPAD = H * HEAD_PAD
BF16_ROWS = 16
VT_PAD = -(-(DH + 1) // BF16_ROWS) * BF16_ROWS
VT_ROWS = H * VT_PAD


def _tile_heads(t):
    return jnp.concatenate([t] * H, axis=-1)


def _dprep_kernel(tail_ref, gq_ref, gkv_ref, wq_ref, wqp_ref, ka_ref, ckv_ref, kpe_ref, qc_ref):
    ka = ka_ref[...]
    kb = pltpu.roll(ka, 64, axis=1)
    nope = _iota(ka.shape, 1) < QK_NOPE
    qcos = jnp.where(nope, 1.0, kb)
    qsin = jnp.where(nope, 0.0, ka)
    hq = _rms(tail_ref[:, :Q_LORA], gq_ref[...]).astype(BF16)
    qc = _dot(hq, wq_ref[...]) * _tile_heads(qcos) + _dot(hq, wqp_ref[...]) * _tile_heads(qsin)
    qc_ref[...] = (qc * (MLA_SCALE * LOG2E)).astype(BF16)
    ckv_ref[...] = _rms(tail_ref[:, Q_LORA:Q_LORA + KV_LORA], gkv_ref[...])
    kr = tail_ref[:, Q_LORA + KV_LORA:]
    kpe = kr * ka + pltpu.roll(kr, 64, axis=1) * kb
    kpe_ref[...] = kpe[:, :QK_ROPE]


def _dprep(proj, gq, gkv, wq, wqp, ka, tm):
    n = proj.shape[0]
    row = lambda w: pl.BlockSpec((tm, w), lambda i: (i, 0))
    const = lambda a, b: pl.BlockSpec((a, b), lambda i: (0, 0))
    return pl.pallas_call(
        _dprep_kernel,
        grid=(n // tm,),
        in_specs=[
            pl.BlockSpec((tm, TAIL_W), lambda i: (i, COL_TAIL // TAIL_W)),
            const(1, Q_LORA), const(1, KV_LORA), const(Q_LORA, DPAD), const(Q_LORA, DPAD),
            row(LANES),
        ],
        out_specs=[row(KV_LORA), row(QK_ROPE), row(DPAD)],
        out_shape=[
            jax.ShapeDtypeStruct((n, KV_LORA), F32),
            jax.ShapeDtypeStruct((n, QK_ROPE), F32),
            jax.ShapeDtypeStruct((n, DPAD), BF16),
        ],
        compiler_params=_cparams("parallel"),
        name="dprep",
    )(proj, gq, gkv, wq, wqp, ka)


LOG2E = 1.4426950408889634


def _kvup_t_kernel(ckv_ref, kpe_ref, wk_ref, wvt_ref, pm_ref, onet_ref, kc_ref, vt_ref):
    c = ckv_ref[...].astype(BF16)
    kc_ref[...] = (_dot(c, wk_ref[...]) + _dot(kpe_ref[...].astype(BF16), pm_ref[...])).astype(BF16)
    vt_ref[...] = (_dot_nt(wvt_ref[...], c) + onet_ref[...]).astype(BF16)


def _kvup_t(ckv, kpe, wk, wvt, pm, onet, tm):
    m = ckv.shape[0]
    row = lambda w: pl.BlockSpec((tm, w), lambda i: (i, 0))
    const = lambda a, b: pl.BlockSpec((a, b), lambda i: (0, 0))
    return pl.pallas_call(
        _kvup_t_kernel,
        grid=(m // tm,),
        in_specs=[row(KV_LORA), row(QK_ROPE), const(KV_LORA, DPAD), const(VT_ROWS, KV_LORA),
                  const(QK_ROPE, DPAD), const(VT_ROWS, 1)],
        out_specs=[row(DPAD), pl.BlockSpec((VT_ROWS, tm), lambda i: (0, i))],
        out_shape=[jax.ShapeDtypeStruct((m, DPAD), BF16), jax.ShapeDtypeStruct((VT_ROWS, m), BF16)],
        compiler_params=_cparams("parallel"),
        name="kvup_t",
    )(ckv, kpe, wk, wvt, pm, onet)


def _mla_prompt_kernel(qi_ref, ki_ref, q_ref, k_ref, vt_ref, ghc_ref, out_ref, m_scr, acc_scr, *, bq, bk,
                       qw, ahead):
    p = pl.program_id(0)
    q_i = qi_ref[p]
    k_i = ki_ref[p]
    last = (q_i * bq) // bk

    @pl.when(k_i == 0)
    def _():
        m_scr[...] = jnp.full(m_scr.shape, NEG_INF, F32)
        acc_scr[...] = jnp.zeros(acc_scr.shape, F32)

    def step(diag):
        if diag:
            key_chunk = k_i * (bk // CHUNK) + _iota((bk, bq), 0) // CHUNK
            qry_chunk = q_i * (bq // CHUNK) + _iota((bk, bq), 1) // CHUNK
            allowed = key_chunk <= qry_chunk

        units = [(h, c) for h in range(H) for c in range(bq // qw)]
        rc = min(bk, 64)

        def scores(u):
            h, c = u
            hs = slice(h * HEAD_PAD, (h + 1) * HEAD_PAD)
            return _dot_nt(k_ref[:, hs], q_ref[c * qw:(c + 1) * qw, hs])

        def update(u, st):
            h, c = u
            hs = slice(h * HEAD_PAD, (h + 1) * HEAD_PAD)
            qs = slice(c * qw, (c + 1) * qw)
            if diag:
                st = jnp.where(allowed[:, qs], st, NEG_INF)
            m_prev = m_scr[h, :, qs]
            mx = st[:rc]
            for r in range(1, bk // rc):
                mx = jnp.maximum(mx, st[r * rc:(r + 1) * rc])
            m_new = jnp.maximum(m_prev, jnp.max(mx, axis=0, keepdims=True))
            alpha = jnp.exp2(m_prev - m_new)[0:1]
            m_row = m_new[0:1]
            pt = jnp.concatenate([jnp.exp2(st[r * rc:(r + 1) * rc] - m_row).astype(BF16)
                                  for r in range(bk // rc)], axis=0)
            acc = alpha * acc_scr[h, :, qs] + _dot(vt_ref[h * VT_PAD:(h + 1) * VT_PAD, :], pt)
            if not diag:
                m_scr[h, :, qs] = m_new
                acc_scr[h, :, qs] = acc
            return acc

        accs = []
        pending = [scores(u) for u in units[:ahead]]
        for idx, u in enumerate(units):
            if idx + ahead < len(units):
                pending.append(scores(units[idx + ahead]))
            accs.append(update(u, pending.pop(0)))
        per_head = bq // qw
        return [jnp.concatenate(accs[h * per_head:(h + 1) * per_head], axis=1) for h in range(H)]

    @pl.when(k_i < last)
    def _():
        step(False)

    @pl.when(k_i == last)
    def _():
        ys = []
        for h, acc in enumerate(step(True)):
            o = acc[:DH] / acc[DH:DH + 1]
            ms = jnp.mean(o * o, axis=0, keepdims=True)
            ys.append(o * lax.rsqrt(ms + EPS) * ghc_ref[h * DH:(h + 1) * DH, :])
        out_ref[...] = jnp.concatenate(ys, axis=0).T.astype(out_ref.dtype)


def _mla_prompt(qc, kc, vt, ghc, bq, bk):
    t = qc.shape[0]
    assert t % bq == 0 and t % bk == 0 and bk % bq == 0
    pairs = [(i, j) for i in range(t // bq) for j in range((i * bq) // bk + 1)]
    qi = jnp.asarray([i for i, _ in pairs], jnp.int32)
    ki = jnp.asarray([j for _, j in pairs], jnp.int32)
    grid_spec = pltpu.PrefetchScalarGridSpec(
        num_scalar_prefetch=2,
        grid=(len(pairs),),
        in_specs=[
            pl.BlockSpec((bq, DPAD), lambda p, qi, ki: (qi[p], 0)),
            pl.BlockSpec((bk, DPAD), lambda p, qi, ki: (ki[p], 0)),
            pl.BlockSpec((VT_ROWS, bk), lambda p, qi, ki: (0, ki[p])),
            pl.BlockSpec((GW, 1), lambda p, qi, ki: (0, 0)),
        ],
        out_specs=pl.BlockSpec((bq, GW), lambda p, qi, ki: (qi[p], 0)),
        scratch_shapes=[
            pltpu.VMEM((H, SUBLANES, bq), F32),
            pltpu.VMEM((H, VT_PAD, bq), F32),
        ],
    )
    return pl.pallas_call(
        functools.partial(_mla_prompt_kernel, bq=bq, bk=bk, qw=min(bq, 256), ahead=2),
        grid_spec=grid_spec,
        out_shape=jax.ShapeDtypeStruct((t, GW), BF16),
        compiler_params=_cparams("arbitrary"),
        name="mla_prompt",
    )(qi, ki, qc, kc, vt, ghc)


LAT_W = 2 * LANES
ONE_LANE = KV_LORA + QK_ROPE


def _mla_sample_kernel(q_ref, ckvp_ref, kpep_ref, ckvn_ref, kpen_ref, wabs_ref, wv_ref, gh_ref, out_ref):
    s_len = q_ref.shape[0]

    def latent_rows(ckv_ref, kpe_ref):
        n = ckv_ref.shape[0]
        tail = jnp.concatenate([kpe_ref[...].astype(BF16), jnp.zeros((n, LANES - QK_ROPE), BF16)], axis=1)
        tail = jnp.where(_iota((n, LANES), 1) == QK_ROPE, jnp.ones((), BF16), tail)
        return jnp.concatenate([ckv_ref[...].astype(BF16), tail], axis=1)

    kvp = latent_rows(ckvp_ref, kpep_ref)
    kvn = latent_rows(ckvn_ref, kpen_ref)
    qabs = jnp.concatenate([_dot(q_ref[:, h * HEAD_PAD:(h + 1) * HEAD_PAD], wabs_ref[h]) for h in range(H)],
                           axis=0).astype(BF16)
    s1 = _dot_nt(qabs, kvp)
    s2 = _dot_nt(qabs, kvn)
    m = jnp.maximum(jnp.max(s1, axis=-1, keepdims=True), jnp.max(s2, axis=-1, keepdims=True))
    acc = _dot(jnp.exp2(s1 - m).astype(BF16), kvp) + _dot(jnp.exp2(s2 - m).astype(BF16), kvn)
    olat = (acc[:, :KV_LORA] / acc[:, ONE_LANE:ONE_LANE + 1]).astype(BF16)
    rows = slice(0, s_len)
    for h in range(H):
        o = _dot(olat[h * s_len:(h + 1) * s_len], wv_ref[h])
        _head_norm_store(out_ref, rows, h, o, gh_ref)


def _mla_sample(qc3, ckvp, kpep, ckvn, kpen, wabs, wv, gh):
    b, s, _ = qc3.shape
    npast = ckvp.shape[1]
    per_b = lambda r, w: pl.BlockSpec((None, r, w), lambda i: (i, 0, 0))
    const = lambda shp: pl.BlockSpec(shp, lambda i: (0,) * len(shp))
    return pl.pallas_call(
        _mla_sample_kernel,
        grid=(b,),
        in_specs=[per_b(s, DPAD), per_b(npast, KV_LORA), per_b(npast, QK_ROPE), per_b(s, KV_LORA),
                  per_b(s, QK_ROPE), const((H, HEAD_PAD, LAT_W)), const((H, KV_LORA, DH)), const((1, GW))],
        out_specs=per_b(s, GW),
        out_shape=jax.ShapeDtypeStruct((b, s, GW), BF16),
        compiler_params=_cparams("parallel"),
        name="mla_sample",
    )(qc3, ckvp, kpep, ckvn, kpen, wabs, wv, gh)


def _outproj_kernel(x_ref, a_ref, b_ref, c_ref, d_ref, w_ref, out_ref):
    acc = x_ref[...]
    for g, m_ref in enumerate((a_ref, b_ref, c_ref, d_ref)):
        acc = acc + _dot(m_ref[...], w_ref[g * GW:(g + 1) * GW, :])
    out_ref[...] = acc


def _outproj(x, ma, mb, mc, md, w, tm):
    n = x.shape[0]
    mix = pl.BlockSpec((tm, GW), lambda i: (i, 0))
    return pl.pallas_call(
        _outproj_kernel,
        grid=(n // tm,),
        in_specs=[pl.BlockSpec((tm, D_MODEL), lambda i: (i, 0)), mix, mix, mix, mix,
                  pl.BlockSpec((D_MODEL, D_MODEL), lambda i: (0, 0))],
        out_specs=pl.BlockSpec((tm, D_MODEL), lambda i: (i, 0)),
        out_shape=jax.ShapeDtypeStruct((n, D_MODEL), F32),
        compiler_params=_cparams("parallel"),
        name="outproj",
    )(x, ma, mb, mc, md, w)


def _ffn_kernel(*refs, seq_len, final_norm, tf):
    if seq_len is None:
        (x_ref, g_ref, wup_ref, cw_ref, wd_ref, gf_ref, out_ref, ga_ref, act_scr, carry_scr) = refs
    else:
        (x_ref, g_ref, wup_ref, cw_ref, wd_ref, gf_ref, h1_ref, h2_ref, out_ref, ga_ref, act_scr) = refs
    tm = x_ref.shape[0]
    nj = D_FF // tf
    h = _rms(x_ref[...], g_ref[...]).astype(BF16)
    row = _iota((tm, tf), 0)

    if seq_len is None:
        @pl.when(pl.program_id(0) == 0)
        def _():
            carry_scr[...] = jnp.zeros(carry_scr.shape, F32)

    def up(j):
        cols = slice(j * tf, (j + 1) * tf)
        ucols = slice(D_FF + j * tf, D_FF + (j + 1) * tf)
        return _dot(h, wup_ref[:, cols]), _dot(h, wup_ref[:, ucols])

    def gate(j, ga, u):
        cols = slice(j * tf, (j + 1) * tf)
        r1 = pltpu.roll(ga, 1, axis=0)
        r2 = pltpu.roll(ga, 2, axis=0)
        if seq_len is None:
            c1 = carry_scr[SUBLANES - 1:SUBLANES, cols]
            c2 = carry_scr[SUBLANES - 2:SUBLANES - 1, cols]
            prev1 = jnp.where(row >= 1, r1, c1)
            prev2 = jnp.where(row >= 2, r2, jnp.where(row == 1, c1, c2))
            tail = ga[tm - SUBLANES:, :]
            carry_scr[:, cols] = tail
            ga_ref[:, cols] = tail
        else:
            t = row % seq_len
            prev1 = jnp.where(t >= 1, r1, h1_ref[:, cols])
            prev2 = jnp.where(t >= 2, r2, h2_ref[:, cols])
            ga_ref[:, cols] = ga
        conv = prev2 * cw_ref[0:1, cols] + prev1 * cw_ref[1:2, cols] + ga * cw_ref[2:3, cols]
        act_scr[:, cols] = (conv * _sigmoid(conv) * u).astype(BF16)

    pending = up(0)
    for j in range(nj):
        nxt = up(j + 1) if j + 1 < nj else None
        gate(j, *pending)
        pending = nxt
    y = x_ref[...] + _dot(act_scr[...], wd_ref[...])
    if final_norm:
        y = _rms(y, gf_ref[...])
    out_ref[...] = y


def _ffn(x, g, w_up, cw, w_down, gf, h1, h2, *, tm, tf, seq_len, final_norm):
    n = x.shape[0]
    ni = n // tm
    resident = lambda a, b: pl.BlockSpec((a, b), lambda i: (0, 0), pipeline_mode=pl.Buffered(1))
    in_specs = [
        pl.BlockSpec((tm, D_MODEL), lambda i: (i, 0)),
        resident(1, D_MODEL),
        resident(D_MODEL, 2 * D_FF),
        resident(3, D_FF),
        resident(D_FF, D_MODEL),
        resident(1, D_MODEL),
    ]
    args = [x, g, w_up, cw, w_down, gf]
    scratch = [pltpu.VMEM((tm, D_FF), BF16)]
    if seq_len is None:
        ga_spec = pl.BlockSpec((None, SUBLANES, D_FF), lambda i: (i, 0, 0))
        ga_shape = jax.ShapeDtypeStruct((ni, SUBLANES, D_FF), F32)
        scratch.append(pltpu.VMEM((SUBLANES, D_FF), F32))
    else:
        in_specs += [pl.BlockSpec((tm, D_FF), lambda i: (i, 0))] * 2
        args += [h1, h2]
        ga_spec = pl.BlockSpec((tm, D_FF), lambda i: (i, 0))
        ga_shape = jax.ShapeDtypeStruct((n, D_FF), F32)
    return pl.pallas_call(
        functools.partial(_ffn_kernel, seq_len=seq_len, final_norm=final_norm, tf=tf),
        grid=(ni,),
        in_specs=in_specs,
        out_specs=[pl.BlockSpec((tm, D_MODEL), lambda i: (i, 0)), ga_spec],
        out_shape=[jax.ShapeDtypeStruct((n, D_MODEL), F32), ga_shape],
        scratch_shapes=scratch,
        compiler_params=_cparams("arbitrary"),
        name="ffn",
    )(*args)


def _rope_tables(offset, t):
    half = QK_ROPE // 2
    per_row = LANES // half
    assert t % per_row == 0
    inv = ROPE_THETA ** (-jnp.arange(half, dtype=F32) / half)
    pos = offset + per_row * _iota((t // per_row, LANES), 0) + _iota((t // per_row, LANES), 1) // half
    ang = pos.astype(F32) * jnp.tile(inv, per_row)[None, :]
    cos, sin = jnp.cos(ang).reshape(t, half), jnp.sin(ang).reshape(t, half)
    z32 = jnp.zeros((t, 32), F32)
    return jnp.concatenate([cos, cos, z32, -sin, sin, z32], -1)


def _rel_bias(table, n_past, n_q, n_k):
    dmax = n_past + n_q - 1
    dmin = n_past - n_k + 1
    diag = table[:, jnp.clip(jnp.arange(dmax, dmin - 1, -1), -REL_MAX, REL_MAX) + REL_MAX]
    return jnp.stack([diag[:, n_q - 1 - i:n_q - 1 - i + n_k] for i in range(n_q)], axis=1)


def _swap_halves(w):
    half = w.shape[-1] // 2
    return jnp.concatenate([w[..., half:], w[..., :half]], -1)


def _layer_weights(lw):
    (g_mix, w_in, a_rel_bias, b_i_bias, b_f_bias, c_conv_w, c_a_log, c_dt_bias,
     d_g_q, d_w_q_up, d_g_kv, d_w_kv_up, g_head, w_out, g_ffn, w_up, f_conv_w, w_down) = lw
    o = 0
    cols = {}
    for name, size in (("a", 3 * GW), ("b", 4 * GW), ("bg", 2 * H), ("c", 3 * GW), ("cz", GW),
                       ("cg", 2 * H), ("dq", Q_LORA), ("dkv", KV_LORA), ("dkr", QK_ROPE)):
        cols[name] = w_in[:, o:o + size]
        o += size
    gates = jnp.concatenate([cols["bg"], cols["cg"]], -1)
    pad16 = jnp.zeros((D_MODEL, 16), F32)
    pad32 = jnp.zeros((D_MODEL, 32), F32)
    w_perm = jnp.concatenate([cols["c"], cols["a"], cols["dq"], cols["dkv"], cols["dkr"], gates, pad16,
                              _swap_halves(cols["dkr"]), pad32, cols["b"], cols["cz"]], -1)
    zc = lambda n: jnp.zeros((1, n), F32)
    zr = lambda n: jnp.zeros((n, 1), F32)
    bias_c = jnp.concatenate([zc(GATE_OFF), b_i_bias[None], b_f_bias[None], zc(LANES - GATE_OFF - 2 * H)], -1)
    bias_r = jnp.concatenate([b_i_bias[:, None], b_f_bias[:, None], zr(2 * H)], 0)
    alog_c = jnp.concatenate([zc(GATE_OFF + 3 * H), c_a_log[None], zc(LANES - GATE_OFF - 4 * H)], -1)
    alog_r = jnp.concatenate([zr(3 * H), c_a_log[:, None]], 0)
    dt_c = jnp.concatenate([zc(GATE_OFF + 3 * H), c_dt_bias[None], zc(LANES - GATE_OFF - 4 * H)], -1)
    dt_r = jnp.concatenate([zr(3 * H), c_dt_bias[:, None]], 0)

    wq = d_w_q_up.reshape(Q_LORA, H, QK_NOPE + QK_ROPE)
    z_h32 = jnp.zeros((Q_LORA, H, 32), F32)
    wq_full = jnp.concatenate([wq, z_h32], -1).reshape(Q_LORA, DPAD)
    wq_part = jnp.concatenate([jnp.zeros((Q_LORA, H, QK_NOPE), F32), _swap_halves(wq[..., QK_NOPE:]), z_h32],
                              -1).reshape(Q_LORA, DPAD)
    wkv = d_w_kv_up.reshape(KV_LORA, H, 2 * DH)
    z_h64 = jnp.zeros((KV_LORA, H, DH), F32)
    wk_full = jnp.concatenate([wkv[..., :DH], z_h64], -1).reshape(KV_LORA, DPAD)
    place = jnp.concatenate([jnp.zeros((QK_ROPE, QK_NOPE), F32), jnp.eye(QK_ROPE, dtype=F32),
                             jnp.zeros((QK_ROPE, 32), F32)], -1)
    pmat = jnp.concatenate([place] * H, -1)
    wk_t = jnp.transpose(wkv[..., :DH], (1, 2, 0))
    rope_rows = jnp.concatenate([jnp.zeros((QK_ROPE, KV_LORA), F32), jnp.eye(QK_ROPE, dtype=F32),
                                 jnp.zeros((QK_ROPE, LAT_W - KV_LORA - QK_ROPE), F32)], -1)
    wabs = jnp.concatenate([
        jnp.concatenate([wk_t, jnp.zeros((H, QK_NOPE, LAT_W - KV_LORA), F32)], -1),
        jnp.broadcast_to(rope_rows, (H, QK_ROPE, LAT_W)),
        jnp.zeros((H, HEAD_PAD - QK_NOPE - QK_ROPE, LAT_W), F32)], 1)
    return dict(
        g_mix=g_mix[None], w_in=w_perm.astype(BF16), w_gt=gates.T.astype(BF16),
        table=a_rel_bias, bias_c=bias_c, bias_r=bias_r, alog_c=alog_c, alog_r=alog_r, dt_c=dt_c, dt_r=dt_r,
        c_conv_w=c_conv_w, g_q=d_g_q[None], g_kv=d_g_kv[None],
        wq=wq_full.astype(BF16), wqp=wq_part.astype(BF16), wk=wk_full.astype(BF16),
        pmat=pmat.astype(BF16), wabs=wabs.astype(BF16),
        wv_heads=jnp.transpose(wkv[..., DH:], (1, 0, 2)).astype(BF16),
        wvt=jnp.concatenate([wkv[..., DH:], jnp.zeros((KV_LORA, H, VT_PAD - DH), F32)], -1)
        .reshape(KV_LORA, VT_ROWS).T.astype(BF16),
        vonest=(jnp.arange(VT_ROWS) % VT_PAD == DH).astype(F32)[:, None],
        g_head=g_head.reshape(4, 1, GW), w_out=w_out.astype(BF16),
        g_ffn=g_ffn[None], w_up=w_up.astype(BF16), f_conv_w=f_conv_w, w_down=w_down.astype(BF16))


def _gates_t3(gt, b, t, l):
    return gt.reshape(N_GATES, b, t // l, l).transpose(1, 2, 0, 3)


def _layer(x, offset, st, w, gf, final_norm, cfg):
    b, t, _ = x.shape
    n = b * t
    first = st is None
    x2 = x.reshape(n, D_MODEL)
    proj, gt = _inproj(x2, w["g_mix"], w["w_in"], w["w_gt"], cfg["tm"])
    proj3 = proj.reshape(b, t, PROJ_W)
    gh = w["g_head"]
    l = min(t, CHUNK)
    gt3 = _gates_t3(gt, b, t, l)

    new_ak = proj3[:, t - min(A_PAST, t):, COL_A + GW:COL_A + 2 * GW].reshape(b, -1, H, DH)
    new_av = proj3[:, t - min(A_PAST, t):, COL_A + 2 * GW:COL_A + 3 * GW].reshape(b, -1, H, DH)
    if first:
        bias = _rel_bias(w["table"], A_PAST, CHUNK, A_PAST + CHUNK)
        oa = _band_prompt(proj, bias, gh[0].T)
    else:
        npast = st[0].shape[1]
        bias = _rel_bias(w["table"], npast, t, npast + t)
        oa = _band_sample(proj3, st[0].reshape(b, npast, GW), st[1].reshape(b, npast, GW), bias, gh[0])
        oa = oa.reshape(n, GW)

    if first:
        c0 = jnp.zeros((b, H, DH, DH), F32)
        n0 = jnp.zeros((b, H, DH), F32)
        m0 = jnp.zeros((b, 1, H), F32)
    else:
        c0, n0, m0 = st[2], st[3], st[4][:, None, :]
    if l == CHUNK:
        ob, cbd, nrow, mrow = _mlstm64(proj3, gt3, w["bias_c"], w["bias_r"], gh[1], _to_block_diag(c0),
                                       n0.reshape(b, 1, GW), jnp.repeat(m0, DH, axis=-1), cfg["nck"])
        new_bc, new_bn, new_bm = _from_block_diag(cbd), nrow.reshape(b, H, DH), mrow[:, 0, ::DH]
    else:
        ob, new_bc, new_bn, new_bm = _mlstm(proj3, gt3, w["bias_c"], w["bias_r"], gh[1], c0, n0, m0,
                                            l, cfg["nck"])
        new_bm = new_bm[:, 0, :]

    if first:
        hist8 = jnp.zeros((b, SUBLANES, 3 * GW), F32)
        s0 = jnp.zeros((b, H, DH, DH), F32)
    else:
        hist8 = jnp.concatenate([jnp.zeros((b, SUBLANES - 3, 3 * GW), F32), st[6]], 1)
        s0 = st[5]
    gdn_args = (proj3, gt3, hist8, w["c_conv_w"], w["alog_c"], w["alog_r"], w["dt_c"], w["dt_r"], gh[2])
    if l == CHUNK:
        oc, sbd = _gdn64(*gdn_args, _to_block_diag(s0), cfg["nck"])
        new_cs = _from_block_diag(sbd)
    else:
        oc, new_cs = _gdn(*gdn_args, s0, l, cfg["nck"])
    new_cconv = proj3[:, t - 3:, COL_CX:COL_CX + 3 * GW]

    ka = jnp.tile(_rope_tables(offset, t), (b, 1))
    ckv, kpe, qc = _dprep(proj, w["g_q"], w["g_kv"], w["wq"], w["wqp"], ka, cfg["tm_d"])
    if first:
        kc, vt = _kvup_t(ckv, kpe, w["wk"], w["wvt"], w["pmat"], w["vonest"], cfg["tm_kv"])
        od = _mla_prompt(qc, kc, vt, gh[3].T, cfg["mla_bq"], cfg["mla_bk"])
    else:
        od = _mla_sample(qc.reshape(b, t, DPAD), st[7], st[8], ckv.reshape(b, t, KV_LORA),
                         kpe.reshape(b, t, QK_ROPE), w["wabs"], w["wv_heads"], gh[3])
        od = od.reshape(n, GW)

    x2 = _outproj(x2, oa, ob.reshape(n, GW), oc.reshape(n, GW), od, w["w_out"], cfg["tm"])

    if first:
        y, ga_tail = _ffn(x2, w["g_ffn"], w["w_up"], w["f_conv_w"], w["w_down"], gf, None, None,
                          tm=cfg["tm"], tf=cfg["tf"], seq_len=None, final_norm=final_norm)
        new_fconv = ga_tail[-1, SUBLANES - 2:, :][None]
    else:
        hist = st[9]
        zrow = jnp.zeros((b, t - 1, D_FF), F32)
        h1 = jnp.concatenate([hist[:, 1:2], zrow], 1).reshape(n, D_FF)
        h2 = jnp.concatenate([hist, zrow[:, 1:]], 1).reshape(n, D_FF)
        y, ga = _ffn(x2, w["g_ffn"], w["w_up"], w["f_conv_w"], w["w_down"], gf, h1, h2,
                     tm=cfg["tm"], tf=cfg["tf"], seq_len=t, final_norm=final_norm)
        new_fconv = ga.reshape(b, t, D_FF)[:, t - 2:]
    state = (new_ak, new_av, new_bc, new_bn, new_bm, new_cs, new_cconv,
             ckv.reshape(b, t, KV_LORA), kpe.reshape(b, t, QK_ROPE), new_fconv)
    return y.reshape(b, t, D_MODEL), state


def _config(b, t):
    n = b * t
    tm = min(n, 1024)
    return dict(tm=tm, tf=256, nck=1 if t <= CHUNK else 8,
                tm_d=min(n, 1024), tm_kv=min(n, 2048), mla_bq=min(t, 512), mla_bk=min(t, 1024))


def kernel(x_prompt, x_sample, cache_a_k, cache_a_v, state_b_c, state_b_n, state_b_m, state_c_s, cache_c_conv, cache_d_ckv, cache_d_kpe, cache_ffn_conv, g_mix, w_in, a_rel_bias, b_i_bias, b_f_bias, c_conv_w, c_a_log, c_dt_bias, d_g_q, d_w_q_up, d_g_kv, d_w_kv_up, g_head, w_out, g_ffn, w_up, f_conv_w, w_down, g_final):
    layer_w = (g_mix, w_in, a_rel_bias, b_i_bias, b_f_bias, c_conv_w, c_a_log, c_dt_bias,
               d_g_q, d_w_q_up, d_g_kv, d_w_kv_up, g_head, w_out, g_ffn, w_up, f_conv_w, w_down)
    depth = g_mix.shape[0]
    past = cache_d_ckv.shape[2]
    xp, xs = x_prompt, x_sample
    cfg_p = _config(*x_prompt.shape[:2])
    cfg_s = _config(*x_sample.shape[:2])
    gf = g_final[None]
    new_p, new_s = [], []
    for l in range(depth):
        w = _layer_weights(tuple(a[l] for a in layer_w))
        last = l == depth - 1
        xp, sp_l = _layer(xp, 0, None, w, gf, last, cfg_p)
        st = (cache_a_k[l], cache_a_v[l], state_b_c[l], state_b_n[l], state_b_m[l],
              state_c_s[l], cache_c_conv[l], cache_d_ckv[l], cache_d_kpe[l], cache_ffn_conv[l])
        xs, ss_l = _layer(xs, past, st, w, gf, last, cfg_s)
        new_p.append(sp_l)
        new_s.append(ss_l)
    outs = [xp, xs]
    for i in range(10):
        outs.append(jnp.stack([s[i] for s in new_p]))
        outs.append(jnp.stack([s[i] for s in new_s]))
    return tuple(outs)
```

```python
import functools
import math

import jax
import jax.numpy as jnp
from jax import lax
from jax.experimental import pallas as pl
from jax.experimental.pallas import tpu as pltpu

F32 = jnp.float32
BF16 = jnp.bfloat16

D_MODEL = 1024
CHUNK = 64
H = 4
DH = 64
GW = H * DH
A_PAST = 8 * CHUNK
REL_MAX = 2 * CHUNK
Q_LORA = 256
KV_LORA = 128
QK_NOPE = 64
QK_ROPE = 32
ROPE_THETA = 10000.0
MLA_SCALE = (QK_NOPE + QK_ROPE) ** -0.5
D_FF = 2816
EPS = 1e-6

COL_CX = 0
COL_A = 3 * GW
COL_TAIL = 6 * GW
TAIL_W = 512
COL_B = COL_TAIL + TAIL_W
COL_CZ = COL_B + 4 * GW
PROJ_W = COL_CZ + GW
GATE_BLK = (COL_TAIL + 384) // 128
GATE_OFF = 32
N_GATES = 16

LANES = 128
SUBLANES = 8
VMEM_LIMIT = 56 * 1024 * 1024

NEG_INF = float("-inf")


def _cparams(*sem):
    return pltpu.CompilerParams(dimension_semantics=sem, vmem_limit_bytes=VMEM_LIMIT)


def _dot(a, b):
    return jnp.dot(a, b, preferred_element_type=F32)


def _dot_nt(a, b):
    return lax.dot_general(a, b, (((1,), (1,)), ((), ())), preferred_element_type=F32)


def _dot_tn(a, b):
    return lax.dot_general(a, b, (((0,), (0,)), ((), ())), preferred_element_type=F32)


def _split3(x):
    hi = x.astype(BF16)
    r1 = x - hi.astype(F32)
    mid = r1.astype(BF16)
    lo = (r1 - mid.astype(F32)).astype(BF16)
    return hi, mid, lo


def _rms(x, g):
    return x * lax.rsqrt(jnp.mean(x * x, axis=-1, keepdims=True) + EPS) * g


def _log_sigmoid(x):
    return jnp.minimum(x, 0.0) - jnp.log1p(jnp.exp(-jnp.abs(x)))


def _softplus(x):
    return jnp.maximum(x, 0.0) + jnp.log1p(jnp.exp(-jnp.abs(x)))


def _sigmoid(x):
    return 1.0 / (1.0 + jnp.exp(-x))


def _iota(shape, dim):
    return lax.broadcasted_iota(jnp.int32, shape, dim)


def _inproj_kernel(x_ref, g_ref, w_ref, wgt_ref, proj_ref, gt_ref):
    h = _rms(x_ref[...], g_ref[...]).astype(BF16)
    gt_ref[...] = _dot_nt(wgt_ref[...], h)
    proj_ref[...] = _dot(h, w_ref[...])


def _inproj(x, g, w, wgt, tm):
    n = x.shape[0]
    resident = lambda a, b: pl.BlockSpec((a, b), lambda i: (0, 0), pipeline_mode=pl.Buffered(1))
    return pl.pallas_call(
        _inproj_kernel,
        grid=(n // tm,),
        in_specs=[
            pl.BlockSpec((tm, D_MODEL), lambda i: (i, 0)),
            resident(1, D_MODEL),
            resident(D_MODEL, PROJ_W),
            resident(N_GATES, D_MODEL),
        ],
        out_specs=[
            pl.BlockSpec((tm, PROJ_W), lambda i: (i, 0)),
            pl.BlockSpec((N_GATES, tm), lambda i: (0, i)),
        ],
        out_shape=[
            jax.ShapeDtypeStruct((n, PROJ_W), F32),
            jax.ShapeDtypeStruct((N_GATES, n), F32),
        ],
        compiler_params=_cparams("parallel"),
        name="inproj",
    )(x, g, w, wgt)


def _head_norm_store(out_ref, rows, h, o, gh_ref):
    g = gh_ref[:, h * DH:(h + 1) * DH]
    y = o * lax.rsqrt(jnp.mean(o * o, axis=-1, keepdims=True) + EPS) * g
    out_ref[rows, h * DH:(h + 1) * DH] = y.astype(out_ref.dtype)


def _band_prompt_kernel(q_ref, kp_ref, kc_ref, vp_ref, vc_ref, bt2_ref, ghc_ref, out_ref, biast_ref, *, qb):
    nk = 2 * qb
    one_lane = (_iota((nk, DH), 1) == 0).astype(BF16)

    @pl.when(pl.program_id(0) == 0)
    def _():
        band = bt2_ref.shape[1]
        left = _iota((nk, LANES), 1) < CHUNK
        def placed(bt, top):
            ninf = lambda n: [jnp.full((n, LANES), NEG_INF, F32)] if n else []
            return jnp.concatenate(ninf(top) + [bt] + ninf(nk - band - top), axis=0)

        for h in range(H):
            bt = bt2_ref[h]
            shifted = [placed(bt, c * CHUNK) for c in range(qb // CHUNK)]
            for t2 in range(qb // LANES):
                biast_ref[h, :, t2 * LANES:(t2 + 1) * LANES] = jnp.where(left, shifted[2 * t2], shifted[2 * t2 + 1])

    def run(first):
        def scores(h):
            hs = slice(h * DH, (h + 1) * DH)
            kcat = jnp.concatenate([kp_ref[:, hs], kc_ref[:, hs]], axis=0).astype(BF16)
            return _dot_nt(kcat, (q_ref[:, hs] * (DH ** -0.5)).astype(BF16))

        def attend(h, st):
            hs = slice(h * DH, (h + 1) * DH)
            st = st + biast_ref[h]
            if first:
                st = jnp.where(_iota((nk, qb), 0) >= qb, st, NEG_INF)
            p = jnp.exp(st - jnp.max(st, axis=0, keepdims=True)).astype(BF16)
            vcat = jnp.concatenate([vp_ref[:, hs], vc_ref[:, hs]], axis=0).astype(BF16)
            acc = _dot_tn(jnp.concatenate([vcat, one_lane], axis=1), p)
            o = acc[:DH] / acc[DH:DH + 1]
            ms = jnp.mean(o * o, axis=0, keepdims=True)
            return o * lax.rsqrt(ms + EPS) * ghc_ref[hs, :]

        ys = []
        pending = scores(0)
        for h in range(H):
            nxt = scores(h + 1) if h + 1 < H else None
            ys.append(attend(h, pending))
            pending = nxt
        out_ref[...] = jnp.concatenate(ys, axis=0).T.astype(out_ref.dtype)

    @pl.when(pl.program_id(0) == 0)
    def _():
        run(True)

    @pl.when(pl.program_id(0) > 0)
    def _():
        run(False)


def _band_prompt(proj, bias, ghc, qb=A_PAST):
    t = proj.shape[0]
    assert qb == A_PAST and t % qb == 0
    prev = lambda i: jnp.maximum(i - 1, 0)
    cq = COL_A // GW
    band = bias.shape[-1]
    bias_t = bias.transpose(0, 2, 1)
    bt2 = jnp.concatenate([bias_t, bias_t], axis=-1)
    return pl.pallas_call(
        functools.partial(_band_prompt_kernel, qb=qb),
        grid=(t // qb,),
        in_specs=[
            pl.BlockSpec((qb, GW), lambda i: (i, cq)),
            pl.BlockSpec((qb, GW), lambda i: (prev(i), cq + 1)),
            pl.BlockSpec((qb, GW), lambda i: (i, cq + 1)),
            pl.BlockSpec((qb, GW), lambda i: (prev(i), cq + 2)),
            pl.BlockSpec((qb, GW), lambda i: (i, cq + 2)),
            pl.BlockSpec((H, band, LANES), lambda i: (0, 0, 0)),
            pl.BlockSpec((GW, 1), lambda i: (0, 0)),
        ],
        out_specs=pl.BlockSpec((qb, GW), lambda i: (i, 0)),
        out_shape=jax.ShapeDtypeStruct((t, GW), BF16),
        scratch_shapes=[pltpu.VMEM((H, 2 * qb, qb), F32)],
        compiler_params=_cparams("arbitrary"),
        name="band_prompt",
    )(proj, proj, proj, proj, proj, bt2, ghc)


def _band_sample_kernel(q_ref, k_ref, v_ref, ck_ref, cv_ref, bias_ref, gh_ref, out_ref):
    npast = ck_ref.shape[0]
    rows = slice(0, q_ref.shape[0])
    for h in range(H):
        hs = slice(h * DH, (h + 1) * DH)
        q = q_ref[:, hs].astype(BF16)
        s1 = _dot_nt(q, ck_ref[:, hs].astype(BF16)) * (DH ** -0.5) + bias_ref[h, :, :npast]
        s2 = _dot_nt(q, k_ref[:, hs].astype(BF16)) * (DH ** -0.5) + bias_ref[h, :, npast:]
        m = jnp.maximum(jnp.max(s1, axis=-1, keepdims=True), jnp.max(s2, axis=-1, keepdims=True))
        p1 = jnp.exp(s1 - m)
        p2 = jnp.exp(s2 - m)
        l = jnp.sum(p1, axis=-1, keepdims=True) + jnp.sum(p2, axis=-1, keepdims=True)
        o = (_dot(p1.astype(BF16), cv_ref[:, hs].astype(BF16))
             + _dot(p2.astype(BF16), v_ref[:, hs].astype(BF16))) / l
        _head_norm_store(out_ref, rows, h, o, gh_ref)


def _band_sample(proj3, ck, cv, bias, gh):
    b, s, _ = proj3.shape
    npast = ck.shape[1]
    return pl.pallas_call(
        _band_sample_kernel,
        grid=(b,),
        in_specs=[
            pl.BlockSpec((None, s, GW), lambda i: (i, 0, COL_A // GW)),
            pl.BlockSpec((None, s, GW), lambda i: (i, 0, COL_A // GW + 1)),
            pl.BlockSpec((None, s, GW), lambda i: (i, 0, COL_A // GW + 2)),
            pl.BlockSpec((None, npast, GW), lambda i: (i, 0, 0)),
            pl.BlockSpec((None, npast, GW), lambda i: (i, 0, 0)),
            pl.BlockSpec((H, s, npast + s), lambda i: (0, 0, 0)),
            pl.BlockSpec((1, GW), lambda i: (0, 0)),
        ],
        out_specs=pl.BlockSpec((None, s, GW), lambda i: (i, 0, 0)),
        out_shape=jax.ShapeDtypeStruct((b, s, GW), BF16),
        compiler_params=_cparams("parallel"),
        name="band_sample",
    )(proj3, proj3, proj3, ck, cv, bias, gh)


def _cumsum_cols(x, lo_tri):
    return sum(_dot(lo_tri, part) for part in _split3(x))


def _cumsum_rows(x, up_tri):
    return sum(_dot(part, up_tri) for part in _split3(x))


def _tri_masks(l):
    r = _iota((l, l), 0)
    c = _iota((l, l), 1)
    return r >= c, r > c


def _mlstm_kernel(q_ref, k_ref, v_ref, o_ref, gc_ref, gr_ref, bc_ref, br_ref, gh_ref,
                  c0_ref, n0_ref, m0_ref, out_ref, c_ref, n_ref, m_ref, *, l, nck):
    @pl.when(pl.program_id(1) == 0)
    def _():
        c_ref[...] = c0_ref[...]
        n_ref[...] = n0_ref[...]
        m_ref[...] = m0_ref[...]

    incl, _ = _tri_masks(l)
    lo_tri = incl.astype(BF16)
    up_tri = (_iota((l, l), 0) <= _iota((l, l), 1)).astype(BF16)

    probs = []
    for ck in range(nck):
        rows = slice(ck * l, (ck + 1) * l)
        gcol = gc_ref[rows, :] + bc_ref[...]
        grow = gr_ref[ck] + br_ref[...]
        gcs = _cumsum_cols(_log_sigmoid(gcol), lo_tri)
        grs = _cumsum_rows(_log_sigmoid(grow), up_tri)
        for h in range(H):
            hs = slice(h * DH, (h + 1) * DH)
            ig_c = gcol[:, GATE_OFF + h:GATE_OFF + h + 1]
            g_c = gcs[:, GATE_OFF + H + h:GATE_OFF + H + h + 1]
            ig_r = grow[h:h + 1, :]
            g_r = grs[H + h:H + h + 1, :]
            q = q_ref[rows, hs]
            kf = k_ref[rows, hs] * (DH ** -0.5)
            lmat = jnp.where(incl, g_c - g_r + ig_r, NEG_INF)
            probs.append(dict(
                rows=rows, h=h, q=q, kf=kf, qb=q.astype(BF16), kb=kf.astype(BF16),
                vb=v_ref[rows, hs].astype(BF16), lmat=lmat, lmax=jnp.max(lmat, axis=-1, keepdims=True),
                g_c=g_c, ig_c=ig_c, g_last=g_c[l - 1:l, :]))
    qks = [_dot_nt(p["qb"], p["kb"]) for p in probs]

    ms = [m_ref[:, h:h + 1] for h in range(H)]
    for p in probs:
        m_old = ms[p["h"]]
        p["linter"] = p["g_c"] + m_old
        p["mt"] = jnp.maximum(p["linter"], p["lmax"])
        m_new = p["mt"][l - 1:l, :]
        p["dprev"] = jnp.exp(p["g_last"] + m_old - m_new)
        p["kw"] = p["kf"] * jnp.exp(p["g_last"] - p["g_c"] + p["ig_c"] - m_new)
        ms[p["h"]] = m_new
    ws_ = [qk * jnp.exp(p["lmat"] - p["mt"]) for p, qk in zip(probs, qks)]
    wvs = [_dot(w.astype(BF16), p["vb"]) for p, w in zip(probs, ws_)]
    upds = [_dot_tn(p["kw"].astype(BF16), p["vb"]) for p in probs]

    cs = [c_ref[h] for h in range(H)]
    ns = [n_ref[h:h + 1, :] for h in range(H)]
    qcs, qns = [], []
    for p, upd in zip(probs, upds):
        h = p["h"]
        qcs.append(_dot(p["qb"], cs[h].astype(BF16)))
        qns.append(jnp.sum(p["q"] * ns[h], axis=-1, keepdims=True))
        cs[h] = p["dprev"] * cs[h] + upd
        ns[h] = p["dprev"] * ns[h] + jnp.sum(p["kw"], axis=0, keepdims=True)
    for h in range(H):
        c_ref[h] = cs[h]
        n_ref[h:h + 1, :] = ns[h]
        m_ref[:, h:h + 1] = ms[h]

    wsums = [jnp.sum(w, axis=-1, keepdims=True) for w in ws_]
    obs = []
    for p, wsum, wv, qc, qn in zip(probs, wsums, wvs, qcs, qns):
        hs = slice(p["h"] * DH, (p["h"] + 1) * DH)
        inter = jnp.exp(p["linter"] - p["mt"])
        den = wsum + inter * qn
        hout = (wv + inter * qc) / jnp.maximum(jnp.abs(den), jnp.exp(-p["mt"]))
        obs.append(hout * _sigmoid(o_ref[p["rows"], hs]))
    msq = [jnp.mean(ob * ob, axis=-1, keepdims=True) for ob in obs]
    for p, ob, ms_ in zip(probs, obs, msq):
        hs = slice(p["h"] * DH, (p["h"] + 1) * DH)
        out_ref[p["rows"], hs] = (ob * lax.rsqrt(ms_ + EPS) * gh_ref[:, hs]).astype(out_ref.dtype)


def _mlstm(proj3, gt3, bias_c, bias_r, gh, c0, n0, m0, l, nck):
    b, t, _ = proj3.shape
    steps = t // (l * nck)
    blk = l * nck
    col = lambda j: pl.BlockSpec((None, blk, GW), lambda bi, s: (bi, s, j))
    state = lambda shp: pl.BlockSpec((None,) + shp, lambda bi, s: (bi,) + (0,) * len(shp))
    return pl.pallas_call(
        functools.partial(_mlstm_kernel, l=l, nck=nck),
        grid=(b, steps),
        in_specs=[
            col(COL_B // GW), col(COL_B // GW + 1), col(COL_B // GW + 2), col(COL_B // GW + 3),
            pl.BlockSpec((None, blk, LANES), lambda bi, s: (bi, s, GATE_BLK)),
            pl.BlockSpec((None, nck, N_GATES, l), lambda bi, s: (bi, s, 0, 0)),
            pl.BlockSpec((1, LANES), lambda bi, s: (0, 0)),
            pl.BlockSpec((N_GATES, 1), lambda bi, s: (0, 0)),
            pl.BlockSpec((1, GW), lambda bi, s: (0, 0)),
            state((H, DH, DH)), state((H, DH)), state((1, H)),
        ],
        out_specs=[
            pl.BlockSpec((None, blk, GW), lambda bi, s: (bi, s, 0)),
            state((H, DH, DH)), state((H, DH)), state((1, H)),
        ],
        out_shape=[
            jax.ShapeDtypeStruct((b, t, GW), BF16),
            jax.ShapeDtypeStruct((b, H, DH, DH), F32),
            jax.ShapeDtypeStruct((b, H, DH), F32),
            jax.ShapeDtypeStruct((b, 1, H), F32),
        ],
        compiler_params=_cparams("parallel", "arbitrary"),
        name="mlstm",
    )(proj3, proj3, proj3, proj3, proj3, gt3, bias_c, bias_r, gh, c0, n0, m0)


def _head_of(idx):
    return idx // DH


def _block_mask(n_rows, n_cols):
    return _head_of(_iota((n_rows, n_cols), 0)) == _head_of(_iota((n_rows, n_cols), 1))


def _expander(first_lane):
    r = _iota((LANES, GW), 0)
    c = _iota((LANES, GW), 1)
    return (r == first_lane + _head_of(c)).astype(BF16)


def _dot_stacked(parts, rhs):
    m = parts[0].shape[0]
    y = _dot(jnp.concatenate(parts, axis=0), rhs)
    return sum(y[i * m:(i + 1) * m] for i in range(len(parts)))


def _expand(x, e):
    return _dot_stacked(_split3(x), e)


def _head_sums(a, bones):
    return _dot_stacked(_split3(a), bones)


def _cumsum_cols_wide(x, lo_tri):
    w = x.shape[1]
    y = _dot(lo_tri, jnp.concatenate(_split3(x), axis=1))
    return y[:, :w] + y[:, w:2 * w] + y[:, 2 * w:]


def _row_select(x_t, first_row):
    r = _iota(x_t.shape, 0)
    c = _iota(x_t.shape, 1)
    return jnp.sum(jnp.where(r == first_row + _head_of(c), x_t, 0.0), axis=0, keepdims=True)


def _block_diag_rows(x, mask):
    return jnp.where(mask, jnp.concatenate([x] * H, axis=0), jnp.zeros((), x.dtype))


def _cummax_rows(x):
    rows = _iota(x.shape, 0)
    sh = 1
    while sh < x.shape[0]:
        x = jnp.maximum(x, jnp.where(rows >= sh, pltpu.roll(x, sh, axis=0), NEG_INF))
        sh *= 2
    return x


def _mlstm64_kernel(q_ref, k_ref, v_ref, o_ref, gc_ref, gr_ref, bc_ref, br_ref, gh_ref,
                    c0_ref, n0_ref, m0_ref, out_ref, c_ref, n_ref, m_ref, *, nck):
    l = CHUNK

    @pl.when(pl.program_id(1) == 0)
    def _():
        c_ref[...] = c0_ref[...]
        n_ref[...] = n0_ref[...]
        m_ref[...] = m0_ref[...]

    bmask = _block_mask(GW, GW)
    bones = bmask.astype(BF16)
    within = _iota((GW, GW), 0) % DH <= _iota((GW, GW), 1) % DH
    up_bd = jnp.logical_and(bmask, within).astype(BF16)
    lo_tri = (_iota((l, l), 0) >= _iota((l, l), 1)).astype(BF16)
    incl = _iota((l, GW), 0) >= _iota((l, GW), 1) % DH
    e_i = _expander(GATE_OFF)
    e_f = _expander(GATE_OFF + H)

    cks = []
    for ck in range(nck):
        rows = slice(ck * l, (ck + 1) * l)
        gcol = gc_ref[rows, :] + bc_ref[...]
        gcs = _cumsum_cols_wide(_log_sigmoid(gcol), lo_tri)
        g_c = _expand(gcs, e_f)
        i_c = _expand(gcol, e_i)
        grow = gr_ref[ck] + br_ref[...]
        grow_t = jnp.concatenate([grow] * H, axis=1)
        grs_t = _dot_stacked(_split3(_log_sigmoid(grow_t)), up_bd)
        a_r = _row_select(grow_t, 0) - _row_select(grs_t, H)
        lmat = jnp.where(incl, g_c + a_r, NEG_INF)
        lmax = g_c + _cummax_rows(i_c - g_c)
        q = q_ref[rows, :]
        kf = k_ref[rows, :] * (DH ** -0.5)
        cks.append(dict(rows=rows, g_c=g_c, i_c=i_c, lmat=lmat, lmax=lmax, q=q, kf=kf,
                        qb=q.astype(BF16), kb=kf.astype(BF16), vb=v_ref[rows, :].astype(BF16),
                        g_last=g_c[l - 1:l, :]))
    scs = [_dot_nt(p["qb"], _block_diag_rows(p["kb"], bmask)) for p in cks]

    m_run = m_ref[...]
    for p in cks:
        p["linter"] = p["g_c"] + m_run
        p["mt"] = jnp.maximum(p["linter"], p["lmax"])
        m_new = p["mt"][l - 1:l, :]
        p["dprev"] = jnp.exp(p["g_last"] + m_run - m_new)
        p["kw"] = p["kf"] * jnp.exp(p["g_last"] - p["g_c"] + p["i_c"] - m_new)
        m_run = m_new
    m_ref[...] = m_run
    wbs = [(s * jnp.exp(p["lmat"] - p["mt"])).astype(BF16) for p, s in zip(cks, scs)]
    nums = [_dot(w, _block_diag_rows(p["vb"], bmask)) for p, w in zip(cks, wbs)]
    wsums = [_dot(w, bones) for w in wbs]
    upds = [jnp.where(bmask, _dot_tn(p["kw"].astype(BF16), p["vb"]), 0.0) for p in cks]

    c_run = c_ref[...]
    n_run = n_ref[...]
    qcs, qns = [], []
    for p, upd in zip(cks, upds):
        qcs.append(_dot(p["qb"], c_run.astype(BF16)))
        qns.append(_dot((p["q"] * n_run).astype(BF16), bones))
        c_run = p["dprev"] * c_run + upd
        n_run = p["dprev"] * n_run + jnp.sum(p["kw"], axis=0, keepdims=True)
    c_ref[...] = c_run
    n_ref[...] = n_run

    obs = []
    for p, num, wsum, qc, qn in zip(cks, nums, wsums, qcs, qns):
        inter = jnp.exp(p["linter"] - p["mt"])
        den = wsum + inter * qn
        hout = (num + inter * qc) / jnp.maximum(jnp.abs(den), jnp.exp(-p["mt"]))
        obs.append(hout * _sigmoid(o_ref[p["rows"], :]))
    msq = [_head_sums(ob * ob, bones) * (1.0 / DH) for ob in obs]
    for p, ob, m2 in zip(cks, obs, msq):
        out_ref[p["rows"], :] = (ob * lax.rsqrt(m2 + EPS) * gh_ref[...]).astype(out_ref.dtype)


def _mlstm64(proj3, gt3, bias_c, bias_r, gh, c0, n0, m0, nck):
    b, t, _ = proj3.shape
    blk = CHUNK * nck
    steps = t // blk
    col = lambda j: pl.BlockSpec((None, blk, GW), lambda bi, s: (bi, s, j))
    state = lambda shp: pl.BlockSpec((None,) + shp, lambda bi, s: (bi,) + (0,) * len(shp))
    return pl.pallas_call(
        functools.partial(_mlstm64_kernel, nck=nck),
        grid=(b, steps),
        in_specs=[
            col(COL_B // GW), col(COL_B // GW + 1), col(COL_B // GW + 2), col(COL_B // GW + 3),
            pl.BlockSpec((None, blk, LANES), lambda bi, s: (bi, s, GATE_BLK)),
            pl.BlockSpec((None, nck, N_GATES, CHUNK), lambda bi, s: (bi, s, 0, 0)),
            pl.BlockSpec((1, LANES), lambda bi, s: (0, 0)),
            pl.BlockSpec((N_GATES, 1), lambda bi, s: (0, 0)),
            pl.BlockSpec((1, GW), lambda bi, s: (0, 0)),
            state((GW, GW)), state((1, GW)), state((1, GW)),
        ],
        out_specs=[
            pl.BlockSpec((None, blk, GW), lambda bi, s: (bi, s, 0)),
            state((GW, GW)), state((1, GW)), state((1, GW)),
        ],
        out_shape=[
            jax.ShapeDtypeStruct((b, t, GW), BF16),
            jax.ShapeDtypeStruct((b, GW, GW), F32),
            jax.ShapeDtypeStruct((b, 1, GW), F32),
            jax.ShapeDtypeStruct((b, 1, GW), F32),
        ],
        compiler_params=_cparams("parallel", "arbitrary"),
        name="mlstm64",
    )(proj3, proj3, proj3, proj3, proj3, gt3, bias_c, bias_r, gh, c0, n0, m0)


def _to_block_diag(c):
    b = c.shape[0]
    eye = jnp.eye(H, dtype=c.dtype)
    return jnp.einsum("bhde,hg->bhdge", c, eye).reshape(b, GW, GW)


def _from_block_diag(cbd):
    b = cbd.shape[0]
    c5 = cbd.reshape(b, H, DH, H, DH)
    return jnp.stack([c5[:, h, :, h, :] for h in range(H)], axis=1)


def _split2(x):
    hi = x.astype(BF16)
    lo = (x - hi.astype(F32)).astype(BF16)
    return hi, lo


def _dot_sp(a, b):
    return _dot(a[0], b[0]) + (_dot(a[0], b[1]) + _dot(a[1], b[0]))


def _unit_lower_inverses(nmats, l):
    eye = (_iota((l, l), 0) == _iota((l, l), 1)).astype(F32)
    ps = [eye - n for n in nmats]
    qs = [_split2(n) for n in nmats]
    qs = [_split2(_dot_sp(q, q)) for q in qs]
    power = 2
    while power < l:
        ps = [p + _dot_sp(_split2(p), q) for p, q in zip(ps, qs)]
        power *= 2
        if power < l:
            qs = [_split2(_dot_sp(q, q)) for q in qs]
    return ps


def _l2norm(x):
    return x * lax.rsqrt(jnp.sum(x * x, axis=-1, keepdims=True) + 1e-6)


def _gdn_kernel(x_ref, z_ref, gc_ref, gr_ref, hist_ref, cw_ref, ac_ref, ar_ref, dc_ref, dr_ref,
                gh_ref, s0_ref, out_ref, s_ref, carry_scr, *, l, nck):
    @pl.when(pl.program_id(1) == 0)
    def _():
        s_ref[...] = s0_ref[...]
        carry_scr[...] = hist_ref[...]

    blk = l * nck
    x = x_ref[...]
    ext = jnp.concatenate([carry_scr[...], x], axis=0)
    carry_scr[...] = x[blk - SUBLANES:, :]
    y = x * cw_ref[3:4, :]
    for j in range(1, 4):
        y = y + ext[SUBLANES - j:SUBLANES - j + blk, :] * cw_ref[3 - j:4 - j, :]
    y = y * _sigmoid(y)

    incl, strict = _tri_masks(l)
    lo_tri = incl.astype(BF16)
    up_tri = (_iota((l, l), 0) <= _iota((l, l), 1)).astype(BF16)

    qraw = [y[ck * l:(ck + 1) * l, h * DH:(h + 1) * DH] for ck in range(nck) for h in range(H)]
    kraw = [y[ck * l:(ck + 1) * l, GW + h * DH:GW + (h + 1) * DH] for ck in range(nck) for h in range(H)]
    vraw = [y[ck * l:(ck + 1) * l, 2 * GW + h * DH:2 * GW + (h + 1) * DH] for ck in range(nck) for h in range(H)]
    qnorm = [_l2norm(a) * (DH ** -0.5) for a in qraw]
    knorm = [_l2norm(a) for a in kraw]
    probs = []
    for ck in range(nck):
        rows = slice(ck * l, (ck + 1) * l)
        gcol = gc_ref[rows, :]
        grow = gr_ref[ck]
        beta_cs = _sigmoid(gcol)
        dec_c = -jnp.exp(ac_ref[...]) * _softplus(gcol + dc_ref[...])
        dec_r = -jnp.exp(ar_ref[...]) * _softplus(grow + dr_ref[...])
        gcs = _cumsum_cols(dec_c, lo_tri)
        grs = _cumsum_rows(dec_r, up_tri)
        for h in range(H):
            beta = beta_cs[:, GATE_OFF + 2 * H + h:GATE_OFF + 2 * H + h + 1]
            g_c = gcs[:, GATE_OFF + 3 * H + h:GATE_OFF + 3 * H + h + 1]
            g_r = grs[3 * H + h:3 * H + h + 1, :]
            q, k, v = qnorm[ck * H + h], knorm[ck * H + h], vraw[ck * H + h]
            decay = jnp.exp(jnp.where(incl, g_c - g_r, NEG_INF))
            eg = jnp.exp(g_c)
            g_last = g_c[l - 1:l, :]
            probs.append(dict(
                rows=rows, h=h, qb=q.astype(BF16), kb=k.astype(BF16), beta=beta, decay=decay,
                rhs=jnp.concatenate([v * beta, k * (beta * eg)], axis=-1),
                qeg=(q * eg).astype(BF16), kdec=(k * jnp.exp(g_last - g_c)).astype(BF16),
                sdec=jnp.exp(g_last)))
    kks = [_dot_nt(p["kb"], p["kb"]) for p in probs]
    qks = [_dot_nt(p["qb"], p["kb"]) for p in probs]
    a_lows = [jnp.where(strict, p["beta"] * kk * p["decay"], 0.0) for p, kk in zip(probs, kks)]
    attns = [(qk * p["decay"]).astype(BF16) for p, qk in zip(probs, qks)]
    tinvs = _unit_lower_inverses(a_lows, l)
    sols = [_dot_sp(_split2(t), _split2(p["rhs"])) for t, p in zip(tinvs, probs)]

    states = [s_ref[h] for h in range(H)]
    for ck in range(nck):
        ps = probs[ck * H:(ck + 1) * H]
        ss = sols[ck * H:(ck + 1) * H]
        at = attns[ck * H:(ck + 1) * H]
        sbs = [s.astype(BF16) for s in states]
        wss = [_dot(sol[:, DH:].astype(BF16), sb) for sol, sb in zip(ss, sbs)]
        qss = [_dot(p["qeg"], sb) for p, sb in zip(ps, sbs)]
        vnbs = [(sol[:, :DH] - ws).astype(BF16) for sol, ws in zip(ss, wss)]
        os_ = [qs + _dot(a, vnb) for qs, a, vnb in zip(qss, at, vnbs)]
        states = [p["sdec"] * s + _dot_tn(p["kdec"], vnb) for p, s, vnb in zip(ps, states, vnbs)]
        for p, o in zip(ps, os_):
            hs = slice(p["h"] * DH, (p["h"] + 1) * DH)
            zg = z_ref[p["rows"], hs]
            yo = (o * lax.rsqrt(jnp.mean(o * o, axis=-1, keepdims=True) + EPS) * gh_ref[:, hs]
                  * (zg * _sigmoid(zg)))
            out_ref[p["rows"], hs] = yo.astype(out_ref.dtype)
    for h in range(H):
        s_ref[h] = states[h]


def _gdn(proj3, gt3, hist8, cw, a_c, a_r, dt_c, dt_r, gh, s0, l, nck):
    b, t, _ = proj3.shape
    blk = l * nck
    steps = t // blk
    state = lambda shp: pl.BlockSpec((None,) + shp, lambda bi, s: (bi,) + (0,) * len(shp))
    const = lambda shp: pl.BlockSpec(shp, lambda bi, s: (0,) * len(shp))
    return pl.pallas_call(
        functools.partial(_gdn_kernel, l=l, nck=nck),
        grid=(b, steps),
        in_specs=[
            pl.BlockSpec((None, blk, 3 * GW), lambda bi, s: (bi, s, COL_CX // (3 * GW))),
            pl.BlockSpec((None, blk, GW), lambda bi, s: (bi, s, COL_CZ // GW)),
            pl.BlockSpec((None, blk, LANES), lambda bi, s: (bi, s, GATE_BLK)),
            pl.BlockSpec((None, nck, N_GATES, l), lambda bi, s: (bi, s, 0, 0)),
            state((SUBLANES, 3 * GW)),
            const((4, 3 * GW)),
            const((1, LANES)), const((N_GATES, 1)), const((1, LANES)), const((N_GATES, 1)),
            const((1, GW)),
            state((H, DH, DH)),
        ],
        out_specs=[
            pl.BlockSpec((None, blk, GW), lambda bi, s: (bi, s, 0)),
            state((H, DH, DH)),
        ],
        out_shape=[
            jax.ShapeDtypeStruct((b, t, GW), BF16),
            jax.ShapeDtypeStruct((b, H, DH, DH), F32),
        ],
        scratch_shapes=[pltpu.VMEM((SUBLANES, 3 * GW), F32)],
        compiler_params=_cparams("parallel", "arbitrary"),
        name="gdn",
    )(proj3, proj3, proj3, gt3, hist8, cw, a_c, a_r, dt_c, dt_r, gh, s0)


def _bd_split(x, mask):
    hi, lo = _split2(x)
    return _block_diag_rows(hi, mask), _block_diag_rows(lo, mask)


def _gdn64_kernel(x_ref, z_ref, gc_ref, gr_ref, hist_ref, cw_ref, ac_ref, ar_ref, dc_ref, dr_ref,
                  gh_ref, s0_ref, out_ref, s_ref, carry_scr, *, nck):
    l = CHUNK

    @pl.when(pl.program_id(1) == 0)
    def _():
        s_ref[...] = s0_ref[...]
        carry_scr[...] = hist_ref[...]

    blk = l * nck
    x = x_ref[...]
    ext = jnp.concatenate([carry_scr[...], x], axis=0)
    carry_scr[...] = x[blk - SUBLANES:, :]
    y = x * cw_ref[3:4, :]
    for j in range(1, 4):
        y = y + ext[SUBLANES - j:SUBLANES - j + blk, :] * cw_ref[3 - j:4 - j, :]
    y = y * _sigmoid(y)

    bmask = _block_mask(GW, GW)
    bones = bmask.astype(BF16)
    within = _iota((GW, GW), 0) % DH <= _iota((GW, GW), 1) % DH
    up_bd = jnp.logical_and(bmask, within).astype(BF16)
    lo_tri = (_iota((l, l), 0) >= _iota((l, l), 1)).astype(BF16)
    key_pos = _iota((l, GW), 1) % DH
    incl = _iota((l, GW), 0) >= key_pos
    strict = _iota((l, GW), 0) > key_pos
    eye_t = (_iota((l, GW), 0) == key_pos).astype(F32)
    e_b = _expander(GATE_OFF + 2 * H)
    e_a = _expander(GATE_OFF + 3 * H)

    def head_sums(a):
        return _head_sums(a, bones)

    def shared_rhs(lhs_splits, rhs_bd):
        n = len(lhs_splits)
        big = _dot(jnp.concatenate([part for sp in lhs_splits for part in sp], axis=0), rhs_bd[0])
        small = _dot(jnp.concatenate([sp[0] for sp in lhs_splits], axis=0), rhs_bd[1])
        return [big[2 * i * l:(2 * i + 1) * l] + big[(2 * i + 1) * l:(2 * i + 2) * l]
                + small[i * l:(i + 1) * l] for i in range(n)]

    yq, yk, yv = y[:, :GW], y[:, GW:2 * GW], y[:, 2 * GW:]
    qn_all = yq * lax.rsqrt(head_sums(yq * yq) + 1e-6) * (DH ** -0.5)
    kn_all = yk * lax.rsqrt(head_sums(yk * yk) + 1e-6)

    cks = []
    for ck in range(nck):
        rows = slice(ck * l, (ck + 1) * l)
        gcol = gc_ref[rows, :]
        dec_c = -jnp.exp(ac_ref[...]) * _softplus(gcol + dc_ref[...])
        beta = _expand(_sigmoid(gcol), e_b)
        g_c = _expand(_cumsum_cols_wide(dec_c, lo_tri), e_a)
        grow_t = jnp.concatenate([gr_ref[ck]] * H, axis=1)
        dec_r = -jnp.exp(ar_ref[...]) * _softplus(grow_t + dr_ref[...])
        g_r = _row_select(_dot_stacked(_split3(dec_r), up_bd), 3 * H)
        decay = jnp.exp(jnp.where(incl, g_c - g_r, NEG_INF))
        eg = jnp.exp(g_c)
        g_last = g_c[l - 1:l, :]
        q, k, v = qn_all[rows], kn_all[rows], yv[rows]
        cks.append(dict(rows=rows, qb=q.astype(BF16), kb=k.astype(BF16),
                        beta=beta, decay=decay, rhs_v=v * beta, rhs_k=k * (beta * eg),
                        qeg=(q * eg).astype(BF16), kdec=(k * jnp.exp(g_last - g_c)).astype(BF16),
                        sdec=jnp.exp(g_last)))
    kqs = [_dot_nt(jnp.concatenate([p["kb"], p["qb"]], axis=0), _block_diag_rows(p["kb"], bmask))
           for p in cks]
    nmats = [jnp.where(strict, p["beta"] * kq[:l] * p["decay"], 0.0) for p, kq in zip(cks, kqs)]
    attns = [(kq[l:] * p["decay"]).astype(BF16) for p, kq in zip(cks, kqs)]

    ps = [eye_t - n for n in nmats]
    qs = [shared_rhs([_split2(n)], _bd_split(n, bmask))[0] for n in nmats]
    power = 2
    while power < l:
        power *= 2
        if power < l:
            res = [shared_rhs([_split2(p), _split2(q)], _bd_split(q, bmask)) for p, q in zip(ps, qs)]
            ps = [p + r[0] for p, r in zip(ps, res)]
            qs = [r[1] for r in res]
        else:
            ps = [p + shared_rhs([_split2(p)], _bd_split(q, bmask))[0] for p, q in zip(ps, qs)]
    tsp = [_split2(p) for p in ps]
    us = [shared_rhs([t], _bd_split(p["rhs_v"], bmask))[0] for t, p in zip(tsp, cks)]
    ws = [shared_rhs([t], _bd_split(p["rhs_k"], bmask))[0].astype(BF16) for t, p in zip(tsp, cks)]

    wu = [jnp.concatenate([w, u.astype(BF16)], axis=1) for w, u in zip(ws, us)]
    kwu = [_dot_tn(p["kdec"], x) for p, x in zip(cks, wu)]
    awu = [_dot(at, jnp.concatenate([_block_diag_rows(x[:, :GW], bmask),
                                     _block_diag_rows(x[:, GW:], bmask)], axis=1))
           for at, x in zip(attns, wu)]
    gmats = [jnp.where(bmask, x[:, :GW], 0.0).astype(BF16) for x in kwu]
    bmats = [jnp.where(bmask, x[:, GW:], 0.0) for x in kwu]
    qts = [(p["qeg"].astype(F32) - x[:, :GW]).astype(BF16) for p, x in zip(cks, awu)]

    s_run = s_ref[...]
    outs = []
    for p, g, bm, qt, x in zip(cks, gmats, bmats, qts, awu):
        ys = _dot(jnp.concatenate([g, qt], axis=0), s_run.astype(BF16))
        outs.append(ys[GW:] + x[:, GW:])
        s_run = p["sdec"] * s_run - ys[:GW] + bm
    s_ref[...] = s_run

    msq = [head_sums(o * o) * (1.0 / DH) for o in outs]
    for p, o, m2 in zip(cks, outs, msq):
        zg = z_ref[p["rows"], :]
        yo = o * lax.rsqrt(m2 + EPS) * gh_ref[...] * (zg * _sigmoid(zg))
        out_ref[p["rows"], :] = yo.astype(out_ref.dtype)


def _gdn64(proj3, gt3, hist8, cw, a_c, a_r, dt_c, dt_r, gh, s0, nck):
    b, t, _ = proj3.shape
    blk = CHUNK * nck
    steps = t // blk
    state = lambda shp: pl.BlockSpec((None,) + shp, lambda bi, s: (bi,) + (0,) * len(shp))
    const = lambda shp: pl.BlockSpec(shp, lambda bi, s: (0,) * len(shp))
    return pl.pallas_call(
        functools.partial(_gdn64_kernel, nck=nck),
        grid=(b, steps),
        in_specs=[
            pl.BlockSpec((None, blk, 3 * GW), lambda bi, s: (bi, s, COL_CX // (3 * GW))),
            pl.BlockSpec((None, blk, GW), lambda bi, s: (bi, s, COL_CZ // GW)),
            pl.BlockSpec((None, blk, LANES), lambda bi, s: (bi, s, GATE_BLK)),
            pl.BlockSpec((None, nck, N_GATES, CHUNK), lambda bi, s: (bi, s, 0, 0)),
            state((SUBLANES, 3 * GW)),
            const((4, 3 * GW)),
            const((1, LANES)), const((N_GATES, 1)), const((1, LANES)), const((N_GATES, 1)),
            const((1, GW)),
            state((GW, GW)),
        ],
        out_specs=[
            pl.BlockSpec((None, blk, GW), lambda bi, s: (bi, s, 0)),
            state((GW, GW)),
        ],
        out_shape=[
            jax.ShapeDtypeStruct((b, t, GW), BF16),
            jax.ShapeDtypeStruct((b, GW, GW), F32),
        ],
        scratch_shapes=[pltpu.VMEM((SUBLANES, 3 * GW), F32)],
        compiler_params=_cparams("parallel", "arbitrary"),
        name="gdn64",
    )(proj3, proj3, proj3, gt3, hist8, cw, a_c, a_r, dt_c, dt_r, gh, s0)


HEAD_PAD = 128
DPAD = H * HEAD_PAD
BF16_ROWS = 16
VT_PAD = -(-(DH + 1) // BF16_ROWS) * BF16_ROWS
VT_ROWS = H * VT_PAD


def _tile_heads(t):
    return jnp.concatenate([t] * H, axis=-1)


def _dprep_kernel(tail_ref, gq_ref, gkv_ref, wq_ref, wqp_ref, ka_ref, ckv_ref, kpe_ref, qc_ref):
    ka = ka_ref[...]
    kb = pltpu.roll(ka, 64, axis=1)
    nope = _iota(ka.shape, 1) < QK_NOPE
    qcos = jnp.where(nope, 1.0, kb)
    qsin = jnp.where(nope, 0.0, ka)
    hq = _rms(tail_ref[:, :Q_LORA], gq_ref[...]).astype(BF16)
    qc = _dot(hq, wq_ref[...]) * _tile_heads(qcos) + _dot(hq, wqp_ref[...]) * _tile_heads(qsin)
    qc_ref[...] = (qc * (MLA_SCALE * LOG2E)).astype(BF16)
    ckv_ref[...] = _rms(tail_ref[:, Q_LORA:Q_LORA + KV_LORA], gkv_ref[...])
    kr = tail_ref[:, Q_LORA + KV_LORA:]
    kpe = kr * ka + pltpu.roll(kr, 64, axis=1) * kb
    kpe_ref[...] = kpe[:, :QK_ROPE]


def _dprep(proj, gq, gkv, wq, wqp, ka, tm):
    n = proj.shape[0]
    row = lambda w: pl.BlockSpec((tm, w), lambda i: (i, 0))
    const = lambda a, b: pl.BlockSpec((a, b), lambda i: (0, 0))
    return pl.pallas_call(
        _dprep_kernel,
        grid=(n // tm,),
        in_specs=[
            pl.BlockSpec((tm, TAIL_W), lambda i: (i, COL_TAIL // TAIL_W)),
            const(1, Q_LORA), const(1, KV_LORA), const(Q_LORA, DPAD), const(Q_LORA, DPAD),
            row(LANES),
        ],
        out_specs=[row(KV_LORA), row(QK_ROPE), row(DPAD)],
        out_shape=[
            jax.ShapeDtypeStruct((n, KV_LORA), F32),
            jax.ShapeDtypeStruct((n, QK_ROPE), F32),
            jax.ShapeDtypeStruct((n, DPAD), BF16),
        ],
        compiler_params=_cparams("parallel"),
        name="dprep",
    )(proj, gq, gkv, wq, wqp, ka)


LOG2E = 1.4426950408889634


def _kvup_t_kernel(ckv_ref, kpe_ref, wk_ref, wvt_ref, pm_ref, onet_ref, kc_ref, vt_ref):
    c = ckv_ref[...].astype(BF16)
    kc_ref[...] = (_dot(c, wk_ref[...]) + _dot(kpe_ref[...].astype(BF16), pm_ref[...])).astype(BF16)
    vt_ref[...] = (_dot_nt(wvt_ref[...], c) + onet_ref[...]).astype(BF16)


def _kvup_t(ckv, kpe, wk, wvt, pm, onet, tm):
    m = ckv.shape[0]
    row = lambda w: pl.BlockSpec((tm, w), lambda i: (i, 0))
    const = lambda a, b: pl.BlockSpec((a, b), lambda i: (0, 0))
    return pl.pallas_call(
        _kvup_t_kernel,
        grid=(m // tm,),
        in_specs=[row(KV_LORA), row(QK_ROPE), const(KV_LORA, DPAD), const(VT_ROWS, KV_LORA),
                  const(QK_ROPE, DPAD), const(VT_ROWS, 1)],
        out_specs=[row(DPAD), pl.BlockSpec((VT_ROWS, tm), lambda i: (0, i))],
        out_shape=[jax.ShapeDtypeStruct((m, DPAD), BF16), jax.ShapeDtypeStruct((VT_ROWS, m), BF16)],
        compiler_params=_cparams("parallel"),
        name="kvup_t",
    )(ckv, kpe, wk, wvt, pm, onet)


def _mla_prompt_kernel(qi_ref, ki_ref, q_ref, k_ref, vt_ref, ghc_ref, out_ref, m_scr, acc_scr, *, bq, bk,
                       qw, ahead):
    p = pl.program_id(0)
    q_i = qi_ref[p]
    k_i = ki_ref[p]
    last = (q_i * bq) // bk

    @pl.when(k_i == 0)
    def _():
        m_scr[...] = jnp.full(m_scr.shape, NEG_INF, F32)
        acc_scr[...] = jnp.zeros(acc_scr.shape, F32)

    def step(diag):
        if diag:
            key_chunk = k_i * (bk // CHUNK) + _iota((bk, bq), 0) // CHUNK
            qry_chunk = q_i * (bq // CHUNK) + _iota((bk, bq), 1) // CHUNK
            allowed = key_chunk <= qry_chunk

        units = [(h, c) for h in range(H) for c in range(bq // qw)]
        rc = min(bk, 64)

        def scores(u):
            h, c = u
            hs = slice(h * HEAD_PAD, (h + 1) * HEAD_PAD)
            return _dot_nt(k_ref[:, hs], q_ref[c * qw:(c + 1) * qw, hs])

        def update(u, st):
            h, c = u
            hs = slice(h * HEAD_PAD, (h + 1) * HEAD_PAD)
            qs = slice(c * qw, (c + 1) * qw)
            if diag:
                st = jnp.where(allowed[:, qs], st, NEG_INF)
            m_prev = m_scr[h, :, qs]
            mx = st[:rc]
            for r in range(1, bk // rc):
                mx = jnp.maximum(mx, st[r * rc:(r + 1) * rc])
            m_new = jnp.maximum(m_prev, jnp.max(mx, axis=0, keepdims=True))
            alpha = jnp.exp2(m_prev - m_new)[0:1]
            m_row = m_new[0:1]
            pt = jnp.concatenate([jnp.exp2(st[r * rc:(r + 1) * rc] - m_row).astype(BF16)
                                  for r in range(bk // rc)], axis=0)
            acc = alpha * acc_scr[h, :, qs] + _dot(vt_ref[h * VT_PAD:(h + 1) * VT_PAD, :], pt)
            if not diag:
                m_scr[h, :, qs] = m_new
                acc_scr[h, :, qs] = acc
            return acc

        accs = []
        pending = [scores(u) for u in units[:ahead]]
        for idx, u in enumerate(units):
            if idx + ahead < len(units):
                pending.append(scores(units[idx + ahead]))
            accs.append(update(u, pending.pop(0)))
        per_head = bq // qw
        return [jnp.concatenate(accs[h * per_head:(h + 1) * per_head], axis=1) for h in range(H)]

    @pl.when(k_i < last)
    def _():
        step(False)

    @pl.when(k_i == last)
    def _():
        ys = []
        for h, acc in enumerate(step(True)):
            o = acc[:DH] / acc[DH:DH + 1]
            ms = jnp.mean(o * o, axis=0, keepdims=True)
            ys.append(o * lax.rsqrt(ms + EPS) * ghc_ref[h * DH:(h + 1) * DH, :])
        out_ref[...] = jnp.concatenate(ys, axis=0).T.astype(out_ref.dtype)


def _mla_prompt(qc, kc, vt, ghc, bq, bk):
    t = qc.shape[0]
    assert t % bq == 0 and t % bk == 0 and bk % bq == 0
    pairs = [(i, j) for i in range(t // bq) for j in range((i * bq) // bk + 1)]
    qi = jnp.asarray([i for i, _ in pairs], jnp.int32)
    ki = jnp.asarray([j for _, j in pairs], jnp.int32)
    grid_spec = pltpu.PrefetchScalarGridSpec(
        num_scalar_prefetch=2,
        grid=(len(pairs),),
        in_specs=[
            pl.BlockSpec((bq, DPAD), lambda p, qi, ki: (qi[p], 0)),
            pl.BlockSpec((bk, DPAD), lambda p, qi, ki: (ki[p], 0)),
            pl.BlockSpec((VT_ROWS, bk), lambda p, qi, ki: (0, ki[p])),
            pl.BlockSpec((GW, 1), lambda p, qi, ki: (0, 0)),
        ],
        out_specs=pl.BlockSpec((bq, GW), lambda p, qi, ki: (qi[p], 0)),
        scratch_shapes=[
            pltpu.VMEM((H, SUBLANES, bq), F32),
            pltpu.VMEM((H, VT_PAD, bq), F32),
        ],
    )
    return pl.pallas_call(
        functools.partial(_mla_prompt_kernel, bq=bq, bk=bk, qw=min(bq, 256), ahead=2),
        grid_spec=grid_spec,
        out_shape=jax.ShapeDtypeStruct((t, GW), BF16),
        compiler_params=_cparams("arbitrary"),
        name="mla_prompt",
    )(qi, ki, qc, kc, vt, ghc)


LAT_W = 2 * LANES
ONE_LANE = KV_LORA + QK_ROPE


def _mla_sample_kernel(q_ref, ckvp_ref, kpep_ref, ckvn_ref, kpen_ref, wabs_ref, wv_ref, gh_ref, out_ref):
    s_len = q_ref.shape[0]

    def latent_rows(ckv_ref, kpe_ref):
        n = ckv_ref.shape[0]
        tail = jnp.concatenate([kpe_ref[...].astype(BF16), jnp.zeros((n, LANES - QK_ROPE), BF16)], axis=1)
        tail = jnp.where(_iota((n, LANES), 1) == QK_ROPE, jnp.ones((), BF16), tail)
        return jnp.concatenate([ckv_ref[...].astype(BF16), tail], axis=1)

    kvp = latent_rows(ckvp_ref, kpep_ref)
    kvn = latent_rows(ckvn_ref, kpen_ref)
    qabs = jnp.concatenate([_dot(q_ref[:, h * HEAD_PAD:(h + 1) * HEAD_PAD], wabs_ref[h]) for h in range(H)],
                           axis=0).astype(BF16)
    s1 = _dot_nt(qabs, kvp)
    s2 = _dot_nt(qabs, kvn)
    m = jnp.maximum(jnp.max(s1, axis=-1, keepdims=True), jnp.max(s2, axis=-1, keepdims=True))
    acc = _dot(jnp.exp2(s1 - m).astype(BF16), kvp) + _dot(jnp.exp2(s2 - m).astype(BF16), kvn)
    olat = (acc[:, :KV_LORA] / acc[:, ONE_LANE:ONE_LANE + 1]).astype(BF16)
    rows = slice(0, s_len)
    for h in range(H):
        o = _dot(olat[h * s_len:(h + 1) * s_len], wv_ref[h])
        _head_norm_store(out_ref, rows, h, o, gh_ref)


def _mla_sample(qc3, ckvp, kpep, ckvn, kpen, wabs, wv, gh):
    b, s, _ = qc3.shape
    npast = ckvp.shape[1]
    per_b = lambda r, w: pl.BlockSpec((None, r, w), lambda i: (i, 0, 0))
    const = lambda shp: pl.BlockSpec(shp, lambda i: (0,) * len(shp))
    return pl.pallas_call(
        _mla_sample_kernel,
        grid=(b,),
        in_specs=[per_b(s, DPAD), per_b(npast, KV_LORA), per_b(npast, QK_ROPE), per_b(s, KV_LORA),
                  per_b(s, QK_ROPE), const((H, HEAD_PAD, LAT_W)), const((H, KV_LORA, DH)), const((1, GW))],
        out_specs=per_b(s, GW),
        out_shape=jax.ShapeDtypeStruct((b, s, GW), BF16),
        compiler_params=_cparams("parallel"),
        name="mla_sample",
    )(qc3, ckvp, kpep, ckvn, kpen, wabs, wv, gh)


def _outproj_kernel(x_ref, a_ref, b_ref, c_ref, d_ref, w_ref, out_ref):
    acc = x_ref[...]
    for g, m_ref in enumerate((a_ref, b_ref, c_ref, d_ref)):
        acc = acc + _dot(m_ref[...], w_ref[g * GW:(g + 1) * GW, :])
    out_ref[...] = acc


def _outproj(x, ma, mb, mc, md, w, tm):
    n = x.shape[0]
    mix = pl.BlockSpec((tm, GW), lambda i: (i, 0))
    return pl.pallas_call(
        _outproj_kernel,
        grid=(n // tm,),
        in_specs=[pl.BlockSpec((tm, D_MODEL), lambda i: (i, 0)), mix, mix, mix, mix,
                  pl.BlockSpec((D_MODEL, D_MODEL), lambda i: (0, 0))],
        out_specs=pl.BlockSpec((tm, D_MODEL), lambda i: (i, 0)),
        out_shape=jax.ShapeDtypeStruct((n, D_MODEL), F32),
        compiler_params=_cparams("parallel"),
        name="outproj",
    )(x, ma, mb, mc, md, w)


def _ffn_kernel(*refs, seq_len, final_norm, tf):
    if seq_len is None:
        (x_ref, g_ref, wup_ref, cw_ref, wd_ref, gf_ref, out_ref, ga_ref, act_scr, carry_scr) = refs
    else:
        (x_ref, g_ref, wup_ref, cw_ref, wd_ref, gf_ref, h1_ref, h2_ref, out_ref, ga_ref, act_scr) = refs
    tm = x_ref.shape[0]
    nj = D_FF // tf
    h = _rms(x_ref[...], g_ref[...]).astype(BF16)
    row = _iota((tm, tf), 0)

    if seq_len is None:
        @pl.when(pl.program_id(0) == 0)
        def _():
            carry_scr[...] = jnp.zeros(carry_scr.shape, F32)

    def up(j):
        cols = slice(j * tf, (j + 1) * tf)
        ucols = slice(D_FF + j * tf, D_FF + (j + 1) * tf)
        return _dot(h, wup_ref[:, cols]), _dot(h, wup_ref[:, ucols])

    def gate(j, ga, u):
        cols = slice(j * tf, (j + 1) * tf)
        r1 = pltpu.roll(ga, 1, axis=0)
        r2 = pltpu.roll(ga, 2, axis=0)
        if seq_len is None:
            c1 = carry_scr[SUBLANES - 1:SUBLANES, cols]
            c2 = carry_scr[SUBLANES - 2:SUBLANES - 1, cols]
            prev1 = jnp.where(row >= 1, r1, c1)
            prev2 = jnp.where(row >= 2, r2, jnp.where(row == 1, c1, c2))
            tail = ga[tm - SUBLANES:, :]
            carry_scr[:, cols] = tail
            ga_ref[:, cols] = tail
        else:
            t = row % seq_len
            prev1 = jnp.where(t >= 1, r1, h1_ref[:, cols])
            prev2 = jnp.where(t >= 2, r2, h2_ref[:, cols])
            ga_ref[:, cols] = ga
        conv = prev2 * cw_ref[0:1, cols] + prev1 * cw_ref[1:2, cols] + ga * cw_ref[2:3, cols]
        act_scr[:, cols] = (conv * _sigmoid(conv) * u).astype(BF16)

    pending = up(0)
    for j in range(nj):
        nxt = up(j + 1) if j + 1 < nj else None
        gate(j, *pending)
        pending = nxt
    y = x_ref[...] + _dot(act_scr[...], wd_ref[...])
    if final_norm:
        y = _rms(y, gf_ref[...])
    out_ref[...] = y


def _ffn(x, g, w_up, cw, w_down, gf, h1, h2, *, tm, tf, seq_len, final_norm):
    n = x.shape[0]
    ni = n // tm
    resident = lambda a, b: pl.BlockSpec((a, b), lambda i: (0, 0), pipeline_mode=pl.Buffered(1))
    in_specs = [
        pl.BlockSpec((tm, D_MODEL), lambda i: (i, 0)),
        resident(1, D_MODEL),
        resident(D_MODEL, 2 * D_FF),
        resident(3, D_FF),
        resident(D_FF, D_MODEL),
        resident(1, D_MODEL),
    ]
    args = [x, g, w_up, cw, w_down, gf]
    scratch = [pltpu.VMEM((tm, D_FF), BF16)]
    if seq_len is None:
        ga_spec = pl.BlockSpec((None, SUBLANES, D_FF), lambda i: (i, 0, 0))
        ga_shape = jax.ShapeDtypeStruct((ni, SUBLANES, D_FF), F32)
        scratch.append(pltpu.VMEM((SUBLANES, D_FF), F32))
    else:
        in_specs += [pl.BlockSpec((tm, D_FF), lambda i: (i, 0))] * 2
        args += [h1, h2]
        ga_spec = pl.BlockSpec((tm, D_FF), lambda i: (i, 0))
        ga_shape = jax.ShapeDtypeStruct((n, D_FF), F32)
    return pl.pallas_call(
        functools.partial(_ffn_kernel, seq_len=seq_len, final_norm=final_norm, tf=tf),
        grid=(ni,),
        in_specs=in_specs,
        out_specs=[pl.BlockSpec((tm, D_MODEL), lambda i: (i, 0)), ga_spec],
        out_shape=[jax.ShapeDtypeStruct((n, D_MODEL), F32), ga_shape],
        scratch_shapes=scratch,
        compiler_params=_cparams("arbitrary"),
        name="ffn",
    )(*args)


def _rope_tables(offset, t):
    half = QK_ROPE // 2
    per_row = LANES // half
    assert t % per_row == 0
    inv = ROPE_THETA ** (-jnp.arange(half, dtype=F32) / half)
    pos = offset + per_row * _iota((t // per_row, LANES), 0) + _iota((t // per_row, LANES), 1) // half
    ang = pos.astype(F32) * jnp.tile(inv, per_row)[None, :]
    cos, sin = jnp.cos(ang).reshape(t, half), jnp.sin(ang).reshape(t, half)
    z32 = jnp.zeros((t, 32), F32)
    return jnp.concatenate([cos, cos, z32, -sin, sin, z32], -1)


def _rel_bias(table, n_past, n_q, n_k):
    period = n_q + n_k
    jmi = jnp.concatenate([jnp.arange(0, n_k + 1), jnp.arange(-(n_q - 1), 0)])
    diag = table[:, jnp.clip(n_past - jmi, -REL_MAX, REL_MAX) + REL_MAX]
    rows = jnp.tile(diag, (1, n_q))[:, :n_q * (period - 1)].reshape(table.shape[0], n_q, period - 1)
    return rows[:, :, :n_k]


def _swap_halves(w):
    half = w.shape[-1] // 2
    return jnp.concatenate([w[..., half:], w[..., :half]], -1)


def _layer_weights(lw):
    (g_mix, w_in, a_rel_bias, b_i_bias, b_f_bias, c_conv_w, c_a_log, c_dt_bias,
     d_g_q, d_w_q_up, d_g_kv, d_w_kv_up, g_head, w_out, g_ffn, w_up, f_conv_w, w_down) = lw
    o = 0
    cols = {}
    for name, size in (("a", 3 * GW), ("b", 4 * GW), ("bg", 2 * H), ("c", 3 * GW), ("cz", GW),
                       ("cg", 2 * H), ("dq", Q_LORA), ("dkv", KV_LORA), ("dkr", QK_ROPE)):
        cols[name] = w_in[:, o:o + size]
        o += size
    gates = jnp.concatenate([cols["bg"], cols["cg"]], -1)
    pad16 = jnp.zeros((D_MODEL, 16), F32)
    pad32 = jnp.zeros((D_MODEL, 32), F32)
    w_perm = jnp.concatenate([cols["c"], cols["a"], cols["dq"], cols["dkv"], cols["dkr"], gates, pad16,
                              _swap_halves(cols["dkr"]), pad32, cols["b"], cols["cz"]], -1)
    zc = lambda n: jnp.zeros((1, n), F32)
    zr = lambda n: jnp.zeros((n, 1), F32)
    bias_c = jnp.concatenate([zc(GATE_OFF), b_i_bias[None], b_f_bias[None], zc(LANES - GATE_OFF - 2 * H)], -1)
    bias_r = jnp.concatenate([b_i_bias[:, None], b_f_bias[:, None], zr(2 * H)], 0)
    alog_c = jnp.concatenate([zc(GATE_OFF + 3 * H), c_a_log[None], zc(LANES - GATE_OFF - 4 * H)], -1)
    alog_r = jnp.concatenate([zr(3 * H), c_a_log[:, None]], 0)
    dt_c = jnp.concatenate([zc(GATE_OFF + 3 * H), c_dt_bias[None], zc(LANES - GATE_OFF - 4 * H)], -1)
    dt_r = jnp.concatenate([zr(3 * H), c_dt_bias[:, None]], 0)

    wq = d_w_q_up.reshape(Q_LORA, H, QK_NOPE + QK_ROPE)
    z_h32 = jnp.zeros((Q_LORA, H, 32), F32)
    wq_full = jnp.concatenate([wq, z_h32], -1).reshape(Q_LORA, DPAD)
    wq_part = jnp.concatenate([jnp.zeros((Q_LORA, H, QK_NOPE), F32), _swap_halves(wq[..., QK_NOPE:]), z_h32],
                              -1).reshape(Q_LORA, DPAD)
    wkv = d_w_kv_up.reshape(KV_LORA, H, 2 * DH)
    z_h64 = jnp.zeros((KV_LORA, H, DH), F32)
    wk_full = jnp.concatenate([wkv[..., :DH], z_h64], -1).reshape(KV_LORA, DPAD)
    place = jnp.concatenate([jnp.zeros((QK_ROPE, QK_NOPE), F32), jnp.eye(QK_ROPE, dtype=F32),
                             jnp.zeros((QK_ROPE, 32), F32)], -1)
    pmat = jnp.concatenate([place] * H, -1)
    wk_t = jnp.transpose(wkv[..., :DH], (1, 2, 0))
    rope_rows = jnp.concatenate([jnp.zeros((QK_ROPE, KV_LORA), F32), jnp.eye(QK_ROPE, dtype=F32),
                                 jnp.zeros((QK_ROPE, LAT_W - KV_LORA - QK_ROPE), F32)], -1)
    wabs = jnp.concatenate([
        jnp.concatenate([wk_t, jnp.zeros((H, QK_NOPE, LAT_W - KV_LORA), F32)], -1),
        jnp.broadcast_to(rope_rows, (H, QK_ROPE, LAT_W)),
        jnp.zeros((H, HEAD_PAD - QK_NOPE - QK_ROPE, LAT_W), F32)], 1)
    return dict(
        g_mix=g_mix[None], w_in=w_perm.astype(BF16), w_gt=gates.T.astype(BF16),
        table=a_rel_bias, bias_c=bias_c, bias_r=bias_r, alog_c=alog_c, alog_r=alog_r, dt_c=dt_c, dt_r=dt_r,
        c_conv_w=c_conv_w, g_q=d_g_q[None], g_kv=d_g_kv[None],
        wq=wq_full.astype(BF16), wqp=wq_part.astype(BF16), wk=wk_full.astype(BF16),
        pmat=pmat.astype(BF16), wabs=wabs.astype(BF16),
        wv_heads=jnp.transpose(wkv[..., DH:], (1, 0, 2)).astype(BF16),
        wvt=jnp.concatenate([wkv[..., DH:], jnp.zeros((KV_LORA, H, VT_PAD - DH), F32)], -1)
        .reshape(KV_LORA, VT_ROWS).T.astype(BF16),
        vonest=(jnp.arange(VT_ROWS) % VT_PAD == DH).astype(F32)[:, None],
        g_head=g_head.reshape(4, 1, GW), w_out=w_out.astype(BF16),
        g_ffn=g_ffn[None], w_up=w_up.astype(BF16), f_conv_w=f_conv_w, w_down=w_down.astype(BF16))


def _gates_t3(gt, b, t, l):
    return gt.reshape(N_GATES, b, t // l, l).transpose(1, 2, 0, 3)


def _layer(x, offset, st, w, gf, final_norm, cfg):
    b, t, _ = x.shape
    n = b * t
    first = st is None
    x2 = x.reshape(n, D_MODEL)
    proj, gt = _inproj(x2, w["g_mix"], w["w_in"], w["w_gt"], cfg["tm"])
    proj3 = proj.reshape(b, t, PROJ_W)
    gh = w["g_head"]
    l = min(t, CHUNK)
    gt3 = _gates_t3(gt, b, t, l)

    new_ak = proj3[:, t - min(A_PAST, t):, COL_A + GW:COL_A + 2 * GW].reshape(b, -1, H, DH)
    new_av = proj3[:, t - min(A_PAST, t):, COL_A + 2 * GW:COL_A + 3 * GW].reshape(b, -1, H, DH)
    if first:
        bias = _rel_bias(w["table"], A_PAST, CHUNK, A_PAST + CHUNK)
        oa = _band_prompt(proj, bias, gh[0].T)
    else:
        npast = st[0].shape[1]
        bias = _rel_bias(w["table"], npast, t, npast + t)
        oa = _band_sample(proj3, st[0].reshape(b, npast, GW), st[1].reshape(b, npast, GW), bias, gh[0])
        oa = oa.reshape(n, GW)

    if first:
        c0 = jnp.zeros((b, H, DH, DH), F32)
        n0 = jnp.zeros((b, H, DH), F32)
        m0 = jnp.zeros((b, 1, H), F32)
    else:
        c0, n0, m0 = st[2], st[3], st[4][:, None, :]
    if l == CHUNK:
        ob, cbd, nrow, mrow = _mlstm64(proj3, gt3, w["bias_c"], w["bias_r"], gh[1], _to_block_diag(c0),
                                       n0.reshape(b, 1, GW), jnp.repeat(m0, DH, axis=-1), cfg["nck"])
        new_bc, new_bn, new_bm = _from_block_diag(cbd), nrow.reshape(b, H, DH), mrow[:, 0, ::DH]
    else:
        ob, new_bc, new_bn, new_bm = _mlstm(proj3, gt3, w["bias_c"], w["bias_r"], gh[1], c0, n0, m0,
                                            l, cfg["nck"])
        new_bm = new_bm[:, 0, :]

    if first:
        hist8 = jnp.zeros((b, SUBLANES, 3 * GW), F32)
        s0 = jnp.zeros((b, H, DH, DH), F32)
    else:
        hist8 = jnp.concatenate([jnp.zeros((b, SUBLANES - 3, 3 * GW), F32), st[6]], 1)
        s0 = st[5]
    gdn_args = (proj3, gt3, hist8, w["c_conv_w"], w["alog_c"], w["alog_r"], w["dt_c"], w["dt_r"], gh[2])
    if l == CHUNK:
        oc, sbd = _gdn64(*gdn_args, _to_block_diag(s0), cfg["nck"])
        new_cs = _from_block_diag(sbd)
    else:
        oc, new_cs = _gdn(*gdn_args, s0, l, cfg["nck"])
    new_cconv = proj3[:, t - 3:, COL_CX:COL_CX + 3 * GW]

    ka = jnp.tile(_rope_tables(offset, t), (b, 1))
    ckv, kpe, qc = _dprep(proj, w["g_q"], w["g_kv"], w["wq"], w["wqp"], ka, cfg["tm_d"])
    if first:
        kc, vt = _kvup_t(ckv, kpe, w["wk"], w["wvt"], w["pmat"], w["vonest"], cfg["tm_kv"])
        od = _mla_prompt(qc, kc, vt, gh[3].T, cfg["mla_bq"], cfg["mla_bk"])
    else:
        od = _mla_sample(qc.reshape(b, t, DPAD), st[7], st[8], ckv.reshape(b, t, KV_LORA),
                         kpe.reshape(b, t, QK_ROPE), w["wabs"], w["wv_heads"], gh[3])
        od = od.reshape(n, GW)

    x2 = _outproj(x2, oa, ob.reshape(n, GW), oc.reshape(n, GW), od, w["w_out"], cfg["tm"])

    if first:
        y, ga_tail = _ffn(x2, w["g_ffn"], w["w_up"], w["f_conv_w"], w["w_down"], gf, None, None,
                          tm=cfg["tm"], tf=cfg["tf"], seq_len=None, final_norm=final_norm)
        new_fconv = ga_tail[-1, SUBLANES - 2:, :][None]
    else:
        hist = st[9]
        zrow = jnp.zeros((b, t - 1, D_FF), F32)
        h1 = jnp.concatenate([hist[:, 1:2], zrow], 1).reshape(n, D_FF)
        h2 = jnp.concatenate([hist, zrow[:, 1:]], 1).reshape(n, D_FF)
        y, ga = _ffn(x2, w["g_ffn"], w["w_up"], w["f_conv_w"], w["w_down"], gf, h1, h2,
                     tm=cfg["tm"], tf=cfg["tf"], seq_len=t, final_norm=final_norm)
        new_fconv = ga.reshape(b, t, D_FF)[:, t - 2:]
    state = (new_ak, new_av, new_bc, new_bn, new_bm, new_cs, new_cconv,
             ckv.reshape(b, t, KV_LORA), kpe.reshape(b, t, QK_ROPE), new_fconv)
    return y.reshape(b, t, D_MODEL), state


def _config(b, t):
    n = b * t
    tm = min(n, 1024)
    return dict(tm=tm, tf=256, nck=1 if t <= CHUNK else 8,
                tm_d=min(n, 1024), tm_kv=min(n, 2048), mla_bq=min(t, 512), mla_bk=min(t, 1024))


def kernel(x_prompt, x_sample, cache_a_k, cache_a_v, state_b_c, state_b_n, state_b_m, state_c_s, cache_c_conv, cache_d_ckv, cache_d_kpe, cache_ffn_conv, g_mix, w_in, a_rel_bias, b_i_bias, b_f_bias, c_conv_w, c_a_log, c_dt_bias, d_g_q, d_w_q_up, d_g_kv, d_w_kv_up, g_head, w_out, g_ffn, w_up, f_conv_w, w_down, g_final):
    layer_w = (g_mix, w_in, a_rel_bias, b_i_bias, b_f_bias, c_conv_w, c_a_log, c_dt_bias,
               d_g_q, d_w_q_up, d_g_kv, d_w_kv_up, g_head, w_out, g_ffn, w_up, f_conv_w, w_down)
    depth = g_mix.shape[0]
    past = cache_d_ckv.shape[2]
    xp, xs = x_prompt, x_sample
    cfg_p = _config(*x_prompt.shape[:2])
    cfg_s = _config(*x_sample.shape[:2])
    gf = g_final[None]
    new_p, new_s = [], []
    for l in range(depth):
        w = _layer_weights(tuple(a[l] for a in layer_w))
        last = l == depth - 1
        xp, sp_l = _layer(xp, 0, None, w, gf, last, cfg_p)
        st = (cache_a_k[l], cache_a_v[l], state_b_c[l], state_b_n[l], state_b_m[l],
              state_c_s[l], cache_c_conv[l], cache_d_ckv[l], cache_d_kpe[l], cache_ffn_conv[l])
        xs, ss_l = _layer(xs, past, st, w, gf, last, cfg_s)
        new_p.append(sp_l)
        new_s.append(ss_l)
    outs = [xp, xs]
    for i in range(10):
        outs.append(jnp.stack([s[i] for s in new_p]))
        outs.append(jnp.stack([s[i] for s in new_s]))
    return tuple(outs)
```

```python
import functools
import math

import jax
import jax.numpy as jnp
from jax import lax
from jax.experimental import pallas as pl
from jax.experimental.pallas import tpu as pltpu

F32 = jnp.float32
BF16 = jnp.bfloat16

D_MODEL = 1024
CHUNK = 64
H = 4
DH = 64
GW = H * DH
A_PAST = 8 * CHUNK
REL_MAX = 2 * CHUNK
Q_LORA = 256
KV_LORA = 128
QK_NOPE = 64
QK_ROPE = 32
ROPE_THETA = 10000.0
MLA_SCALE = (QK_NOPE + QK_ROPE) ** -0.5
D_FF = 2816
EPS = 1e-6

COL_CX = 0
COL_A = 3 * GW
COL_TAIL = 6 * GW
TAIL_W = 512
COL_B = COL_TAIL + TAIL_W
COL_CZ = COL_B + 4 * GW
PROJ_W = COL_CZ + GW
GATE_BLK = (COL_TAIL + 384) // 128
GATE_OFF = 32
N_GATES = 16

LANES = 128
SUBLANES = 8
VMEM_LIMIT = 56 * 1024 * 1024

NEG_INF = float("-inf")


def _cparams(*sem):
    return pltpu.CompilerParams(dimension_semantics=sem, vmem_limit_bytes=VMEM_LIMIT)


def _dot(a, b):
    return jnp.dot(a, b, preferred_element_type=F32)


def _dot_nt(a, b):
    return lax.dot_general(a, b, (((1,), (1,)), ((), ())), preferred_element_type=F32)


def _dot_tn(a, b):
    return lax.dot_general(a, b, (((0,), (0,)), ((), ())), preferred_element_type=F32)


def _split3(x):
    hi = x.astype(BF16)
    r1 = x - hi.astype(F32)
    mid = r1.astype(BF16)
    lo = (r1 - mid.astype(F32)).astype(BF16)
    return hi, mid, lo


def _rms(x, g):
    return x * lax.rsqrt(jnp.mean(x * x, axis=-1, keepdims=True) + EPS) * g


def _log_sigmoid(x):
    return jnp.minimum(x, 0.0) - jnp.log1p(jnp.exp(-jnp.abs(x)))


def _softplus(x):
    return jnp.maximum(x, 0.0) + jnp.log1p(jnp.exp(-jnp.abs(x)))


def _sigmoid(x):
    return 1.0 / (1.0 + jnp.exp(-x))


def _iota(shape, dim):
    return lax.broadcasted_iota(jnp.int32, shape, dim)


def _inproj_kernel(x_ref, g_ref, w_ref, wgt_ref, proj_ref, gt_ref):
    h = _rms(x_ref[...], g_ref[...]).astype(BF16)
    gt_ref[...] = _dot_nt(wgt_ref[...], h)
    proj_ref[...] = _dot(h, w_ref[...])


def _inproj(x, g, w, wgt, tm):
    n = x.shape[0]
    resident = lambda a, b: pl.BlockSpec((a, b), lambda i: (0, 0), pipeline_mode=pl.Buffered(1))
    return pl.pallas_call(
        _inproj_kernel,
        grid=(n // tm,),
        in_specs=[
            pl.BlockSpec((tm, D_MODEL), lambda i: (i, 0)),
            resident(1, D_MODEL),
            resident(D_MODEL, PROJ_W),
            resident(N_GATES, D_MODEL),
        ],
        out_specs=[
            pl.BlockSpec((tm, PROJ_W), lambda i: (i, 0)),
            pl.BlockSpec((N_GATES, tm), lambda i: (0, i)),
        ],
        out_shape=[
            jax.ShapeDtypeStruct((n, PROJ_W), F32),
            jax.ShapeDtypeStruct((N_GATES, n), F32),
        ],
        compiler_params=_cparams("parallel"),
        name="inproj",
    )(x, g, w, wgt)


def _head_norm_store(out_ref, rows, h, o, gh_ref):
    g = gh_ref[:, h * DH:(h + 1) * DH]
    y = o * lax.rsqrt(jnp.mean(o * o, axis=-1, keepdims=True) + EPS) * g
    out_ref[rows, h * DH:(h + 1) * DH] = y.astype(out_ref.dtype)


def _band_prompt_kernel(q_ref, kp_ref, kc_ref, vp_ref, vc_ref, bt2_ref, ghc_ref, out_ref, biast_ref, *, qb):
    nk = 2 * qb
    one_lane = (_iota((nk, DH), 1) == 0).astype(BF16)

    @pl.when(pl.program_id(0) == 0)
    def _():
        band = bt2_ref.shape[1]
        left = _iota((nk, LANES), 1) < CHUNK
        def placed(bt, top):
            ninf = lambda n: [jnp.full((n, LANES), NEG_INF, F32)] if n else []
            return jnp.concatenate(ninf(top) + [bt] + ninf(nk - band - top), axis=0)

        for h in range(H):
            bt = bt2_ref[h]
            shifted = [placed(bt, c * CHUNK) for c in range(qb // CHUNK)]
            for t2 in range(qb // LANES):
                biast_ref[h, :, t2 * LANES:(t2 + 1) * LANES] = jnp.where(left, shifted[2 * t2], shifted[2 * t2 + 1])

    def run(first):
        def scores(h):
            hs = slice(h * DH, (h + 1) * DH)
            kcat = jnp.concatenate([kp_ref[:, hs], kc_ref[:, hs]], axis=0).astype(BF16)
            return _dot_nt(kcat, (q_ref[:, hs] * (DH ** -0.5)).astype(BF16))

        def attend(h, st):
            hs = slice(h * DH, (h + 1) * DH)
            st = st + biast_ref[h]
            if first:
                st = jnp.where(_iota((nk, qb), 0) >= qb, st, NEG_INF)
            p = jnp.exp(st - jnp.max(st, axis=0, keepdims=True)).astype(BF16)
            vcat = jnp.concatenate([vp_ref[:, hs], vc_ref[:, hs]], axis=0).astype(BF16)
            acc = _dot_tn(jnp.concatenate([vcat, one_lane], axis=1), p)
            o = acc[:DH] / acc[DH:DH + 1]
            ms = jnp.mean(o * o, axis=0, keepdims=True)
            return o * lax.rsqrt(ms + EPS) * ghc_ref[hs, :]

        ys = []
        pending = scores(0)
        for h in range(H):
            nxt = scores(h + 1) if h + 1 < H else None
            ys.append(attend(h, pending))
            pending = nxt
        out_ref[...] = jnp.concatenate(ys, axis=0).T.astype(out_ref.dtype)

    @pl.when(pl.program_id(0) == 0)
    def _():
        run(True)

    @pl.when(pl.program_id(0) > 0)
    def _():
        run(False)


def _band_prompt(proj, bias, ghc, qb=A_PAST):
    t = proj.shape[0]
    assert qb == A_PAST and t % qb == 0
    prev = lambda i: jnp.maximum(i - 1, 0)
    cq = COL_A // GW
    band = bias.shape[-1]
    bias_t = bias.transpose(0, 2, 1)
    bt2 = jnp.concatenate([bias_t, bias_t], axis=-1)
    return pl.pallas_call(
        functools.partial(_band_prompt_kernel, qb=qb),
        grid=(t // qb,),
        in_specs=[
            pl.BlockSpec((qb, GW), lambda i: (i, cq)),
            pl.BlockSpec((qb, GW), lambda i: (prev(i), cq + 1)),
            pl.BlockSpec((qb, GW), lambda i: (i, cq + 1)),
            pl.BlockSpec((qb, GW), lambda i: (prev(i), cq + 2)),
            pl.BlockSpec((qb, GW), lambda i: (i, cq + 2)),
            pl.BlockSpec((H, band, LANES), lambda i: (0, 0, 0)),
            pl.BlockSpec((GW, 1), lambda i: (0, 0)),
        ],
        out_specs=pl.BlockSpec((qb, GW), lambda i: (i, 0)),
        out_shape=jax.ShapeDtypeStruct((t, GW), BF16),
        scratch_shapes=[pltpu.VMEM((H, 2 * qb, qb), F32)],
        compiler_params=_cparams("arbitrary"),
        name="band_prompt",
    )(proj, proj, proj, proj, proj, bt2, ghc)


def _band_sample_kernel(q_ref, k_ref, v_ref, ck_ref, cv_ref, bias_ref, gh_ref, out_ref):
    npast = ck_ref.shape[0]
    rows = slice(0, q_ref.shape[0])
    for h in range(H):
        hs = slice(h * DH, (h + 1) * DH)
        q = q_ref[:, hs].astype(BF16)
        s1 = _dot_nt(q, ck_ref[:, hs].astype(BF16)) * (DH ** -0.5) + bias_ref[h, :, :npast]
        s2 = _dot_nt(q, k_ref[:, hs].astype(BF16)) * (DH ** -0.5) + bias_ref[h, :, npast:]
        m = jnp.maximum(jnp.max(s1, axis=-1, keepdims=True), jnp.max(s2, axis=-1, keepdims=True))
        p1 = jnp.exp(s1 - m)
        p2 = jnp.exp(s2 - m)
        l = jnp.sum(p1, axis=-1, keepdims=True) + jnp.sum(p2, axis=-1, keepdims=True)
        o = (_dot(p1.astype(BF16), cv_ref[:, hs].astype(BF16))
             + _dot(p2.astype(BF16), v_ref[:, hs].astype(BF16))) / l
        _head_norm_store(out_ref, rows, h, o, gh_ref)


def _band_sample(proj3, ck, cv, bias, gh):
    b, s, _ = proj3.shape
    npast = ck.shape[1]
    return pl.pallas_call(
        _band_sample_kernel,
        grid=(b,),
        in_specs=[
            pl.BlockSpec((None, s, GW), lambda i: (i, 0, COL_A // GW)),
            pl.BlockSpec((None, s, GW), lambda i: (i, 0, COL_A // GW + 1)),
            pl.BlockSpec((None, s, GW), lambda i: (i, 0, COL_A // GW + 2)),
            pl.BlockSpec((None, npast, GW), lambda i: (i, 0, 0)),
            pl.BlockSpec((None, npast, GW), lambda i: (i, 0, 0)),
            pl.BlockSpec((H, s, npast + s), lambda i: (0, 0, 0)),
            pl.BlockSpec((1, GW), lambda i: (0, 0)),
        ],
        out_specs=pl.BlockSpec((None, s, GW), lambda i: (i, 0, 0)),
        out_shape=jax.ShapeDtypeStruct((b, s, GW), BF16),
        compiler_params=_cparams("parallel"),
        name="band_sample",
    )(proj3, proj3, proj3, ck, cv, bias, gh)


def _cumsum_cols(x, lo_tri):
    return sum(_dot(lo_tri, part) for part in _split3(x))


def _cumsum_rows(x, up_tri):
    return sum(_dot(part, up_tri) for part in _split3(x))


def _tri_masks(l):
    r = _iota((l, l), 0)
    c = _iota((l, l), 1)
    return r >= c, r > c


def _mlstm_kernel(q_ref, k_ref, v_ref, o_ref, gc_ref, gr_ref, bc_ref, br_ref, gh_ref,
                  c0_ref, n0_ref, m0_ref, out_ref, c_ref, n_ref, m_ref, *, l, nck):
    @pl.when(pl.program_id(1) == 0)
    def _():
        c_ref[...] = c0_ref[...]
        n_ref[...] = n0_ref[...]
        m_ref[...] = m0_ref[...]

    incl, _ = _tri_masks(l)
    lo_tri = incl.astype(BF16)
    up_tri = (_iota((l, l), 0) <= _iota((l, l), 1)).astype(BF16)

    probs = []
    for ck in range(nck):
        rows = slice(ck * l, (ck + 1) * l)
        gcol = gc_ref[rows, :] + bc_ref[...]
        grow = gr_ref[ck] + br_ref[...]
        gcs = _cumsum_cols(_log_sigmoid(gcol), lo_tri)
        grs = _cumsum_rows(_log_sigmoid(grow), up_tri)
        for h in range(H):
            hs = slice(h * DH, (h + 1) * DH)
            ig_c = gcol[:, GATE_OFF + h:GATE_OFF + h + 1]
            g_c = gcs[:, GATE_OFF + H + h:GATE_OFF + H + h + 1]
            ig_r = grow[h:h + 1, :]
            g_r = grs[H + h:H + h + 1, :]
            q = q_ref[rows, hs]
            kf = k_ref[rows, hs] * (DH ** -0.5)
            lmat = jnp.where(incl, g_c - g_r + ig_r, NEG_INF)
            probs.append(dict(
                rows=rows, h=h, q=q, kf=kf, qb=q.astype(BF16), kb=kf.astype(BF16),
                vb=v_ref[rows, hs].astype(BF16), lmat=lmat, lmax=jnp.max(lmat, axis=-1, keepdims=True),
                g_c=g_c, ig_c=ig_c, g_last=g_c[l - 1:l, :]))
    qks = [_dot_nt(p["qb"], p["kb"]) for p in probs]

    ms = [m_ref[:, h:h + 1] for h in range(H)]
    for p in probs:
        m_old = ms[p["h"]]
        p["linter"] = p["g_c"] + m_old
        p["mt"] = jnp.maximum(p["linter"], p["lmax"])
        m_new = p["mt"][l - 1:l, :]
        p["dprev"] = jnp.exp(p["g_last"] + m_old - m_new)
        p["kw"] = p["kf"] * jnp.exp(p["g_last"] - p["g_c"] + p["ig_c"] - m_new)
        ms[p["h"]] = m_new
    ws_ = [qk * jnp.exp(p["lmat"] - p["mt"]) for p, qk in zip(probs, qks)]
    wvs = [_dot(w.astype(BF16), p["vb"]) for p, w in zip(probs, ws_)]
    upds = [_dot_tn(p["kw"].astype(BF16), p["vb"]) for p in probs]

    cs = [c_ref[h] for h in range(H)]
    ns = [n_ref[h:h + 1, :] for h in range(H)]
    qcs, qns = [], []
    for p, upd in zip(probs, upds):
        h = p["h"]
        qcs.append(_dot(p["qb"], cs[h].astype(BF16)))
        qns.append(jnp.sum(p["q"] * ns[h], axis=-1, keepdims=True))
        cs[h] = p["dprev"] * cs[h] + upd
        ns[h] = p["dprev"] * ns[h] + jnp.sum(p["kw"], axis=0, keepdims=True)
    for h in range(H):
        c_ref[h] = cs[h]
        n_ref[h:h + 1, :] = ns[h]
        m_ref[:, h:h + 1] = ms[h]

    wsums = [jnp.sum(w, axis=-1, keepdims=True) for w in ws_]
    obs = []
    for p, wsum, wv, qc, qn in zip(probs, wsums, wvs, qcs, qns):
        hs = slice(p["h"] * DH, (p["h"] + 1) * DH)
        inter = jnp.exp(p["linter"] - p["mt"])
        den = wsum + inter * qn
        hout = (wv + inter * qc) / jnp.maximum(jnp.abs(den), jnp.exp(-p["mt"]))
        obs.append(hout * _sigmoid(o_ref[p["rows"], hs]))
    msq = [jnp.mean(ob * ob, axis=-1, keepdims=True) for ob in obs]
    for p, ob, ms_ in zip(probs, obs, msq):
        hs = slice(p["h"] * DH, (p["h"] + 1) * DH)
        out_ref[p["rows"], hs] = (ob * lax.rsqrt(ms_ + EPS) * gh_ref[:, hs]).astype(out_ref.dtype)


def _mlstm(proj3, gt3, bias_c, bias_r, gh, c0, n0, m0, l, nck):
    b, t, _ = proj3.shape
    steps = t // (l * nck)
    blk = l * nck
    col = lambda j: pl.BlockSpec((None, blk, GW), lambda bi, s: (bi, s, j))
    state = lambda shp: pl.BlockSpec((None,) + shp, lambda bi, s: (bi,) + (0,) * len(shp))
    return pl.pallas_call(
        functools.partial(_mlstm_kernel, l=l, nck=nck),
        grid=(b, steps),
        in_specs=[
            col(COL_B // GW), col(COL_B // GW + 1), col(COL_B // GW + 2), col(COL_B // GW + 3),
            pl.BlockSpec((None, blk, LANES), lambda bi, s: (bi, s, GATE_BLK)),
            pl.BlockSpec((None, nck, N_GATES, l), lambda bi, s: (bi, s, 0, 0)),
            pl.BlockSpec((1, LANES), lambda bi, s: (0, 0)),
            pl.BlockSpec((N_GATES, 1), lambda bi, s: (0, 0)),
            pl.BlockSpec((1, GW), lambda bi, s: (0, 0)),
            state((H, DH, DH)), state((H, DH)), state((1, H)),
        ],
        out_specs=[
            pl.BlockSpec((None, blk, GW), lambda bi, s: (bi, s, 0)),
            state((H, DH, DH)), state((H, DH)), state((1, H)),
        ],
        out_shape=[
            jax.ShapeDtypeStruct((b, t, GW), BF16),
            jax.ShapeDtypeStruct((b, H, DH, DH), F32),
            jax.ShapeDtypeStruct((b, H, DH), F32),
            jax.ShapeDtypeStruct((b, 1, H), F32),
        ],
        compiler_params=_cparams("parallel", "arbitrary"),
        name="mlstm",
    )(proj3, proj3, proj3, proj3, proj3, gt3, bias_c, bias_r, gh, c0, n0, m0)


def _head_of(idx):
    return idx // DH


def _block_mask(n_rows, n_cols):
    return _head_of(_iota((n_rows, n_cols), 0)) == _head_of(_iota((n_rows, n_cols), 1))


def _expander(first_lane):
    r = _iota((LANES, GW), 0)
    c = _iota((LANES, GW), 1)
    return (r == first_lane + _head_of(c)).astype(BF16)


def _dot_stacked(parts, rhs):
    m = parts[0].shape[0]
    y = _dot(jnp.concatenate(parts, axis=0), rhs)
    return sum(y[i * m:(i + 1) * m] for i in range(len(parts)))


def _expand(x, e):
    return _dot_stacked(_split3(x), e)


def _head_sums(a, bones):
    return _dot_stacked(_split3(a), bones)


def _cumsum_cols_wide(x, lo_tri):
    w = x.shape[1]
    y = _dot(lo_tri, jnp.concatenate(_split3(x), axis=1))
    return y[:, :w] + y[:, w:2 * w] + y[:, 2 * w:]


def _row_select(x_t, first_row):
    r = _iota(x_t.shape, 0)
    c = _iota(x_t.shape, 1)
    return jnp.sum(jnp.where(r == first_row + _head_of(c), x_t, 0.0), axis=0, keepdims=True)


def _block_diag_rows(x, mask):
    return jnp.where(mask, jnp.concatenate([x] * H, axis=0), jnp.zeros((), x.dtype))


def _cummax_rows(x):
    rows = _iota(x.shape, 0)
    sh = 1
    while sh < x.shape[0]:
        x = jnp.maximum(x, jnp.where(rows >= sh, pltpu.roll(x, sh, axis=0), NEG_INF))
        sh *= 2
    return x


def _mlstm64_kernel(q_ref, k_ref, v_ref, o_ref, gc_ref, gr_ref, bc_ref, br_ref, gh_ref,
                    c0_ref, n0_ref, m0_ref, out_ref, c_ref, n_ref, m_ref, *, nck):
    l = CHUNK

    @pl.when(pl.program_id(1) == 0)
    def _():
        c_ref[...] = c0_ref[...]
        n_ref[...] = n0_ref[...]
        m_ref[...] = m0_ref[...]

    bmask = _block_mask(GW, GW)
    bones = bmask.astype(BF16)
    within = _iota((GW, GW), 0) % DH <= _iota((GW, GW), 1) % DH
    up_bd = jnp.logical_and(bmask, within).astype(BF16)
    lo_tri = (_iota((l, l), 0) >= _iota((l, l), 1)).astype(BF16)
    incl = _iota((l, GW), 0) >= _iota((l, GW), 1) % DH
    e_i = _expander(GATE_OFF)
    e_f = _expander(GATE_OFF + H)

    cks = []
    for ck in range(nck):
        rows = slice(ck * l, (ck + 1) * l)
        gcol = gc_ref[rows, :] + bc_ref[...]
        gcs = _cumsum_cols_wide(_log_sigmoid(gcol), lo_tri)
        g_c = _expand(gcs, e_f)
        i_c = _expand(gcol, e_i)
        grow = gr_ref[ck] + br_ref[...]
        grow_t = jnp.concatenate([grow] * H, axis=1)
        grs_t = _dot_stacked(_split3(_log_sigmoid(grow_t)), up_bd)
        a_r = _row_select(grow_t, 0) - _row_select(grs_t, H)
        lmat = jnp.where(incl, g_c + a_r, NEG_INF)
        lmax = g_c + _cummax_rows(i_c - g_c)
        q = q_ref[rows, :]
        kf = k_ref[rows, :] * (DH ** -0.5)
        cks.append(dict(rows=rows, g_c=g_c, i_c=i_c, lmat=lmat, lmax=lmax, q=q, kf=kf,
                        qb=q.astype(BF16), kb=kf.astype(BF16), vb=v_ref[rows, :].astype(BF16),
                        g_last=g_c[l - 1:l, :]))
    scs = [_dot_nt(p["qb"], _block_diag_rows(p["kb"], bmask)) for p in cks]

    m_run = m_ref[...]
    for p in cks:
        p["linter"] = p["g_c"] + m_run
        p["mt"] = jnp.maximum(p["linter"], p["lmax"])
        m_new = p["mt"][l - 1:l, :]
        p["dprev"] = jnp.exp(p["g_last"] + m_run - m_new)
        p["kw"] = p["kf"] * jnp.exp(p["g_last"] - p["g_c"] + p["i_c"] - m_new)
        m_run = m_new
    m_ref[...] = m_run
    wbs = [(s * jnp.exp(p["lmat"] - p["mt"])).astype(BF16) for p, s in zip(cks, scs)]
    nums = [_dot(w, _block_diag_rows(p["vb"], bmask)) for p, w in zip(cks, wbs)]
    wsums = [_dot(w, bones) for w in wbs]
    upds = [jnp.where(bmask, _dot_tn(p["kw"].astype(BF16), p["vb"]), 0.0) for p in cks]

    c_run = c_ref[...]
    n_run = n_ref[...]
    qcs, qns = [], []
    for p, upd in zip(cks, upds):
        qcs.append(_dot(p["qb"], c_run.astype(BF16)))
        qns.append(_dot((p["q"] * n_run).astype(BF16), bones))
        c_run = p["dprev"] * c_run + upd
        n_run = p["dprev"] * n_run + jnp.sum(p["kw"], axis=0, keepdims=True)
    c_ref[...] = c_run
    n_ref[...] = n_run

    obs = []
    for p, num, wsum, qc, qn in zip(cks, nums, wsums, qcs, qns):
        inter = jnp.exp(p["linter"] - p["mt"])
        den = wsum + inter * qn
        hout = (num + inter * qc) / jnp.maximum(jnp.abs(den), jnp.exp(-p["mt"]))
        obs.append(hout * _sigmoid(o_ref[p["rows"], :]))
    msq = [_head_sums(ob * ob, bones) * (1.0 / DH) for ob in obs]
    for p, ob, m2 in zip(cks, obs, msq):
        out_ref[p["rows"], :] = (ob * lax.rsqrt(m2 + EPS) * gh_ref[...]).astype(out_ref.dtype)


def _mlstm64(proj3, gt3, bias_c, bias_r, gh, c0, n0, m0, nck):
    b, t, _ = proj3.shape
    blk = CHUNK * nck
    steps = t // blk
    col = lambda j: pl.BlockSpec((None, blk, GW), lambda bi, s: (bi, s, j))
    state = lambda shp: pl.BlockSpec((None,) + shp, lambda bi, s: (bi,) + (0,) * len(shp))
    return pl.pallas_call(
        functools.partial(_mlstm64_kernel, nck=nck),
        grid=(b, steps),
        in_specs=[
            col(COL_B // GW), col(COL_B // GW + 1), col(COL_B // GW + 2), col(COL_B // GW + 3),
            pl.BlockSpec((None, blk, LANES), lambda bi, s: (bi, s, GATE_BLK)),
            pl.BlockSpec((None, nck, N_GATES, CHUNK), lambda bi, s: (bi, s, 0, 0)),
            pl.BlockSpec((1, LANES), lambda bi, s: (0, 0)),
            pl.BlockSpec((N_GATES, 1), lambda bi, s: (0, 0)),
            pl.BlockSpec((1, GW), lambda bi, s: (0, 0)),
            state((GW, GW)), state((1, GW)), state((1, GW)),
        ],
        out_specs=[
            pl.BlockSpec((None, blk, GW), lambda bi, s: (bi, s, 0)),
            state((GW, GW)), state((1, GW)), state((1, GW)),
        ],
        out_shape=[
            jax.ShapeDtypeStruct((b, t, GW), BF16),
            jax.ShapeDtypeStruct((b, GW, GW), F32),
            jax.ShapeDtypeStruct((b, 1, GW), F32),
            jax.ShapeDtypeStruct((b, 1, GW), F32),
        ],
        compiler_params=_cparams("parallel", "arbitrary"),
        name="mlstm64",
    )(proj3, proj3, proj3, proj3, proj3, gt3, bias_c, bias_r, gh, c0, n0, m0)


def _to_block_diag(c):
    b = c.shape[0]
    eye = jnp.eye(H, dtype=c.dtype)
    return jnp.einsum("bhde,hg->bhdge", c, eye).reshape(b, GW, GW)


def _from_block_diag(cbd):
    b = cbd.shape[0]
    c5 = cbd.reshape(b, H, DH, H, DH)
    return jnp.stack([c5[:, h, :, h, :] for h in range(H)], axis=1)


def _split2(x):
    hi = x.astype(BF16)
    lo = (x - hi.astype(F32)).astype(BF16)
    return hi, lo


def _dot_sp(a, b):
    return _dot(a[0], b[0]) + (_dot(a[0], b[1]) + _dot(a[1], b[0]))


def _unit_lower_inverses(nmats, l):
    eye = (_iota((l, l), 0) == _iota((l, l), 1)).astype(F32)
    ps = [eye - n for n in nmats]
    qs = [_split2(n) for n in nmats]
    qs = [_split2(_dot_sp(q, q)) for q in qs]
    power = 2
    while power < l:
        ps = [p + _dot_sp(_split2(p), q) for p, q in zip(ps, qs)]
        power *= 2
        if power < l:
            qs = [_split2(_dot_sp(q, q)) for q in qs]
    return ps


def _l2norm(x):
    return x * lax.rsqrt(jnp.sum(x * x, axis=-1, keepdims=True) + 1e-6)


def _gdn_kernel(x_ref, z_ref, gc_ref, gr_ref, hist_ref, cw_ref, ac_ref, ar_ref, dc_ref, dr_ref,
                gh_ref, s0_ref, out_ref, s_ref, carry_scr, *, l, nck):
    @pl.when(pl.program_id(1) == 0)
    def _():
        s_ref[...] = s0_ref[...]
        carry_scr[...] = hist_ref[...]

    blk = l * nck
    x = x_ref[...]
    ext = jnp.concatenate([carry_scr[...], x], axis=0)
    carry_scr[...] = x[blk - SUBLANES:, :]
    y = x * cw_ref[3:4, :]
    for j in range(1, 4):
        y = y + ext[SUBLANES - j:SUBLANES - j + blk, :] * cw_ref[3 - j:4 - j, :]
    y = y * _sigmoid(y)

    incl, strict = _tri_masks(l)
    lo_tri = incl.astype(BF16)
    up_tri = (_iota((l, l), 0) <= _iota((l, l), 1)).astype(BF16)

    qraw = [y[ck * l:(ck + 1) * l, h * DH:(h + 1) * DH] for ck in range(nck) for h in range(H)]
    kraw = [y[ck * l:(ck + 1) * l, GW + h * DH:GW + (h + 1) * DH] for ck in range(nck) for h in range(H)]
    vraw = [y[ck * l:(ck + 1) * l, 2 * GW + h * DH:2 * GW + (h + 1) * DH] for ck in range(nck) for h in range(H)]
    qnorm = [_l2norm(a) * (DH ** -0.5) for a in qraw]
    knorm = [_l2norm(a) for a in kraw]
    probs = []
    for ck in range(nck):
        rows = slice(ck * l, (ck + 1) * l)
        gcol = gc_ref[rows, :]
        grow = gr_ref[ck]
        beta_cs = _sigmoid(gcol)
        dec_c = -jnp.exp(ac_ref[...]) * _softplus(gcol + dc_ref[...])
        dec_r = -jnp.exp(ar_ref[...]) * _softplus(grow + dr_ref[...])
        gcs = _cumsum_cols(dec_c, lo_tri)
        grs = _cumsum_rows(dec_r, up_tri)
        for h in range(H):
            beta = beta_cs[:, GATE_OFF + 2 * H + h:GATE_OFF + 2 * H + h + 1]
            g_c = gcs[:, GATE_OFF + 3 * H + h:GATE_OFF + 3 * H + h + 1]
            g_r = grs[3 * H + h:3 * H + h + 1, :]
            q, k, v = qnorm[ck * H + h], knorm[ck * H + h], vraw[ck * H + h]
            decay = jnp.exp(jnp.where(incl, g_c - g_r, NEG_INF))
            eg = jnp.exp(g_c)
            g_last = g_c[l - 1:l, :]
            probs.append(dict(
                rows=rows, h=h, qb=q.astype(BF16), kb=k.astype(BF16), beta=beta, decay=decay,
                rhs=jnp.concatenate([v * beta, k * (beta * eg)], axis=-1),
                qeg=(q * eg).astype(BF16), kdec=(k * jnp.exp(g_last - g_c)).astype(BF16),
                sdec=jnp.exp(g_last)))
    kks = [_dot_nt(p["kb"], p["kb"]) for p in probs]
    qks = [_dot_nt(p["qb"], p["kb"]) for p in probs]
    a_lows = [jnp.where(strict, p["beta"] * kk * p["decay"], 0.0) for p, kk in zip(probs, kks)]
    attns = [(qk * p["decay"]).astype(BF16) for p, qk in zip(probs, qks)]
    tinvs = _unit_lower_inverses(a_lows, l)
    sols = [_dot_sp(_split2(t), _split2(p["rhs"])) for t, p in zip(tinvs, probs)]

    states = [s_ref[h] for h in range(H)]
    for ck in range(nck):
        ps = probs[ck * H:(ck + 1) * H]
        ss = sols[ck * H:(ck + 1) * H]
        at = attns[ck * H:(ck + 1) * H]
        sbs = [s.astype(BF16) for s in states]
        wss = [_dot(sol[:, DH:].astype(BF16), sb) for sol, sb in zip(ss, sbs)]
        qss = [_dot(p["qeg"], sb) for p, sb in zip(ps, sbs)]
        vnbs = [(sol[:, :DH] - ws).astype(BF16) for sol, ws in zip(ss, wss)]
        os_ = [qs + _dot(a, vnb) for qs, a, vnb in zip(qss, at, vnbs)]
        states = [p["sdec"] * s + _dot_tn(p["kdec"], vnb) for p, s, vnb in zip(ps, states, vnbs)]
        for p, o in zip(ps, os_):
            hs = slice(p["h"] * DH, (p["h"] + 1) * DH)
            zg = z_ref[p["rows"], hs]
            yo = (o * lax.rsqrt(jnp.mean(o * o, axis=-1, keepdims=True) + EPS) * gh_ref[:, hs]
                  * (zg * _sigmoid(zg)))
            out_ref[p["rows"], hs] = yo.astype(out_ref.dtype)
    for h in range(H):
        s_ref[h] = states[h]


def _gdn(proj3, gt3, hist8, cw, a_c, a_r, dt_c, dt_r, gh, s0, l, nck):
    b, t, _ = proj3.shape
    blk = l * nck
    steps = t // blk
    state = lambda shp: pl.BlockSpec((None,) + shp, lambda bi, s: (bi,) + (0,) * len(shp))
    const = lambda shp: pl.BlockSpec(shp, lambda bi, s: (0,) * len(shp))
    return pl.pallas_call(
        functools.partial(_gdn_kernel, l=l, nck=nck),
        grid=(b, steps),
        in_specs=[
            pl.BlockSpec((None, blk, 3 * GW), lambda bi, s: (bi, s, COL_CX // (3 * GW))),
            pl.BlockSpec((None, blk, GW), lambda bi, s: (bi, s, COL_CZ // GW)),
            pl.BlockSpec((None, blk, LANES), lambda bi, s: (bi, s, GATE_BLK)),
            pl.BlockSpec((None, nck, N_GATES, l), lambda bi, s: (bi, s, 0, 0)),
            state((SUBLANES, 3 * GW)),
            const((4, 3 * GW)),
            const((1, LANES)), const((N_GATES, 1)), const((1, LANES)), const((N_GATES, 1)),
            const((1, GW)),
            state((H, DH, DH)),
        ],
        out_specs=[
            pl.BlockSpec((None, blk, GW), lambda bi, s: (bi, s, 0)),
            state((H, DH, DH)),
        ],
        out_shape=[
            jax.ShapeDtypeStruct((b, t, GW), BF16),
            jax.ShapeDtypeStruct((b, H, DH, DH), F32),
        ],
        scratch_shapes=[pltpu.VMEM((SUBLANES, 3 * GW), F32)],
        compiler_params=_cparams("parallel", "arbitrary"),
        name="gdn",
    )(proj3, proj3, proj3, gt3, hist8, cw, a_c, a_r, dt_c, dt_r, gh, s0)


def _bd_split(x, mask):
    hi, lo = _split2(x)
    return _block_diag_rows(hi, mask), _block_diag_rows(lo, mask)


def _gdn64_kernel(x_ref, z_ref, gc_ref, gr_ref, hist_ref, cw_ref, ac_ref, ar_ref, dc_ref, dr_ref,
                  gh_ref, s0_ref, out_ref, s_ref, carry_scr, *, nck):
    l = CHUNK

    @pl.when(pl.program_id(1) == 0)
    def _():
        s_ref[...] = s0_ref[...]
        carry_scr[...] = hist_ref[...]

    blk = l * nck
    x = x_ref[...]
    ext = jnp.concatenate([carry_scr[...], x], axis=0)
    carry_scr[...] = x[blk - SUBLANES:, :]
    y = x * cw_ref[3:4, :]
    for j in range(1, 4):
        y = y + ext[SUBLANES - j:SUBLANES - j + blk, :] * cw_ref[3 - j:4 - j, :]
    y = y * _sigmoid(y)

    bmask = _block_mask(GW, GW)
    bones = bmask.astype(BF16)
    within = _iota((GW, GW), 0) % DH <= _iota((GW, GW), 1) % DH
    up_bd = jnp.logical_and(bmask, within).astype(BF16)
    lo_tri = (_iota((l, l), 0) >= _iota((l, l), 1)).astype(BF16)
    key_pos = _iota((l, GW), 1) % DH
    incl = _iota((l, GW), 0) >= key_pos
    strict = _iota((l, GW), 0) > key_pos
    eye_t = (_iota((l, GW), 0) == key_pos).astype(F32)
    e_b = _expander(GATE_OFF + 2 * H)
    e_a = _expander(GATE_OFF + 3 * H)

    def head_sums(a):
        return _head_sums(a, bones)

    def shared_rhs(lhs_splits, rhs_bd):
        n = len(lhs_splits)
        big = _dot(jnp.concatenate([part for sp in lhs_splits for part in sp], axis=0), rhs_bd[0])
        small = _dot(jnp.concatenate([sp[0] for sp in lhs_splits], axis=0), rhs_bd[1])
        return [big[2 * i * l:(2 * i + 1) * l] + big[(2 * i + 1) * l:(2 * i + 2) * l]
                + small[i * l:(i + 1) * l] for i in range(n)]

    yq, yk, yv = y[:, :GW], y[:, GW:2 * GW], y[:, 2 * GW:]
    qn_all = yq * lax.rsqrt(head_sums(yq * yq) + 1e-6) * (DH ** -0.5)
    kn_all = yk * lax.rsqrt(head_sums(yk * yk) + 1e-6)

    cks = []
    for ck in range(nck):
        rows = slice(ck * l, (ck + 1) * l)
        gcol = gc_ref[rows, :]
        dec_c = -jnp.exp(ac_ref[...]) * _softplus(gcol + dc_ref[...])
        beta = _expand(_sigmoid(gcol), e_b)
        g_c = _expand(_cumsum_cols_wide(dec_c, lo_tri), e_a)
        grow_t = jnp.concatenate([gr_ref[ck]] * H, axis=1)
        dec_r = -jnp.exp(ar_ref[...]) * _softplus(grow_t + dr_ref[...])
        g_r = _row_select(_dot_stacked(_split3(dec_r), up_bd), 3 * H)
        decay = jnp.exp(jnp.where(incl, g_c - g_r, NEG_INF))
        eg = jnp.exp(g_c)
        g_last = g_c[l - 1:l, :]
        q, k, v = qn_all[rows], kn_all[rows], yv[rows]
        cks.append(dict(rows=rows, qb=q.astype(BF16), kb=k.astype(BF16),
                        beta=beta, decay=decay, rhs_v=v * beta, rhs_k=k * (beta * eg),
                        qeg=(q * eg).astype(BF16), kdec=(k * jnp.exp(g_last - g_c)).astype(BF16),
                        sdec=jnp.exp(g_last)))
    kqs = [_dot_nt(jnp.concatenate([p["kb"], p["qb"]], axis=0), _block_diag_rows(p["kb"], bmask))
           for p in cks]
    nmats = [jnp.where(strict, p["beta"] * kq[:l] * p["decay"], 0.0) for p, kq in zip(cks, kqs)]
    attns = [(kq[l:] * p["decay"]).astype(BF16) for p, kq in zip(cks, kqs)]

    ps = [eye_t - n for n in nmats]
    qs = [shared_rhs([_split2(n)], _bd_split(n, bmask))[0] for n in nmats]
    power = 2
    while power < l:
        power *= 2
        if power < l:
            res = [shared_rhs([_split2(p), _split2(q)], _bd_split(q, bmask)) for p, q in zip(ps, qs)]
            ps = [p + r[0] for p, r in zip(ps, res)]
            qs = [r[1] for r in res]
        else:
            ps = [p + shared_rhs([_split2(p)], _bd_split(q, bmask))[0] for p, q in zip(ps, qs)]
    tsp = [_split2(p) for p in ps]
    us = [shared_rhs([t], _bd_split(p["rhs_v"], bmask))[0] for t, p in zip(tsp, cks)]
    ws = [shared_rhs([t], _bd_split(p["rhs_k"], bmask))[0].astype(BF16) for t, p in zip(tsp, cks)]

    wu = [jnp.concatenate([w, u.astype(BF16)], axis=1) for w, u in zip(ws, us)]
    kwu = [_dot_tn(p["kdec"], x) for p, x in zip(cks, wu)]
    awu = [_dot(at, jnp.concatenate([_block_diag_rows(x[:, :GW], bmask),
                                     _block_diag_rows(x[:, GW:], bmask)], axis=1))
           for at, x in zip(attns, wu)]
    gmats = [jnp.where(bmask, x[:, :GW], 0.0).astype(BF16) for x in kwu]
    bmats = [jnp.where(bmask, x[:, GW:], 0.0) for x in kwu]
    qts = [(p["qeg"].astype(F32) - x[:, :GW]).astype(BF16) for p, x in zip(cks, awu)]

    s_run = s_ref[...]
    outs = []
    for p, g, bm, qt, x in zip(cks, gmats, bmats, qts, awu):
        ys = _dot(jnp.concatenate([g, qt], axis=0), s_run.astype(BF16))
        outs.append(ys[GW:] + x[:, GW:])
        s_run = p["sdec"] * s_run - ys[:GW] + bm
    s_ref[...] = s_run

    msq = [head_sums(o * o) * (1.0 / DH) for o in outs]
    for p, o, m2 in zip(cks, outs, msq):
        zg = z_ref[p["rows"], :]
        yo = o * lax.rsqrt(m2 + EPS) * gh_ref[...] * (zg * _sigmoid(zg))
        out_ref[p["rows"], :] = yo.astype(out_ref.dtype)


def _gdn64(proj3, gt3, hist8, cw, a_c, a_r, dt_c, dt_r, gh, s0, nck):
    b, t, _ = proj3.shape
    blk = CHUNK * nck
    steps = t // blk
    state = lambda shp: pl.BlockSpec((None,) + shp, lambda bi, s: (bi,) + (0,) * len(shp))
    const = lambda shp: pl.BlockSpec(shp, lambda bi, s: (0,) * len(shp))
    return pl.pallas_call(
        functools.partial(_gdn64_kernel, nck=nck),
        grid=(b, steps),
        in_specs=[
            pl.BlockSpec((None, blk, 3 * GW), lambda bi, s: (bi, s, COL_CX // (3 * GW))),
            pl.BlockSpec((None, blk, GW), lambda bi, s: (bi, s, COL_CZ // GW)),
            pl.BlockSpec((None, blk, LANES), lambda bi, s: (bi, s, GATE_BLK)),
            pl.BlockSpec((None, nck, N_GATES, CHUNK), lambda bi, s: (bi, s, 0, 0)),
            state((SUBLANES, 3 * GW)),
            const((4, 3 * GW)),
            const((1, LANES)), const((N_GATES, 1)), const((1, LANES)), const((N_GATES, 1)),
            const((1, GW)),
            state((GW, GW)),
        ],
        out_specs=[
            pl.BlockSpec((None, blk, GW), lambda bi, s: (bi, s, 0)),
            state((GW, GW)),
        ],
        out_shape=[
            jax.ShapeDtypeStruct((b, t, GW), BF16),
            jax.ShapeDtypeStruct((b, GW, GW), F32),
        ],
        scratch_shapes=[pltpu.VMEM((SUBLANES, 3 * GW), F32)],
        compiler_params=_cparams("parallel", "arbitrary"),
        name="gdn64",
    )(proj3, proj3, proj3, gt3, hist8, cw, a_c, a_r, dt_c, dt_r, gh, s0)


HEAD_PAD = 128
DPAD = H * HEAD_PAD
BF16_ROWS = 16
VT_PAD = -(-(DH + 1) // BF16_ROWS) * BF16_ROWS
VT_ROWS = H * VT_PAD


def _tile_heads(t):
    return jnp.concatenate([t] * H, axis=-1)


def _dprep_kernel(tail_ref, gq_ref, gkv_ref, wq_ref, wqp_ref, ka_ref, ckv_ref, kpe_ref, qc_ref):
    ka = ka_ref[...]
    kb = pltpu.roll(ka, 64, axis=1)
    nope = _iota(ka.shape, 1) < QK_NOPE
    qcos = jnp.where(nope, 1.0, kb)
    qsin = jnp.where(nope, 0.0, ka)
    hq = _rms(tail_ref[:, :Q_LORA], gq_ref[...]).astype(BF16)
    qc = _dot(hq, wq_ref[...]) * _tile_heads(qcos) + _dot(hq, wqp_ref[...]) * _tile_heads(qsin)
    qc_ref[...] = (qc * (MLA_SCALE * LOG2E)).astype(BF16)
    ckv_ref[...] = _rms(tail_ref[:, Q_LORA:Q_LORA + KV_LORA], gkv_ref[...])
    kr = tail_ref[:, Q_LORA + KV_LORA:]
    kpe = kr * ka + pltpu.roll(kr, 64, axis=1) * kb
    kpe_ref[...] = kpe[:, :QK_ROPE]


def _dprep(proj, gq, gkv, wq, wqp, ka, tm):
    n = proj.shape[0]
    row = lambda w: pl.BlockSpec((tm, w), lambda i: (i, 0))
    const = lambda a, b: pl.BlockSpec((a, b), lambda i: (0, 0))
    return pl.pallas_call(
        _dprep_kernel,
        grid=(n // tm,),
        in_specs=[
            pl.BlockSpec((tm, TAIL_W), lambda i: (i, COL_TAIL // TAIL_W)),
            const(1, Q_LORA), const(1, KV_LORA), const(Q_LORA, DPAD), const(Q_LORA, DPAD),
            row(LANES),
        ],
        out_specs=[row(KV_LORA), row(QK_ROPE), row(DPAD)],
        out_shape=[
            jax.ShapeDtypeStruct((n, KV_LORA), F32),
            jax.ShapeDtypeStruct((n, QK_ROPE), F32),
            jax.ShapeDtypeStruct((n, DPAD), BF16),
        ],
        compiler_params=_cparams("parallel"),
        name="dprep",
    )(proj, gq, gkv, wq, wqp, ka)


LOG2E = 1.4426950408889634


def _kvup_t_kernel(ckv_ref, kpe_ref, wk_ref, wvt_ref, pm_ref, onet_ref, kc_ref, vt_ref):
    c = ckv_ref[...].astype(BF16)
    kc_ref[...] = (_dot(c, wk_ref[...]) + _dot(kpe_ref[...].astype(BF16), pm_ref[...])).astype(BF16)
    vt_ref[...] = (_dot_nt(wvt_ref[...], c) + onet_ref[...]).astype(BF16)


def _kvup_t(ckv, kpe, wk, wvt, pm, onet, tm):
    m = ckv.shape[0]
    row = lambda w: pl.BlockSpec((tm, w), lambda i: (i, 0))
    const = lambda a, b: pl.BlockSpec((a, b), lambda i: (0, 0))
    return pl.pallas_call(
        _kvup_t_kernel,
        grid=(m // tm,),
        in_specs=[row(KV_LORA), row(QK_ROPE), const(KV_LORA, DPAD), const(VT_ROWS, KV_LORA),
                  const(QK_ROPE, DPAD), const(VT_ROWS, 1)],
        out_specs=[row(DPAD), pl.BlockSpec((VT_ROWS, tm), lambda i: (0, i))],
        out_shape=[jax.ShapeDtypeStruct((m, DPAD), BF16), jax.ShapeDtypeStruct((VT_ROWS, m), BF16)],
        compiler_params=_cparams("parallel"),
        name="kvup_t",
    )(ckv, kpe, wk, wvt, pm, onet)


def _mla_prompt_kernel(qi_ref, ki_ref, q_ref, k_ref, vt_ref, ghc_ref, out_ref, m_scr, acc_scr, *, bq, bk,
                       qw, ahead):
    p = pl.program_id(0)
    q_i = qi_ref[p]
    k_i = ki_ref[p]
    last = (q_i * bq) // bk

    @pl.when(k_i == 0)
    def _():
        m_scr[...] = jnp.full(m_scr.shape, NEG_INF, F32)
        acc_scr[...] = jnp.zeros(acc_scr.shape, F32)

    def step(diag):
        if diag:
            key_chunk = k_i * (bk // CHUNK) + _iota((bk, bq), 0) // CHUNK
            qry_chunk = q_i * (bq // CHUNK) + _iota((bk, bq), 1) // CHUNK
            allowed = key_chunk <= qry_chunk

        units = [(h, c) for h in range(H) for c in range(bq // qw)]
        rc = min(bk, 64)

        def scores(u):
            h, c = u
            hs = slice(h * HEAD_PAD, (h + 1) * HEAD_PAD)
            return _dot_nt(k_ref[:, hs], q_ref[c * qw:(c + 1) * qw, hs])

        def update(u, st):
            h, c = u
            hs = slice(h * HEAD_PAD, (h + 1) * HEAD_PAD)
            qs = slice(c * qw, (c + 1) * qw)
            if diag:
                st = jnp.where(allowed[:, qs], st, NEG_INF)
            m_prev = m_scr[h, :, qs]
            mx = st[:rc]
            for r in range(1, bk // rc):
                mx = jnp.maximum(mx, st[r * rc:(r + 1) * rc])
            m_new = jnp.maximum(m_prev, jnp.max(mx, axis=0, keepdims=True))
            alpha = jnp.exp2(m_prev - m_new)[0:1]
            m_row = m_new[0:1]
            pt = jnp.concatenate([jnp.exp2(st[r * rc:(r + 1) * rc] - m_row).astype(BF16)
                                  for r in range(bk // rc)], axis=0)
            acc = alpha * acc_scr[h, :, qs] + _dot(vt_ref[h * VT_PAD:(h + 1) * VT_PAD, :], pt)
            if not diag:
                m_scr[h, :, qs] = m_new
                acc_scr[h, :, qs] = acc
            return acc

        accs = []
        pending = [scores(u) for u in units[:ahead]]
        for idx, u in enumerate(units):
            if idx + ahead < len(units):
                pending.append(scores(units[idx + ahead]))
            accs.append(update(u, pending.pop(0)))
        per_head = bq // qw
        return [jnp.concatenate(accs[h * per_head:(h + 1) * per_head], axis=1) for h in range(H)]

    @pl.when(k_i < last)
    def _():
        step(False)

    @pl.when(k_i == last)
    def _():
        ys = []
        for h, acc in enumerate(step(True)):
            o = acc[:DH] / acc[DH:DH + 1]
            ms = jnp.mean(o * o, axis=0, keepdims=True)
            ys.append(o * lax.rsqrt(ms + EPS) * ghc_ref[h * DH:(h + 1) * DH, :])
        out_ref[...] = jnp.concatenate(ys, axis=0).T.astype(out_ref.dtype)


def _mla_prompt(qc, kc, vt, ghc, bq, bk):
    t = qc.shape[0]
    assert t % bq == 0 and t % bk == 0 and bk % bq == 0
    pairs = [(i, j) for i in range(t // bq) for j in range((i * bq) // bk + 1)]
    qi = jnp.asarray([i for i, _ in pairs], jnp.int32)
    ki = jnp.asarray([j for _, j in pairs], jnp.int32)
    grid_spec = pltpu.PrefetchScalarGridSpec(
        num_scalar_prefetch=2,
        grid=(len(pairs),),
        in_specs=[
            pl.BlockSpec((bq, DPAD), lambda p, qi, ki: (qi[p], 0)),
            pl.BlockSpec((bk, DPAD), lambda p, qi, ki: (ki[p], 0)),
            pl.BlockSpec((VT_ROWS, bk), lambda p, qi, ki: (0, ki[p])),
            pl.BlockSpec((GW, 1), lambda p, qi, ki: (0, 0)),
        ],
        out_specs=pl.BlockSpec((bq, GW), lambda p, qi, ki: (qi[p], 0)),
        scratch_shapes=[
            pltpu.VMEM((H, SUBLANES, bq), F32),
            pltpu.VMEM((H, VT_PAD, bq), F32),
        ],
    )
    return pl.pallas_call(
        functools.partial(_mla_prompt_kernel, bq=bq, bk=bk, qw=min(bq, 256), ahead=2),
        grid_spec=grid_spec,
        out_shape=jax.ShapeDtypeStruct((t, GW), BF16),
        compiler_params=_cparams("arbitrary"),
        name="mla_prompt",
    )(qi, ki, qc, kc, vt, ghc)


LAT_W = 2 * LANES
ONE_LANE = KV_LORA + QK_ROPE


def _mla_sample_kernel(q_ref, ckvp_ref, kpep_ref, ckvn_ref, kpen_ref, wabs_ref, wv_ref, gh_ref, out_ref):
    s_len = q_ref.shape[0]

    def latent_rows(ckv_ref, kpe_ref):
        n = ckv_ref.shape[0]
        tail = jnp.concatenate([kpe_ref[...].astype(BF16), jnp.zeros((n, LANES - QK_ROPE), BF16)], axis=1)
        tail = jnp.where(_iota((n, LANES), 1) == QK_ROPE, jnp.ones((), BF16), tail)
        return jnp.concatenate([ckv_ref[...].astype(BF16), tail], axis=1)

    kvp = latent_rows(ckvp_ref, kpep_ref)
    kvn = latent_rows(ckvn_ref, kpen_ref)
    qabs = jnp.concatenate([_dot(q_ref[:, h * HEAD_PAD:(h + 1) * HEAD_PAD], wabs_ref[h]) for h in range(H)],
                           axis=0).astype(BF16)
    s1 = _dot_nt(qabs, kvp)
    s2 = _dot_nt(qabs, kvn)
    m = jnp.maximum(jnp.max(s1, axis=-1, keepdims=True), jnp.max(s2, axis=-1, keepdims=True))
    acc = _dot(jnp.exp2(s1 - m).astype(BF16), kvp) + _dot(jnp.exp2(s2 - m).astype(BF16), kvn)
    olat = (acc[:, :KV_LORA] / acc[:, ONE_LANE:ONE_LANE + 1]).astype(BF16)
    rows = slice(0, s_len)
    for h in range(H):
        o = _dot(olat[h * s_len:(h + 1) * s_len], wv_ref[h])
        _head_norm_store(out_ref, rows, h, o, gh_ref)


def _mla_sample(qc3, ckvp, kpep, ckvn, kpen, wabs, wv, gh):
    b, s, _ = qc3.shape
    npast = ckvp.shape[1]
    per_b = lambda r, w: pl.BlockSpec((None, r, w), lambda i: (i, 0, 0))
    const = lambda shp: pl.BlockSpec(shp, lambda i: (0,) * len(shp))
    return pl.pallas_call(
        _mla_sample_kernel,
        grid=(b,),
        in_specs=[per_b(s, DPAD), per_b(npast, KV_LORA), per_b(npast, QK_ROPE), per_b(s, KV_LORA),
                  per_b(s, QK_ROPE), const((H, HEAD_PAD, LAT_W)), const((H, KV_LORA, DH)), const((1, GW))],
        out_specs=per_b(s, GW),
        out_shape=jax.ShapeDtypeStruct((b, s, GW), BF16),
        compiler_params=_cparams("parallel"),
        name="mla_sample",
    )(qc3, ckvp, kpep, ckvn, kpen, wabs, wv, gh)


def _ffn_kernel(*refs, seq_len, final_norm, tf):
    mix_refs, refs = refs[1:5], refs[:1] + refs[5:]
    if seq_len is None:
        (x_ref, wo_ref, g_ref, wup_ref, cw_ref, wd_ref, gf_ref, out_ref, ga_ref,
         act_scr, x2_scr, carry_scr) = refs
    else:
        (x_ref, wo_ref, g_ref, wup_ref, cw_ref, wd_ref, gf_ref, h1_ref, h2_ref, out_ref, ga_ref,
         act_scr, x2_scr) = refs
    tm = x_ref.shape[0]
    nj = D_FF // tf
    x2 = x_ref[...]
    for grp, m_ref in enumerate(mix_refs):
        x2 = x2 + _dot(m_ref[...], wo_ref[grp * GW:(grp + 1) * GW, :])
    x2_scr[...] = x2
    h = _rms(x2, g_ref[...]).astype(BF16)
    row = _iota((tm, tf), 0)

    if seq_len is None:
        @pl.when(pl.program_id(0) == 0)
        def _():
            carry_scr[...] = jnp.zeros(carry_scr.shape, F32)

    def up(j):
        cols = slice(j * tf, (j + 1) * tf)
        ucols = slice(D_FF + j * tf, D_FF + (j + 1) * tf)
        return _dot(h, wup_ref[:, cols]), _dot(h, wup_ref[:, ucols])

    def gate(j, ga, u):
        cols = slice(j * tf, (j + 1) * tf)
        r1 = pltpu.roll(ga, 1, axis=0)
        r2 = pltpu.roll(ga, 2, axis=0)
        if seq_len is None:
            c1 = carry_scr[SUBLANES - 1:SUBLANES, cols]
            c2 = carry_scr[SUBLANES - 2:SUBLANES - 1, cols]
            prev1 = jnp.where(row >= 1, r1, c1)
            prev2 = jnp.where(row >= 2, r2, jnp.where(row == 1, c1, c2))
            tail = ga[tm - SUBLANES:, :]
            carry_scr[:, cols] = tail
            ga_ref[:, cols] = tail
        else:
            t = row % seq_len
            prev1 = jnp.where(t >= 1, r1, h1_ref[:, cols])
            prev2 = jnp.where(t >= 2, r2, h2_ref[:, cols])
            ga_ref[:, cols] = ga
        conv = prev2 * cw_ref[0:1, cols] + prev1 * cw_ref[1:2, cols] + ga * cw_ref[2:3, cols]
        act_scr[:, cols] = (conv * _sigmoid(conv) * u).astype(BF16)

    pending = up(0)
    for j in range(nj):
        nxt = up(j + 1) if j + 1 < nj else None
        gate(j, *pending)
        pending = nxt
    y = x2_scr[...] + _dot(act_scr[...], wd_ref[...])
    if final_norm:
        y = _rms(y, gf_ref[...])
    out_ref[...] = y


def _ffn(x, mixed, w_out, g, w_up, cw, w_down, gf, h1, h2, *, tm, tf, seq_len, final_norm):
    n = x.shape[0]
    ni = n // tm
    resident = lambda a, b: pl.BlockSpec((a, b), lambda i: (0, 0), pipeline_mode=pl.Buffered(1))
    mix_spec = pl.BlockSpec((tm, GW), lambda i: (i, 0))
    in_specs = [
        pl.BlockSpec((tm, D_MODEL), lambda i: (i, 0)),
        mix_spec, mix_spec, mix_spec, mix_spec,
        resident(D_MODEL, D_MODEL),
        resident(1, D_MODEL),
        resident(D_MODEL, 2 * D_FF),
        resident(3, D_FF),
        resident(D_FF, D_MODEL),
        resident(1, D_MODEL),
    ]
    args = [x, *mixed, w_out, g, w_up, cw, w_down, gf]
    scratch = [pltpu.VMEM((tm, D_FF), BF16), pltpu.VMEM((tm, D_MODEL), F32)]
    if seq_len is None:
        ga_spec = pl.BlockSpec((None, SUBLANES, D_FF), lambda i: (i, 0, 0))
        ga_shape = jax.ShapeDtypeStruct((ni, SUBLANES, D_FF), F32)
        scratch.append(pltpu.VMEM((SUBLANES, D_FF), F32))
    else:
        in_specs += [pl.BlockSpec((tm, D_FF), lambda i: (i, 0))] * 2
        args += [h1, h2]
        ga_spec = pl.BlockSpec((tm, D_FF), lambda i: (i, 0))
        ga_shape = jax.ShapeDtypeStruct((n, D_FF), F32)
    return pl.pallas_call(
        functools.partial(_ffn_kernel, seq_len=seq_len, final_norm=final_norm, tf=tf),
        grid=(ni,),
        in_specs=in_specs,
        out_specs=[pl.BlockSpec((tm, D_MODEL), lambda i: (i, 0)), ga_spec],
        out_shape=[jax.ShapeDtypeStruct((n, D_MODEL), F32), ga_shape],
        scratch_shapes=scratch,
        compiler_params=_cparams("arbitrary"),
        name="ffn",
    )(*args)


def _rope_tables(offset, t):
    half = QK_ROPE // 2
    per_row = LANES // half
    assert t % per_row == 0
    inv = ROPE_THETA ** (-jnp.arange(half, dtype=F32) / half)
    pos = offset + per_row * _iota((t // per_row, LANES), 0) + _iota((t // per_row, LANES), 1) // half
    ang = pos.astype(F32) * jnp.tile(inv, per_row)[None, :]
    cos, sin = jnp.cos(ang).reshape(t, half), jnp.sin(ang).reshape(t, half)
    z32 = jnp.zeros((t, 32), F32)
    return jnp.concatenate([cos, cos, z32, -sin, sin, z32], -1)


def _rel_bias(table, n_past, n_q, n_k):
    period = n_q + n_k
    jmi = jnp.concatenate([jnp.arange(0, n_k + 1), jnp.arange(-(n_q - 1), 0)])
    diag = table[:, jnp.clip(n_past - jmi, -REL_MAX, REL_MAX) + REL_MAX]
    rows = jnp.tile(diag, (1, n_q))[:, :n_q * (period - 1)].reshape(table.shape[0], n_q, period - 1)
    return rows[:, :, :n_k]


def _swap_halves(w):
    half = w.shape[-1] // 2
    return jnp.concatenate([w[..., half:], w[..., :half]], -1)


def _layer_weights(lw):
    (g_mix, w_in, a_rel_bias, b_i_bias, b_f_bias, c_conv_w, c_a_log, c_dt_bias,
     d_g_q, d_w_q_up, d_g_kv, d_w_kv_up, g_head, w_out, g_ffn, w_up, f_conv_w, w_down) = lw
    o = 0
    cols = {}
    for name, size in (("a", 3 * GW), ("b", 4 * GW), ("bg", 2 * H), ("c", 3 * GW), ("cz", GW),
                       ("cg", 2 * H), ("dq", Q_LORA), ("dkv", KV_LORA), ("dkr", QK_ROPE)):
        cols[name] = w_in[:, o:o + size]
        o += size
    gates = jnp.concatenate([cols["bg"], cols["cg"]], -1)
    pad16 = jnp.zeros((D_MODEL, 16), F32)
    pad32 = jnp.zeros((D_MODEL, 32), F32)
    w_perm = jnp.concatenate([cols["c"], cols["a"], cols["dq"], cols["dkv"], cols["dkr"], gates, pad16,
                              _swap_halves(cols["dkr"]), pad32, cols["b"], cols["cz"]], -1)
    zc = lambda n: jnp.zeros((1, n), F32)
    zr = lambda n: jnp.zeros((n, 1), F32)
    bias_c = jnp.concatenate([zc(GATE_OFF), b_i_bias[None], b_f_bias[None], zc(LANES - GATE_OFF - 2 * H)], -1)
    bias_r = jnp.concatenate([b_i_bias[:, None], b_f_bias[:, None], zr(2 * H)], 0)
    alog_c = jnp.concatenate([zc(GATE_OFF + 3 * H), c_a_log[None], zc(LANES - GATE_OFF - 4 * H)], -1)
    alog_r = jnp.concatenate([zr(3 * H), c_a_log[:, None]], 0)
    dt_c = jnp.concatenate([zc(GATE_OFF + 3 * H), c_dt_bias[None], zc(LANES - GATE_OFF - 4 * H)], -1)
    dt_r = jnp.concatenate([zr(3 * H), c_dt_bias[:, None]], 0)

    wq = d_w_q_up.reshape(Q_LORA, H, QK_NOPE + QK_ROPE)
    z_h32 = jnp.zeros((Q_LORA, H, 32), F32)
    wq_full = jnp.concatenate([wq, z_h32], -1).reshape(Q_LORA, DPAD)
    wq_part = jnp.concatenate([jnp.zeros((Q_LORA, H, QK_NOPE), F32), _swap_halves(wq[..., QK_NOPE:]), z_h32],
                              -1).reshape(Q_LORA, DPAD)
    wkv = d_w_kv_up.reshape(KV_LORA, H, 2 * DH)
    z_h64 = jnp.zeros((KV_LORA, H, DH), F32)
    wk_full = jnp.concatenate([wkv[..., :DH], z_h64], -1).reshape(KV_LORA, DPAD)
    place = jnp.concatenate([jnp.zeros((QK_ROPE, QK_NOPE), F32), jnp.eye(QK_ROPE, dtype=F32),
                             jnp.zeros((QK_ROPE, 32), F32)], -1)
    pmat = jnp.concatenate([place] * H, -1)
    wk_t = jnp.transpose(wkv[..., :DH], (1, 2, 0))
    rope_rows = jnp.concatenate([jnp.zeros((QK_ROPE, KV_LORA), F32), jnp.eye(QK_ROPE, dtype=F32),
                                 jnp.zeros((QK_ROPE, LAT_W - KV_LORA - QK_ROPE), F32)], -1)
    wabs = jnp.concatenate([
        jnp.concatenate([wk_t, jnp.zeros((H, QK_NOPE, LAT_W - KV_LORA), F32)], -1),
        jnp.broadcast_to(rope_rows, (H, QK_ROPE, LAT_W)),
        jnp.zeros((H, HEAD_PAD - QK_NOPE - QK_ROPE, LAT_W), F32)], 1)
    return dict(
        g_mix=g_mix[None], w_in=w_perm.astype(BF16), w_gt=gates.T.astype(BF16),
        table=a_rel_bias, bias_c=bias_c, bias_r=bias_r, alog_c=alog_c, alog_r=alog_r, dt_c=dt_c, dt_r=dt_r,
        c_conv_w=c_conv_w, g_q=d_g_q[None], g_kv=d_g_kv[None],
        wq=wq_full.astype(BF16), wqp=wq_part.astype(BF16), wk=wk_full.astype(BF16),
        pmat=pmat.astype(BF16), wabs=wabs.astype(BF16),
        wv_heads=jnp.transpose(wkv[..., DH:], (1, 0, 2)).astype(BF16),
        wvt=jnp.concatenate([wkv[..., DH:], jnp.zeros((KV_LORA, H, VT_PAD - DH), F32)], -1)
        .reshape(KV_LORA, VT_ROWS).T.astype(BF16),
        vonest=(jnp.arange(VT_ROWS) % VT_PAD == DH).astype(F32)[:, None],
        g_head=g_head.reshape(4, 1, GW), w_out=w_out.astype(BF16),
        g_ffn=g_ffn[None], w_up=w_up.astype(BF16), f_conv_w=f_conv_w, w_down=w_down.astype(BF16))


def _gates_t3(gt, b, t, l):
    return gt.reshape(N_GATES, b, t // l, l).transpose(1, 2, 0, 3)


def _layer(x, offset, st, w, gf, final_norm, cfg):
    b, t, _ = x.shape
    n = b * t
    first = st is None
    x2 = x.reshape(n, D_MODEL)
    proj, gt = _inproj(x2, w["g_mix"], w["w_in"], w["w_gt"], cfg["tm"])
    proj3 = proj.reshape(b, t, PROJ_W)
    gh = w["g_head"]
    l = min(t, CHUNK)
    gt3 = _gates_t3(gt, b, t, l)

    new_ak = proj3[:, t - min(A_PAST, t):, COL_A + GW:COL_A + 2 * GW].reshape(b, -1, H, DH)
    new_av = proj3[:, t - min(A_PAST, t):, COL_A + 2 * GW:COL_A + 3 * GW].reshape(b, -1, H, DH)
    if first:
        bias = _rel_bias(w["table"], A_PAST, CHUNK, A_PAST + CHUNK)
        oa = _band_prompt(proj, bias, gh[0].T)
    else:
        npast = st[0].shape[1]
        bias = _rel_bias(w["table"], npast, t, npast + t)
        oa = _band_sample(proj3, st[0].reshape(b, npast, GW), st[1].reshape(b, npast, GW), bias, gh[0])
        oa = oa.reshape(n, GW)

    if first:
        c0 = jnp.zeros((b, H, DH, DH), F32)
        n0 = jnp.zeros((b, H, DH), F32)
        m0 = jnp.zeros((b, 1, H), F32)
    else:
        c0, n0, m0 = st[2], st[3], st[4][:, None, :]
    if l == CHUNK:
        ob, cbd, nrow, mrow = _mlstm64(proj3, gt3, w["bias_c"], w["bias_r"], gh[1], _to_block_diag(c0),
                                       n0.reshape(b, 1, GW), jnp.repeat(m0, DH, axis=-1), cfg["nck"])
        new_bc, new_bn, new_bm = _from_block_diag(cbd), nrow.reshape(b, H, DH), mrow[:, 0, ::DH]
    else:
        ob, new_bc, new_bn, new_bm = _mlstm(proj3, gt3, w["bias_c"], w["bias_r"], gh[1], c0, n0, m0,
                                            l, cfg["nck"])
        new_bm = new_bm[:, 0, :]

    if first:
        hist8 = jnp.zeros((b, SUBLANES, 3 * GW), F32)
        s0 = jnp.zeros((b, H, DH, DH), F32)
    else:
        hist8 = jnp.concatenate([jnp.zeros((b, SUBLANES - 3, 3 * GW), F32), st[6]], 1)
        s0 = st[5]
    gdn_args = (proj3, gt3, hist8, w["c_conv_w"], w["alog_c"], w["alog_r"], w["dt_c"], w["dt_r"], gh[2])
    if l == CHUNK:
        oc, sbd = _gdn64(*gdn_args, _to_block_diag(s0), cfg["nck"])
        new_cs = _from_block_diag(sbd)
    else:
        oc, new_cs = _gdn(*gdn_args, s0, l, cfg["nck"])
    new_cconv = proj3[:, t - 3:, COL_CX:COL_CX + 3 * GW]

    ka = jnp.tile(_rope_tables(offset, t), (b, 1))
    ckv, kpe, qc = _dprep(proj, w["g_q"], w["g_kv"], w["wq"], w["wqp"], ka, cfg["tm_d"])
    if first:
        kc, vt = _kvup_t(ckv, kpe, w["wk"], w["wvt"], w["pmat"], w["vonest"], cfg["tm_kv"])
        od = _mla_prompt(qc, kc, vt, gh[3].T, cfg["mla_bq"], cfg["mla_bk"])
    else:
        od = _mla_sample(qc.reshape(b, t, DPAD), st[7], st[8], ckv.reshape(b, t, KV_LORA),
                         kpe.reshape(b, t, QK_ROPE), w["wabs"], w["wv_heads"], gh[3])
        od = od.reshape(n, GW)

    ffn_args = (x2, (oa, ob.reshape(n, GW), oc.reshape(n, GW), od), w["w_out"],
                w["g_ffn"], w["w_up"], w["f_conv_w"], w["w_down"], gf)
    if first:
        y, ga_tail = _ffn(*ffn_args, None, None,
                          tm=cfg["tm"], tf=cfg["tf"], seq_len=None, final_norm=final_norm)
        new_fconv = ga_tail[-1, SUBLANES - 2:, :][None]
    else:
        hist = st[9]
        zrow = jnp.zeros((b, t - 1, D_FF), F32)
        h1 = jnp.concatenate([hist[:, 1:2], zrow], 1).reshape(n, D_FF)
        h2 = jnp.concatenate([hist, zrow[:, 1:]], 1).reshape(n, D_FF)
        y, ga = _ffn(*ffn_args, h1, h2,
                     tm=cfg["tm"], tf=cfg["tf"], seq_len=t, final_norm=final_norm)
        new_fconv = ga.reshape(b, t, D_FF)[:, t - 2:]
    state = (new_ak, new_av, new_bc, new_bn, new_bm, new_cs, new_cconv,
             ckv.reshape(b, t, KV_LORA), kpe.reshape(b, t, QK_ROPE), new_fconv)
    return y.reshape(b, t, D_MODEL), state


def _config(b, t):
    n = b * t
    tm = min(n, 1024)
    return dict(tm=tm, tf=256, nck=1 if t <= CHUNK else 8,
                tm_d=min(n, 1024), tm_kv=min(n, 2048), mla_bq=min(t, 512), mla_bk=min(t, 1024))


def kernel(x_prompt, x_sample, cache_a_k, cache_a_v, state_b_c, state_b_n, state_b_m, state_c_s, cache_c_conv, cache_d_ckv, cache_d_kpe, cache_ffn_conv, g_mix, w_in, a_rel_bias, b_i_bias, b_f_bias, c_conv_w, c_a_log, c_dt_bias, d_g_q, d_w_q_up, d_g_kv, d_w_kv_up, g_head, w_out, g_ffn, w_up, f_conv_w, w_down, g_final):
    layer_w = (g_mix, w_in, a_rel_bias, b_i_bias, b_f_bias, c_conv_w, c_a_log, c_dt_bias,
               d_g_q, d_w_q_up, d_g_kv, d_w_kv_up, g_head, w_out, g_ffn, w_up, f_conv_w, w_down)
    depth = g_mix.shape[0]
    past = cache_d_ckv.shape[2]
    xp, xs = x_prompt, x_sample
    cfg_p = _config(*x_prompt.shape[:2])
    cfg_s = _config(*x_sample.shape[:2])
    gf = g_final[None]
    new_p, new_s = [], []
    for l in range(depth):
        w = _layer_weights(tuple(a[l] for a in layer_w))
        last = l == depth - 1
        xp, sp_l = _layer(xp, 0, None, w, gf, last, cfg_p)
        st = (cache_a_k[l], cache_a_v[l], state_b_c[l], state_b_n[l], state_b_m[l],
              state_c_s[l], cache_c_conv[l], cache_d_ckv[l], cache_d_kpe[l], cache_ffn_conv[l])
        xs, ss_l = _layer(xs, past, st, w, gf, last, cfg_s)
        new_p.append(sp_l)
        new_s.append(ss_l)
    outs = [xp, xs]
    for i in range(10):
        outs.append(jnp.stack([s[i] for s in new_p]))
        outs.append(jnp.stack([s[i] for s in new_s]))
    return tuple(outs)
```

```python
import functools
import math

import jax
import jax.numpy as jnp
from jax import lax
from jax.experimental import pallas as pl
from jax.experimental.pallas import tpu as pltpu

F32 = jnp.float32
BF16 = jnp.bfloat16

D_MODEL = 1024
CHUNK = 64
H = 4
DH = 64
GW = H * DH
A_PAST = 8 * CHUNK
REL_MAX = 2 * CHUNK
Q_LORA = 256
KV_LORA = 128
QK_NOPE = 64
QK_ROPE = 32
ROPE_THETA = 10000.0
MLA_SCALE = (QK_NOPE + QK_ROPE) ** -0.5
D_FF = 2816
EPS = 1e-6

COL_CX = 0
COL_A = 3 * GW
COL_TAIL = 6 * GW
TAIL_W = 512
COL_B = COL_TAIL + TAIL_W
COL_CZ = COL_B + 4 * GW
PROJ_W = COL_CZ + GW
GATE_BLK = (COL_TAIL + 384) // 128
GATE_OFF = 32
N_GATES = 16

LANES = 128
SUBLANES = 8
VMEM_LIMIT = 56 * 1024 * 1024

NEG_INF = float("-inf")


def _cparams(*sem):
    return pltpu.CompilerParams(dimension_semantics=sem, vmem_limit_bytes=VMEM_LIMIT)


def _dot(a, b):
    return jnp.dot(a, b, preferred_element_type=F32)


def _dot_nt(a, b):
    return lax.dot_general(a, b, (((1,), (1,)), ((), ())), preferred_element_type=F32)


def _dot_tn(a, b):
    return lax.dot_general(a, b, (((0,), (0,)), ((), ())), preferred_element_type=F32)


def _split3(x):
    hi = x.astype(BF16)
    r1 = x - hi.astype(F32)
    mid = r1.astype(BF16)
    lo = (r1 - mid.astype(F32)).astype(BF16)
    return hi, mid, lo


def _rms(x, g):
    return x * lax.rsqrt(jnp.mean(x * x, axis=-1, keepdims=True) + EPS) * g


def _log_sigmoid(x):
    return jnp.minimum(x, 0.0) - jnp.log1p(jnp.exp(-jnp.abs(x)))


def _softplus(x):
    return jnp.maximum(x, 0.0) + jnp.log1p(jnp.exp(-jnp.abs(x)))


def _sigmoid(x):
    return 1.0 / (1.0 + jnp.exp(-x))


def _iota(shape, dim):
    return lax.broadcasted_iota(jnp.int32, shape, dim)


def _inproj_kernel(x_ref, g_ref, w_ref, wgt_ref, proj_ref, gt_ref):
    h = _rms(x_ref[...], g_ref[...]).astype(BF16)
    gt_ref[...] = _dot_nt(wgt_ref[...], h)
    proj_ref[...] = _dot(h, w_ref[...])


def _inproj(x, g, w, wgt, tm):
    n = x.shape[0]
    resident = lambda a, b: pl.BlockSpec((a, b), lambda i: (0, 0), pipeline_mode=pl.Buffered(1))
    return pl.pallas_call(
        _inproj_kernel,
        grid=(n // tm,),
        in_specs=[
            pl.BlockSpec((tm, D_MODEL), lambda i: (i, 0)),
            resident(1, D_MODEL),
            resident(D_MODEL, PROJ_W),
            resident(N_GATES, D_MODEL),
        ],
        out_specs=[
            pl.BlockSpec((tm, PROJ_W), lambda i: (i, 0)),
            pl.BlockSpec((N_GATES, tm), lambda i: (0, i)),
        ],
        out_shape=[
            jax.ShapeDtypeStruct((n, PROJ_W), F32),
            jax.ShapeDtypeStruct((N_GATES, n), F32),
        ],
        compiler_params=_cparams("parallel"),
        name="inproj",
    )(x, g, w, wgt)


def _head_norm_store(out_ref, rows, h, o, gh_ref):
    g = gh_ref[:, h * DH:(h + 1) * DH]
    y = o * lax.rsqrt(jnp.mean(o * o, axis=-1, keepdims=True) + EPS) * g
    out_ref[rows, h * DH:(h + 1) * DH] = y.astype(out_ref.dtype)


def _band_prompt_kernel(q_ref, kp_ref, kc_ref, vp_ref, vc_ref, bt2_ref, ghc_ref, out_ref, biast_ref, *, qb):
    nk = 2 * qb
    one_lane = (_iota((nk, DH), 1) == 0).astype(BF16)

    @pl.when(pl.program_id(0) == 0)
    def _():
        band = bt2_ref.shape[1]
        left = _iota((nk, LANES), 1) < CHUNK
        def placed(bt, top):
            ninf = lambda n: [jnp.full((n, LANES), NEG_INF, F32)] if n else []
            return jnp.concatenate(ninf(top) + [bt] + ninf(nk - band - top), axis=0)

        for h in range(H):
            bt = bt2_ref[h]
            shifted = [placed(bt, c * CHUNK) for c in range(qb // CHUNK)]
            for t2 in range(qb // LANES):
                biast_ref[h, :, t2 * LANES:(t2 + 1) * LANES] = jnp.where(left, shifted[2 * t2], shifted[2 * t2 + 1])

    def run(first):
        def scores(h):
            hs = slice(h * DH, (h + 1) * DH)
            kcat = jnp.concatenate([kp_ref[:, hs], kc_ref[:, hs]], axis=0).astype(BF16)
            return _dot_nt(kcat, (q_ref[:, hs] * (DH ** -0.5)).astype(BF16))

        def attend(h, st):
            hs = slice(h * DH, (h + 1) * DH)
            st = st + biast_ref[h]
            if first:
                st = jnp.where(_iota((nk, qb), 0) >= qb, st, NEG_INF)
            p = jnp.exp(st - jnp.max(st, axis=0, keepdims=True)).astype(BF16)
            vcat = jnp.concatenate([vp_ref[:, hs], vc_ref[:, hs]], axis=0).astype(BF16)
            acc = _dot_tn(jnp.concatenate([vcat, one_lane], axis=1), p)
            o = acc[:DH] / acc[DH:DH + 1]
            ms = jnp.mean(o * o, axis=0, keepdims=True)
            return o * lax.rsqrt(ms + EPS) * ghc_ref[hs, :]

        ys = []
        pending = scores(0)
        for h in range(H):
            nxt = scores(h + 1) if h + 1 < H else None
            ys.append(attend(h, pending))
            pending = nxt
        out_ref[...] = jnp.concatenate(ys, axis=0).T.astype(out_ref.dtype)

    @pl.when(pl.program_id(0) == 0)
    def _():
        run(True)

    @pl.when(pl.program_id(0) > 0)
    def _():
        run(False)


def _band_prompt(proj, bias, ghc, qb=A_PAST):
    t = proj.shape[0]
    assert qb == A_PAST and t % qb == 0
    prev = lambda i: jnp.maximum(i - 1, 0)
    cq = COL_A // GW
    band = bias.shape[-1]
    bias_t = bias.transpose(0, 2, 1)
    bt2 = jnp.concatenate([bias_t, bias_t], axis=-1)
    return pl.pallas_call(
        functools.partial(_band_prompt_kernel, qb=qb),
        grid=(t // qb,),
        in_specs=[
            pl.BlockSpec((qb, GW), lambda i: (i, cq)),
            pl.BlockSpec((qb, GW), lambda i: (prev(i), cq + 1)),
            pl.BlockSpec((qb, GW), lambda i: (i, cq + 1)),
            pl.BlockSpec((qb, GW), lambda i: (prev(i), cq + 2)),
            pl.BlockSpec((qb, GW), lambda i: (i, cq + 2)),
            pl.BlockSpec((H, band, LANES), lambda i: (0, 0, 0)),
            pl.BlockSpec((GW, 1), lambda i: (0, 0)),
        ],
        out_specs=pl.BlockSpec((qb, GW), lambda i: (i, 0)),
        out_shape=jax.ShapeDtypeStruct((t, GW), BF16),
        scratch_shapes=[pltpu.VMEM((H, 2 * qb, qb), F32)],
        compiler_params=_cparams("arbitrary"),
        name="band_prompt",
    )(proj, proj, proj, proj, proj, bt2, ghc)


def _band_sample_kernel(q_ref, k_ref, v_ref, ck_ref, cv_ref, bias_ref, gh_ref, out_ref):
    npast = ck_ref.shape[0]
    rows = slice(0, q_ref.shape[0])
    for h in range(H):
        hs = slice(h * DH, (h + 1) * DH)
        q = q_ref[:, hs].astype(BF16)
        s1 = _dot_nt(q, ck_ref[:, hs].astype(BF16)) * (DH ** -0.5) + bias_ref[h, :, :npast]
        s2 = _dot_nt(q, k_ref[:, hs].astype(BF16)) * (DH ** -0.5) + bias_ref[h, :, npast:]
        m = jnp.maximum(jnp.max(s1, axis=-1, keepdims=True), jnp.max(s2, axis=-1, keepdims=True))
        p1 = jnp.exp(s1 - m)
        p2 = jnp.exp(s2 - m)
        l = jnp.sum(p1, axis=-1, keepdims=True) + jnp.sum(p2, axis=-1, keepdims=True)
        o = (_dot(p1.astype(BF16), cv_ref[:, hs].astype(BF16))
             + _dot(p2.astype(BF16), v_ref[:, hs].astype(BF16))) / l
        _head_norm_store(out_ref, rows, h, o, gh_ref)


def _band_sample(proj3, ck, cv, bias, gh):
    b, s, _ = proj3.shape
    npast = ck.shape[1]
    return pl.pallas_call(
        _band_sample_kernel,
        grid=(b,),
        in_specs=[
            pl.BlockSpec((None, s, GW), lambda i: (i, 0, COL_A // GW)),
            pl.BlockSpec((None, s, GW), lambda i: (i, 0, COL_A // GW + 1)),
            pl.BlockSpec((None, s, GW), lambda i: (i, 0, COL_A // GW + 2)),
            pl.BlockSpec((None, npast, GW), lambda i: (i, 0, 0)),
            pl.BlockSpec((None, npast, GW), lambda i: (i, 0, 0)),
            pl.BlockSpec((H, s, npast + s), lambda i: (0, 0, 0)),
            pl.BlockSpec((1, GW), lambda i: (0, 0)),
        ],
        out_specs=pl.BlockSpec((None, s, GW), lambda i: (i, 0, 0)),
        out_shape=jax.ShapeDtypeStruct((b, s, GW), BF16),
        compiler_params=_cparams("parallel"),
        name="band_sample",
    )(proj3, proj3, proj3, ck, cv, bias, gh)


def _cumsum_cols(x, lo_tri):
    return sum(_dot(lo_tri, part) for part in _split3(x))


def _cumsum_rows(x, up_tri):
    return sum(_dot(part, up_tri) for part in _split3(x))


def _tri_masks(l):
    r = _iota((l, l), 0)
    c = _iota((l, l), 1)
    return r >= c, r > c


def _mlstm_kernel(q_ref, k_ref, v_ref, o_ref, gc_ref, gr_ref, bc_ref, br_ref, gh_ref,
                  c0_ref, n0_ref, m0_ref, out_ref, c_ref, n_ref, m_ref, *, l, nck):
    @pl.when(pl.program_id(1) == 0)
    def _():
        c_ref[...] = c0_ref[...]
        n_ref[...] = n0_ref[...]
        m_ref[...] = m0_ref[...]

    incl, _ = _tri_masks(l)
    lo_tri = incl.astype(BF16)
    up_tri = (_iota((l, l), 0) <= _iota((l, l), 1)).astype(BF16)

    probs = []
    for ck in range(nck):
        rows = slice(ck * l, (ck + 1) * l)
        gcol = gc_ref[rows, :] + bc_ref[...]
        grow = gr_ref[ck] + br_ref[...]
        gcs = _cumsum_cols(_log_sigmoid(gcol), lo_tri)
        grs = _cumsum_rows(_log_sigmoid(grow), up_tri)
        for h in range(H):
            hs = slice(h * DH, (h + 1) * DH)
            ig_c = gcol[:, GATE_OFF + h:GATE_OFF + h + 1]
            g_c = gcs[:, GATE_OFF + H + h:GATE_OFF + H + h + 1]
            ig_r = grow[h:h + 1, :]
            g_r = grs[H + h:H + h + 1, :]
            q = q_ref[rows, hs]
            kf = k_ref[rows, hs] * (DH ** -0.5)
            lmat = jnp.where(incl, g_c - g_r + ig_r, NEG_INF)
            probs.append(dict(
                rows=rows, h=h, q=q, kf=kf, qb=q.astype(BF16), kb=kf.astype(BF16),
                vb=v_ref[rows, hs].astype(BF16), lmat=lmat, lmax=jnp.max(lmat, axis=-1, keepdims=True),
                g_c=g_c, ig_c=ig_c, g_last=g_c[l - 1:l, :]))
    qks = [_dot_nt(p["qb"], p["kb"]) for p in probs]

    ms = [m_ref[:, h:h + 1] for h in range(H)]
    for p in probs:
        m_old = ms[p["h"]]
        p["linter"] = p["g_c"] + m_old
        p["mt"] = jnp.maximum(p["linter"], p["lmax"])
        m_new = p["mt"][l - 1:l, :]
        p["dprev"] = jnp.exp(p["g_last"] + m_old - m_new)
        p["kw"] = p["kf"] * jnp.exp(p["g_last"] - p["g_c"] + p["ig_c"] - m_new)
        ms[p["h"]] = m_new
    ws_ = [qk * jnp.exp(p["lmat"] - p["mt"]) for p, qk in zip(probs, qks)]
    wvs = [_dot(w.astype(BF16), p["vb"]) for p, w in zip(probs, ws_)]
    upds = [_dot_tn(p["kw"].astype(BF16), p["vb"]) for p in probs]

    cs = [c_ref[h] for h in range(H)]
    ns = [n_ref[h:h + 1, :] for h in range(H)]
    qcs, qns = [], []
    for p, upd in zip(probs, upds):
        h = p["h"]
        qcs.append(_dot(p["qb"], cs[h].astype(BF16)))
        qns.append(jnp.sum(p["q"] * ns[h], axis=-1, keepdims=True))
        cs[h] = p["dprev"] * cs[h] + upd
        ns[h] = p["dprev"] * ns[h] + jnp.sum(p["kw"], axis=0, keepdims=True)
    for h in range(H):
        c_ref[h] = cs[h]
        n_ref[h:h + 1, :] = ns[h]
        m_ref[:, h:h + 1] = ms[h]

    wsums = [jnp.sum(w, axis=-1, keepdims=True) for w in ws_]
    obs = []
    for p, wsum, wv, qc, qn in zip(probs, wsums, wvs, qcs, qns):
        hs = slice(p["h"] * DH, (p["h"] + 1) * DH)
        inter = jnp.exp(p["linter"] - p["mt"])
        den = wsum + inter * qn
        hout = (wv + inter * qc) / jnp.maximum(jnp.abs(den), jnp.exp(-p["mt"]))
        obs.append(hout * _sigmoid(o_ref[p["rows"], hs]))
    msq = [jnp.mean(ob * ob, axis=-1, keepdims=True) for ob in obs]
    for p, ob, ms_ in zip(probs, obs, msq):
        hs = slice(p["h"] * DH, (p["h"] + 1) * DH)
        out_ref[p["rows"], hs] = (ob * lax.rsqrt(ms_ + EPS) * gh_ref[:, hs]).astype(out_ref.dtype)


def _mlstm(proj3, gt3, bias_c, bias_r, gh, c0, n0, m0, l, nck):
    b, t, _ = proj3.shape
    steps = t // (l * nck)
    blk = l * nck
    col = lambda j: pl.BlockSpec((None, blk, GW), lambda bi, s: (bi, s, j))
    state = lambda shp: pl.BlockSpec((None,) + shp, lambda bi, s: (bi,) + (0,) * len(shp))
    return pl.pallas_call(
        functools.partial(_mlstm_kernel, l=l, nck=nck),
        grid=(b, steps),
        in_specs=[
            col(COL_B // GW), col(COL_B // GW + 1), col(COL_B // GW + 2), col(COL_B // GW + 3),
            pl.BlockSpec((None, blk, LANES), lambda bi, s: (bi, s, GATE_BLK)),
            pl.BlockSpec((None, nck, N_GATES, l), lambda bi, s: (bi, s, 0, 0)),
            pl.BlockSpec((1, LANES), lambda bi, s: (0, 0)),
            pl.BlockSpec((N_GATES, 1), lambda bi, s: (0, 0)),
            pl.BlockSpec((1, GW), lambda bi, s: (0, 0)),
            state((H, DH, DH)), state((H, DH)), state((1, H)),
        ],
        out_specs=[
            pl.BlockSpec((None, blk, GW), lambda bi, s: (bi, s, 0)),
            state((H, DH, DH)), state((H, DH)), state((1, H)),
        ],
        out_shape=[
            jax.ShapeDtypeStruct((b, t, GW), BF16),
            jax.ShapeDtypeStruct((b, H, DH, DH), F32),
            jax.ShapeDtypeStruct((b, H, DH), F32),
            jax.ShapeDtypeStruct((b, 1, H), F32),
        ],
        compiler_params=_cparams("parallel", "arbitrary"),
        name="mlstm",
    )(proj3, proj3, proj3, proj3, proj3, gt3, bias_c, bias_r, gh, c0, n0, m0)


def _head_of(idx):
    return idx // DH


def _block_mask(n_rows, n_cols):
    return _head_of(_iota((n_rows, n_cols), 0)) == _head_of(_iota((n_rows, n_cols), 1))


def _expander(first_lane):
    r = _iota((LANES, GW), 0)
    c = _iota((LANES, GW), 1)
    return (r == first_lane + _head_of(c)).astype(BF16)


def _dot_stacked(parts, rhs):
    m = parts[0].shape[0]
    y = _dot(jnp.concatenate(parts, axis=0), rhs)
    return sum(y[i * m:(i + 1) * m] for i in range(len(parts)))


def _expand(x, e):
    return _dot_stacked(_split3(x), e)


def _head_sums(a, bones):
    return _dot_stacked(_split3(a), bones)


def _cumsum_cols_wide(x, lo_tri):
    w = x.shape[1]
    y = _dot(lo_tri, jnp.concatenate(_split3(x), axis=1))
    return y[:, :w] + y[:, w:2 * w] + y[:, 2 * w:]


def _row_select(x_t, first_row):
    r = _iota(x_t.shape, 0)
    c = _iota(x_t.shape, 1)
    return jnp.sum(jnp.where(r == first_row + _head_of(c), x_t, 0.0), axis=0, keepdims=True)


def _block_diag_rows(x, mask):
    return jnp.where(mask, jnp.concatenate([x] * H, axis=0), jnp.zeros((), x.dtype))


def _cummax_rows(x):
    rows = _iota(x.shape, 0)
    sh = 1
    while sh < x.shape[0]:
        x = jnp.maximum(x, jnp.where(rows >= sh, pltpu.roll(x, sh, axis=0), NEG_INF))
        sh *= 2
    return x


def _mlstm64_kernel(q_ref, k_ref, v_ref, o_ref, gc_ref, gr_ref, bc_ref, br_ref, gh_ref,
                    c0_ref, n0_ref, m0_ref, out_ref, c_ref, n_ref, m_ref, *, nck):
    l = CHUNK

    @pl.when(pl.program_id(1) == 0)
    def _():
        c_ref[...] = c0_ref[...]
        n_ref[...] = n0_ref[...]
        m_ref[...] = m0_ref[...]

    bmask = _block_mask(GW, GW)
    bones = bmask.astype(BF16)
    within = _iota((GW, GW), 0) % DH <= _iota((GW, GW), 1) % DH
    up_bd = jnp.logical_and(bmask, within).astype(BF16)
    lo_tri = (_iota((l, l), 0) >= _iota((l, l), 1)).astype(BF16)
    incl = _iota((l, GW), 0) >= _iota((l, GW), 1) % DH
    e_i = _expander(GATE_OFF)
    e_f = _expander(GATE_OFF + H)

    cks = []
    for ck in range(nck):
        rows = slice(ck * l, (ck + 1) * l)
        gcol = gc_ref[rows, :] + bc_ref[...]
        gcs = _cumsum_cols_wide(_log_sigmoid(gcol), lo_tri)
        g_c = _expand(gcs, e_f)
        i_c = _expand(gcol, e_i)
        grow = gr_ref[ck] + br_ref[...]
        grow_t = jnp.concatenate([grow] * H, axis=1)
        grs_t = _dot_stacked(_split3(_log_sigmoid(grow_t)), up_bd)
        a_r = _row_select(grow_t, 0) - _row_select(grs_t, H)
        lmat = jnp.where(incl, g_c + a_r, NEG_INF)
        lmax = g_c + _cummax_rows(i_c - g_c)
        q = q_ref[rows, :]
        kf = k_ref[rows, :] * (DH ** -0.5)
        cks.append(dict(rows=rows, g_c=g_c, i_c=i_c, lmat=lmat, lmax=lmax, q=q, kf=kf,
                        qb=q.astype(BF16), kb=kf.astype(BF16), vb=v_ref[rows, :].astype(BF16),
                        g_last=g_c[l - 1:l, :]))
    scs = [_dot_nt(p["qb"], _block_diag_rows(p["kb"], bmask)) for p in cks]

    m_run = m_ref[...]
    for p in cks:
        p["linter"] = p["g_c"] + m_run
        p["mt"] = jnp.maximum(p["linter"], p["lmax"])
        m_new = p["mt"][l - 1:l, :]
        p["dprev"] = jnp.exp(p["g_last"] + m_run - m_new)
        p["kw"] = p["kf"] * jnp.exp(p["g_last"] - p["g_c"] + p["i_c"] - m_new)
        m_run = m_new
    m_ref[...] = m_run
    wbs = [(s * jnp.exp(p["lmat"] - p["mt"])).astype(BF16) for p, s in zip(cks, scs)]
    nums = [_dot(w, _block_diag_rows(p["vb"], bmask)) for p, w in zip(cks, wbs)]
    wsums = [_dot(w, bones) for w in wbs]
    upds = [jnp.where(bmask, _dot_tn(p["kw"].astype(BF16), p["vb"]), 0.0) for p in cks]

    c_run = c_ref[...]
    n_run = n_ref[...]
    qcs, qns = [], []
    for p, upd in zip(cks, upds):
        qcs.append(_dot(p["qb"], c_run.astype(BF16)))
        qns.append(_dot((p["q"] * n_run).astype(BF16), bones))
        c_run = p["dprev"] * c_run + upd
        n_run = p["dprev"] * n_run + jnp.sum(p["kw"], axis=0, keepdims=True)
    c_ref[...] = c_run
    n_ref[...] = n_run

    obs = []
    for p, num, wsum, qc, qn in zip(cks, nums, wsums, qcs, qns):
        inter = jnp.exp(p["linter"] - p["mt"])
        den = wsum + inter * qn
        hout = (num + inter * qc) / jnp.maximum(jnp.abs(den), jnp.exp(-p["mt"]))
        obs.append(hout * _sigmoid(o_ref[p["rows"], :]))
    msq = [_head_sums(ob * ob, bones) * (1.0 / DH) for ob in obs]
    for p, ob, m2 in zip(cks, obs, msq):
        out_ref[p["rows"], :] = (ob * lax.rsqrt(m2 + EPS) * gh_ref[...]).astype(out_ref.dtype)


def _mlstm64(proj3, gt3, bias_c, bias_r, gh, c0, n0, m0, nck):
    b, t, _ = proj3.shape
    blk = CHUNK * nck
    steps = t // blk
    col = lambda j: pl.BlockSpec((None, blk, GW), lambda bi, s: (bi, s, j))
    state = lambda shp: pl.BlockSpec((None,) + shp, lambda bi, s: (bi,) + (0,) * len(shp))
    return pl.pallas_call(
        functools.partial(_mlstm64_kernel, nck=nck),
        grid=(b, steps),
        in_specs=[
            col(COL_B // GW), col(COL_B // GW + 1), col(COL_B // GW + 2), col(COL_B // GW + 3),
            pl.BlockSpec((None, blk, LANES), lambda bi, s: (bi, s, GATE_BLK)),
            pl.BlockSpec((None, nck, N_GATES, CHUNK), lambda bi, s: (bi, s, 0, 0)),
            pl.BlockSpec((1, LANES), lambda bi, s: (0, 0)),
            pl.BlockSpec((N_GATES, 1), lambda bi, s: (0, 0)),
            pl.BlockSpec((1, GW), lambda bi, s: (0, 0)),
            state((GW, GW)), state((1, GW)), state((1, GW)),
        ],
        out_specs=[
            pl.BlockSpec((None, blk, GW), lambda bi, s: (bi, s, 0)),
            state((GW, GW)), state((1, GW)), state((1, GW)),
        ],
        out_shape=[
            jax.ShapeDtypeStruct((b, t, GW), BF16),
            jax.ShapeDtypeStruct((b, GW, GW), F32),
            jax.ShapeDtypeStruct((b, 1, GW), F32),
            jax.ShapeDtypeStruct((b, 1, GW), F32),
        ],
        compiler_params=_cparams("parallel", "arbitrary"),
        name="mlstm64",
    )(proj3, proj3, proj3, proj3, proj3, gt3, bias_c, bias_r, gh, c0, n0, m0)


def _to_block_diag(c):
    b = c.shape[0]
    eye = jnp.eye(H, dtype=c.dtype)
    return jnp.einsum("bhde,hg->bhdge", c, eye).reshape(b, GW, GW)


def _from_block_diag(cbd):
    b = cbd.shape[0]
    c5 = cbd.reshape(b, H, DH, H, DH)
    return jnp.stack([c5[:, h, :, h, :] for h in range(H)], axis=1)


def _split2(x):
    hi = x.astype(BF16)
    lo = (x - hi.astype(F32)).astype(BF16)
    return hi, lo


def _dot_sp(a, b):
    return _dot(a[0], b[0]) + (_dot(a[0], b[1]) + _dot(a[1], b[0]))


def _unit_lower_inverses(nmats, l):
    eye = (_iota((l, l), 0) == _iota((l, l), 1)).astype(F32)
    ps = [eye - n for n in nmats]
    qs = [_split2(n) for n in nmats]
    qs = [_split2(_dot_sp(q, q)) for q in qs]
    power = 2
    while power < l:
        ps = [p + _dot_sp(_split2(p), q) for p, q in zip(ps, qs)]
        power *= 2
        if power < l:
            qs = [_split2(_dot_sp(q, q)) for q in qs]
    return ps


def _l2norm(x):
    return x * lax.rsqrt(jnp.sum(x * x, axis=-1, keepdims=True) + 1e-6)


def _gdn_kernel(x_ref, z_ref, gc_ref, gr_ref, hist_ref, cw_ref, ac_ref, ar_ref, dc_ref, dr_ref,
                gh_ref, s0_ref, out_ref, s_ref, carry_scr, *, l, nck):
    @pl.when(pl.program_id(1) == 0)
    def _():
        s_ref[...] = s0_ref[...]
        carry_scr[...] = hist_ref[...]

    blk = l * nck
    x = x_ref[...]
    ext = jnp.concatenate([carry_scr[...], x], axis=0)
    carry_scr[...] = x[blk - SUBLANES:, :]
    y = x * cw_ref[3:4, :]
    for j in range(1, 4):
        y = y + ext[SUBLANES - j:SUBLANES - j + blk, :] * cw_ref[3 - j:4 - j, :]
    y = y * _sigmoid(y)

    incl, strict = _tri_masks(l)
    lo_tri = incl.astype(BF16)
    up_tri = (_iota((l, l), 0) <= _iota((l, l), 1)).astype(BF16)

    qraw = [y[ck * l:(ck + 1) * l, h * DH:(h + 1) * DH] for ck in range(nck) for h in range(H)]
    kraw = [y[ck * l:(ck + 1) * l, GW + h * DH:GW + (h + 1) * DH] for ck in range(nck) for h in range(H)]
    vraw = [y[ck * l:(ck + 1) * l, 2 * GW + h * DH:2 * GW + (h + 1) * DH] for ck in range(nck) for h in range(H)]
    qnorm = [_l2norm(a) * (DH ** -0.5) for a in qraw]
    knorm = [_l2norm(a) for a in kraw]
    probs = []
    for ck in range(nck):
        rows = slice(ck * l, (ck + 1) * l)
        gcol = gc_ref[rows, :]
        grow = gr_ref[ck]
        beta_cs = _sigmoid(gcol)
        dec_c = -jnp.exp(ac_ref[...]) * _softplus(gcol + dc_ref[...])
        dec_r = -jnp.exp(ar_ref[...]) * _softplus(grow + dr_ref[...])
        gcs = _cumsum_cols(dec_c, lo_tri)
        grs = _cumsum_rows(dec_r, up_tri)
        for h in range(H):
            beta = beta_cs[:, GATE_OFF + 2 * H + h:GATE_OFF + 2 * H + h + 1]
            g_c = gcs[:, GATE_OFF + 3 * H + h:GATE_OFF + 3 * H + h + 1]
            g_r = grs[3 * H + h:3 * H + h + 1, :]
            q, k, v = qnorm[ck * H + h], knorm[ck * H + h], vraw[ck * H + h]
            decay = jnp.exp(jnp.where(incl, g_c - g_r, NEG_INF))
            eg = jnp.exp(g_c)
            g_last = g_c[l - 1:l, :]
            probs.append(dict(
                rows=rows, h=h, qb=q.astype(BF16), kb=k.astype(BF16), beta=beta, decay=decay,
                rhs=jnp.concatenate([v * beta, k * (beta * eg)], axis=-1),
                qeg=(q * eg).astype(BF16), kdec=(k * jnp.exp(g_last - g_c)).astype(BF16),
                sdec=jnp.exp(g_last)))
    kks = [_dot_nt(p["kb"], p["kb"]) for p in probs]
    qks = [_dot_nt(p["qb"], p["kb"]) for p in probs]
    a_lows = [jnp.where(strict, p["beta"] * kk * p["decay"], 0.0) for p, kk in zip(probs, kks)]
    attns = [(qk * p["decay"]).astype(BF16) for p, qk in zip(probs, qks)]
    tinvs = _unit_lower_inverses(a_lows, l)
    sols = [_dot_sp(_split2(t), _split2(p["rhs"])) for t, p in zip(tinvs, probs)]

    states = [s_ref[h] for h in range(H)]
    for ck in range(nck):
        ps = probs[ck * H:(ck + 1) * H]
        ss = sols[ck * H:(ck + 1) * H]
        at = attns[ck * H:(ck + 1) * H]
        sbs = [s.astype(BF16) for s in states]
        wss = [_dot(sol[:, DH:].astype(BF16), sb) for sol, sb in zip(ss, sbs)]
        qss = [_dot(p["qeg"], sb) for p, sb in zip(ps, sbs)]
        vnbs = [(sol[:, :DH] - ws).astype(BF16) for sol, ws in zip(ss, wss)]
        os_ = [qs + _dot(a, vnb) for qs, a, vnb in zip(qss, at, vnbs)]
        states = [p["sdec"] * s + _dot_tn(p["kdec"], vnb) for p, s, vnb in zip(ps, states, vnbs)]
        for p, o in zip(ps, os_):
            hs = slice(p["h"] * DH, (p["h"] + 1) * DH)
            zg = z_ref[p["rows"], hs]
            yo = (o * lax.rsqrt(jnp.mean(o * o, axis=-1, keepdims=True) + EPS) * gh_ref[:, hs]
                  * (zg * _sigmoid(zg)))
            out_ref[p["rows"], hs] = yo.astype(out_ref.dtype)
    for h in range(H):
        s_ref[h] = states[h]


def _gdn(proj3, gt3, hist8, cw, a_c, a_r, dt_c, dt_r, gh, s0, l, nck):
    b, t, _ = proj3.shape
    blk = l * nck
    steps = t // blk
    state = lambda shp: pl.BlockSpec((None,) + shp, lambda bi, s: (bi,) + (0,) * len(shp))
    const = lambda shp: pl.BlockSpec(shp, lambda bi, s: (0,) * len(shp))
    return pl.pallas_call(
        functools.partial(_gdn_kernel, l=l, nck=nck),
        grid=(b, steps),
        in_specs=[
            pl.BlockSpec((None, blk, 3 * GW), lambda bi, s: (bi, s, COL_CX // (3 * GW))),
            pl.BlockSpec((None, blk, GW), lambda bi, s: (bi, s, COL_CZ // GW)),
            pl.BlockSpec((None, blk, LANES), lambda bi, s: (bi, s, GATE_BLK)),
            pl.BlockSpec((None, nck, N_GATES, l), lambda bi, s: (bi, s, 0, 0)),
            state((SUBLANES, 3 * GW)),
            const((4, 3 * GW)),
            const((1, LANES)), const((N_GATES, 1)), const((1, LANES)), const((N_GATES, 1)),
            const((1, GW)),
            state((H, DH, DH)),
        ],
        out_specs=[
            pl.BlockSpec((None, blk, GW), lambda bi, s: (bi, s, 0)),
            state((H, DH, DH)),
        ],
        out_shape=[
            jax.ShapeDtypeStruct((b, t, GW), BF16),
            jax.ShapeDtypeStruct((b, H, DH, DH), F32),
        ],
        scratch_shapes=[pltpu.VMEM((SUBLANES, 3 * GW), F32)],
        compiler_params=_cparams("parallel", "arbitrary"),
        name="gdn",
    )(proj3, proj3, proj3, gt3, hist8, cw, a_c, a_r, dt_c, dt_r, gh, s0)


def _bd_split(x, mask):
    hi, lo = _split2(x)
    return _block_diag_rows(hi, mask), _block_diag_rows(lo, mask)


def _gdn64_kernel(x_ref, z_ref, gc_ref, gr_ref, hist_ref, cw_ref, ac_ref, ar_ref, dc_ref, dr_ref,
                  gh_ref, s0_ref, out_ref, s_ref, carry_scr, *, nck):
    l = CHUNK

    @pl.when(pl.program_id(1) == 0)
    def _():
        s_ref[...] = s0_ref[...]
        carry_scr[...] = hist_ref[...]

    blk = l * nck
    x = x_ref[...]
    ext = jnp.concatenate([carry_scr[...], x], axis=0)
    carry_scr[...] = x[blk - SUBLANES:, :]
    y = x * cw_ref[3:4, :]
    for j in range(1, 4):
        y = y + ext[SUBLANES - j:SUBLANES - j + blk, :] * cw_ref[3 - j:4 - j, :]
    y = y * _sigmoid(y)

    bmask = _block_mask(GW, GW)
    bones = bmask.astype(BF16)
    within = _iota((GW, GW), 0) % DH <= _iota((GW, GW), 1) % DH
    up_bd = jnp.logical_and(bmask, within).astype(BF16)
    lo_tri = (_iota((l, l), 0) >= _iota((l, l), 1)).astype(BF16)
    key_pos = _iota((l, GW), 1) % DH
    incl = _iota((l, GW), 0) >= key_pos
    strict = _iota((l, GW), 0) > key_pos
    eye_t = (_iota((l, GW), 0) == key_pos).astype(F32)
    e_b = _expander(GATE_OFF + 2 * H)
    e_a = _expander(GATE_OFF + 3 * H)

    def head_sums(a):
        return _head_sums(a, bones)

    def shared_rhs(lhs_splits, rhs_bd):
        n = len(lhs_splits)
        big = _dot(jnp.concatenate([part for sp in lhs_splits for part in sp], axis=0), rhs_bd[0])
        small = _dot(jnp.concatenate([sp[0] for sp in lhs_splits], axis=0), rhs_bd[1])
        return [big[2 * i * l:(2 * i + 1) * l] + big[(2 * i + 1) * l:(2 * i + 2) * l]
                + small[i * l:(i + 1) * l] for i in range(n)]

    yq, yk, yv = y[:, :GW], y[:, GW:2 * GW], y[:, 2 * GW:]
    qn_all = yq * lax.rsqrt(head_sums(yq * yq) + 1e-6) * (DH ** -0.5)
    kn_all = yk * lax.rsqrt(head_sums(yk * yk) + 1e-6)

    cks = []
    for ck in range(nck):
        rows = slice(ck * l, (ck + 1) * l)
        gcol = gc_ref[rows, :]
        dec_c = -jnp.exp(ac_ref[...]) * _softplus(gcol + dc_ref[...])
        beta = _expand(_sigmoid(gcol), e_b)
        g_c = _expand(_cumsum_cols_wide(dec_c, lo_tri), e_a)
        grow_t = jnp.concatenate([gr_ref[ck]] * H, axis=1)
        dec_r = -jnp.exp(ar_ref[...]) * _softplus(grow_t + dr_ref[...])
        g_r = _row_select(_dot_stacked(_split3(dec_r), up_bd), 3 * H)
        decay = jnp.exp(jnp.where(incl, g_c - g_r, NEG_INF))
        eg = jnp.exp(g_c)
        g_last = g_c[l - 1:l, :]
        q, k, v = qn_all[rows], kn_all[rows], yv[rows]
        cks.append(dict(rows=rows, qb=q.astype(BF16), kb=k.astype(BF16),
                        beta=beta, decay=decay, rhs_v=v * beta, rhs_k=k * (beta * eg),
                        qeg=(q * eg).astype(BF16), kdec=(k * jnp.exp(g_last - g_c)).astype(BF16),
                        sdec=jnp.exp(g_last)))
    kqs = [_dot_nt(jnp.concatenate([p["kb"], p["qb"]], axis=0), _block_diag_rows(p["kb"], bmask))
           for p in cks]
    nmats = [jnp.where(strict, p["beta"] * kq[:l] * p["decay"], 0.0) for p, kq in zip(cks, kqs)]
    attns = [(kq[l:] * p["decay"]).astype(BF16) for p, kq in zip(cks, kqs)]

    ps = [eye_t - n for n in nmats]
    qs = [shared_rhs([_split2(n)], _bd_split(n, bmask))[0] for n in nmats]
    power = 2
    while power < l:
        power *= 2
        if power < l:
            res = [shared_rhs([_split2(p), _split2(q)], _bd_split(q, bmask)) for p, q in zip(ps, qs)]
            ps = [p + r[0] for p, r in zip(ps, res)]
            qs = [r[1] for r in res]
        else:
            ps = [p + shared_rhs([_split2(p)], _bd_split(q, bmask))[0] for p, q in zip(ps, qs)]
    tsp = [_split2(p) for p in ps]
    us = [shared_rhs([t], _bd_split(p["rhs_v"], bmask))[0] for t, p in zip(tsp, cks)]
    ws = [shared_rhs([t], _bd_split(p["rhs_k"], bmask))[0].astype(BF16) for t, p in zip(tsp, cks)]

    wu = [jnp.concatenate([w, u.astype(BF16)], axis=1) for w, u in zip(ws, us)]
    kwu = [_dot_tn(p["kdec"], x) for p, x in zip(cks, wu)]
    awu = [_dot(at, jnp.concatenate([_block_diag_rows(x[:, :GW], bmask),
                                     _block_diag_rows(x[:, GW:], bmask)], axis=1))
           for at, x in zip(attns, wu)]
    gmats = [jnp.where(bmask, x[:, :GW], 0.0).astype(BF16) for x in kwu]
    bmats = [jnp.where(bmask, x[:, GW:], 0.0) for x in kwu]
    qts = [(p["qeg"].astype(F32) - x[:, :GW]).astype(BF16) for p, x in zip(cks, awu)]

    s_run = s_ref[...]
    outs = []
    for p, g, bm, qt, x in zip(cks, gmats, bmats, qts, awu):
        ys = _dot(jnp.concatenate([g, qt], axis=0), s_run.astype(BF16))
        outs.append(ys[GW:] + x[:, GW:])
        s_run = p["sdec"] * s_run - ys[:GW] + bm
    s_ref[...] = s_run

    msq = [head_sums(o * o) * (1.0 / DH) for o in outs]
    for p, o, m2 in zip(cks, outs, msq):
        zg = z_ref[p["rows"], :]
        yo = o * lax.rsqrt(m2 + EPS) * gh_ref[...] * (zg * _sigmoid(zg))
        out_ref[p["rows"], :] = yo.astype(out_ref.dtype)


def _gdn64(proj3, gt3, hist8, cw, a_c, a_r, dt_c, dt_r, gh, s0, nck):
    b, t, _ = proj3.shape
    blk = CHUNK * nck
    steps = t // blk
    state = lambda shp: pl.BlockSpec((None,) + shp, lambda bi, s: (bi,) + (0,) * len(shp))
    const = lambda shp: pl.BlockSpec(shp, lambda bi, s: (0,) * len(shp))
    return pl.pallas_call(
        functools.partial(_gdn64_kernel, nck=nck),
        grid=(b, steps),
        in_specs=[
            pl.BlockSpec((None, blk, 3 * GW), lambda bi, s: (bi, s, COL_CX // (3 * GW))),
            pl.BlockSpec((None, blk, GW), lambda bi, s: (bi, s, COL_CZ // GW)),
            pl.BlockSpec((None, blk, LANES), lambda bi, s: (bi, s, GATE_BLK)),
            pl.BlockSpec((None, nck, N_GATES, CHUNK), lambda bi, s: (bi, s, 0, 0)),
            state((SUBLANES, 3 * GW)),
            const((4, 3 * GW)),
            const((1, LANES)), const((N_GATES, 1)), const((1, LANES)), const((N_GATES, 1)),
            const((1, GW)),
            state((GW, GW)),
        ],
        out_specs=[
            pl.BlockSpec((None, blk, GW), lambda bi, s: (bi, s, 0)),
            state((GW, GW)),
        ],
        out_shape=[
            jax.ShapeDtypeStruct((b, t, GW), BF16),
            jax.ShapeDtypeStruct((b, GW, GW), F32),
        ],
        scratch_shapes=[pltpu.VMEM((SUBLANES, 3 * GW), F32)],
        compiler_params=_cparams("parallel", "arbitrary"),
        name="gdn64",
    )(proj3, proj3, proj3, gt3, hist8, cw, a_c, a_r, dt_c, dt_r, gh, s0)


HEAD_PAD = 128
DPAD = H * HEAD_PAD
BF16_ROWS = 16
VT_PAD = -(-(DH + 1) // BF16_ROWS) * BF16_ROWS
VT_ROWS = H * VT_PAD


def _tile_heads(t):
    return jnp.concatenate([t] * H, axis=-1)


def _dprep_kernel(tail_ref, gq_ref, gkv_ref, wq_ref, wqp_ref, ka_ref, ckv_ref, kpe_ref, qc_ref):
    ka = ka_ref[...]
    kb = pltpu.roll(ka, 64, axis=1)
    nope = _iota(ka.shape, 1) < QK_NOPE
    qcos = jnp.where(nope, 1.0, kb)
    qsin = jnp.where(nope, 0.0, ka)
    hq = _rms(tail_ref[:, :Q_LORA], gq_ref[...]).astype(BF16)
    qc = _dot(hq, wq_ref[...]) * _tile_heads(qcos) + _dot(hq, wqp_ref[...]) * _tile_heads(qsin)
    qc_ref[...] = (qc * (MLA_SCALE * LOG2E)).astype(BF16)
    ckv_ref[...] = _rms(tail_ref[:, Q_LORA:Q_LORA + KV_LORA], gkv_ref[...])
    kr = tail_ref[:, Q_LORA + KV_LORA:]
    kpe = kr * ka + pltpu.roll(kr, 64, axis=1) * kb
    kpe_ref[...] = kpe[:, :QK_ROPE]


def _dprep(proj, gq, gkv, wq, wqp, ka, tm):
    n = proj.shape[0]
    row = lambda w: pl.BlockSpec((tm, w), lambda i: (i, 0))
    const = lambda a, b: pl.BlockSpec((a, b), lambda i: (0, 0))
    return pl.pallas_call(
        _dprep_kernel,
        grid=(n // tm,),
        in_specs=[
            pl.BlockSpec((tm, TAIL_W), lambda i: (i, COL_TAIL // TAIL_W)),
            const(1, Q_LORA), const(1, KV_LORA), const(Q_LORA, DPAD), const(Q_LORA, DPAD),
            row(LANES),
        ],
        out_specs=[row(KV_LORA), row(QK_ROPE), row(DPAD)],
        out_shape=[
            jax.ShapeDtypeStruct((n, KV_LORA), F32),
            jax.ShapeDtypeStruct((n, QK_ROPE), F32),
            jax.ShapeDtypeStruct((n, DPAD), BF16),
        ],
        compiler_params=_cparams("parallel"),
        name="dprep",
    )(proj, gq, gkv, wq, wqp, ka)


LOG2E = 1.4426950408889634


def _kvup_t_kernel(ckv_ref, kpe_ref, wk_ref, wvt_ref, pm_ref, onet_ref, kc_ref, vt_ref):
    c = ckv_ref[...].astype(BF16)
    kc_ref[...] = (_dot(c, wk_ref[...]) + _dot(kpe_ref[...].astype(BF16), pm_ref[...])).astype(BF16)
    vt_ref[...] = (_dot_nt(wvt_ref[...], c) + onet_ref[...]).astype(BF16)


def _kvup_t(ckv, kpe, wk, wvt, pm, onet, tm):
    m = ckv.shape[0]
    row = lambda w: pl.BlockSpec((tm, w), lambda i: (i, 0))
    const = lambda a, b: pl.BlockSpec((a, b), lambda i: (0, 0))
    return pl.pallas_call(
        _kvup_t_kernel,
        grid=(m // tm,),
        in_specs=[row(KV_LORA), row(QK_ROPE), const(KV_LORA, DPAD), const(VT_ROWS, KV_LORA),
                  const(QK_ROPE, DPAD), const(VT_ROWS, 1)],
        out_specs=[row(DPAD), pl.BlockSpec((VT_ROWS, tm), lambda i: (0, i))],
        out_shape=[jax.ShapeDtypeStruct((m, DPAD), BF16), jax.ShapeDtypeStruct((VT_ROWS, m), BF16)],
        compiler_params=_cparams("parallel"),
        name="kvup_t",
    )(ckv, kpe, wk, wvt, pm, onet)


def _mla_prompt_kernel(qi_ref, ki_ref, q_ref, k_ref, vt_ref, ghc_ref, out_ref, m_scr, acc_scr, *, bq, bk,
                       qw, ahead):
    p = pl.program_id(0)
    q_i = qi_ref[p]
    k_i = ki_ref[p]
    last = (q_i * bq) // bk

    @pl.when(k_i == 0)
    def _():
        m_scr[...] = jnp.full(m_scr.shape, NEG_INF, F32)
        acc_scr[...] = jnp.zeros(acc_scr.shape, F32)

    def step(diag):
        if diag:
            key_chunk = k_i * (bk // CHUNK) + _iota((bk, bq), 0) // CHUNK
            qry_chunk = q_i * (bq // CHUNK) + _iota((bk, bq), 1) // CHUNK
            allowed = key_chunk <= qry_chunk

        units = [(h, c) for h in range(H) for c in range(bq // qw)]
        rc = min(bk, 64)

        def scores(u):
            h, c = u
            hs = slice(h * HEAD_PAD, (h + 1) * HEAD_PAD)
            return _dot_nt(k_ref[:, hs], q_ref[c * qw:(c + 1) * qw, hs])

        def update(u, st):
            h, c = u
            hs = slice(h * HEAD_PAD, (h + 1) * HEAD_PAD)
            qs = slice(c * qw, (c + 1) * qw)
            if diag:
                st = jnp.where(allowed[:, qs], st, NEG_INF)
            m_prev = m_scr[h, :, qs]
            mx = st[:rc]
            for r in range(1, bk // rc):
                mx = jnp.maximum(mx, st[r * rc:(r + 1) * rc])
            m_new = jnp.maximum(m_prev, jnp.max(mx, axis=0, keepdims=True))
            alpha = jnp.exp2(m_prev - m_new)[0:1]
            m_row = m_new[0:1]
            pt = jnp.concatenate([jnp.exp2(st[r * rc:(r + 1) * rc] - m_row).astype(BF16)
                                  for r in range(bk // rc)], axis=0)
            acc = alpha * acc_scr[h, :, qs] + _dot(vt_ref[h * VT_PAD:(h + 1) * VT_PAD, :], pt)
            if not diag:
                m_scr[h, :, qs] = m_new
                acc_scr[h, :, qs] = acc
            return acc

        accs = []
        pending = [scores(u) for u in units[:ahead]]
        for idx, u in enumerate(units):
            if idx + ahead < len(units):
                pending.append(scores(units[idx + ahead]))
            accs.append(update(u, pending.pop(0)))
        per_head = bq // qw
        return [jnp.concatenate(accs[h * per_head:(h + 1) * per_head], axis=1) for h in range(H)]

    @pl.when(k_i < last)
    def _():
        step(False)

    @pl.when(k_i == last)
    def _():
        ys = []
        for h, acc in enumerate(step(True)):
            o = acc[:DH] / acc[DH:DH + 1]
            ms = jnp.mean(o * o, axis=0, keepdims=True)
            ys.append(o * lax.rsqrt(ms + EPS) * ghc_ref[h * DH:(h + 1) * DH, :])
        out_ref[...] = jnp.concatenate(ys, axis=0).T.astype(out_ref.dtype)


def _mla_prompt(qc, kc, vt, ghc, bq, bk):
    t = qc.shape[0]
    assert t % bq == 0 and t % bk == 0 and bk % bq == 0
    pairs = [(i, j) for i in range(t // bq) for j in range((i * bq) // bk + 1)]
    qi = jnp.asarray([i for i, _ in pairs], jnp.int32)
    ki = jnp.asarray([j for _, j in pairs], jnp.int32)
    grid_spec = pltpu.PrefetchScalarGridSpec(
        num_scalar_prefetch=2,
        grid=(len(pairs),),
        in_specs=[
            pl.BlockSpec((bq, DPAD), lambda p, qi, ki: (qi[p], 0)),
            pl.BlockSpec((bk, DPAD), lambda p, qi, ki: (ki[p], 0)),
            pl.BlockSpec((VT_ROWS, bk), lambda p, qi, ki: (0, ki[p])),
            pl.BlockSpec((GW, 1), lambda p, qi, ki: (0, 0)),
        ],
        out_specs=pl.BlockSpec((bq, GW), lambda p, qi, ki: (qi[p], 0)),
        scratch_shapes=[
            pltpu.VMEM((H, SUBLANES, bq), F32),
            pltpu.VMEM((H, VT_PAD, bq), F32),
        ],
    )
    return pl.pallas_call(
        functools.partial(_mla_prompt_kernel, bq=bq, bk=bk, qw=min(bq, 256), ahead=2),
        grid_spec=grid_spec,
        out_shape=jax.ShapeDtypeStruct((t, GW), BF16),
        compiler_params=_cparams("arbitrary"),
        name="mla_prompt",
    )(qi, ki, qc, kc, vt, ghc)


LAT_W = 2 * LANES
ONE_LANE = KV_LORA + QK_ROPE


def _mla_sample_kernel(q_ref, ckvp_ref, kpep_ref, ckvn_ref, kpen_ref, wabs_ref, wv_ref, gh_ref, out_ref):
    s_len = q_ref.shape[0]

    def latent_rows(ckv_ref, kpe_ref):
        n = ckv_ref.shape[0]
        tail = jnp.concatenate([kpe_ref[...].astype(BF16), jnp.zeros((n, LANES - QK_ROPE), BF16)], axis=1)
        tail = jnp.where(_iota((n, LANES), 1) == QK_ROPE, jnp.ones((), BF16), tail)
        return jnp.concatenate([ckv_ref[...].astype(BF16), tail], axis=1)

    kvp = latent_rows(ckvp_ref, kpep_ref)
    kvn = latent_rows(ckvn_ref, kpen_ref)
    qabs = jnp.concatenate([_dot(q_ref[:, h * HEAD_PAD:(h + 1) * HEAD_PAD], wabs_ref[h]) for h in range(H)],
                           axis=0).astype(BF16)
    s1 = _dot_nt(qabs, kvp)
    s2 = _dot_nt(qabs, kvn)
    m = jnp.maximum(jnp.max(s1, axis=-1, keepdims=True), jnp.max(s2, axis=-1, keepdims=True))
    acc = _dot(jnp.exp2(s1 - m).astype(BF16), kvp) + _dot(jnp.exp2(s2 - m).astype(BF16), kvn)
    olat = (acc[:, :KV_LORA] / acc[:, ONE_LANE:ONE_LANE + 1]).astype(BF16)
    rows = slice(0, s_len)
    for h in range(H):
        o = _dot(olat[h * s_len:(h + 1) * s_len], wv_ref[h])
        _head_norm_store(out_ref, rows, h, o, gh_ref)


def _mla_sample(qc3, ckvp, kpep, ckvn, kpen, wabs, wv, gh):
    b, s, _ = qc3.shape
    npast = ckvp.shape[1]
    per_b = lambda r, w: pl.BlockSpec((None, r, w), lambda i: (i, 0, 0))
    const = lambda shp: pl.BlockSpec(shp, lambda i: (0,) * len(shp))
    return pl.pallas_call(
        _mla_sample_kernel,
        grid=(b,),
        in_specs=[per_b(s, DPAD), per_b(npast, KV_LORA), per_b(npast, QK_ROPE), per_b(s, KV_LORA),
                  per_b(s, QK_ROPE), const((H, HEAD_PAD, LAT_W)), const((H, KV_LORA, DH)), const((1, GW))],
        out_specs=per_b(s, GW),
        out_shape=jax.ShapeDtypeStruct((b, s, GW), BF16),
        compiler_params=_cparams("parallel"),
        name="mla_sample",
    )(qc3, ckvp, kpep, ckvn, kpen, wabs, wv, gh)


def _ffn_kernel(*refs, seq_len, final_norm, tf):
    mix_refs, refs = refs[1:5], refs[:1] + refs[5:]
    if seq_len is None:
        (x_ref, wo_ref, g_ref, wup_ref, cw_ref, wd_ref, gf_ref, out_ref, ga_ref,
         act_scr, x2_scr, carry_scr) = refs
    else:
        (x_ref, wo_ref, g_ref, wup_ref, cw_ref, wd_ref, gf_ref, h1_ref, h2_ref, out_ref, ga_ref,
         act_scr, x2_scr) = refs
    tm = x_ref.shape[0]
    nj = D_FF // tf
    x2 = x_ref[...]
    for grp, m_ref in enumerate(mix_refs):
        x2 = x2 + _dot(m_ref[...], wo_ref[grp * GW:(grp + 1) * GW, :])
    x2_scr[...] = x2
    h = _rms(x2, g_ref[...]).astype(BF16)
    row = _iota((tm, tf), 0)

    if seq_len is None:
        @pl.when(pl.program_id(0) == 0)
        def _():
            carry_scr[...] = jnp.zeros(carry_scr.shape, F32)

    def up(j):
        cols = slice(j * tf, (j + 1) * tf)
        ucols = slice(D_FF + j * tf, D_FF + (j + 1) * tf)
        return _dot(h, wup_ref[:, cols]), _dot(h, wup_ref[:, ucols])

    def gate(j, ga, u):
        cols = slice(j * tf, (j + 1) * tf)
        r1 = pltpu.roll(ga, 1, axis=0)
        r2 = pltpu.roll(ga, 2, axis=0)
        if seq_len is None:
            c1 = carry_scr[SUBLANES - 1:SUBLANES, cols]
            c2 = carry_scr[SUBLANES - 2:SUBLANES - 1, cols]
            prev1 = jnp.where(row >= 1, r1, c1)
            prev2 = jnp.where(row >= 2, r2, jnp.where(row == 1, c1, c2))
            tail = ga[tm - SUBLANES:, :]
            carry_scr[:, cols] = tail
            ga_ref[:, cols] = tail
        else:
            t = row % seq_len
            prev1 = jnp.where(t >= 1, r1, h1_ref[:, cols])
            prev2 = jnp.where(t >= 2, r2, h2_ref[:, cols])
            ga_ref[:, cols] = ga
        conv = prev2 * cw_ref[0:1, cols] + prev1 * cw_ref[1:2, cols] + ga * cw_ref[2:3, cols]
        act_scr[:, cols] = (conv * _sigmoid(conv) * u).astype(BF16)

    pending = up(0)
    for j in range(nj):
        nxt = up(j + 1) if j + 1 < nj else None
        gate(j, *pending)
        pending = nxt
    y = x2_scr[...] + _dot(act_scr[...], wd_ref[...])
    if final_norm:
        y = _rms(y, gf_ref[...])
    out_ref[...] = y


def _ffn(x, mixed, w_out, g, w_up, cw, w_down, gf, h1, h2, *, tm, tf, seq_len, final_norm):
    n = x.shape[0]
    ni = n // tm
    resident = lambda a, b: pl.BlockSpec((a, b), lambda i: (0, 0), pipeline_mode=pl.Buffered(1))
    mix_spec = pl.BlockSpec((tm, GW), lambda i: (i, 0))
    in_specs = [
        pl.BlockSpec((tm, D_MODEL), lambda i: (i, 0)),
        mix_spec, mix_spec, mix_spec, mix_spec,
        resident(D_MODEL, D_MODEL),
        resident(1, D_MODEL),
        resident(D_MODEL, 2 * D_FF),
        resident(3, D_FF),
        resident(D_FF, D_MODEL),
        resident(1, D_MODEL),
    ]
    args = [x, *mixed, w_out, g, w_up, cw, w_down, gf]
    scratch = [pltpu.VMEM((tm, D_FF), BF16), pltpu.VMEM((tm, D_MODEL), F32)]
    if seq_len is None:
        ga_spec = pl.BlockSpec((None, SUBLANES, D_FF), lambda i: (i, 0, 0))
        ga_shape = jax.ShapeDtypeStruct((ni, SUBLANES, D_FF), F32)
        scratch.append(pltpu.VMEM((SUBLANES, D_FF), F32))
    else:
        in_specs += [pl.BlockSpec((tm, D_FF), lambda i: (i, 0))] * 2
        args += [h1, h2]
        ga_spec = pl.BlockSpec((tm, D_FF), lambda i: (i, 0))
        ga_shape = jax.ShapeDtypeStruct((n, D_FF), F32)
    return pl.pallas_call(
        functools.partial(_ffn_kernel, seq_len=seq_len, final_norm=final_norm, tf=tf),
        grid=(ni,),
        in_specs=in_specs,
        out_specs=[pl.BlockSpec((tm, D_MODEL), lambda i: (i, 0)), ga_spec],
        out_shape=[jax.ShapeDtypeStruct((n, D_MODEL), F32), ga_shape],
        scratch_shapes=scratch,
        compiler_params=_cparams("arbitrary"),
        name="ffn",
    )(*args)


def _rope_tables(offset, t):
    half = QK_ROPE // 2
    per_row = LANES // half
    assert t % per_row == 0
    inv = ROPE_THETA ** (-jnp.arange(half, dtype=F32) / half)
    pos = offset + per_row * _iota((t // per_row, LANES), 0) + _iota((t // per_row, LANES), 1) // half
    ang = pos.astype(F32) * jnp.tile(inv, per_row)[None, :]
    cos, sin = jnp.cos(ang).reshape(t, half), jnp.sin(ang).reshape(t, half)
    z32 = jnp.zeros((t, 32), F32)
    return jnp.concatenate([cos, cos, z32, -sin, sin, z32], -1)


def _rel_bias(table, n_past, n_q, n_k):
    period = n_q + n_k
    jmi = jnp.concatenate([jnp.arange(0, n_k + 1), jnp.arange(-(n_q - 1), 0)])
    diag = table[:, jnp.clip(n_past - jmi, -REL_MAX, REL_MAX) + REL_MAX]
    rows = jnp.tile(diag, (1, n_q))[:, :n_q * (period - 1)].reshape(table.shape[0], n_q, period - 1)
    return rows[:, :, :n_k]


def _swap_halves(w):
    half = w.shape[-1] // 2
    return jnp.concatenate([w[..., half:], w[..., :half]], -1)


def _layer_weights(lw):
    (g_mix, w_in, a_rel_bias, b_i_bias, b_f_bias, c_conv_w, c_a_log, c_dt_bias,
     d_g_q, d_w_q_up, d_g_kv, d_w_kv_up, g_head, w_out, g_ffn, w_up, f_conv_w, w_down) = lw
    o = 0
    cols = {}
    for name, size in (("a", 3 * GW), ("b", 4 * GW), ("bg", 2 * H), ("c", 3 * GW), ("cz", GW),
                       ("cg", 2 * H), ("dq", Q_LORA), ("dkv", KV_LORA), ("dkr", QK_ROPE)):
        cols[name] = w_in[:, o:o + size]
        o += size
    gates = jnp.concatenate([cols["bg"], cols["cg"]], -1)
    pad16 = jnp.zeros((D_MODEL, 16), F32)
    pad32 = jnp.zeros((D_MODEL, 32), F32)
    w_perm = jnp.concatenate([cols["c"], cols["a"], cols["dq"], cols["dkv"], cols["dkr"], gates, pad16,
                              _swap_halves(cols["dkr"]), pad32, cols["b"], cols["cz"]], -1)
    zc = lambda n: jnp.zeros((1, n), F32)
    zr = lambda n: jnp.zeros((n, 1), F32)
    bias_c = jnp.concatenate([zc(GATE_OFF), b_i_bias[None], b_f_bias[None], zc(LANES - GATE_OFF - 2 * H)], -1)
    bias_r = jnp.concatenate([b_i_bias[:, None], b_f_bias[:, None], zr(2 * H)], 0)
    alog_c = jnp.concatenate([zc(GATE_OFF + 3 * H), c_a_log[None], zc(LANES - GATE_OFF - 4 * H)], -1)
    alog_r = jnp.concatenate([zr(3 * H), c_a_log[:, None]], 0)
    dt_c = jnp.concatenate([zc(GATE_OFF + 3 * H), c_dt_bias[None], zc(LANES - GATE_OFF - 4 * H)], -1)
    dt_r = jnp.concatenate([zr(3 * H), c_dt_bias[:, None]], 0)

    wq = d_w_q_up.reshape(Q_LORA, H, QK_NOPE + QK_ROPE)
    z_h32 = jnp.zeros((Q_LORA, H, 32), F32)
    wq_full = jnp.concatenate([wq, z_h32], -1).reshape(Q_LORA, DPAD)
    wq_part = jnp.concatenate([jnp.zeros((Q_LORA, H, QK_NOPE), F32), _swap_halves(wq[..., QK_NOPE:]), z_h32],
                              -1).reshape(Q_LORA, DPAD)
    wkv = d_w_kv_up.reshape(KV_LORA, H, 2 * DH)
    z_h64 = jnp.zeros((KV_LORA, H, DH), F32)
    wk_full = jnp.concatenate([wkv[..., :DH], z_h64], -1).reshape(KV_LORA, DPAD)
    place = jnp.concatenate([jnp.zeros((QK_ROPE, QK_NOPE), F32), jnp.eye(QK_ROPE, dtype=F32),
                             jnp.zeros((QK_ROPE, 32), F32)], -1)
    pmat = jnp.concatenate([place] * H, -1)
    wk_t = jnp.transpose(wkv[..., :DH], (1, 2, 0))
    rope_rows = jnp.concatenate([jnp.zeros((QK_ROPE, KV_LORA), F32), jnp.eye(QK_ROPE, dtype=F32),
                                 jnp.zeros((QK_ROPE, LAT_W - KV_LORA - QK_ROPE), F32)], -1)
    wabs = jnp.concatenate([
        jnp.concatenate([wk_t, jnp.zeros((H, QK_NOPE, LAT_W - KV_LORA), F32)], -1),
        jnp.broadcast_to(rope_rows, (H, QK_ROPE, LAT_W)),
        jnp.zeros((H, HEAD_PAD - QK_NOPE - QK_ROPE, LAT_W), F32)], 1)
    return dict(
        g_mix=g_mix[None], w_in=w_perm.astype(BF16), w_gt=gates.T.astype(BF16),
        table=a_rel_bias, bias_c=bias_c, bias_r=bias_r, alog_c=alog_c, alog_r=alog_r, dt_c=dt_c, dt_r=dt_r,
        c_conv_w=c_conv_w, g_q=d_g_q[None], g_kv=d_g_kv[None],
        wq=wq_full.astype(BF16), wqp=wq_part.astype(BF16), wk=wk_full.astype(BF16),
        pmat=pmat.astype(BF16), wabs=wabs.astype(BF16),
        wv_heads=jnp.transpose(wkv[..., DH:], (1, 0, 2)).astype(BF16),
        wvt=jnp.concatenate([wkv[..., DH:], jnp.zeros((KV_LORA, H, VT_PAD - DH), F32)], -1)
        .reshape(KV_LORA, VT_ROWS).T.astype(BF16),
        vonest=(jnp.arange(VT_ROWS) % VT_PAD == DH).astype(F32)[:, None],
        g_head=g_head.reshape(4, 1, GW), w_out=w_out.astype(BF16),
        g_ffn=g_ffn[None], w_up=w_up.astype(BF16), f_conv_w=f_conv_w, w_down=w_down.astype(BF16))


def _gates_t3(gt, b, t, l):
    return gt.reshape(N_GATES, b, t // l, l).transpose(1, 2, 0, 3)


def _layer(x, offset, st, w, gf, final_norm, cfg):
    b, t, _ = x.shape
    n = b * t
    first = st is None
    x2 = x.reshape(n, D_MODEL)
    proj, gt = _inproj(x2, w["g_mix"], w["w_in"], w["w_gt"], cfg["tm"])
    proj3 = proj.reshape(b, t, PROJ_W)
    gh = w["g_head"]
    l = min(t, CHUNK)
    gt3 = _gates_t3(gt, b, t, l)

    new_ak = proj3[:, t - min(A_PAST, t):, COL_A + GW:COL_A + 2 * GW].reshape(b, -1, H, DH)
    new_av = proj3[:, t - min(A_PAST, t):, COL_A + 2 * GW:COL_A + 3 * GW].reshape(b, -1, H, DH)
    if first:
        bias = _rel_bias(w["table"], A_PAST, CHUNK, A_PAST + CHUNK)
        oa = _band_prompt(proj, bias, gh[0].T)
    else:
        npast = st[0].shape[1]
        bias = _rel_bias(w["table"], npast, t, npast + t)
        oa = _band_sample(proj3, st[0].reshape(b, npast, GW), st[1].reshape(b, npast, GW), bias, gh[0])
        oa = oa.reshape(n, GW)

    if first:
        c0 = jnp.zeros((b, H, DH, DH), F32)
        n0 = jnp.zeros((b, H, DH), F32)
        m0 = jnp.zeros((b, 1, H), F32)
    else:
        c0, n0, m0 = st[2], st[3], st[4][:, None, :]
    if l == CHUNK:
        ob, cbd, nrow, mrow = _mlstm64(proj3, gt3, w["bias_c"], w["bias_r"], gh[1], _to_block_diag(c0),
                                       n0.reshape(b, 1, GW), jnp.repeat(m0, DH, axis=-1), cfg["nck"])
        new_bc, new_bn, new_bm = _from_block_diag(cbd), nrow.reshape(b, H, DH), mrow[:, 0, ::DH]
    else:
        ob, new_bc, new_bn, new_bm = _mlstm(proj3, gt3, w["bias_c"], w["bias_r"], gh[1], c0, n0, m0,
                                            l, cfg["nck"])
        new_bm = new_bm[:, 0, :]

    if first:
        hist8 = jnp.zeros((b, SUBLANES, 3 * GW), F32)
        s0 = jnp.zeros((b, H, DH, DH), F32)
    else:
        hist8 = jnp.concatenate([jnp.zeros((b, SUBLANES - 3, 3 * GW), F32), st[6]], 1)
        s0 = st[5]
    gdn_args = (proj3, gt3, hist8, w["c_conv_w"], w["alog_c"], w["alog_r"], w["dt_c"], w["dt_r"], gh[2])
    if l == CHUNK:
        oc, sbd = _gdn64(*gdn_args, _to_block_diag(s0), cfg["nck"])
        new_cs = _from_block_diag(sbd)
    else:
        oc, new_cs = _gdn(*gdn_args, s0, l, cfg["nck"])
    new_cconv = proj3[:, t - 3:, COL_CX:COL_CX + 3 * GW]

    ka = jnp.tile(_rope_tables(offset, t), (b, 1))
    ckv, kpe, qc = _dprep(proj, w["g_q"], w["g_kv"], w["wq"], w["wqp"], ka, cfg["tm_d"])
    if first:
        kc, vt = _kvup_t(ckv, kpe, w["wk"], w["wvt"], w["pmat"], w["vonest"], cfg["tm_kv"])
        od = _mla_prompt(qc, kc, vt, gh[3].T, cfg["mla_bq"], cfg["mla_bk"])
    else:
        od = _mla_sample(qc.reshape(b, t, DPAD), st[7], st[8], ckv.reshape(b, t, KV_LORA),
                         kpe.reshape(b, t, QK_ROPE), w["wabs"], w["wv_heads"], gh[3])
        od = od.reshape(n, GW)

    ffn_args = (x2, (oa, ob.reshape(n, GW), oc.reshape(n, GW), od), w["w_out"],
                w["g_ffn"], w["w_up"], w["f_conv_w"], w["w_down"], gf)
    if first:
        y, ga_tail = _ffn(*ffn_args, None, None,
                          tm=cfg["tm"], tf=cfg["tf"], seq_len=None, final_norm=final_norm)
        new_fconv = ga_tail[-1, SUBLANES - 2:, :][None]
    else:
        hist = st[9]
        zrow = jnp.zeros((b, t - 1, D_FF), F32)
        h1 = jnp.concatenate([hist[:, 1:2], zrow], 1).reshape(n, D_FF)
        h2 = jnp.concatenate([hist, zrow[:, 1:]], 1).reshape(n, D_FF)
        y, ga = _ffn(*ffn_args, h1, h2,
                     tm=cfg["tm"], tf=cfg["tf"], seq_len=t, final_norm=final_norm)
        new_fconv = ga.reshape(b, t, D_FF)[:, t - 2:]
    state = (new_ak, new_av, new_bc, new_bn, new_bm, new_cs, new_cconv,
             ckv.reshape(b, t, KV_LORA), kpe.reshape(b, t, QK_ROPE), new_fconv)
    return y.reshape(b, t, D_MODEL), state


def _config(b, t):
    n = b * t
    tm = min(n, 1024)
    return dict(tm=tm, tf=256, nck=1 if t <= CHUNK else 16,
                tm_d=min(n, 1024), tm_kv=min(n, 2048), mla_bq=min(t, 512), mla_bk=min(t, 2048))


def kernel(x_prompt, x_sample, cache_a_k, cache_a_v, state_b_c, state_b_n, state_b_m, state_c_s, cache_c_conv, cache_d_ckv, cache_d_kpe, cache_ffn_conv, g_mix, w_in, a_rel_bias, b_i_bias, b_f_bias, c_conv_w, c_a_log, c_dt_bias, d_g_q, d_w_q_up, d_g_kv, d_w_kv_up, g_head, w_out, g_ffn, w_up, f_conv_w, w_down, g_final):
    layer_w = (g_mix, w_in, a_rel_bias, b_i_bias, b_f_bias, c_conv_w, c_a_log, c_dt_bias,
               d_g_q, d_w_q_up, d_g_kv, d_w_kv_up, g_head, w_out, g_ffn, w_up, f_conv_w, w_down)
    depth = g_mix.shape[0]
    past = cache_d_ckv.shape[2]
    xp, xs = x_prompt, x_sample
    cfg_p = _config(*x_prompt.shape[:2])
    cfg_s = _config(*x_sample.shape[:2])
    gf = g_final[None]
    new_p, new_s = [], []
    for l in range(depth):
        w = _layer_weights(tuple(a[l] for a in layer_w))
        last = l == depth - 1
        xp, sp_l = _layer(xp, 0, None, w, gf, last, cfg_p)
        st = (cache_a_k[l], cache_a_v[l], state_b_c[l], state_b_n[l], state_b_m[l],
              state_c_s[l], cache_c_conv[l], cache_d_ckv[l], cache_d_kpe[l], cache_ffn_conv[l])
        xs, ss_l = _layer(xs, past, st, w, gf, last, cfg_s)
        new_p.append(sp_l)
        new_s.append(ss_l)
    outs = [xp, xs]
    for i in range(10):
        outs.append(jnp.stack([s[i] for s in new_p]))
        outs.append(jnp.stack([s[i] for s in new_s]))
    return tuple(outs)
```

```python
import functools
import math

import jax
import jax.numpy as jnp
from jax import lax
from jax.experimental import pallas as pl
from jax.experimental.pallas import tpu as pltpu

F32 = jnp.float32
BF16 = jnp.bfloat16

D_MODEL = 1024
CHUNK = 64
H = 4
DH = 64
GW = H * DH
A_PAST = 8 * CHUNK
REL_MAX = 2 * CHUNK
Q_LORA = 256
KV_LORA = 128
QK_NOPE = 64
QK_ROPE = 32
ROPE_THETA = 10000.0
MLA_SCALE = (QK_NOPE + QK_ROPE) ** -0.5
D_FF = 2816
EPS = 1e-6

COL_CX = 0
COL_A = 3 * GW
COL_TAIL = 6 * GW
TAIL_W = 512
COL_B = COL_TAIL + TAIL_W
COL_CZ = COL_B + 4 * GW
PROJ_W = COL_CZ + GW
GATE_BLK = (COL_TAIL + 384) // 128
GATE_OFF = 32
N_GATES = 16

LANES = 128
SUBLANES = 8
VMEM_LIMIT = 56 * 1024 * 1024

NEG_INF = float("-inf")


def _cparams(*sem):
    return pltpu.CompilerParams(dimension_semantics=sem, vmem_limit_bytes=VMEM_LIMIT)


def _dot(a, b):
    return jnp.dot(a, b, preferred_element_type=F32)


def _dot_nt(a, b):
    return lax.dot_general(a, b, (((1,), (1,)), ((), ())), preferred_element_type=F32)


def _dot_tn(a, b):
    return lax.dot_general(a, b, (((0,), (0,)), ((), ())), preferred_element_type=F32)


def _split3(x):
    hi = x.astype(BF16)
    r1 = x - hi.astype(F32)
    mid = r1.astype(BF16)
    lo = (r1 - mid.astype(F32)).astype(BF16)
    return hi, mid, lo


def _rms(x, g):
    return x * lax.rsqrt(jnp.mean(x * x, axis=-1, keepdims=True) + EPS) * g


def _log_sigmoid(x):
    return jnp.minimum(x, 0.0) - jnp.log1p(jnp.exp(-jnp.abs(x)))


def _softplus(x):
    return jnp.maximum(x, 0.0) + jnp.log1p(jnp.exp(-jnp.abs(x)))


def _sigmoid(x):
    return 1.0 / (1.0 + jnp.exp(-x))


def _iota(shape, dim):
    return lax.broadcasted_iota(jnp.int32, shape, dim)


def _inproj_kernel(x_ref, g_ref, w_ref, wgt_ref, proj_ref, gt_ref):
    h = _rms(x_ref[...], g_ref[...]).astype(BF16)
    gt_ref[...] = _dot_nt(wgt_ref[...], h)
    proj_ref[...] = _dot(h, w_ref[...])


def _inproj(x, g, w, wgt, tm):
    n = x.shape[0]
    resident = lambda a, b: pl.BlockSpec((a, b), lambda i: (0, 0), pipeline_mode=pl.Buffered(1))
    return pl.pallas_call(
        _inproj_kernel,
        grid=(n // tm,),
        in_specs=[
            pl.BlockSpec((tm, D_MODEL), lambda i: (i, 0)),
            resident(1, D_MODEL),
            resident(D_MODEL, PROJ_W),
            resident(N_GATES, D_MODEL),
        ],
        out_specs=[
            pl.BlockSpec((tm, PROJ_W), lambda i: (i, 0)),
            pl.BlockSpec((N_GATES, tm), lambda i: (0, i)),
        ],
        out_shape=[
            jax.ShapeDtypeStruct((n, PROJ_W), F32),
            jax.ShapeDtypeStruct((N_GATES, n), F32),
        ],
        compiler_params=_cparams("parallel"),
        name="inproj",
    )(x, g, w, wgt)


def _head_norm_store(out_ref, rows, h, o, gh_ref):
    g = gh_ref[:, h * DH:(h + 1) * DH]
    y = o * lax.rsqrt(jnp.mean(o * o, axis=-1, keepdims=True) + EPS) * g
    out_ref[rows, h * DH:(h + 1) * DH] = y.astype(out_ref.dtype)


def _band_prompt_kernel(q_ref, kp_ref, kc_ref, vp_ref, vc_ref, bt2_ref, ghc_ref, out_ref, biast_ref, *, qb):
    nk = 2 * qb
    one_lane = (_iota((nk, DH), 1) == 0).astype(BF16)

    @pl.when(pl.program_id(0) == 0)
    def _():
        band = bt2_ref.shape[1]
        left = _iota((nk, LANES), 1) < CHUNK
        def placed(bt, top):
            ninf = lambda n: [jnp.full((n, LANES), NEG_INF, F32)] if n else []
            return jnp.concatenate(ninf(top) + [bt] + ninf(nk - band - top), axis=0)

        for h in range(H):
            bt = bt2_ref[h]
            shifted = [placed(bt, c * CHUNK) for c in range(qb // CHUNK)]
            for t2 in range(qb // LANES):
                biast_ref[h, :, t2 * LANES:(t2 + 1) * LANES] = jnp.where(left, shifted[2 * t2], shifted[2 * t2 + 1])

    def run(first):
        def scores(h):
            hs = slice(h * DH, (h + 1) * DH)
            kcat = jnp.concatenate([kp_ref[:, hs], kc_ref[:, hs]], axis=0).astype(BF16)
            return _dot_nt(kcat, (q_ref[:, hs] * (DH ** -0.5)).astype(BF16))

        def attend(h, st):
            hs = slice(h * DH, (h + 1) * DH)
            st = st + biast_ref[h]
            if first:
                st = jnp.where(_iota((nk, qb), 0) >= qb, st, NEG_INF)
            p = jnp.exp(st - jnp.max(st, axis=0, keepdims=True)).astype(BF16)
            vcat = jnp.concatenate([vp_ref[:, hs], vc_ref[:, hs]], axis=0).astype(BF16)
            acc = _dot_tn(jnp.concatenate([vcat, one_lane], axis=1), p)
            o = acc[:DH] / acc[DH:DH + 1]
            ms = jnp.mean(o * o, axis=0, keepdims=True)
            return o * lax.rsqrt(ms + EPS) * ghc_ref[hs, :]

        ys = []
        pending = scores(0)
        for h in range(H):
            nxt = scores(h + 1) if h + 1 < H else None
            ys.append(attend(h, pending))
            pending = nxt
        out_ref[...] = jnp.concatenate(ys, axis=0).T.astype(out_ref.dtype)

    @pl.when(pl.program_id(0) == 0)
    def _():
        run(True)

    @pl.when(pl.program_id(0) > 0)
    def _():
        run(False)


def _band_prompt(proj, bias, ghc, qb=A_PAST):
    t = proj.shape[0]
    assert qb == A_PAST and t % qb == 0
    prev = lambda i: jnp.maximum(i - 1, 0)
    cq = COL_A // GW
    band = bias.shape[-1]
    bias_t = bias.transpose(0, 2, 1)
    bt2 = jnp.concatenate([bias_t, bias_t], axis=-1)
    return pl.pallas_call(
        functools.partial(_band_prompt_kernel, qb=qb),
        grid=(t // qb,),
        in_specs=[
            pl.BlockSpec((qb, GW), lambda i: (i, cq)),
            pl.BlockSpec((qb, GW), lambda i: (prev(i), cq + 1)),
            pl.BlockSpec((qb, GW), lambda i: (i, cq + 1)),
            pl.BlockSpec((qb, GW), lambda i: (prev(i), cq + 2)),
            pl.BlockSpec((qb, GW), lambda i: (i, cq + 2)),
            pl.BlockSpec((H, band, LANES), lambda i: (0, 0, 0)),
            pl.BlockSpec((GW, 1), lambda i: (0, 0)),
        ],
        out_specs=pl.BlockSpec((qb, GW), lambda i: (i, 0)),
        out_shape=jax.ShapeDtypeStruct((t, GW), BF16),
        scratch_shapes=[pltpu.VMEM((H, 2 * qb, qb), F32)],
        compiler_params=_cparams("arbitrary"),
        name="band_prompt",
    )(proj, proj, proj, proj, proj, bt2, ghc)


def _band_sample_kernel(q_ref, k_ref, v_ref, ck_ref, cv_ref, bias_ref, gh_ref, out_ref):
    npast = ck_ref.shape[0]
    rows = slice(0, q_ref.shape[0])
    for h in range(H):
        hs = slice(h * DH, (h + 1) * DH)
        q = q_ref[:, hs].astype(BF16)
        s1 = _dot_nt(q, ck_ref[:, hs].astype(BF16)) * (DH ** -0.5) + bias_ref[h, :, :npast]
        s2 = _dot_nt(q, k_ref[:, hs].astype(BF16)) * (DH ** -0.5) + bias_ref[h, :, npast:]
        m = jnp.maximum(jnp.max(s1, axis=-1, keepdims=True), jnp.max(s2, axis=-1, keepdims=True))
        p1 = jnp.exp(s1 - m)
        p2 = jnp.exp(s2 - m)
        l = jnp.sum(p1, axis=-1, keepdims=True) + jnp.sum(p2, axis=-1, keepdims=True)
        o = (_dot(p1.astype(BF16), cv_ref[:, hs].astype(BF16))
             + _dot(p2.astype(BF16), v_ref[:, hs].astype(BF16))) / l
        _head_norm_store(out_ref, rows, h, o, gh_ref)


def _band_sample(proj3, ck, cv, bias, gh):
    b, s, _ = proj3.shape
    npast = ck.shape[1]
    return pl.pallas_call(
        _band_sample_kernel,
        grid=(b,),
        in_specs=[
            pl.BlockSpec((None, s, GW), lambda i: (i, 0, COL_A // GW)),
            pl.BlockSpec((None, s, GW), lambda i: (i, 0, COL_A // GW + 1)),
            pl.BlockSpec((None, s, GW), lambda i: (i, 0, COL_A // GW + 2)),
            pl.BlockSpec((None, npast, GW), lambda i: (i, 0, 0)),
            pl.BlockSpec((None, npast, GW), lambda i: (i, 0, 0)),
            pl.BlockSpec((H, s, npast + s), lambda i: (0, 0, 0)),
            pl.BlockSpec((1, GW), lambda i: (0, 0)),
        ],
        out_specs=pl.BlockSpec((None, s, GW), lambda i: (i, 0, 0)),
        out_shape=jax.ShapeDtypeStruct((b, s, GW), BF16),
        compiler_params=_cparams("parallel"),
        name="band_sample",
    )(proj3, proj3, proj3, ck, cv, bias, gh)


def _cumsum_cols(x, lo_tri):
    return sum(_dot(lo_tri, part) for part in _split3(x))


def _cumsum_rows(x, up_tri):
    return sum(_dot(part, up_tri) for part in _split3(x))


def _tri_masks(l):
    r = _iota((l, l), 0)
    c = _iota((l, l), 1)
    return r >= c, r > c


def _mlstm_kernel(q_ref, k_ref, v_ref, o_ref, gc_ref, gr_ref, bc_ref, br_ref, gh_ref,
                  c0_ref, n0_ref, m0_ref, out_ref, c_ref, n_ref, m_ref, *, l, nck):
    @pl.when(pl.program_id(1) == 0)
    def _():
        c_ref[...] = c0_ref[...]
        n_ref[...] = n0_ref[...]
        m_ref[...] = m0_ref[...]

    incl, _ = _tri_masks(l)
    lo_tri = incl.astype(BF16)
    up_tri = (_iota((l, l), 0) <= _iota((l, l), 1)).astype(BF16)

    probs = []
    for ck in range(nck):
        rows = slice(ck * l, (ck + 1) * l)
        gcol = gc_ref[rows, :] + bc_ref[...]
        grow = gr_ref[ck] + br_ref[...]
        gcs = _cumsum_cols(_log_sigmoid(gcol), lo_tri)
        grs = _cumsum_rows(_log_sigmoid(grow), up_tri)
        for h in range(H):
            hs = slice(h * DH, (h + 1) * DH)
            ig_c = gcol[:, GATE_OFF + h:GATE_OFF + h + 1]
            g_c = gcs[:, GATE_OFF + H + h:GATE_OFF + H + h + 1]
            ig_r = grow[h:h + 1, :]
            g_r = grs[H + h:H + h + 1, :]
            q = q_ref[rows, hs]
            kf = k_ref[rows, hs] * (DH ** -0.5)
            lmat = jnp.where(incl, g_c - g_r + ig_r, NEG_INF)
            probs.append(dict(
                rows=rows, h=h, q=q, kf=kf, qb=q.astype(BF16), kb=kf.astype(BF16),
                vb=v_ref[rows, hs].astype(BF16), lmat=lmat, lmax=jnp.max(lmat, axis=-1, keepdims=True),
                g_c=g_c, ig_c=ig_c, g_last=g_c[l - 1:l, :]))
    qks = [_dot_nt(p["qb"], p["kb"]) for p in probs]

    ms = [m_ref[:, h:h + 1] for h in range(H)]
    for p in probs:
        m_old = ms[p["h"]]
        p["linter"] = p["g_c"] + m_old
        p["mt"] = jnp.maximum(p["linter"], p["lmax"])
        m_new = p["mt"][l - 1:l, :]
        p["dprev"] = jnp.exp(p["g_last"] + m_old - m_new)
        p["kw"] = p["kf"] * jnp.exp(p["g_last"] - p["g_c"] + p["ig_c"] - m_new)
        ms[p["h"]] = m_new
    ws_ = [qk * jnp.exp(p["lmat"] - p["mt"]) for p, qk in zip(probs, qks)]
    wvs = [_dot(w.astype(BF16), p["vb"]) for p, w in zip(probs, ws_)]
    upds = [_dot_tn(p["kw"].astype(BF16), p["vb"]) for p in probs]

    cs = [c_ref[h] for h in range(H)]
    ns = [n_ref[h:h + 1, :] for h in range(H)]
    qcs, qns = [], []
    for p, upd in zip(probs, upds):
        h = p["h"]
        qcs.append(_dot(p["qb"], cs[h].astype(BF16)))
        qns.append(jnp.sum(p["q"] * ns[h], axis=-1, keepdims=True))
        cs[h] = p["dprev"] * cs[h] + upd
        ns[h] = p["dprev"] * ns[h] + jnp.sum(p["kw"], axis=0, keepdims=True)
    for h in range(H):
        c_ref[h] = cs[h]
        n_ref[h:h + 1, :] = ns[h]
        m_ref[:, h:h + 1] = ms[h]

    wsums = [jnp.sum(w, axis=-1, keepdims=True) for w in ws_]
    obs = []
    for p, wsum, wv, qc, qn in zip(probs, wsums, wvs, qcs, qns):
        hs = slice(p["h"] * DH, (p["h"] + 1) * DH)
        inter = jnp.exp(p["linter"] - p["mt"])
        den = wsum + inter * qn
        hout = (wv + inter * qc) / jnp.maximum(jnp.abs(den), jnp.exp(-p["mt"]))
        obs.append(hout * _sigmoid(o_ref[p["rows"], hs]))
    msq = [jnp.mean(ob * ob, axis=-1, keepdims=True) for ob in obs]
    for p, ob, ms_ in zip(probs, obs, msq):
        hs = slice(p["h"] * DH, (p["h"] + 1) * DH)
        out_ref[p["rows"], hs] = (ob * lax.rsqrt(ms_ + EPS) * gh_ref[:, hs]).astype(out_ref.dtype)


def _mlstm(proj3, gt3, bias_c, bias_r, gh, c0, n0, m0, l, nck):
    b, t, _ = proj3.shape
    steps = t // (l * nck)
    blk = l * nck
    col = lambda j: pl.BlockSpec((None, blk, GW), lambda bi, s: (bi, s, j))
    state = lambda shp: pl.BlockSpec((None,) + shp, lambda bi, s: (bi,) + (0,) * len(shp))
    return pl.pallas_call(
        functools.partial(_mlstm_kernel, l=l, nck=nck),
        grid=(b, steps),
        in_specs=[
            col(COL_B // GW), col(COL_B // GW + 1), col(COL_B // GW + 2), col(COL_B // GW + 3),
            pl.BlockSpec((None, blk, LANES), lambda bi, s: (bi, s, GATE_BLK)),
            pl.BlockSpec((None, nck, N_GATES, l), lambda bi, s: (bi, s, 0, 0)),
            pl.BlockSpec((1, LANES), lambda bi, s: (0, 0)),
            pl.BlockSpec((N_GATES, 1), lambda bi, s: (0, 0)),
            pl.BlockSpec((1, GW), lambda bi, s: (0, 0)),
            state((H, DH, DH)), state((H, DH)), state((1, H)),
        ],
        out_specs=[
            pl.BlockSpec((None, blk, GW), lambda bi, s: (bi, s, 0)),
            state((H, DH, DH)), state((H, DH)), state((1, H)),
        ],
        out_shape=[
            jax.ShapeDtypeStruct((b, t, GW), BF16),
            jax.ShapeDtypeStruct((b, H, DH, DH), F32),
            jax.ShapeDtypeStruct((b, H, DH), F32),
            jax.ShapeDtypeStruct((b, 1, H), F32),
        ],
        compiler_params=_cparams("parallel", "arbitrary"),
        name="mlstm",
    )(proj3, proj3, proj3, proj3, proj3, gt3, bias_c, bias_r, gh, c0, n0, m0)


def _head_of(idx):
    return idx // DH


def _block_mask(n_rows, n_cols):
    return _head_of(_iota((n_rows, n_cols), 0)) == _head_of(_iota((n_rows, n_cols), 1))


def _expander(first_lane):
    r = _iota((LANES, GW), 0)
    c = _iota((LANES, GW), 1)
    return (r == first_lane + _head_of(c)).astype(BF16)


def _dot_stacked(parts, rhs):
    m = parts[0].shape[0]
    y = _dot(jnp.concatenate(parts, axis=0), rhs)
    return sum(y[i * m:(i + 1) * m] for i in range(len(parts)))


def _expand(x, e):
    return _dot_stacked(_split3(x), e)


def _head_sums(a, bones):
    return _dot_stacked(_split3(a), bones)


def _cumsum_cols_wide(x, lo_tri):
    w = x.shape[1]
    y = _dot(lo_tri, jnp.concatenate(_split3(x), axis=1))
    return y[:, :w] + y[:, w:2 * w] + y[:, 2 * w:]


def _row_select(x_t, first_row):
    r = _iota(x_t.shape, 0)
    c = _iota(x_t.shape, 1)
    return jnp.sum(jnp.where(r == first_row + _head_of(c), x_t, 0.0), axis=0, keepdims=True)


def _block_diag_rows(x, mask):
    return jnp.where(mask, jnp.concatenate([x] * H, axis=0), jnp.zeros((), x.dtype))


def _cummax_rows(x):
    rows = _iota(x.shape, 0)
    sh = 1
    while sh < x.shape[0]:
        x = jnp.maximum(x, jnp.where(rows >= sh, pltpu.roll(x, sh, axis=0), NEG_INF))
        sh *= 2
    return x


def _mlstm64_kernel(q_ref, k_ref, v_ref, o_ref, gc_ref, gr_ref, bc_ref, br_ref, gh_ref,
                    c0_ref, n0_ref, m0_ref, out_ref, c_ref, n_ref, m_ref, *, nck):
    l = CHUNK

    @pl.when(pl.program_id(1) == 0)
    def _():
        c_ref[...] = c0_ref[...]
        n_ref[...] = n0_ref[...]
        m_ref[...] = m0_ref[...]

    bmask = _block_mask(GW, GW)
    bones = bmask.astype(BF16)
    within = _iota((GW, GW), 0) % DH <= _iota((GW, GW), 1) % DH
    up_bd = jnp.logical_and(bmask, within).astype(BF16)
    lo_tri = (_iota((l, l), 0) >= _iota((l, l), 1)).astype(BF16)
    incl = _iota((l, GW), 0) >= _iota((l, GW), 1) % DH
    e_i = _expander(GATE_OFF)
    e_f = _expander(GATE_OFF + H)

    cks = []
    for ck in range(nck):
        rows = slice(ck * l, (ck + 1) * l)
        gcol = gc_ref[rows, :] + bc_ref[...]
        gcs = _cumsum_cols_wide(_log_sigmoid(gcol), lo_tri)
        g_c = _expand(gcs, e_f)
        i_c = _expand(gcol, e_i)
        grow = gr_ref[ck] + br_ref[...]
        grow_t = jnp.concatenate([grow] * H, axis=1)
        grs_t = _dot_stacked(_split3(_log_sigmoid(grow_t)), up_bd)
        a_r = _row_select(grow_t, 0) - _row_select(grs_t, H)
        lmat = jnp.where(incl, g_c + a_r, NEG_INF)
        lmax = g_c + _cummax_rows(i_c - g_c)
        q = q_ref[rows, :]
        kf = k_ref[rows, :] * (DH ** -0.5)
        cks.append(dict(rows=rows, g_c=g_c, i_c=i_c, lmat=lmat, lmax=lmax, q=q, kf=kf,
                        qb=q.astype(BF16), kb=kf.astype(BF16), vb=v_ref[rows, :].astype(BF16),
                        g_last=g_c[l - 1:l, :]))
    scs = [_dot_nt(p["qb"], _block_diag_rows(p["kb"], bmask)) for p in cks]

    m_run = m_ref[...]
    for p in cks:
        p["linter"] = p["g_c"] + m_run
        p["mt"] = jnp.maximum(p["linter"], p["lmax"])
        m_new = p["mt"][l - 1:l, :]
        p["dprev"] = jnp.exp(p["g_last"] + m_run - m_new)
        p["kw"] = p["kf"] * jnp.exp(p["g_last"] - p["g_c"] + p["i_c"] - m_new)
        m_run = m_new
    m_ref[...] = m_run
    wbs = [(s * jnp.exp(p["lmat"] - p["mt"])).astype(BF16) for p, s in zip(cks, scs)]
    nums = [_dot(w, _block_diag_rows(p["vb"], bmask)) for p, w in zip(cks, wbs)]
    wsums = [_dot(w, bones) for w in wbs]
    upds = [jnp.where(bmask, _dot_tn(p["kw"].astype(BF16), p["vb"]), 0.0) for p in cks]

    c_run = c_ref[...]
    n_run = n_ref[...]
    qcs, qns = [], []
    for p, upd in zip(cks, upds):
        qcs.append(_dot(p["qb"], c_run.astype(BF16)))
        qns.append(_dot((p["q"] * n_run).astype(BF16), bones))
        c_run = p["dprev"] * c_run + upd
        n_run = p["dprev"] * n_run + jnp.sum(p["kw"], axis=0, keepdims=True)
    c_ref[...] = c_run
    n_ref[...] = n_run

    obs = []
    for p, num, wsum, qc, qn in zip(cks, nums, wsums, qcs, qns):
        inter = jnp.exp(p["linter"] - p["mt"])
        den = wsum + inter * qn
        hout = (num + inter * qc) / jnp.maximum(jnp.abs(den), jnp.exp(-p["mt"]))
        obs.append(hout * _sigmoid(o_ref[p["rows"], :]))
    msq = [_head_sums(ob * ob, bones) * (1.0 / DH) for ob in obs]
    for p, ob, m2 in zip(cks, obs, msq):
        out_ref[p["rows"], :] = (ob * lax.rsqrt(m2 + EPS) * gh_ref[...]).astype(out_ref.dtype)


def _mlstm64(proj3, gt3, bias_c, bias_r, gh, c0, n0, m0, nck):
    b, t, _ = proj3.shape
    blk = CHUNK * nck
    steps = t // blk
    col = lambda j: pl.BlockSpec((None, blk, GW), lambda bi, s: (bi, s, j))
    state = lambda shp: pl.BlockSpec((None,) + shp, lambda bi, s: (bi,) + (0,) * len(shp))
    return pl.pallas_call(
        functools.partial(_mlstm64_kernel, nck=nck),
        grid=(b, steps),
        in_specs=[
            col(COL_B // GW), col(COL_B // GW + 1), col(COL_B // GW + 2), col(COL_B // GW + 3),
            pl.BlockSpec((None, blk, LANES), lambda bi, s: (bi, s, GATE_BLK)),
            pl.BlockSpec((None, nck, N_GATES, CHUNK), lambda bi, s: (bi, s, 0, 0)),
            pl.BlockSpec((1, LANES), lambda bi, s: (0, 0)),
            pl.BlockSpec((N_GATES, 1), lambda bi, s: (0, 0)),
            pl.BlockSpec((1, GW), lambda bi, s: (0, 0)),
            state((GW, GW)), state((1, GW)), state((1, GW)),
        ],
        out_specs=[
            pl.BlockSpec((None, blk, GW), lambda bi, s: (bi, s, 0)),
            state((GW, GW)), state((1, GW)), state((1, GW)),
        ],
        out_shape=[
            jax.ShapeDtypeStruct((b, t, GW), BF16),
            jax.ShapeDtypeStruct((b, GW, GW), F32),
            jax.ShapeDtypeStruct((b, 1, GW), F32),
            jax.ShapeDtypeStruct((b, 1, GW), F32),
        ],
        compiler_params=_cparams("parallel", "arbitrary"),
        name="mlstm64",
    )(proj3, proj3, proj3, proj3, proj3, gt3, bias_c, bias_r, gh, c0, n0, m0)


def _to_block_diag(c):
    b = c.shape[0]
    eye = jnp.eye(H, dtype=c.dtype)
    return jnp.einsum("bhde,hg->bhdge", c, eye).reshape(b, GW, GW)


def _from_block_diag(cbd):
    b = cbd.shape[0]
    c5 = cbd.reshape(b, H, DH, H, DH)
    return jnp.stack([c5[:, h, :, h, :] for h in range(H)], axis=1)


def _split2(x):
    hi = x.astype(BF16)
    lo = (x - hi.astype(F32)).astype(BF16)
    return hi, lo


def _dot_sp(a, b):
    return _dot(a[0], b[0]) + (_dot(a[0], b[1]) + _dot(a[1], b[0]))


def _unit_lower_inverses(nmats, l):
    eye = (_iota((l, l), 0) == _iota((l, l), 1)).astype(F32)
    ps = [eye - n for n in nmats]
    qs = [_split2(n) for n in nmats]
    qs = [_split2(_dot_sp(q, q)) for q in qs]
    power = 2
    while power < l:
        ps = [p + _dot_sp(_split2(p), q) for p, q in zip(ps, qs)]
        power *= 2
        if power < l:
            qs = [_split2(_dot_sp(q, q)) for q in qs]
    return ps


def _l2norm(x):
    return x * lax.rsqrt(jnp.sum(x * x, axis=-1, keepdims=True) + 1e-6)


def _gdn_kernel(x_ref, z_ref, gc_ref, gr_ref, hist_ref, cw_ref, ac_ref, ar_ref, dc_ref, dr_ref,
                gh_ref, s0_ref, out_ref, s_ref, carry_scr, *, l, nck):
    @pl.when(pl.program_id(1) == 0)
    def _():
        s_ref[...] = s0_ref[...]
        carry_scr[...] = hist_ref[...]

    blk = l * nck
    x = x_ref[...]
    ext = jnp.concatenate([carry_scr[...], x], axis=0)
    carry_scr[...] = x[blk - SUBLANES:, :]
    y = x * cw_ref[3:4, :]
    for j in range(1, 4):
        y = y + ext[SUBLANES - j:SUBLANES - j + blk, :] * cw_ref[3 - j:4 - j, :]
    y = y * _sigmoid(y)

    incl, strict = _tri_masks(l)
    lo_tri = incl.astype(BF16)
    up_tri = (_iota((l, l), 0) <= _iota((l, l), 1)).astype(BF16)

    qraw = [y[ck * l:(ck + 1) * l, h * DH:(h + 1) * DH] for ck in range(nck) for h in range(H)]
    kraw = [y[ck * l:(ck + 1) * l, GW + h * DH:GW + (h + 1) * DH] for ck in range(nck) for h in range(H)]
    vraw = [y[ck * l:(ck + 1) * l, 2 * GW + h * DH:2 * GW + (h + 1) * DH] for ck in range(nck) for h in range(H)]
    qnorm = [_l2norm(a) * (DH ** -0.5) for a in qraw]
    knorm = [_l2norm(a) for a in kraw]
    probs = []
    for ck in range(nck):
        rows = slice(ck * l, (ck + 1) * l)
        gcol = gc_ref[rows, :]
        grow = gr_ref[ck]
        beta_cs = _sigmoid(gcol)
        dec_c = -jnp.exp(ac_ref[...]) * _softplus(gcol + dc_ref[...])
        dec_r = -jnp.exp(ar_ref[...]) * _softplus(grow + dr_ref[...])
        gcs = _cumsum_cols(dec_c, lo_tri)
        grs = _cumsum_rows(dec_r, up_tri)
        for h in range(H):
            beta = beta_cs[:, GATE_OFF + 2 * H + h:GATE_OFF + 2 * H + h + 1]
            g_c = gcs[:, GATE_OFF + 3 * H + h:GATE_OFF + 3 * H + h + 1]
            g_r = grs[3 * H + h:3 * H + h + 1, :]
            q, k, v = qnorm[ck * H + h], knorm[ck * H + h], vraw[ck * H + h]
            decay = jnp.exp(jnp.where(incl, g_c - g_r, NEG_INF))
            eg = jnp.exp(g_c)
            g_last = g_c[l - 1:l, :]
            probs.append(dict(
                rows=rows, h=h, qb=q.astype(BF16), kb=k.astype(BF16), beta=beta, decay=decay,
                rhs=jnp.concatenate([v * beta, k * (beta * eg)], axis=-1),
                qeg=(q * eg).astype(BF16), kdec=(k * jnp.exp(g_last - g_c)).astype(BF16),
                sdec=jnp.exp(g_last)))
    kks = [_dot_nt(p["kb"], p["kb"]) for p in probs]
    qks = [_dot_nt(p["qb"], p["kb"]) for p in probs]
    a_lows = [jnp.where(strict, p["beta"] * kk * p["decay"], 0.0) for p, kk in zip(probs, kks)]
    attns = [(qk * p["decay"]).astype(BF16) for p, qk in zip(probs, qks)]
    tinvs = _unit_lower_inverses(a_lows, l)
    sols = [_dot_sp(_split2(t), _split2(p["rhs"])) for t, p in zip(tinvs, probs)]

    states = [s_ref[h] for h in range(H)]
    for ck in range(nck):
        ps = probs[ck * H:(ck + 1) * H]
        ss = sols[ck * H:(ck + 1) * H]
        at = attns[ck * H:(ck + 1) * H]
        sbs = [s.astype(BF16) for s in states]
        wss = [_dot(sol[:, DH:].astype(BF16), sb) for sol, sb in zip(ss, sbs)]
        qss = [_dot(p["qeg"], sb) for p, sb in zip(ps, sbs)]
        vnbs = [(sol[:, :DH] - ws).astype(BF16) for sol, ws in zip(ss, wss)]
        os_ = [qs + _dot(a, vnb) for qs, a, vnb in zip(qss, at, vnbs)]
        states = [p["sdec"] * s + _dot_tn(p["kdec"], vnb) for p, s, vnb in zip(ps, states, vnbs)]
        for p, o in zip(ps, os_):
            hs = slice(p["h"] * DH, (p["h"] + 1) * DH)
            zg = z_ref[p["rows"], hs]
            yo = (o * lax.rsqrt(jnp.mean(o * o, axis=-1, keepdims=True) + EPS) * gh_ref[:, hs]
                  * (zg * _sigmoid(zg)))
            out_ref[p["rows"], hs] = yo.astype(out_ref.dtype)
    for h in range(H):
        s_ref[h] = states[h]


def _gdn(proj3, gt3, hist8, cw, a_c, a_r, dt_c, dt_r, gh, s0, l, nck):
    b, t, _ = proj3.shape
    blk = l * nck
    steps = t // blk
    state = lambda shp: pl.BlockSpec((None,) + shp, lambda bi, s: (bi,) + (0,) * len(shp))
    const = lambda shp: pl.BlockSpec(shp, lambda bi, s: (0,) * len(shp))
    return pl.pallas_call(
        functools.partial(_gdn_kernel, l=l, nck=nck),
        grid=(b, steps),
        in_specs=[
            pl.BlockSpec((None, blk, 3 * GW), lambda bi, s: (bi, s, COL_CX // (3 * GW))),
            pl.BlockSpec((None, blk, GW), lambda bi, s: (bi, s, COL_CZ // GW)),
            pl.BlockSpec((None, blk, LANES), lambda bi, s: (bi, s, GATE_BLK)),
            pl.BlockSpec((None, nck, N_GATES, l), lambda bi, s: (bi, s, 0, 0)),
            state((SUBLANES, 3 * GW)),
            const((4, 3 * GW)),
            const((1, LANES)), const((N_GATES, 1)), const((1, LANES)), const((N_GATES, 1)),
            const((1, GW)),
            state((H, DH, DH)),
        ],
        out_specs=[
            pl.BlockSpec((None, blk, GW), lambda bi, s: (bi, s, 0)),
            state((H, DH, DH)),
        ],
        out_shape=[
            jax.ShapeDtypeStruct((b, t, GW), BF16),
            jax.ShapeDtypeStruct((b, H, DH, DH), F32),
        ],
        scratch_shapes=[pltpu.VMEM((SUBLANES, 3 * GW), F32)],
        compiler_params=_cparams("parallel", "arbitrary"),
        name="gdn",
    )(proj3, proj3, proj3, gt3, hist8, cw, a_c, a_r, dt_c, dt_r, gh, s0)


def _bd_split(x, mask):
    hi, lo = _split2(x)
    return _block_diag_rows(hi, mask), _block_diag_rows(lo, mask)


def _gdn64_kernel(x_ref, z_ref, gc_ref, gr_ref, hist_ref, cw_ref, ac_ref, ar_ref, dc_ref, dr_ref,
                  gh_ref, s0_ref, out_ref, s_ref, carry_scr, *, nck):
    l = CHUNK

    @pl.when(pl.program_id(1) == 0)
    def _():
        s_ref[...] = s0_ref[...]
        carry_scr[...] = hist_ref[...]

    blk = l * nck
    x = x_ref[...]
    ext = jnp.concatenate([carry_scr[...], x], axis=0)
    carry_scr[...] = x[blk - SUBLANES:, :]
    y = x * cw_ref[3:4, :]
    for j in range(1, 4):
        y = y + ext[SUBLANES - j:SUBLANES - j + blk, :] * cw_ref[3 - j:4 - j, :]
    y = y * _sigmoid(y)

    bmask = _block_mask(GW, GW)
    bones = bmask.astype(BF16)
    within = _iota((GW, GW), 0) % DH <= _iota((GW, GW), 1) % DH
    up_bd = jnp.logical_and(bmask, within).astype(BF16)
    lo_tri = (_iota((l, l), 0) >= _iota((l, l), 1)).astype(BF16)
    key_pos = _iota((l, GW), 1) % DH
    incl = _iota((l, GW), 0) >= key_pos
    strict = _iota((l, GW), 0) > key_pos
    eye_t = (_iota((l, GW), 0) == key_pos).astype(F32)
    e_b = _expander(GATE_OFF + 2 * H)
    e_a = _expander(GATE_OFF + 3 * H)

    def head_sums(a):
        return _head_sums(a, bones)

    def shared_rhs(lhs_splits, rhs_bd):
        n = len(lhs_splits)
        big = _dot(jnp.concatenate([part for sp in lhs_splits for part in sp], axis=0), rhs_bd[0])
        small = _dot(jnp.concatenate([sp[0] for sp in lhs_splits], axis=0), rhs_bd[1])
        return [big[2 * i * l:(2 * i + 1) * l] + big[(2 * i + 1) * l:(2 * i + 2) * l]
                + small[i * l:(i + 1) * l] for i in range(n)]

    yq, yk, yv = y[:, :GW], y[:, GW:2 * GW], y[:, 2 * GW:]
    qn_all = yq * lax.rsqrt(head_sums(yq * yq) + 1e-6) * (DH ** -0.5)
    kn_all = yk * lax.rsqrt(head_sums(yk * yk) + 1e-6)

    cks = []
    for ck in range(nck):
        rows = slice(ck * l, (ck + 1) * l)
        gcol = gc_ref[rows, :]
        dec_c = -jnp.exp(ac_ref[...]) * _softplus(gcol + dc_ref[...])
        beta = _expand(_sigmoid(gcol), e_b)
        g_c = _expand(_cumsum_cols_wide(dec_c, lo_tri), e_a)
        grow_t = jnp.concatenate([gr_ref[ck]] * H, axis=1)
        dec_r = -jnp.exp(ar_ref[...]) * _softplus(grow_t + dr_ref[...])
        g_r = _row_select(_dot_stacked(_split3(dec_r), up_bd), 3 * H)
        decay = jnp.exp(jnp.where(incl, g_c - g_r, NEG_INF))
        eg = jnp.exp(g_c)
        g_last = g_c[l - 1:l, :]
        q, k, v = qn_all[rows], kn_all[rows], yv[rows]
        cks.append(dict(rows=rows, qb=q.astype(BF16), kb=k.astype(BF16),
                        beta=beta, decay=decay, rhs_v=v * beta, rhs_k=k * (beta * eg),
                        qeg=(q * eg).astype(BF16), kdec=(k * jnp.exp(g_last - g_c)).astype(BF16),
                        sdec=jnp.exp(g_last)))
    kqs = [_dot_nt(jnp.concatenate([p["kb"], p["qb"]], axis=0), _block_diag_rows(p["kb"], bmask))
           for p in cks]
    nmats = [jnp.where(strict, p["beta"] * kq[:l] * p["decay"], 0.0) for p, kq in zip(cks, kqs)]
    attns = [(kq[l:] * p["decay"]).astype(BF16) for p, kq in zip(cks, kqs)]

    ps = [eye_t - n for n in nmats]
    qs = [shared_rhs([_split2(n)], _bd_split(n, bmask))[0] for n in nmats]
    power = 2
    while power < l:
        power *= 2
        if power < l:
            res = [shared_rhs([_split2(p), _split2(q)], _bd_split(q, bmask)) for p, q in zip(ps, qs)]
            ps = [p + r[0] for p, r in zip(ps, res)]
            qs = [r[1] for r in res]
        else:
            ps = [p + shared_rhs([_split2(p)], _bd_split(q, bmask))[0] for p, q in zip(ps, qs)]
    tsp = [_split2(p) for p in ps]
    us = [shared_rhs([t], _bd_split(p["rhs_v"], bmask))[0] for t, p in zip(tsp, cks)]
    ws = [shared_rhs([t], _bd_split(p["rhs_k"], bmask))[0].astype(BF16) for t, p in zip(tsp, cks)]

    wu = [jnp.concatenate([w, u.astype(BF16)], axis=1) for w, u in zip(ws, us)]
    kwu = [_dot_tn(p["kdec"], x) for p, x in zip(cks, wu)]
    awu = [_dot(at, jnp.concatenate([_block_diag_rows(x[:, :GW], bmask),
                                     _block_diag_rows(x[:, GW:], bmask)], axis=1))
           for at, x in zip(attns, wu)]
    gmats = [jnp.where(bmask, x[:, :GW], 0.0).astype(BF16) for x in kwu]
    bmats = [jnp.where(bmask, x[:, GW:], 0.0) for x in kwu]
    qts = [(p["qeg"].astype(F32) - x[:, :GW]).astype(BF16) for p, x in zip(cks, awu)]

    s_run = s_ref[...]
    outs = []
    for p, g, bm, qt, x in zip(cks, gmats, bmats, qts, awu):
        ys = _dot(jnp.concatenate([g, qt], axis=0), s_run.astype(BF16))
        outs.append(ys[GW:] + x[:, GW:])
        s_run = p["sdec"] * s_run - ys[:GW] + bm
    s_ref[...] = s_run

    msq = [head_sums(o * o) * (1.0 / DH) for o in outs]
    for p, o, m2 in zip(cks, outs, msq):
        zg = z_ref[p["rows"], :]
        yo = o * lax.rsqrt(m2 + EPS) * gh_ref[...] * (zg * _sigmoid(zg))
        out_ref[p["rows"], :] = yo.astype(out_ref.dtype)


def _gdn64(proj3, gt3, hist8, cw, a_c, a_r, dt_c, dt_r, gh, s0, nck):
    b, t, _ = proj3.shape
    blk = CHUNK * nck
    steps = t // blk
    state = lambda shp: pl.BlockSpec((None,) + shp, lambda bi, s: (bi,) + (0,) * len(shp))
    const = lambda shp: pl.BlockSpec(shp, lambda bi, s: (0,) * len(shp))
    return pl.pallas_call(
        functools.partial(_gdn64_kernel, nck=nck),
        grid=(b, steps),
        in_specs=[
            pl.BlockSpec((None, blk, 3 * GW), lambda bi, s: (bi, s, COL_CX // (3 * GW))),
            pl.BlockSpec((None, blk, GW), lambda bi, s: (bi, s, COL_CZ // GW)),
            pl.BlockSpec((None, blk, LANES), lambda bi, s: (bi, s, GATE_BLK)),
            pl.BlockSpec((None, nck, N_GATES, CHUNK), lambda bi, s: (bi, s, 0, 0)),
            state((SUBLANES, 3 * GW)),
            const((4, 3 * GW)),
            const((1, LANES)), const((N_GATES, 1)), const((1, LANES)), const((N_GATES, 1)),
            const((1, GW)),
            state((GW, GW)),
        ],
        out_specs=[
            pl.BlockSpec((None, blk, GW), lambda bi, s: (bi, s, 0)),
            state((GW, GW)),
        ],
        out_shape=[
            jax.ShapeDtypeStruct((b, t, GW), BF16),
            jax.ShapeDtypeStruct((b, GW, GW), F32),
        ],
        scratch_shapes=[pltpu.VMEM((SUBLANES, 3 * GW), F32)],
        compiler_params=_cparams("parallel", "arbitrary"),
        name="gdn64",
    )(proj3, proj3, proj3, gt3, hist8, cw, a_c, a_r, dt_c, dt_r, gh, s0)


HEAD_PAD = 128
DPAD = H * HEAD_PAD
BF16_ROWS = 16
VT_PAD = -(-(DH + 1) // BF16_ROWS) * BF16_ROWS
VT_ROWS = H * VT_PAD


def _tile_heads(t):
    return jnp.concatenate([t] * H, axis=-1)


def _dprep_kernel(tail_ref, gq_ref, gkv_ref, wq_ref, wqp_ref, ka_ref, ckv_ref, kpe_ref, qc_ref):
    ka = ka_ref[...]
    kb = pltpu.roll(ka, 64, axis=1)
    nope = _iota(ka.shape, 1) < QK_NOPE
    qcos = jnp.where(nope, 1.0, kb)
    qsin = jnp.where(nope, 0.0, ka)
    hq = _rms(tail_ref[:, :Q_LORA], gq_ref[...]).astype(BF16)
    qc = _dot(hq, wq_ref[...]) * _tile_heads(qcos) + _dot(hq, wqp_ref[...]) * _tile_heads(qsin)
    qc_ref[...] = (qc * (MLA_SCALE * LOG2E)).astype(BF16)
    ckv_ref[...] = _rms(tail_ref[:, Q_LORA:Q_LORA + KV_LORA], gkv_ref[...])
    kr = tail_ref[:, Q_LORA + KV_LORA:]
    kpe = kr * ka + pltpu.roll(kr, 64, axis=1) * kb
    kpe_ref[...] = kpe[:, :QK_ROPE]


def _dprep(proj, gq, gkv, wq, wqp, ka, tm):
    n = proj.shape[0]
    row = lambda w: pl.BlockSpec((tm, w), lambda i: (i, 0))
    const = lambda a, b: pl.BlockSpec((a, b), lambda i: (0, 0))
    return pl.pallas_call(
        _dprep_kernel,
        grid=(n // tm,),
        in_specs=[
            pl.BlockSpec((tm, TAIL_W), lambda i: (i, COL_TAIL // TAIL_W)),
            const(1, Q_LORA), const(1, KV_LORA), const(Q_LORA, DPAD), const(Q_LORA, DPAD),
            row(LANES),
        ],
        out_specs=[row(KV_LORA), row(QK_ROPE), row(DPAD)],
        out_shape=[
            jax.ShapeDtypeStruct((n, KV_LORA), F32),
            jax.ShapeDtypeStruct((n, QK_ROPE), F32),
            jax.ShapeDtypeStruct((n, DPAD), BF16),
        ],
        compiler_params=_cparams("parallel"),
        name="dprep",
    )(proj, gq, gkv, wq, wqp, ka)


LOG2E = 1.4426950408889634


def _kvup_t_kernel(ckv_ref, kpe_ref, wk_ref, wvt_ref, pm_ref, onet_ref, kc_ref, vt_ref):
    c = ckv_ref[...].astype(BF16)
    kc_ref[...] = (_dot(c, wk_ref[...]) + _dot(kpe_ref[...].astype(BF16), pm_ref[...])).astype(BF16)
    vt_ref[...] = (_dot_nt(wvt_ref[...], c) + onet_ref[...]).astype(BF16)


def _kvup_t(ckv, kpe, wk, wvt, pm, onet, tm):
    m = ckv.shape[0]
    row = lambda w: pl.BlockSpec((tm, w), lambda i: (i, 0))
    const = lambda a, b: pl.BlockSpec((a, b), lambda i: (0, 0))
    return pl.pallas_call(
        _kvup_t_kernel,
        grid=(m // tm,),
        in_specs=[row(KV_LORA), row(QK_ROPE), const(KV_LORA, DPAD), const(VT_ROWS, KV_LORA),
                  const(QK_ROPE, DPAD), const(VT_ROWS, 1)],
        out_specs=[row(DPAD), pl.BlockSpec((VT_ROWS, tm), lambda i: (0, i))],
        out_shape=[jax.ShapeDtypeStruct((m, DPAD), BF16), jax.ShapeDtypeStruct((VT_ROWS, m), BF16)],
        compiler_params=_cparams("parallel"),
        name="kvup_t",
    )(ckv, kpe, wk, wvt, pm, onet)


def _mla_prompt_kernel(qi_ref, ki_ref, q_ref, k_ref, vt_ref, ghc_ref, out_ref, m_scr, acc_scr, *, bq, bk,
                       qw, ahead):
    p = pl.program_id(0)
    q_i = qi_ref[p]
    k_i = ki_ref[p]
    last = (q_i * bq) // bk

    @pl.when(k_i == 0)
    def _():
        m_scr[...] = jnp.full(m_scr.shape, NEG_INF, F32)
        acc_scr[...] = jnp.zeros(acc_scr.shape, F32)

    def step(diag, nk=bk):
        if diag:
            key_chunk = k_i * (bk // CHUNK) + _iota((nk, bq), 0) // CHUNK
            qry_chunk = q_i * (bq // CHUNK) + _iota((nk, bq), 1) // CHUNK
            allowed = key_chunk <= qry_chunk

        units = [(h, c) for h in range(H) for c in range(bq // qw)]
        rc = min(nk, 64)

        def scores(u):
            h, c = u
            hs = slice(h * HEAD_PAD, (h + 1) * HEAD_PAD)
            return _dot_nt(k_ref[:nk, hs], q_ref[c * qw:(c + 1) * qw, hs])

        def update(u, st):
            h, c = u
            hs = slice(h * HEAD_PAD, (h + 1) * HEAD_PAD)
            qs = slice(c * qw, (c + 1) * qw)
            if diag:
                st = jnp.where(allowed[:, qs], st, NEG_INF)
            m_prev = m_scr[h, :, qs]
            mx = st[:rc]
            for r in range(1, nk // rc):
                mx = jnp.maximum(mx, st[r * rc:(r + 1) * rc])
            m_new = jnp.maximum(m_prev, jnp.max(mx, axis=0, keepdims=True))
            alpha = jnp.exp2(m_prev - m_new)[0:1]
            m_row = m_new[0:1]
            pt = jnp.concatenate([jnp.exp2(st[r * rc:(r + 1) * rc] - m_row).astype(BF16)
                                  for r in range(nk // rc)], axis=0)
            acc = alpha * acc_scr[h, :, qs] + _dot(vt_ref[h * VT_PAD:(h + 1) * VT_PAD, :nk], pt)
            if not diag:
                m_scr[h, :, qs] = m_new
                acc_scr[h, :, qs] = acc
            return acc

        accs = []
        pending = [scores(u) for u in units[:ahead]]
        for idx, u in enumerate(units):
            if idx + ahead < len(units):
                pending.append(scores(units[idx + ahead]))
            accs.append(update(u, pending.pop(0)))
        per_head = bq // qw
        return [jnp.concatenate(accs[h * per_head:(h + 1) * per_head], axis=1) for h in range(H)]

    @pl.when(k_i < last)
    def _():
        step(False)

    def finish(accs):
        ys = []
        for h, acc in enumerate(accs):
            o = acc[:DH] / acc[DH:DH + 1]
            ms = jnp.mean(o * o, axis=0, keepdims=True)
            ys.append(o * lax.rsqrt(ms + EPS) * ghc_ref[h * DH:(h + 1) * DH, :])
        out_ref[...] = jnp.concatenate(ys, axis=0).T.astype(out_ref.dtype)

    nsub = bk // bq
    for sub in range(nsub):
        @pl.when(jnp.logical_and(k_i == last, q_i % nsub == sub))
        def _(sub=sub):
            finish(step(True, (sub + 1) * bq))


def _mla_prompt(qc, kc, vt, ghc, bq, bk):
    t = qc.shape[0]
    assert t % bq == 0 and t % bk == 0 and bk % bq == 0
    pairs = [(i, j) for i in range(t // bq) for j in range((i * bq) // bk + 1)]
    qi = jnp.asarray([i for i, _ in pairs], jnp.int32)
    ki = jnp.asarray([j for _, j in pairs], jnp.int32)
    grid_spec = pltpu.PrefetchScalarGridSpec(
        num_scalar_prefetch=2,
        grid=(len(pairs),),
        in_specs=[
            pl.BlockSpec((bq, DPAD), lambda p, qi, ki: (qi[p], 0)),
            pl.BlockSpec((bk, DPAD), lambda p, qi, ki: (ki[p], 0)),
            pl.BlockSpec((VT_ROWS, bk), lambda p, qi, ki: (0, ki[p])),
            pl.BlockSpec((GW, 1), lambda p, qi, ki: (0, 0)),
        ],
        out_specs=pl.BlockSpec((bq, GW), lambda p, qi, ki: (qi[p], 0)),
        scratch_shapes=[
            pltpu.VMEM((H, SUBLANES, bq), F32),
            pltpu.VMEM((H, VT_PAD, bq), F32),
        ],
    )
    return pl.pallas_call(
        functools.partial(_mla_prompt_kernel, bq=bq, bk=bk, qw=min(bq, 256), ahead=2),
        grid_spec=grid_spec,
        out_shape=jax.ShapeDtypeStruct((t, GW), BF16),
        compiler_params=_cparams("arbitrary"),
        name="mla_prompt",
    )(qi, ki, qc, kc, vt, ghc)


LAT_W = 2 * LANES
ONE_LANE = KV_LORA + QK_ROPE


def _mla_sample_kernel(q_ref, ckvp_ref, kpep_ref, ckvn_ref, kpen_ref, wabs_ref, wv_ref, gh_ref, out_ref):
    s_len = q_ref.shape[0]

    def latent_rows(ckv_ref, kpe_ref):
        n = ckv_ref.shape[0]
        tail = jnp.concatenate([kpe_ref[...].astype(BF16), jnp.zeros((n, LANES - QK_ROPE), BF16)], axis=1)
        tail = jnp.where(_iota((n, LANES), 1) == QK_ROPE, jnp.ones((), BF16), tail)
        return jnp.concatenate([ckv_ref[...].astype(BF16), tail], axis=1)

    kvp = latent_rows(ckvp_ref, kpep_ref)
    kvn = latent_rows(ckvn_ref, kpen_ref)
    qabs = jnp.concatenate([_dot(q_ref[:, h * HEAD_PAD:(h + 1) * HEAD_PAD], wabs_ref[h]) for h in range(H)],
                           axis=0).astype(BF16)
    s1 = _dot_nt(qabs, kvp)
    s2 = _dot_nt(qabs, kvn)
    m = jnp.maximum(jnp.max(s1, axis=-1, keepdims=True), jnp.max(s2, axis=-1, keepdims=True))
    acc = _dot(jnp.exp2(s1 - m).astype(BF16), kvp) + _dot(jnp.exp2(s2 - m).astype(BF16), kvn)
    olat = (acc[:, :KV_LORA] / acc[:, ONE_LANE:ONE_LANE + 1]).astype(BF16)
    rows = slice(0, s_len)
    for h in range(H):
        o = _dot(olat[h * s_len:(h + 1) * s_len], wv_ref[h])
        _head_norm_store(out_ref, rows, h, o, gh_ref)


def _mla_sample(qc3, ckvp, kpep, ckvn, kpen, wabs, wv, gh):
    b, s, _ = qc3.shape
    npast = ckvp.shape[1]
    per_b = lambda r, w: pl.BlockSpec((None, r, w), lambda i: (i, 0, 0))
    const = lambda shp: pl.BlockSpec(shp, lambda i: (0,) * len(shp))
    return pl.pallas_call(
        _mla_sample_kernel,
        grid=(b,),
        in_specs=[per_b(s, DPAD), per_b(npast, KV_LORA), per_b(npast, QK_ROPE), per_b(s, KV_LORA),
                  per_b(s, QK_ROPE), const((H, HEAD_PAD, LAT_W)), const((H, KV_LORA, DH)), const((1, GW))],
        out_specs=per_b(s, GW),
        out_shape=jax.ShapeDtypeStruct((b, s, GW), BF16),
        compiler_params=_cparams("parallel"),
        name="mla_sample",
    )(qc3, ckvp, kpep, ckvn, kpen, wabs, wv, gh)


def _ffn_kernel(*refs, seq_len, final_norm, tf):
    mix_refs, refs = refs[1:5], refs[:1] + refs[5:]
    if seq_len is None:
        (x_ref, wo_ref, g_ref, wup_ref, cw_ref, wd_ref, gf_ref, out_ref, ga_ref,
         act_scr, x2_scr, carry_scr) = refs
    else:
        (x_ref, wo_ref, g_ref, wup_ref, cw_ref, wd_ref, gf_ref, h1_ref, h2_ref, out_ref, ga_ref,
         act_scr, x2_scr) = refs
    tm = x_ref.shape[0]
    nj = D_FF // tf
    x2 = x_ref[...]
    for grp, m_ref in enumerate(mix_refs):
        x2 = x2 + _dot(m_ref[...], wo_ref[grp * GW:(grp + 1) * GW, :])
    x2_scr[...] = x2
    h = _rms(x2, g_ref[...]).astype(BF16)
    row = _iota((tm, tf), 0)

    if seq_len is None:
        @pl.when(pl.program_id(0) == 0)
        def _():
            carry_scr[...] = jnp.zeros(carry_scr.shape, F32)

    def up(j):
        cols = slice(j * tf, (j + 1) * tf)
        ucols = slice(D_FF + j * tf, D_FF + (j + 1) * tf)
        return _dot(h, wup_ref[:, cols]), _dot(h, wup_ref[:, ucols])

    def gate(j, ga, u):
        cols = slice(j * tf, (j + 1) * tf)
        r1 = pltpu.roll(ga, 1, axis=0)
        r2 = pltpu.roll(ga, 2, axis=0)
        if seq_len is None:
            c1 = carry_scr[SUBLANES - 1:SUBLANES, cols]
            c2 = carry_scr[SUBLANES - 2:SUBLANES - 1, cols]
            prev1 = jnp.where(row >= 1, r1, c1)
            prev2 = jnp.where(row >= 2, r2, jnp.where(row == 1, c1, c2))
            tail = ga[tm - SUBLANES:, :]
            carry_scr[:, cols] = tail
            ga_ref[:, cols] = tail
        else:
            t = row % seq_len
            prev1 = jnp.where(t >= 1, r1, h1_ref[:, cols])
            prev2 = jnp.where(t >= 2, r2, h2_ref[:, cols])
            ga_ref[:, cols] = ga
        conv = prev2 * cw_ref[0:1, cols] + prev1 * cw_ref[1:2, cols] + ga * cw_ref[2:3, cols]
        act_scr[:, cols] = (conv * _sigmoid(conv) * u).astype(BF16)

    pending = up(0)
    for j in range(nj):
        nxt = up(j + 1) if j + 1 < nj else None
        gate(j, *pending)
        pending = nxt
    y = x2_scr[...] + _dot(act_scr[...], wd_ref[...])
    if final_norm:
        y = _rms(y, gf_ref[...])
    out_ref[...] = y


def _ffn(x, mixed, w_out, g, w_up, cw, w_down, gf, h1, h2, *, tm, tf, seq_len, final_norm):
    n = x.shape[0]
    ni = n // tm
    resident = lambda a, b: pl.BlockSpec((a, b), lambda i: (0, 0), pipeline_mode=pl.Buffered(1))
    mix_spec = pl.BlockSpec((tm, GW), lambda i: (i, 0))
    in_specs = [
        pl.BlockSpec((tm, D_MODEL), lambda i: (i, 0)),
        mix_spec, mix_spec, mix_spec, mix_spec,
        resident(D_MODEL, D_MODEL),
        resident(1, D_MODEL),
        resident(D_MODEL, 2 * D_FF),
        resident(3, D_FF),
        resident(D_FF, D_MODEL),
        resident(1, D_MODEL),
    ]
    args = [x, *mixed, w_out, g, w_up, cw, w_down, gf]
    scratch = [pltpu.VMEM((tm, D_FF), BF16), pltpu.VMEM((tm, D_MODEL), F32)]
    if seq_len is None:
        ga_spec = pl.BlockSpec((None, SUBLANES, D_FF), lambda i: (i, 0, 0))
        ga_shape = jax.ShapeDtypeStruct((ni, SUBLANES, D_FF), F32)
        scratch.append(pltpu.VMEM((SUBLANES, D_FF), F32))
    else:
        in_specs += [pl.BlockSpec((tm, D_FF), lambda i: (i, 0))] * 2
        args += [h1, h2]
        ga_spec = pl.BlockSpec((tm, D_FF), lambda i: (i, 0))
        ga_shape = jax.ShapeDtypeStruct((n, D_FF), F32)
    return pl.pallas_call(
        functools.partial(_ffn_kernel, seq_len=seq_len, final_norm=final_norm, tf=tf),
        grid=(ni,),
        in_specs=in_specs,
        out_specs=[pl.BlockSpec((tm, D_MODEL), lambda i: (i, 0)), ga_spec],
        out_shape=[jax.ShapeDtypeStruct((n, D_MODEL), F32), ga_shape],
        scratch_shapes=scratch,
        compiler_params=_cparams("arbitrary"),
        name="ffn",
    )(*args)


def _rope_tables(offset, t):
    half = QK_ROPE // 2
    per_row = LANES // half
    assert t % per_row == 0
    inv = ROPE_THETA ** (-jnp.arange(half, dtype=F32) / half)
    pos = offset + per_row * _iota((t // per_row, LANES), 0) + _iota((t // per_row, LANES), 1) // half
    ang = pos.astype(F32) * jnp.tile(inv, per_row)[None, :]
    cos, sin = jnp.cos(ang).reshape(t, half), jnp.sin(ang).reshape(t, half)
    z32 = jnp.zeros((t, 32), F32)
    return jnp.concatenate([cos, cos, z32, -sin, sin, z32], -1)


def _rel_bias(table, n_past, n_q, n_k):
    period = n_q + n_k
    jmi = jnp.concatenate([jnp.arange(0, n_k + 1), jnp.arange(-(n_q - 1), 0)])
    diag = table[:, jnp.clip(n_past - jmi, -REL_MAX, REL_MAX) + REL_MAX]
    rows = jnp.tile(diag, (1, n_q))[:, :n_q * (period - 1)].reshape(table.shape[0], n_q, period - 1)
    return rows[:, :, :n_k]


def _swap_halves(w):
    half = w.shape[-1] // 2
    return jnp.concatenate([w[..., half:], w[..., :half]], -1)


def _layer_weights(lw):
    (g_mix, w_in, a_rel_bias, b_i_bias, b_f_bias, c_conv_w, c_a_log, c_dt_bias,
     d_g_q, d_w_q_up, d_g_kv, d_w_kv_up, g_head, w_out, g_ffn, w_up, f_conv_w, w_down) = lw
    o = 0
    cols = {}
    for name, size in (("a", 3 * GW), ("b", 4 * GW), ("bg", 2 * H), ("c", 3 * GW), ("cz", GW),
                       ("cg", 2 * H), ("dq", Q_LORA), ("dkv", KV_LORA), ("dkr", QK_ROPE)):
        cols[name] = w_in[:, o:o + size]
        o += size
    gates = jnp.concatenate([cols["bg"], cols["cg"]], -1)
    pad16 = jnp.zeros((D_MODEL, 16), F32)
    pad32 = jnp.zeros((D_MODEL, 32), F32)
    w_perm = jnp.concatenate([cols["c"], cols["a"], cols["dq"], cols["dkv"], cols["dkr"], gates, pad16,
                              _swap_halves(cols["dkr"]), pad32, cols["b"], cols["cz"]], -1)
    zc = lambda n: jnp.zeros((1, n), F32)
    zr = lambda n: jnp.zeros((n, 1), F32)
    bias_c = jnp.concatenate([zc(GATE_OFF), b_i_bias[None], b_f_bias[None], zc(LANES - GATE_OFF - 2 * H)], -1)
    bias_r = jnp.concatenate([b_i_bias[:, None], b_f_bias[:, None], zr(2 * H)], 0)
    alog_c = jnp.concatenate([zc(GATE_OFF + 3 * H), c_a_log[None], zc(LANES - GATE_OFF - 4 * H)], -1)
    alog_r = jnp.concatenate([zr(3 * H), c_a_log[:, None]], 0)
    dt_c = jnp.concatenate([zc(GATE_OFF + 3 * H), c_dt_bias[None], zc(LANES - GATE_OFF - 4 * H)], -1)
    dt_r = jnp.concatenate([zr(3 * H), c_dt_bias[:, None]], 0)

    wq = d_w_q_up.reshape(Q_LORA, H, QK_NOPE + QK_ROPE)
    z_h32 = jnp.zeros((Q_LORA, H, 32), F32)
    wq_full = jnp.concatenate([wq, z_h32], -1).reshape(Q_LORA, DPAD)
    wq_part = jnp.concatenate([jnp.zeros((Q_LORA, H, QK_NOPE), F32), _swap_halves(wq[..., QK_NOPE:]), z_h32],
                              -1).reshape(Q_LORA, DPAD)
    wkv = d_w_kv_up.reshape(KV_LORA, H, 2 * DH)
    z_h64 = jnp.zeros((KV_LORA, H, DH), F32)
    wk_full = jnp.concatenate([wkv[..., :DH], z_h64], -1).reshape(KV_LORA, DPAD)
    place = jnp.concatenate([jnp.zeros((QK_ROPE, QK_NOPE), F32), jnp.eye(QK_ROPE, dtype=F32),
                             jnp.zeros((QK_ROPE, 32), F32)], -1)
    pmat = jnp.concatenate([place] * H, -1)
    wk_t = jnp.transpose(wkv[..., :DH], (1, 2, 0))
    rope_rows = jnp.concatenate([jnp.zeros((QK_ROPE, KV_LORA), F32), jnp.eye(QK_ROPE, dtype=F32),
                                 jnp.zeros((QK_ROPE, LAT_W - KV_LORA - QK_ROPE), F32)], -1)
    wabs = jnp.concatenate([
        jnp.concatenate([wk_t, jnp.zeros((H, QK_NOPE, LAT_W - KV_LORA), F32)], -1),
        jnp.broadcast_to(rope_rows, (H, QK_ROPE, LAT_W)),
        jnp.zeros((H, HEAD_PAD - QK_NOPE - QK_ROPE, LAT_W), F32)], 1)
    return dict(
        g_mix=g_mix[None], w_in=w_perm.astype(BF16), w_gt=gates.T.astype(BF16),
        table=a_rel_bias, bias_c=bias_c, bias_r=bias_r, alog_c=alog_c, alog_r=alog_r, dt_c=dt_c, dt_r=dt_r,
        c_conv_w=c_conv_w, g_q=d_g_q[None], g_kv=d_g_kv[None],
        wq=wq_full.astype(BF16), wqp=wq_part.astype(BF16), wk=wk_full.astype(BF16),
        pmat=pmat.astype(BF16), wabs=wabs.astype(BF16),
        wv_heads=jnp.transpose(wkv[..., DH:], (1, 0, 2)).astype(BF16),
        wvt=jnp.concatenate([wkv[..., DH:], jnp.zeros((KV_LORA, H, VT_PAD - DH), F32)], -1)
        .reshape(KV_LORA, VT_ROWS).T.astype(BF16),
        vonest=(jnp.arange(VT_ROWS) % VT_PAD == DH).astype(F32)[:, None],
        g_head=g_head.reshape(4, 1, GW), w_out=w_out.astype(BF16),
        g_ffn=g_ffn[None], w_up=w_up.astype(BF16), f_conv_w=f_conv_w, w_down=w_down.astype(BF16))


def _gates_t3(gt, b, t, l):
    return gt.reshape(N_GATES, b, t // l, l).transpose(1, 2, 0, 3)


def _layer(x, offset, st, w, gf, final_norm, cfg):
    b, t, _ = x.shape
    n = b * t
    first = st is None
    x2 = x.reshape(n, D_MODEL)
    proj, gt = _inproj(x2, w["g_mix"], w["w_in"], w["w_gt"], cfg["tm"])
    proj3 = proj.reshape(b, t, PROJ_W)
    gh = w["g_head"]
    l = min(t, CHUNK)
    gt3 = _gates_t3(gt, b, t, l)

    new_ak = proj3[:, t - min(A_PAST, t):, COL_A + GW:COL_A + 2 * GW].reshape(b, -1, H, DH)
    new_av = proj3[:, t - min(A_PAST, t):, COL_A + 2 * GW:COL_A + 3 * GW].reshape(b, -1, H, DH)
    if first:
        bias = _rel_bias(w["table"], A_PAST, CHUNK, A_PAST + CHUNK)
        oa = _band_prompt(proj, bias, gh[0].T)
    else:
        npast = st[0].shape[1]
        bias = _rel_bias(w["table"], npast, t, npast + t)
        oa = _band_sample(proj3, st[0].reshape(b, npast, GW), st[1].reshape(b, npast, GW), bias, gh[0])
        oa = oa.reshape(n, GW)

    if first:
        c0 = jnp.zeros((b, H, DH, DH), F32)
        n0 = jnp.zeros((b, H, DH), F32)
        m0 = jnp.zeros((b, 1, H), F32)
    else:
        c0, n0, m0 = st[2], st[3], st[4][:, None, :]
    if l == CHUNK:
        ob, cbd, nrow, mrow = _mlstm64(proj3, gt3, w["bias_c"], w["bias_r"], gh[1], _to_block_diag(c0),
                                       n0.reshape(b, 1, GW), jnp.repeat(m0, DH, axis=-1), cfg["nck"])
        new_bc, new_bn, new_bm = _from_block_diag(cbd), nrow.reshape(b, H, DH), mrow[:, 0, ::DH]
    else:
        ob, new_bc, new_bn, new_bm = _mlstm(proj3, gt3, w["bias_c"], w["bias_r"], gh[1], c0, n0, m0,
                                            l, cfg["nck"])
        new_bm = new_bm[:, 0, :]

    if first:
        hist8 = jnp.zeros((b, SUBLANES, 3 * GW), F32)
        s0 = jnp.zeros((b, H, DH, DH), F32)
    else:
        hist8 = jnp.concatenate([jnp.zeros((b, SUBLANES - 3, 3 * GW), F32), st[6]], 1)
        s0 = st[5]
    gdn_args = (proj3, gt3, hist8, w["c_conv_w"], w["alog_c"], w["alog_r"], w["dt_c"], w["dt_r"], gh[2])
    if l == CHUNK:
        oc, sbd = _gdn64(*gdn_args, _to_block_diag(s0), cfg["nck"])
        new_cs = _from_block_diag(sbd)
    else:
        oc, new_cs = _gdn(*gdn_args, s0, l, cfg["nck"])
    new_cconv = proj3[:, t - 3:, COL_CX:COL_CX + 3 * GW]

    ka = jnp.tile(_rope_tables(offset, t), (b, 1))
    ckv, kpe, qc = _dprep(proj, w["g_q"], w["g_kv"], w["wq"], w["wqp"], ka, cfg["tm_d"])
    if first:
        kc, vt = _kvup_t(ckv, kpe, w["wk"], w["wvt"], w["pmat"], w["vonest"], cfg["tm_kv"])
        od = _mla_prompt(qc, kc, vt, gh[3].T, cfg["mla_bq"], cfg["mla_bk"])
    else:
        od = _mla_sample(qc.reshape(b, t, DPAD), st[7], st[8], ckv.reshape(b, t, KV_LORA),
                         kpe.reshape(b, t, QK_ROPE), w["wabs"], w["wv_heads"], gh[3])
        od = od.reshape(n, GW)

    ffn_args = (x2, (oa, ob.reshape(n, GW), oc.reshape(n, GW), od), w["w_out"],
                w["g_ffn"], w["w_up"], w["f_conv_w"], w["w_down"], gf)
    if first:
        y, ga_tail = _ffn(*ffn_args, None, None,
                          tm=cfg["tm"], tf=cfg["tf"], seq_len=None, final_norm=final_norm)
        new_fconv = ga_tail[-1, SUBLANES - 2:, :][None]
    else:
        hist = st[9]
        zrow = jnp.zeros((b, t - 1, D_FF), F32)
        h1 = jnp.concatenate([hist[:, 1:2], zrow], 1).reshape(n, D_FF)
        h2 = jnp.concatenate([hist, zrow[:, 1:]], 1).reshape(n, D_FF)
        y, ga = _ffn(*ffn_args, h1, h2,
                     tm=cfg["tm"], tf=cfg["tf"], seq_len=t, final_norm=final_norm)
        new_fconv = ga.reshape(b, t, D_FF)[:, t - 2:]
    state = (new_ak, new_av, new_bc, new_bn, new_bm, new_cs, new_cconv,
             ckv.reshape(b, t, KV_LORA), kpe.reshape(b, t, QK_ROPE), new_fconv)
    return y.reshape(b, t, D_MODEL), state


def _config(b, t):
    n = b * t
    tm = min(n, 1024)
    return dict(tm=tm, tf=256, nck=1 if t <= CHUNK else 16,
                tm_d=min(n, 1024), tm_kv=min(n, 2048), mla_bq=min(t, 512), mla_bk=min(t, 2048))


def kernel(x_prompt, x_sample, cache_a_k, cache_a_v, state_b_c, state_b_n, state_b_m, state_c_s, cache_c_conv, cache_d_ckv, cache_d_kpe, cache_ffn_conv, g_mix, w_in, a_rel_bias, b_i_bias, b_f_bias, c_conv_w, c_a_log, c_dt_bias, d_g_q, d_w_q_up, d_g_kv, d_w_kv_up, g_head, w_out, g_ffn, w_up, f_conv_w, w_down, g_final):
    layer_w = (g_mix, w_in, a_rel_bias, b_i_bias, b_f_bias, c_conv_w, c_a_log, c_dt_bias,
               d_g_q, d_w_q_up, d_g_kv, d_w_kv_up, g_head, w_out, g_ffn, w_up, f_conv_w, w_down)
    depth = g_mix.shape[0]
    past = cache_d_ckv.shape[2]
    xp, xs = x_prompt, x_sample
    cfg_p = _config(*x_prompt.shape[:2])
    cfg_s = _config(*x_sample.shape[:2])
    gf = g_final[None]
    new_p, new_s = [], []
    for l in range(depth):
        w = _layer_weights(tuple(a[l] for a in layer_w))
        last = l == depth - 1
        xp, sp_l = _layer(xp, 0, None, w, gf, last, cfg_p)
        st = (cache_a_k[l], cache_a_v[l], state_b_c[l], state_b_n[l], state_b_m[l],
              state_c_s[l], cache_c_conv[l], cache_d_ckv[l], cache_d_kpe[l], cache_ffn_conv[l])
        xs, ss_l = _layer(xs, past, st, w, gf, last, cfg_s)
        new_p.append(sp_l)
        new_s.append(ss_l)
    outs = [xp, xs]
    for i in range(10):
        outs.append(jnp.stack([s[i] for s in new_p]))
        outs.append(jnp.stack([s[i] for s in new_s]))
    return tuple(outs)
```
